```python
import math
import jax
import jax.numpy as jnp
from jax import lax
import numpy as np

D_MODEL = 4096
BATCH = 2
SEQ = 4096
DEPTH = 4

N_META = 16
GRID_W = 64
HEAD_DIM = 128
NA_WIDTH = D_MODEL // 2
NA_HEADS = NA_WIDTH // HEAD_DIM
NA_KH = 8
NA_KW = 16
HG_WIDTH = D_MODEL // 2
HG_HEADS = HG_WIDTH // HEAD_DIM
HG_CHUNK = 64
EVEN_MIX = NA_WIDTH + HG_WIDTH
EVEN_SPLITS = [NA_WIDTH, 2 * NA_WIDTH, 3 * NA_WIDTH, 3 * NA_WIDTH + HG_WIDTH, 3 * NA_WIDTH + 2 * HG_WIDTH, 3 * NA_WIDTH + 3 * HG_WIDTH, 3 * NA_WIDTH + 4 * HG_WIDTH]
EVEN_IN = 3 * NA_WIDTH + 4 * HG_WIDTH + EVEN_MIX
HY_WIDTH = D_MODEL
HY_ORDER = 2
HY_SHORT = 3
HY_EMB = 33
HY_BANDS = (HY_EMB - 1) // 2
HY_FFN = 64
HY_FAST_DECAY = 0.3
HY_SLOW_DECAY = 1.5
HY_TARGET = 1e-2
ODD_IN = (HY_ORDER + 1) * HY_WIDTH + HY_WIDTH
N_EVEN = (DEPTH + 1) // 2
N_ODD = DEPTH // 2
EPS = 1e-6

kernel_name = 'hybrid_natten_hgrn2_hyena_encoder'


def rms_norm(x, g):
    xf = x.astype(jnp.float32)
    xf = xf * lax.rsqrt(jnp.mean(xf * xf, axis=-1, keepdims=True) + EPS)
    return (xf * g.astype(jnp.float32)).astype(x.dtype)


def split_heads(a):
    b, l, w = a.shape
    return a.reshape(b, l, w // HEAD_DIM, HEAD_DIM).transpose(0, 2, 1, 3)


def merge_heads(a):
    b, h, l, d = a.shape
    return a.transpose(0, 2, 1, 3).reshape(b, l, h * d)


def neighbourhood_attention(q, k, v, rpb, meta_bias):
    b, h, l, dh = q.shape
    t = l - N_META
    rows = t // GRID_W
    kh = min(NA_KH, rows)
    scale = dh ** -0.5
    f32 = jnp.float32
    qm, km, vm = q[:, :, :N_META], k[:, :, :N_META], v[:, :, :N_META]
    qg = q[:, :, N_META:].reshape(b, h, rows, GRID_W, dh)
    kg = k[:, :, N_META:].reshape(b, h, rows, GRID_W, dh)
    vg = v[:, :, N_META:].reshape(b, h, rows, GRID_W, dh)
    mbias = meta_bias.astype(f32)[None, :, None, :]
    s_mm = jnp.einsum('bhqd,bhkd->bhqk', qm, km).astype(f32) * scale + mbias
    o_meta = jnp.einsum('bhqk,bhkd->bhqd', jax.nn.softmax(s_mm, axis=-1).astype(v.dtype), vm)
    cols = np.arange(GRID_W)
    c0 = np.clip(cols - NA_KW // 2, 0, GRID_W - NA_KW)
    col_ok = (cols[None, :] >= c0[:, None]) & (cols[None, :] < c0[:, None] + NA_KW)
    col_idx = np.clip(cols[None, :] - cols[:, None] + NA_KW - 1, 0, 2 * NA_KW - 2)
    rpb_cols = rpb.astype(f32)[:, :, col_idx]

    def row_block(r):
        r0 = jnp.clip(r - kh // 2, 0, rows - kh)
        kb = lax.dynamic_slice_in_dim(kg, r0, kh, axis=2)
        vb = lax.dynamic_slice_in_dim(vg, r0, kh, axis=2)
        qr = lax.dynamic_index_in_dim(qg, r, axis=2, keepdims=False)
        s_win = jnp.einsum('bhqd,bhikd->bhqik', qr, kb).astype(f32) * scale
        row_off = r0 + jnp.arange(kh) - r + NA_KH - 1
        bias = jnp.take(rpb_cols, row_off, axis=1).transpose(0, 2, 1, 3)
        s_win = jnp.where(col_ok[None, None, :, None, :], s_win + bias[None], -jnp.inf)
        s_meta = jnp.einsum('bhqd,bhmd->bhqm', qr, km).astype(f32) * scale + mbias
        s = jnp.concatenate([s_meta, s_win.reshape(b, h, GRID_W, kh * GRID_W)], axis=-1)
        p = jax.nn.softmax(s, axis=-1).astype(v.dtype)
        p_win = p[..., N_META:].reshape(b, h, GRID_W, kh, GRID_W)
        return (jnp.einsum('bhqm,bhmd->bhqd', p[..., :N_META], vm)
                + jnp.einsum('bhqik,bhikd->bhqd', p_win, vb))

    o_grid = lax.map(row_block, jnp.arange(rows))
    o_grid = o_grid.transpose(1, 2, 0, 3, 4).reshape(b, h, t, dh)
    return jnp.concatenate([o_meta, o_grid], axis=2)


def hgrn2_bidirectional(q, i, f_fwd, f_bwd, lb, norm_gain):
    out_dtype = q.dtype
    f32 = jnp.float32
    b, l, _ = q.shape
    pad = HG_CHUNK - N_META
    q = jax.nn.silu(q.astype(f32)) * HEAD_DIM ** -0.5
    v = i.astype(f32)

    def gates(f_raw, lb_d):
        f_raw = f_raw.astype(f32)
        log_f = jnp.logaddexp(jnp.log(lb_d), jnp.log1p(-lb_d) + jax.nn.log_sigmoid(f_raw))
        key = (1.0 - lb_d) * jax.nn.sigmoid(-f_raw)
        return log_f, key

    g_f, k_f = gates(f_fwd, lb[0])
    g_b, k_b = gates(f_bwd, lb[1])

    def both_dirs(a_fwd, a_bwd):
        pa = lambda a: jnp.pad(a, ((0, 0), (pad, 0), (0, 0)))
        s = jnp.stack([pa(a_fwd), jnp.flip(pa(a_bwd), axis=1)])
        n_chunks = s.shape[2] // HG_CHUNK
        s = s.reshape(2, b, n_chunks, HG_CHUNK, HG_HEADS, HEAD_DIM)
        return s.transpose(2, 0, 1, 4, 3, 5)

    qs, vs = both_dirs(q, q), both_dirs(v, v)
    ks, gs = both_dirs(k_f, k_b), both_dirs(g_f, g_b)
    mask = np.tril(np.ones((HG_CHUNK, HG_CHUNK), dtype=bool))

    def chunk_step(state, inp):
        qc, kc, vc, gc = inp
        gcum = jnp.cumsum(gc, axis=-2)
        diff = gcum[..., :, None, :] - gcum[..., None, :, :]
        decay = jnp.exp(jnp.where(mask[:, :, None], diff, -jnp.inf))
        attn = jnp.einsum('zbhtd,zbhsd,zbhtsd->zbhts', qc, kc, decay)
        o = (jnp.einsum('zbhts,zbhsv->zbhtv', attn, vc)
             + jnp.einsum('zbhtd,zbhdv->zbhtv', qc * jnp.exp(gcum), state))
        g_end = gcum[..., -1:, :]
        state = (jnp.exp(g_end)[..., 0, :, None] * state
                 + jnp.einsum('zbhsd,zbhsv->zbhdv', kc * jnp.exp(g_end - gcum), vc))
        return state, o

    s0 = jnp.zeros((2, b, HG_HEADS, HEAD_DIM, HEAD_DIM), f32)
    _, o = lax.scan(chunk_step, s0, (qs, ks, vs, gs))
    o = o.transpose(1, 2, 0, 4, 3, 5).reshape(2, b, -1, HG_HEADS, HEAD_DIM)
    o = (o[0] + jnp.flip(o[1], axis=1))[:, pad:]
    o = o * lax.rsqrt(jnp.mean(o * o, axis=-1, keepdims=True) + EPS)
    o = o * norm_gain.astype(f32).reshape(HG_HEADS, HEAD_DIM)
    return o.reshape(b, l, HG_WIDTH).astype(out_dtype)


def short_conv(u, w):
    up = jnp.pad(u, ((0, 0), (1, 1), (0, 0)))
    return up[:, :-2] * w[:, 0] + up[:, 1:-1] * w[:, 1] + up[:, 2:] * w[:, 2]


def hyena_filters(l, w1, b1, fr1, w2, b2, fr2, w3):
    f32 = jnp.float32
    t = jnp.linspace(0.0, 1.0, l, dtype=f32)[:, None]
    w = 2.0 * math.pi * jnp.arange(l, dtype=f32)[:, None] / l
    bands = jnp.linspace(1e-4, HY_BANDS - 1, HY_BANDS, dtype=f32)[None, :]
    z = jnp.concatenate([t, jnp.cos(bands * w), -jnp.sin(bands * w)], axis=-1)
    hid = jnp.sin(fr1.astype(f32) * (z @ w1.astype(f32) + b1.astype(f32)))
    hid = jnp.sin(fr2.astype(f32) * (hid @ w2.astype(f32) + b2.astype(f32)))
    filt = (hid @ w3.astype(f32)).reshape(l, 2, HY_ORDER, HY_WIDTH)
    min_decay = math.log(HY_TARGET) / HY_SLOW_DECAY
    max_decay = math.log(HY_TARGET) / HY_FAST_DECAY
    deltas = jnp.abs(jnp.linspace(min_decay, max_decay, HY_WIDTH, dtype=f32))
    window = jnp.exp(-t * deltas[None, :])
    return filt * window[:, None, None, :]


def long_conv(u, filt, bias):
    l = u.shape[1]
    hf, hb = filt[:, 0], filt[:, 1]
    kern = jnp.concatenate([hf[:1] + hb[:1], hf[1:], jnp.zeros_like(hf[:1]), hb[:0:-1]], axis=0)
    uf = jnp.fft.rfft(u.astype(jnp.float32), n=2 * l, axis=1)
    kf = jnp.fft.rfft(kern, n=2 * l, axis=0)
    y = jnp.fft.irfft(uf * kf[None], n=2 * l, axis=1)[:, :l]
    return (y + u.astype(jnp.float32) * bias.astype(jnp.float32)).astype(u.dtype)


def even_mixer(hn, w_in, w_out, rpb, meta_bias, lb, norm_gain):
    p = hn @ w_in
    qa, ka, va, qb, ib, f_fwd, f_bwd, gate = jnp.split(p, EVEN_SPLITS, axis=-1)
    oa = merge_heads(neighbourhood_attention(split_heads(qa), split_heads(ka), split_heads(va), rpb, meta_bias))
    ob = hgrn2_bidirectional(qb, ib, f_fwd, f_bwd, lb, norm_gain)
    y = jnp.concatenate([oa, ob], axis=-1) * jax.nn.silu(gate)
    return y @ w_out


def odd_mixer(hn, w_in, w_out, w_short, w1, b1, fr1, w2, b2, fr2, w3, bias):
    p = hn @ w_in
    u, gate = p[..., :3 * HY_WIDTH], p[..., 3 * HY_WIDTH:]
    u = short_conv(u, w_short)
    v, x1, x2 = jnp.split(u, 3, axis=-1)
    filt = hyena_filters(hn.shape[1], w1, b1, fr1, w2, b2, fr2, w3)
    z = v
    for n, x_n in enumerate((x1, x2)):
        z = x_n * long_conv(z, filt[:, :, n], bias[n])
    return (z * jax.nn.silu(gate)) @ w_out


def setup_inputs(seed: int = 0) -> dict:
    key = jax.random.key(seed)
    ks = jax.random.split(key, 21)
    f32 = jnp.float32

    def nrm(k, shape, std):
        return jax.random.normal(k, shape, f32) * std

    return {
        'x': nrm(ks[0], (BATCH, SEQ, D_MODEL), 1.0),
        'meta_tokens': nrm(ks[1], (N_META, D_MODEL), 1.0),
        'norm_pre': 1.0 + nrm(ks[2], (DEPTH, D_MODEL), 0.02),
        'norm_post': 1.0 + nrm(ks[3], (DEPTH, D_MODEL), 0.02),
        'ev_w_in': nrm(ks[4], (N_EVEN, D_MODEL, EVEN_IN), D_MODEL ** -0.5),
        'ev_w_out': nrm(ks[5], (N_EVEN, EVEN_MIX, D_MODEL), EVEN_MIX ** -0.5),
        'na_rpb': nrm(ks[6], (N_EVEN, NA_HEADS, 2 * NA_KH - 1, 2 * NA_KW - 1), 0.02),
        'na_meta_bias': nrm(ks[7], (N_EVEN, NA_HEADS, N_META), 0.02),
        'hg_lower': 1.0 + nrm(ks[8], (N_EVEN, 2, HG_WIDTH), 0.1),
        'hg_norm': 1.0 + nrm(ks[9], (N_EVEN, HG_WIDTH), 0.02),
        'od_w_in': nrm(ks[10], (N_ODD, D_MODEL, ODD_IN), D_MODEL ** -0.5),
        'od_w_out': nrm(ks[11], (N_ODD, HY_WIDTH, D_MODEL), HY_WIDTH ** -0.5),
        'hy_short': nrm(ks[12], (N_ODD, 3 * HY_WIDTH, HY_SHORT), HY_SHORT ** -0.5),
        'hy_ffn_w1': nrm(ks[13], (N_ODD, HY_EMB, HY_FFN), HY_EMB ** -0.5),
        'hy_ffn_b1': nrm(ks[14], (N_ODD, HY_FFN), 0.1),
        'hy_ffn_freq1': 1.0 + nrm(ks[15], (N_ODD, HY_FFN), 0.1),
        'hy_ffn_w2': nrm(ks[16], (N_ODD, HY_FFN, HY_FFN), HY_FFN ** -0.5),
        'hy_ffn_b2': nrm(ks[17], (N_ODD, HY_FFN), 0.1),
        'hy_ffn_freq2': 1.0 + nrm(ks[18], (N_ODD, HY_FFN), 0.1),
        'hy_ffn_w3': nrm(ks[19], (N_ODD, HY_FFN, 2 * HY_ORDER * HY_WIDTH), 0.02),
        'hy_bias': nrm(ks[20], (N_ODD, HY_ORDER, HY_WIDTH), 0.5),
    }


def reference(x, meta_tokens, norm_pre, norm_post, ev_w_in, ev_w_out, na_rpb, na_meta_bias, hg_lower, hg_norm,
              od_w_in, od_w_out, hy_short, hy_ffn_w1, hy_ffn_b1, hy_ffn_freq1, hy_ffn_w2, hy_ffn_b2, hy_ffn_freq2,
              hy_ffn_w3, hy_bias):
    b = x.shape[0]
    lb_all = jnp.cumsum(jax.nn.softmax(hg_lower.astype(jnp.float32), axis=0), axis=0)
    lb_all = lb_all - lb_all[:1]
    meta = jnp.broadcast_to(meta_tokens.astype(x.dtype)[None], (b, N_META, D_MODEL))
    h = jnp.concatenate([meta, x], axis=1)
    for layer in range(DEPTH):
        j = layer // 2
        hn = rms_norm(h, norm_pre[layer])
        if layer % 2 == 0:
            out = even_mixer(hn, ev_w_in[j], ev_w_out[j], na_rpb[j], na_meta_bias[j], lb_all[j], hg_norm[j])
        else:
            out = odd_mixer(hn, od_w_in[j], od_w_out[j], hy_short[j], hy_ffn_w1[j], hy_ffn_b1[j], hy_ffn_freq1[j],
                            hy_ffn_w2[j], hy_ffn_b2[j], hy_ffn_freq2[j], hy_ffn_w3[j], hy_bias[j])
        h = h + rms_norm(out, norm_post[layer])
    return h[:, N_META:]
```

```python
import functools
import math

import jax
import jax.numpy as jnp
import numpy as np
from jax import lax
from jax.experimental import pallas as pl
from jax.experimental.pallas import tpu as pltpu

D_MODEL = 4096
SEQ = 4096
N_META = 16
GRID_W = 64
HEAD_DIM = 128
NA_WIDTH = D_MODEL // 2
NA_KH = 8
NA_KW = 16
HG_WIDTH = D_MODEL // 2
HG_HEADS = HG_WIDTH // HEAD_DIM
HG_CHUNK = 64
HY_WIDTH = D_MODEL
HY_ORDER = 2
HY_EMB = 33
HY_BANDS = (HY_EMB - 1) // 2
HY_FAST_DECAY = 0.3
HY_SLOW_DECAY = 1.5
HY_TARGET = 1e-2
EPS = 1e-6

PAD = HG_CHUNK - N_META
LP = PAD + N_META + SEQ
L_TOK = N_META + SEQ

V7X_VMEM_LIMIT = 56 * 1024 * 1024


def _rmsnorm_kernel(x_ref, g_ref, o_ref):
    x = x_ref[...]
    r = lax.rsqrt(jnp.mean(x * x, axis=-1, keepdims=True) + EPS)
    o_ref[...] = (x * r * g_ref[...]).astype(o_ref.dtype)


def _rmsnorm(x2d, g, out_dtype, tr=320):
    m, d = x2d.shape
    return pl.pallas_call(
        _rmsnorm_kernel,
        grid=(m // tr,),
        in_specs=[pl.BlockSpec((tr, d), lambda i: (i, 0)), pl.BlockSpec((1, d), lambda i: (0, 0))],
        out_specs=pl.BlockSpec((tr, d), lambda i: (i, 0)),
        out_shape=jax.ShapeDtypeStruct((m, d), out_dtype),
        compiler_params=pltpu.CompilerParams(vmem_limit_bytes=V7X_VMEM_LIMIT),
        name="rmsnorm_pre",
    )(x2d, g.reshape(1, d))


def _post_kernel(h_ref, y_ref, g_ref, o_ref):
    y = y_ref[...]
    r = lax.rsqrt(jnp.mean(y * y, axis=-1, keepdims=True) + EPS)
    o_ref[...] = h_ref[...] + y * r * g_ref[...]


def _residual_rmsnorm(h2d, y2d, g, tr=320):
    m, d = h2d.shape
    return pl.pallas_call(
        _post_kernel,
        grid=(m // tr,),
        in_specs=[pl.BlockSpec((tr, d), lambda i: (i, 0)), pl.BlockSpec((tr, d), lambda i: (i, 0)),
                  pl.BlockSpec((1, d), lambda i: (0, 0))],
        out_specs=pl.BlockSpec((tr, d), lambda i: (i, 0)),
        out_shape=jax.ShapeDtypeStruct((m, d), jnp.float32),
        compiler_params=pltpu.CompilerParams(vmem_limit_bytes=V7X_VMEM_LIMIT),
        name="residual_rmsnorm_post",
    )(h2d, y2d, g.reshape(1, d))


def _matmul_kernel(x_ref, w_ref, o_ref, acc_ref):
    k = pl.program_id(2)

    @pl.when(k == 0)
    def _():
        acc_ref[...] = jnp.zeros_like(acc_ref)

    acc_ref[...] += jnp.dot(x_ref[...].astype(jnp.bfloat16), w_ref[...].astype(jnp.bfloat16),
                            preferred_element_type=jnp.float32)

    @pl.when(k == pl.num_programs(2) - 1)
    def _():
        o_ref[...] = acc_ref[...].astype(o_ref.dtype)


def _matmul(x, w, tm=2080, tn=1024, tk=512, out_dtype=jnp.float32):
    m, kd = x.shape
    _, n = w.shape
    assert m % tm == 0 and n % tn == 0 and kd % tk == 0
    return pl.pallas_call(
        _matmul_kernel,
        grid=(m // tm, n // tn, kd // tk),
        in_specs=[pl.BlockSpec((tm, tk), lambda i, j, k: (i, k)), pl.BlockSpec((tk, tn), lambda i, j, k: (k, j))],
        out_specs=pl.BlockSpec((tm, tn), lambda i, j, k: (i, j)),
        out_shape=jax.ShapeDtypeStruct((m, n), out_dtype),
        scratch_shapes=[pltpu.VMEM((tm, tn), jnp.float32)],
        compiler_params=pltpu.CompilerParams(
            dimension_semantics=("parallel", "parallel", "arbitrary"), vmem_limit_bytes=V7X_VMEM_LIMIT),
        name="projection_matmul",
    )(x, w)


def _split_heads(a):
    b, l, w = a.shape
    return a.reshape(b, l, w // HEAD_DIM, HEAD_DIM).transpose(0, 2, 1, 3)


def _merge_heads(a):
    b, h, l, d = a.shape
    return a.transpose(0, 2, 1, 3).reshape(b, l, h * d)


def _natten(q, k, v, rpb, meta_bias):
    b, h, l, dh = q.shape
    t = l - N_META
    rows = t // GRID_W
    kh = min(NA_KH, rows)
    scale = dh ** -0.5
    f32 = jnp.float32
    qm, km, vm = q[:, :, :N_META], k[:, :, :N_META], v[:, :, :N_META]
    qg = q[:, :, N_META:].reshape(b, h, rows, GRID_W, dh)
    kg = k[:, :, N_META:].reshape(b, h, rows, GRID_W, dh)
    vg = v[:, :, N_META:].reshape(b, h, rows, GRID_W, dh)
    mbias = meta_bias.astype(f32)[None, :, None, :]
    s_mm = jnp.einsum('bhqd,bhkd->bhqk', qm, km).astype(f32) * scale + mbias
    o_meta = jnp.einsum('bhqk,bhkd->bhqd', jax.nn.softmax(s_mm, axis=-1).astype(v.dtype), vm)
    cols = np.arange(GRID_W)
    c0 = np.clip(cols - NA_KW // 2, 0, GRID_W - NA_KW)
    col_ok = (cols[None, :] >= c0[:, None]) & (cols[None, :] < c0[:, None] + NA_KW)
    col_idx = np.clip(cols[None, :] - cols[:, None] + NA_KW - 1, 0, 2 * NA_KW - 2)
    rpb_cols = rpb.astype(f32)[:, :, col_idx]

    def row_block(r):
        r0 = jnp.clip(r - kh // 2, 0, rows - kh)
        kb = lax.dynamic_slice_in_dim(kg, r0, kh, axis=2)
        vb = lax.dynamic_slice_in_dim(vg, r0, kh, axis=2)
        qr = lax.dynamic_index_in_dim(qg, r, axis=2, keepdims=False)
        s_win = jnp.einsum('bhqd,bhikd->bhqik', qr, kb).astype(f32) * scale
        row_off = r0 + jnp.arange(kh) - r + NA_KH - 1
        bias = jnp.take(rpb_cols, row_off, axis=1).transpose(0, 2, 1, 3)
        s_win = jnp.where(col_ok[None, None, :, None, :], s_win + bias[None], -jnp.inf)
        s_meta = jnp.einsum('bhqd,bhmd->bhqm', qr, km).astype(f32) * scale + mbias
        s = jnp.concatenate([s_meta, s_win.reshape(b, h, GRID_W, kh * GRID_W)], axis=-1)
        p = jax.nn.softmax(s, axis=-1).astype(v.dtype)
        p_win = p[..., N_META:].reshape(b, h, GRID_W, kh, GRID_W)
        return (jnp.einsum('bhqm,bhmd->bhqd', p[..., :N_META], vm)
                + jnp.einsum('bhqik,bhikd->bhqd', p_win, vb))

    o_grid = lax.map(row_block, jnp.arange(rows))
    o_grid = o_grid.transpose(1, 2, 0, 3, 4).reshape(b, h, t, dh)
    return jnp.concatenate([o_meta, o_grid], axis=2)


def _hgrn2(q, i, f_fwd, f_bwd, lb, norm_gain):
    f32 = jnp.float32
    b, l, _ = q.shape
    pad = HG_CHUNK - N_META
    q = jax.nn.silu(q.astype(f32)) * HEAD_DIM ** -0.5
    v = i.astype(f32)

    def gates(f_raw, lb_d):
        log_f = jnp.logaddexp(jnp.log(lb_d), jnp.log1p(-lb_d) + jax.nn.log_sigmoid(f_raw))
        key = (1.0 - lb_d) * jax.nn.sigmoid(-f_raw)
        return log_f, key

    g_f, k_f = gates(f_fwd, lb[0])
    g_b, k_b = gates(f_bwd, lb[1])

    def both_dirs(a_fwd, a_bwd):
        pa = lambda a: jnp.pad(a, ((0, 0), (pad, 0), (0, 0)))
        s = jnp.stack([pa(a_fwd), jnp.flip(pa(a_bwd), axis=1)])
        n_chunks = s.shape[2] // HG_CHUNK
        s = s.reshape(2, b, n_chunks, HG_CHUNK, HG_HEADS, HEAD_DIM)
        return s.transpose(2, 0, 1, 4, 3, 5)

    qs, vs = both_dirs(q, q), both_dirs(v, v)
    ks, gs = both_dirs(k_f, k_b), both_dirs(g_f, g_b)
    mask = np.tril(np.ones((HG_CHUNK, HG_CHUNK), dtype=bool))

    def chunk_step(state, inp):
        qc, kc, vc, gc = inp
        gcum = jnp.cumsum(gc, axis=-2)
        diff = gcum[..., :, None, :] - gcum[..., None, :, :]
        decay = jnp.exp(jnp.where(mask[:, :, None], diff, -jnp.inf))
        attn = jnp.einsum('zbhtd,zbhsd,zbhtsd->zbhts', qc, kc, decay)
        o = (jnp.einsum('zbhts,zbhsv->zbhtv', attn, vc)
             + jnp.einsum('zbhtd,zbhdv->zbhtv', qc * jnp.exp(gcum), state))
        g_end = gcum[..., -1:, :]
        state = (jnp.exp(g_end)[..., 0, :, None] * state
                 + jnp.einsum('zbhsd,zbhsv->zbhdv', kc * jnp.exp(g_end - gcum), vc))
        return state, o

    s0 = jnp.zeros((2, b, HG_HEADS, HEAD_DIM, HEAD_DIM), f32)
    _, o = lax.scan(chunk_step, s0, (qs, ks, vs, gs))
    o = o.transpose(1, 2, 0, 4, 3, 5).reshape(2, b, -1, HG_HEADS, HEAD_DIM)
    o = (o[0] + jnp.flip(o[1], axis=1))[:, pad:]
    o = o * lax.rsqrt(jnp.mean(o * o, axis=-1, keepdims=True) + EPS)
    o = o * norm_gain.astype(f32).reshape(HG_HEADS, HEAD_DIM)
    return o.reshape(b, l, HG_WIDTH)


def _short_conv(u, w):
    up = jnp.pad(u, ((0, 0), (1, 1), (0, 0)))
    return up[:, :-2] * w[:, 0] + up[:, 1:-1] * w[:, 1] + up[:, 2:] * w[:, 2]


def _hyena_filters(l, w1, b1, fr1, w2, b2, fr2, w3):
    f32 = jnp.float32
    t = jnp.linspace(0.0, 1.0, l, dtype=f32)[:, None]
    w = 2.0 * math.pi * jnp.arange(l, dtype=f32)[:, None] / l
    bands = jnp.linspace(1e-4, HY_BANDS - 1, HY_BANDS, dtype=f32)[None, :]
    z = jnp.concatenate([t, jnp.cos(bands * w), -jnp.sin(bands * w)], axis=-1)
    hid = jnp.sin(fr1 * (z @ w1 + b1))
    hid = jnp.sin(fr2 * (hid @ w2 + b2))
    filt = (hid @ w3).reshape(l, 2, HY_ORDER, HY_WIDTH)
    min_decay = math.log(HY_TARGET) / HY_SLOW_DECAY
    max_decay = math.log(HY_TARGET) / HY_FAST_DECAY
    deltas = jnp.abs(jnp.linspace(min_decay, max_decay, HY_WIDTH, dtype=f32))
    window = jnp.exp(-t * deltas[None, :])
    return filt * window[:, None, None, :]


def _long_conv(u, filt, bias):
    l = u.shape[1]
    hf, hb = filt[:, 0], filt[:, 1]
    kern = jnp.concatenate([hf[:1] + hb[:1], hf[1:], jnp.zeros_like(hf[:1]), hb[:0:-1]], axis=0)
    uf = jnp.fft.rfft(u, n=2 * l, axis=1)
    kf = jnp.fft.rfft(kern, n=2 * l, axis=0)
    y = jnp.fft.irfft(uf * kf[None], n=2 * l, axis=1)[:, :l]
    return y + u * bias


def _even_mix(p, rpb, meta_bias, lb, norm_gain):
    nw, hw = NA_WIDTH, HG_WIDTH
    qa, ka, va = p[..., :nw], p[..., nw:2 * nw], p[..., 2 * nw:3 * nw]
    o = 3 * nw
    qb, ib, f_fwd, f_bwd = (p[..., o + j * hw:o + (j + 1) * hw] for j in range(4))
    gate = p[..., o + 4 * hw:]
    oa = _merge_heads(_natten(_split_heads(qa), _split_heads(ka), _split_heads(va), rpb, meta_bias))
    ob = _hgrn2(qb, ib, f_fwd, f_bwd, lb, norm_gain)
    return jnp.concatenate([oa, ob], axis=-1) * jax.nn.silu(gate)


def _odd_mix(p, w_short, w1, b1, fr1, w2, b2, fr2, w3, bias):
    u, gate = p[..., :3 * HY_WIDTH], p[..., 3 * HY_WIDTH:]
    u = _short_conv(u, w_short)
    v, x1, x2 = jnp.split(u, 3, axis=-1)
    filt = _hyena_filters(p.shape[1], w1, b1, fr1, w2, b2, fr2, w3)
    z = v
    for n, x_n in enumerate((x1, x2)):
        z = x_n * _long_conv(z, filt[:, :, n], bias[n])
    return z * jax.nn.silu(gate)


def kernel(x, meta_tokens, norm_pre, norm_post, ev_w_in, ev_w_out, na_rpb, na_meta_bias, hg_lower, hg_norm,
           od_w_in, od_w_out, hy_short, hy_ffn_w1, hy_ffn_b1, hy_ffn_freq1, hy_ffn_w2, hy_ffn_b2, hy_ffn_freq2,
           hy_ffn_w3, hy_bias):
    b = x.shape[0]
    depth = norm_pre.shape[0]
    f32 = jnp.float32
    lb_all = jnp.cumsum(jax.nn.softmax(hg_lower.astype(f32), axis=0), axis=0)
    lb_all = lb_all - lb_all[:1]
    meta = jnp.broadcast_to(meta_tokens.astype(f32)[None], (b, N_META, D_MODEL))
    h = jnp.concatenate([jnp.zeros((b, PAD, D_MODEL), f32), meta, x.astype(f32)], axis=1)
    h = h.reshape(b * LP, D_MODEL)
    pad_rows = jnp.zeros((b, PAD, D_MODEL), f32)
    for layer in range(depth):
        j = layer // 2
        hn = _rmsnorm(h, norm_pre[layer], jnp.bfloat16)
        if layer % 2 == 0:
            p = _matmul(hn, ev_w_in[j]).reshape(b, LP, -1)[:, PAD:]
            y = _even_mix(p, na_rpb[j], na_meta_bias[j], lb_all[j], hg_norm[j])
            w_out = ev_w_out[j]
        else:
            p = _matmul(hn, od_w_in[j]).reshape(b, LP, -1)[:, PAD:]
            y = _odd_mix(p, hy_short[j], hy_ffn_w1[j], hy_ffn_b1[j], hy_ffn_freq1[j], hy_ffn_w2[j],
                         hy_ffn_b2[j], hy_ffn_freq2[j], hy_ffn_w3[j], hy_bias[j])
            w_out = od_w_out[j]
        y = jnp.concatenate([pad_rows, y], axis=1).reshape(b * LP, D_MODEL).astype(jnp.bfloat16)
        out = _matmul(y, w_out)
        h = _residual_rmsnorm(h, out, norm_post[layer])
    return h.reshape(b, LP, D_MODEL)[:, PAD + N_META:].astype(x.dtype)
```

```python
import functools
import math

import jax
import jax.numpy as jnp
import numpy as np
from jax import lax
from jax.experimental import pallas as pl
from jax.experimental.pallas import tpu as pltpu

D_MODEL = 4096
SEQ = 4096
N_META = 16
GRID_W = 64
HEAD_DIM = 128
NA_WIDTH = D_MODEL // 2
NA_KH = 8
NA_KW = 16
HG_WIDTH = D_MODEL // 2
HG_HEADS = HG_WIDTH // HEAD_DIM
HG_CHUNK = 64
HY_WIDTH = D_MODEL
HY_ORDER = 2
HY_EMB = 33
HY_BANDS = (HY_EMB - 1) // 2
HY_FAST_DECAY = 0.3
HY_SLOW_DECAY = 1.5
HY_TARGET = 1e-2
EPS = 1e-6

PAD = HG_CHUNK - N_META
LP = PAD + N_META + SEQ
L_TOK = N_META + SEQ

V7X_VMEM_LIMIT = 56 * 1024 * 1024


def _rmsnorm_kernel(x_ref, g_ref, o_ref):
    x = x_ref[...]
    r = lax.rsqrt(jnp.mean(x * x, axis=-1, keepdims=True) + EPS)
    o_ref[...] = (x * r * g_ref[...]).astype(o_ref.dtype)


def _rmsnorm(x2d, g, out_dtype, tr=320):
    m, d = x2d.shape
    return pl.pallas_call(
        _rmsnorm_kernel,
        grid=(m // tr,),
        in_specs=[pl.BlockSpec((tr, d), lambda i: (i, 0)), pl.BlockSpec((1, d), lambda i: (0, 0))],
        out_specs=pl.BlockSpec((tr, d), lambda i: (i, 0)),
        out_shape=jax.ShapeDtypeStruct((m, d), out_dtype),
        compiler_params=pltpu.CompilerParams(vmem_limit_bytes=V7X_VMEM_LIMIT),
        name="rmsnorm_pre",
    )(x2d, g.reshape(1, d))


def _post_kernel(h_ref, y_ref, g_ref, o_ref):
    y = y_ref[...]
    r = lax.rsqrt(jnp.mean(y * y, axis=-1, keepdims=True) + EPS)
    o_ref[...] = h_ref[...] + y * r * g_ref[...]


def _residual_rmsnorm(h2d, y2d, g, tr=320):
    m, d = h2d.shape
    return pl.pallas_call(
        _post_kernel,
        grid=(m // tr,),
        in_specs=[pl.BlockSpec((tr, d), lambda i: (i, 0)), pl.BlockSpec((tr, d), lambda i: (i, 0)),
                  pl.BlockSpec((1, d), lambda i: (0, 0))],
        out_specs=pl.BlockSpec((tr, d), lambda i: (i, 0)),
        out_shape=jax.ShapeDtypeStruct((m, d), jnp.float32),
        compiler_params=pltpu.CompilerParams(vmem_limit_bytes=V7X_VMEM_LIMIT),
        name="residual_rmsnorm_post",
    )(h2d, y2d, g.reshape(1, d))


def _matmul_kernel(x_ref, w_ref, o_ref, acc_ref):
    k = pl.program_id(2)

    @pl.when(k == 0)
    def _():
        acc_ref[...] = jnp.zeros_like(acc_ref)

    acc_ref[...] += jnp.dot(x_ref[...].astype(jnp.bfloat16), w_ref[...].astype(jnp.bfloat16),
                            preferred_element_type=jnp.float32)

    @pl.when(k == pl.num_programs(2) - 1)
    def _():
        o_ref[...] = acc_ref[...].astype(o_ref.dtype)


def _matmul(x, w, tm=2080, tn=1024, tk=512, out_dtype=jnp.float32):
    m, kd = x.shape
    _, n = w.shape
    assert m % tm == 0 and n % tn == 0 and kd % tk == 0
    return pl.pallas_call(
        _matmul_kernel,
        grid=(m // tm, n // tn, kd // tk),
        in_specs=[pl.BlockSpec((tm, tk), lambda i, j, k: (i, k)), pl.BlockSpec((tk, tn), lambda i, j, k: (k, j))],
        out_specs=pl.BlockSpec((tm, tn), lambda i, j, k: (i, j)),
        out_shape=jax.ShapeDtypeStruct((m, n), out_dtype),
        scratch_shapes=[pltpu.VMEM((tm, tn), jnp.float32)],
        compiler_params=pltpu.CompilerParams(
            dimension_semantics=("parallel", "parallel", "arbitrary"), vmem_limit_bytes=V7X_VMEM_LIMIT),
        name="projection_matmul",
    )(x, w)


FFT_N1 = 128
FFT_N2 = 65
FFT_N = FFT_N1 * FFT_N2
N2P = 80
SLAB = 264
TC = 128
MID_GROUP = 4


def _dft_constants():
    a = np.arange(FFT_N1)
    c = np.arange(FFT_N1)
    w1 = np.exp(-2j * np.pi * np.outer(c, a) / FFT_N1)
    w1r, w1i = w1.real, w1.imag
    half = FFT_N1 // 2
    f1 = np.block([[w1r[:, :half], -w1i[:, :half]], [w1i[:, :half], w1r[:, :half]]])
    f1_real = np.concatenate([w1r, w1i], axis=0)
    g = np.conj(w1).T
    gr, gi = g.real[:half], g.imag[:half]
    i2 = np.block([[gr, -gi], [gi, gr]])
    b = np.arange(FFT_N2)
    d = np.arange(FFT_N2)
    w2 = np.exp(-2j * np.pi * np.outer(d, b) / FFT_N2)
    tw = np.exp(-2j * np.pi * np.outer(c, b) / FFT_N)
    m = w2[None, :, :] * tw[:, None, :]
    ab = np.zeros((FFT_N1, 2 * N2P, 128), np.float64)
    ab[:, :FFT_N2, :FFT_N2] = m.real
    ab[:, N2P:N2P + FFT_N2, :FFT_N2] = m.imag
    bf = jnp.bfloat16
    return (jnp.asarray(f1, bf), jnp.asarray(f1_real, bf), jnp.asarray(i2, bf), jnp.asarray(ab, bf))


def _middle_stage_inputs(y_ref, c):
    yr = y_ref[pl.ds(c, N2P, stride=SLAB), :]
    yi = y_ref[pl.ds(FFT_N1 + c, N2P, stride=SLAB), :]
    return jnp.concatenate([yr, yi], axis=1).astype(jnp.bfloat16)


def _forward_middle(ab, y2):
    y2 = jnp.concatenate([y2, jnp.zeros((128 - N2P, 2 * TC), jnp.bfloat16)], axis=0)
    q = jnp.dot(ab, y2, preferred_element_type=jnp.float32)
    zr = q[:N2P, :TC] - q[N2P:, TC:]
    zi = q[N2P:, :TC] + q[:N2P, TC:]
    return zr, zi


def _spectrum_kernel(h_ref, w3f_ref, w3b_ref, delta_ref, f1_ref, ab_ref, k_ref, kt_ref, y_ref):
    w3 = jnp.concatenate([w3f_ref[...], w3b_ref[...]], axis=0).astype(jnp.bfloat16)
    delta = delta_ref[...]
    rows = FFT_N // 13

    def fill(i, carry):
        r0 = pl.multiple_of(i * rows, 8)
        kt = jnp.dot(h_ref[pl.ds(r0, rows), :], w3, preferred_element_type=jnp.float32)
        j = lax.broadcasted_iota(jnp.int32, (rows, TC), 0) + r0
        lag = jnp.where(j < L_TOK, j, FFT_N - j).astype(jnp.float32)
        kt_ref[pl.ds(r0, rows), :] = kt * jnp.exp(-(lag * (1.0 / (L_TOK - 1))) * delta)
        return carry

    lax.fori_loop(0, 13, fill, 0)
    y_ref[pl.ds(FFT_N2 * SLAB, (N2P - FFT_N2) * SLAB), :] = jnp.zeros(((N2P - FFT_N2) * SLAB, TC), jnp.float32)

    def stage1(b, carry):
        x = kt_ref[pl.ds(b, FFT_N1, stride=FFT_N2), :].astype(jnp.bfloat16)
        y = jnp.dot(f1_ref[...], x, preferred_element_type=jnp.float32)
        y_ref[pl.ds(pl.multiple_of(b * SLAB, 8), 2 * FFT_N1), :] = y
        return carry

    lax.fori_loop(0, FFT_N2, stage1, 0, unroll=5)

    def stage2(c, carry):
        zr, zi = _forward_middle(ab_ref[c], _middle_stage_inputs(y_ref, c))
        k_ref[c] = (jnp.concatenate([zr, zi], axis=1) * (1.0 / FFT_N)).astype(jnp.bfloat16)
        return carry

    lax.fori_loop(0, FFT_N1, stage2, 0, unroll=4)


def _filter_spectrum(h2, w3, deltas, consts):
    _, f1_real, _, ab = consts
    nt = HY_WIDTH // TC
    return pl.pallas_call(
        _spectrum_kernel,
        grid=(HY_ORDER, nt),
        in_specs=[
            pl.BlockSpec((FFT_N, 128), lambda n, i: (0, 0)),
            pl.BlockSpec((64, TC), lambda n, i: (0, n * nt + i)),
            pl.BlockSpec((64, TC), lambda n, i: (0, (HY_ORDER + n) * nt + i)),
            pl.BlockSpec((1, TC), lambda n, i: (0, i)),
            pl.BlockSpec((2 * FFT_N1, FFT_N1), lambda n, i: (0, 0)),
            pl.BlockSpec((FFT_N1, 2 * N2P, 128), lambda n, i: (0, 0, 0)),
        ],
        out_specs=pl.BlockSpec((None, None, FFT_N1, N2P, 2 * TC), lambda n, i: (n, i, 0, 0, 0)),
        out_shape=jax.ShapeDtypeStruct((HY_ORDER, nt, FFT_N1, N2P, 2 * TC), jnp.bfloat16),
        scratch_shapes=[pltpu.VMEM((FFT_N, TC), jnp.float32), pltpu.VMEM((N2P * SLAB, TC), jnp.float32)],
        compiler_params=pltpu.CompilerParams(
            dimension_semantics=("parallel", "parallel"), vmem_limit_bytes=V7X_VMEM_LIMIT),
        name="hyena_filter_spectrum",
    )(h2, w3, w3, deltas, f1_real, ab)


def _conv_kernel(u_ref, k_ref, f1_ref, i2_ref, ab_ref, o_ref, y_ref):
    y_ref[pl.ds(FFT_N2 * SLAB, (N2P - FFT_N2) * SLAB), :] = jnp.zeros(((N2P - FFT_N2) * SLAB, TC), jnp.float32)
    half = FFT_N1 // 2

    def stage1(b, carry):
        x0 = u_ref.at[0][pl.ds(b, half, stride=FFT_N2), :]
        x1 = u_ref.at[1][pl.ds(b, half, stride=FFT_N2), :]
        x = jnp.concatenate([x0, x1], axis=0).astype(jnp.bfloat16)
        y = jnp.dot(f1_ref[...], x, preferred_element_type=jnp.float32)
        y_ref[pl.ds(pl.multiple_of(b * SLAB, 8), 2 * FFT_N1), :] = y
        return carry

    lax.fori_loop(0, FFT_N2, stage1, 0, unroll=5)

    def middle_one(c, y2):
        ab = ab_ref[c]
        zr, zi = _forward_middle(ab, y2)
        kk = k_ref[c].astype(jnp.float32)
        kr, ki = kk[:, :TC], kk[:, TC:]
        pr = zr * kr - zi * ki
        pi = zr * ki + zi * kr
        rhs = jnp.concatenate([jnp.concatenate([pr, pi], axis=1), jnp.concatenate([pi, -pr], axis=1)], axis=0)
        return lax.dot_general(ab, rhs.astype(jnp.bfloat16), (((0,), (0,)), ((), ())),
                               preferred_element_type=jnp.float32)

    def middle(g, carry):
        cs = [g * MID_GROUP + i for i in range(MID_GROUP)]
        ins = [_middle_stage_inputs(y_ref, c) for c in cs]
        outs = [middle_one(c, y2) for c, y2 in zip(cs, ins)]
        for c, v in zip(cs, outs):
            y_ref[pl.ds(c, N2P, stride=SLAB), :] = v[:N2P, :TC]
            y_ref[pl.ds(FFT_N1 + c, N2P, stride=SLAB), :] = v[:N2P, TC:]
        return carry

    lax.fori_loop(0, FFT_N1 // MID_GROUP, middle, 0)

    def stage4(b, carry):
        v = y_ref[pl.ds(pl.multiple_of(b * SLAB, 8), 2 * FFT_N1), :].astype(jnp.bfloat16)
        x = jnp.dot(i2_ref[...], v, preferred_element_type=jnp.float32)
        o_ref.at[0][pl.ds(b, half, stride=FFT_N2), :] = x[:half]
        o_ref.at[1][pl.ds(b, half, stride=FFT_N2), :] = x[half:]
        return carry

    lax.fori_loop(0, FFT_N2, stage4, 0, unroll=5)


def _long_conv(u, col0, kspec, order, consts):
    f1, _, i2, ab = consts
    bsz, lp, _ = u.shape
    assert bsz == 2 and lp == LP and col0 % TC == 0
    nt = HY_WIDTH // TC
    off = col0 // TC
    return pl.pallas_call(
        _conv_kernel,
        grid=(nt,),
        in_specs=[
            pl.BlockSpec((bsz, lp, TC), lambda i: (0, 0, off + i)),
            pl.BlockSpec((None, None, FFT_N1, N2P, 2 * TC), lambda i: (order, i, 0, 0, 0)),
            pl.BlockSpec((2 * FFT_N1, FFT_N1), lambda i: (0, 0)),
            pl.BlockSpec((FFT_N1, 2 * FFT_N1), lambda i: (0, 0)),
            pl.BlockSpec((FFT_N1, 2 * N2P, 128), lambda i: (0, 0, 0)),
        ],
        out_specs=pl.BlockSpec((bsz, lp, TC), lambda i: (0, 0, i)),
        out_shape=jax.ShapeDtypeStruct((bsz, lp, HY_WIDTH), jnp.float32),
        scratch_shapes=[pltpu.VMEM((N2P * SLAB, TC), jnp.float32)],
        compiler_params=pltpu.CompilerParams(dimension_semantics=("parallel",), vmem_limit_bytes=V7X_VMEM_LIMIT),
        name="hyena_long_conv",
    )(u, kspec, f1, i2, ab)


def _short_conv_kernel(u_ref, w_ref, o_ref):
    u = u_ref[...]
    w = w_ref[...]
    y = pltpu.roll(u, 1, 0) * w[0:1] + u * w[1:2] + pltpu.roll(u, LP - 1, 0) * w[2:3]
    row = lax.broadcasted_iota(jnp.int32, y.shape, 0)
    o_ref[...] = jnp.where(row >= PAD, y, 0.0)


def _short_conv(p3, w_short):
    bsz = p3.shape[0]
    width = 3 * HY_WIDTH
    return pl.pallas_call(
        _short_conv_kernel,
        grid=(bsz, width // TC),
        in_specs=[pl.BlockSpec((None, LP, TC), lambda b, i: (b, 0, i)), pl.BlockSpec((3, TC), lambda b, i: (0, i))],
        out_specs=pl.BlockSpec((None, LP, TC), lambda b, i: (b, 0, i)),
        out_shape=jax.ShapeDtypeStruct((bsz, LP, width), jnp.float32),
        compiler_params=pltpu.CompilerParams(
            dimension_semantics=("parallel", "parallel"), vmem_limit_bytes=V7X_VMEM_LIMIT),
        name="hyena_short_conv",
    )(p3, w_short.T)


GATE_ROWS = 520
GATE_COLS = 512


def _gate1_kernel(v_ref, x_ref, y_ref, b_ref, o_ref):
    o_ref[...] = x_ref[...] * (y_ref[...] + v_ref[...] * b_ref[...])


def _gate2_kernel(z_ref, x_ref, y_ref, b_ref, g_ref, o_ref):
    g = g_ref[...]
    z = x_ref[...] * (y_ref[...] + z_ref[...] * b_ref[...])
    o_ref[...] = (z * (g * jax.nn.sigmoid(g))).astype(o_ref.dtype)


def _hyena_gate1(uc, y, bias):
    bsz = uc.shape[0]
    nc = HY_WIDTH // GATE_COLS
    blk = lambda off: pl.BlockSpec((None, GATE_ROWS, GATE_COLS), lambda b, r, c: (b, r, off + c))
    return pl.pallas_call(
        _gate1_kernel,
        grid=(bsz, LP // GATE_ROWS, nc),
        in_specs=[blk(0), blk(nc), blk(0), pl.BlockSpec((1, GATE_COLS), lambda b, r, c: (0, c))],
        out_specs=blk(0),
        out_shape=jax.ShapeDtypeStruct((bsz, LP, HY_WIDTH), jnp.float32),
        compiler_params=pltpu.CompilerParams(
            dimension_semantics=("parallel", "parallel", "parallel"), vmem_limit_bytes=V7X_VMEM_LIMIT),
        name="hyena_gate1",
    )(uc, uc, y, bias.reshape(1, HY_WIDTH))


def _hyena_gate2(z1, uc, y, bias, p3):
    bsz = uc.shape[0]
    nc = HY_WIDTH // GATE_COLS
    blk = lambda off: pl.BlockSpec((None, GATE_ROWS, GATE_COLS), lambda b, r, c: (b, r, off + c))
    return pl.pallas_call(
        _gate2_kernel,
        grid=(bsz, LP // GATE_ROWS, nc),
        in_specs=[blk(0), blk(2 * nc), blk(0), pl.BlockSpec((1, GATE_COLS), lambda b, r, c: (0, c)), blk(3 * nc)],
        out_specs=blk(0),
        out_shape=jax.ShapeDtypeStruct((bsz, LP, HY_WIDTH), jnp.bfloat16),
        compiler_params=pltpu.CompilerParams(
            dimension_semantics=("parallel", "parallel", "parallel"), vmem_limit_bytes=V7X_VMEM_LIMIT),
        name="hyena_gate2",
    )(z1, uc, y, bias.reshape(1, HY_WIDTH), p3)


def _hyena_hidden(w1, b1, fr1, w2, b2, fr2):
    f32 = jnp.float32
    l = L_TOK
    hi = lax.Precision.HIGHEST
    t = jnp.linspace(0.0, 1.0, l, dtype=f32)[:, None]
    w = 2.0 * math.pi * jnp.arange(l, dtype=f32)[:, None] / l
    bands = jnp.linspace(1e-4, HY_BANDS - 1, HY_BANDS, dtype=f32)[None, :]
    z = jnp.concatenate([t, jnp.cos(bands * w), -jnp.sin(bands * w)], axis=-1)
    hid = jnp.sin(fr1 * (jnp.dot(z, w1, precision=hi) + b1))
    hid = jnp.sin(fr2 * (jnp.dot(hid, w2, precision=hi) + b2))
    j = np.arange(FFT_N)
    lag = np.minimum(np.where(j < l, j, FFT_N - j), l - 1)
    fwd = (j < l).astype(np.float32)[:, None]
    bwd = ((j > FFT_N - l) | (j == 0)).astype(np.float32)[:, None]
    rows = hid[lag]
    return jnp.concatenate([rows * fwd, rows * bwd], axis=1).astype(jnp.bfloat16)


def _odd_mix(p3, w_short, w1, b1, fr1, w2, b2, fr2, w3, bias, consts):
    min_decay = math.log(HY_TARGET) / HY_SLOW_DECAY
    max_decay = math.log(HY_TARGET) / HY_FAST_DECAY
    deltas = jnp.abs(jnp.linspace(min_decay, max_decay, HY_WIDTH, dtype=jnp.float32)).reshape(1, HY_WIDTH)
    kspec = _filter_spectrum(_hyena_hidden(w1, b1, fr1, w2, b2, fr2), w3, deltas, consts)
    uc = _short_conv(p3, w_short)
    y1 = _long_conv(uc, 0, kspec, 0, consts)
    z1 = _hyena_gate1(uc, y1, bias[0])
    y2 = _long_conv(z1, 0, kspec, 1, consts)
    return _hyena_gate2(z1, uc, y2, bias[1], p3)


def _split_heads(a):
    b, l, w = a.shape
    return a.reshape(b, l, w // HEAD_DIM, HEAD_DIM).transpose(0, 2, 1, 3)


def _merge_heads(a):
    b, h, l, d = a.shape
    return a.transpose(0, 2, 1, 3).reshape(b, l, h * d)


def _natten(q, k, v, rpb, meta_bias):
    b, h, l, dh = q.shape
    t = l - N_META
    rows = t // GRID_W
    kh = min(NA_KH, rows)
    scale = dh ** -0.5
    f32 = jnp.float32
    qm, km, vm = q[:, :, :N_META], k[:, :, :N_META], v[:, :, :N_META]
    qg = q[:, :, N_META:].reshape(b, h, rows, GRID_W, dh)
    kg = k[:, :, N_META:].reshape(b, h, rows, GRID_W, dh)
    vg = v[:, :, N_META:].reshape(b, h, rows, GRID_W, dh)
    mbias = meta_bias.astype(f32)[None, :, None, :]
    s_mm = jnp.einsum('bhqd,bhkd->bhqk', qm, km).astype(f32) * scale + mbias
    o_meta = jnp.einsum('bhqk,bhkd->bhqd', jax.nn.softmax(s_mm, axis=-1).astype(v.dtype), vm)
    cols = np.arange(GRID_W)
    c0 = np.clip(cols - NA_KW // 2, 0, GRID_W - NA_KW)
    col_ok = (cols[None, :] >= c0[:, None]) & (cols[None, :] < c0[:, None] + NA_KW)
    col_idx = np.clip(cols[None, :] - cols[:, None] + NA_KW - 1, 0, 2 * NA_KW - 2)
    rpb_cols = rpb.astype(f32)[:, :, col_idx]

    def row_block(r):
        r0 = jnp.clip(r - kh // 2, 0, rows - kh)
        kb = lax.dynamic_slice_in_dim(kg, r0, kh, axis=2)
        vb = lax.dynamic_slice_in_dim(vg, r0, kh, axis=2)
        qr = lax.dynamic_index_in_dim(qg, r, axis=2, keepdims=False)
        s_win = jnp.einsum('bhqd,bhikd->bhqik', qr, kb).astype(f32) * scale
        row_off = r0 + jnp.arange(kh) - r + NA_KH - 1
        bias = jnp.take(rpb_cols, row_off, axis=1).transpose(0, 2, 1, 3)
        s_win = jnp.where(col_ok[None, None, :, None, :], s_win + bias[None], -jnp.inf)
        s_meta = jnp.einsum('bhqd,bhmd->bhqm', qr, km).astype(f32) * scale + mbias
        s = jnp.concatenate([s_meta, s_win.reshape(b, h, GRID_W, kh * GRID_W)], axis=-1)
        p = jax.nn.softmax(s, axis=-1).astype(v.dtype)
        p_win = p[..., N_META:].reshape(b, h, GRID_W, kh, GRID_W)
        return (jnp.einsum('bhqm,bhmd->bhqd', p[..., :N_META], vm)
                + jnp.einsum('bhqik,bhikd->bhqd', p_win, vb))

    o_grid = lax.map(row_block, jnp.arange(rows))
    o_grid = o_grid.transpose(1, 2, 0, 3, 4).reshape(b, h, t, dh)
    return jnp.concatenate([o_meta, o_grid], axis=2)


def _hgrn2(q, i, f_fwd, f_bwd, lb, norm_gain):
    f32 = jnp.float32
    b, l, _ = q.shape
    pad = HG_CHUNK - N_META
    q = jax.nn.silu(q.astype(f32)) * HEAD_DIM ** -0.5
    v = i.astype(f32)

    def gates(f_raw, lb_d):
        log_f = jnp.logaddexp(jnp.log(lb_d), jnp.log1p(-lb_d) + jax.nn.log_sigmoid(f_raw))
        key = (1.0 - lb_d) * jax.nn.sigmoid(-f_raw)
        return log_f, key

    g_f, k_f = gates(f_fwd, lb[0])
    g_b, k_b = gates(f_bwd, lb[1])

    def both_dirs(a_fwd, a_bwd):
        pa = lambda a: jnp.pad(a, ((0, 0), (pad, 0), (0, 0)))
        s = jnp.stack([pa(a_fwd), jnp.flip(pa(a_bwd), axis=1)])
        n_chunks = s.shape[2] // HG_CHUNK
        s = s.reshape(2, b, n_chunks, HG_CHUNK, HG_HEADS, HEAD_DIM)
        return s.transpose(2, 0, 1, 4, 3, 5)

    qs, vs = both_dirs(q, q), both_dirs(v, v)
    ks, gs = both_dirs(k_f, k_b), both_dirs(g_f, g_b)
    mask = np.tril(np.ones((HG_CHUNK, HG_CHUNK), dtype=bool))

    def chunk_step(state, inp):
        qc, kc, vc, gc = inp
        gcum = jnp.cumsum(gc, axis=-2)
        diff = gcum[..., :, None, :] - gcum[..., None, :, :]
        decay = jnp.exp(jnp.where(mask[:, :, None], diff, -jnp.inf))
        attn = jnp.einsum('zbhtd,zbhsd,zbhtsd->zbhts', qc, kc, decay)
        o = (jnp.einsum('zbhts,zbhsv->zbhtv', attn, vc)
             + jnp.einsum('zbhtd,zbhdv->zbhtv', qc * jnp.exp(gcum), state))
        g_end = gcum[..., -1:, :]
        state = (jnp.exp(g_end)[..., 0, :, None] * state
                 + jnp.einsum('zbhsd,zbhsv->zbhdv', kc * jnp.exp(g_end - gcum), vc))
        return state, o

    s0 = jnp.zeros((2, b, HG_HEADS, HEAD_DIM, HEAD_DIM), f32)
    _, o = lax.scan(chunk_step, s0, (qs, ks, vs, gs))
    o = o.transpose(1, 2, 0, 4, 3, 5).reshape(2, b, -1, HG_HEADS, HEAD_DIM)
    o = (o[0] + jnp.flip(o[1], axis=1))[:, pad:]
    o = o * lax.rsqrt(jnp.mean(o * o, axis=-1, keepdims=True) + EPS)
    o = o * norm_gain.astype(f32).reshape(HG_HEADS, HEAD_DIM)
    return o.reshape(b, l, HG_WIDTH)


def _even_mix(p, rpb, meta_bias, lb, norm_gain):
    nw, hw = NA_WIDTH, HG_WIDTH
    qa, ka, va = p[..., :nw], p[..., nw:2 * nw], p[..., 2 * nw:3 * nw]
    o = 3 * nw
    qb, ib, f_fwd, f_bwd = (p[..., o + j * hw:o + (j + 1) * hw] for j in range(4))
    gate = p[..., o + 4 * hw:]
    oa = _merge_heads(_natten(_split_heads(qa), _split_heads(ka), _split_heads(va), rpb, meta_bias))
    ob = _hgrn2(qb, ib, f_fwd, f_bwd, lb, norm_gain)
    return jnp.concatenate([oa, ob], axis=-1) * jax.nn.silu(gate)


def kernel(x, meta_tokens, norm_pre, norm_post, ev_w_in, ev_w_out, na_rpb, na_meta_bias, hg_lower, hg_norm,
           od_w_in, od_w_out, hy_short, hy_ffn_w1, hy_ffn_b1, hy_ffn_freq1, hy_ffn_w2, hy_ffn_b2, hy_ffn_freq2,
           hy_ffn_w3, hy_bias):
    b = x.shape[0]
    depth = norm_pre.shape[0]
    f32 = jnp.float32
    lb_all = jnp.cumsum(jax.nn.softmax(hg_lower.astype(f32), axis=0), axis=0)
    lb_all = lb_all - lb_all[:1]
    meta = jnp.broadcast_to(meta_tokens.astype(f32)[None], (b, N_META, D_MODEL))
    h = jnp.concatenate([jnp.zeros((b, PAD, D_MODEL), f32), meta, x.astype(f32)], axis=1)
    h = h.reshape(b * LP, D_MODEL)
    pad_rows = jnp.zeros((b, PAD, D_MODEL), f32)
    consts = _dft_constants()
    for layer in range(depth):
        j = layer // 2
        hn = _rmsnorm(h, norm_pre[layer], jnp.bfloat16)
        if layer % 2 == 0:
            p = _matmul(hn, ev_w_in[j]).reshape(b, LP, -1)[:, PAD:]
            y = _even_mix(p, na_rpb[j], na_meta_bias[j], lb_all[j], hg_norm[j])
            y = jnp.concatenate([pad_rows, y], axis=1).reshape(b * LP, D_MODEL).astype(jnp.bfloat16)
            w_out = ev_w_out[j]
        else:
            p3 = _matmul(hn, od_w_in[j]).reshape(b, LP, -1)
            y = _odd_mix(p3, hy_short[j], hy_ffn_w1[j], hy_ffn_b1[j], hy_ffn_freq1[j], hy_ffn_w2[j],
                         hy_ffn_b2[j], hy_ffn_freq2[j], hy_ffn_w3[j], hy_bias[j], consts)
            y = y.reshape(b * LP, D_MODEL)
            w_out = od_w_out[j]
        out = _matmul(y, w_out)
        h = _residual_rmsnorm(h, out, norm_post[layer])
    return h.reshape(b, LP, D_MODEL)[:, PAD + N_META:].astype(x.dtype)
```

```python
import functools
import math

import jax
import jax.numpy as jnp
import numpy as np
from jax import lax
from jax.experimental import pallas as pl
from jax.experimental.pallas import tpu as pltpu

D_MODEL = 4096
SEQ = 4096
N_META = 16
GRID_W = 64
HEAD_DIM = 128
NA_WIDTH = D_MODEL // 2
NA_KH = 8
NA_KW = 16
HG_WIDTH = D_MODEL // 2
HG_HEADS = HG_WIDTH // HEAD_DIM
HG_CHUNK = 64
HY_WIDTH = D_MODEL
HY_ORDER = 2
HY_EMB = 33
HY_BANDS = (HY_EMB - 1) // 2
HY_FAST_DECAY = 0.3
HY_SLOW_DECAY = 1.5
HY_TARGET = 1e-2
EPS = 1e-6

PAD = HG_CHUNK - N_META
LP = PAD + N_META + SEQ
L_TOK = N_META + SEQ

V7X_VMEM_LIMIT = 56 * 1024 * 1024


def _rmsnorm_kernel(x_ref, g_ref, o_ref):
    x = x_ref[...]
    r = lax.rsqrt(jnp.mean(x * x, axis=-1, keepdims=True) + EPS)
    o_ref[...] = (x * r * g_ref[...]).astype(o_ref.dtype)


def _rmsnorm(x2d, g, out_dtype, tr=320):
    m, d = x2d.shape
    return pl.pallas_call(
        _rmsnorm_kernel,
        grid=(m // tr,),
        in_specs=[pl.BlockSpec((tr, d), lambda i: (i, 0)), pl.BlockSpec((1, d), lambda i: (0, 0))],
        out_specs=pl.BlockSpec((tr, d), lambda i: (i, 0)),
        out_shape=jax.ShapeDtypeStruct((m, d), out_dtype),
        compiler_params=pltpu.CompilerParams(vmem_limit_bytes=V7X_VMEM_LIMIT),
        name="rmsnorm_pre",
    )(x2d, g.reshape(1, d))


def _post_kernel(h_ref, y_ref, g_ref, o_ref):
    y = y_ref[...]
    r = lax.rsqrt(jnp.mean(y * y, axis=-1, keepdims=True) + EPS)
    o_ref[...] = h_ref[...] + y * r * g_ref[...]


def _residual_rmsnorm(h2d, y2d, g, tr=320):
    m, d = h2d.shape
    return pl.pallas_call(
        _post_kernel,
        grid=(m // tr,),
        in_specs=[pl.BlockSpec((tr, d), lambda i: (i, 0)), pl.BlockSpec((tr, d), lambda i: (i, 0)),
                  pl.BlockSpec((1, d), lambda i: (0, 0))],
        out_specs=pl.BlockSpec((tr, d), lambda i: (i, 0)),
        out_shape=jax.ShapeDtypeStruct((m, d), jnp.float32),
        compiler_params=pltpu.CompilerParams(vmem_limit_bytes=V7X_VMEM_LIMIT),
        name="residual_rmsnorm_post",
    )(h2d, y2d, g.reshape(1, d))


def _matmul_kernel(x_ref, w_ref, o_ref, acc_ref):
    k = pl.program_id(2)

    @pl.when(k == 0)
    def _():
        acc_ref[...] = jnp.zeros_like(acc_ref)

    acc_ref[...] += jnp.dot(x_ref[...].astype(jnp.bfloat16), w_ref[...].astype(jnp.bfloat16),
                            preferred_element_type=jnp.float32)

    @pl.when(k == pl.num_programs(2) - 1)
    def _():
        o_ref[...] = acc_ref[...].astype(o_ref.dtype)


def _matmul(x, w, tm=2080, tn=1024, tk=512, out_dtype=jnp.float32):
    m, kd = x.shape
    _, n = w.shape
    assert m % tm == 0 and n % tn == 0 and kd % tk == 0
    return pl.pallas_call(
        _matmul_kernel,
        grid=(m // tm, n // tn, kd // tk),
        in_specs=[pl.BlockSpec((tm, tk), lambda i, j, k: (i, k)), pl.BlockSpec((tk, tn), lambda i, j, k: (k, j))],
        out_specs=pl.BlockSpec((tm, tn), lambda i, j, k: (i, j)),
        out_shape=jax.ShapeDtypeStruct((m, n), out_dtype),
        scratch_shapes=[pltpu.VMEM((tm, tn), jnp.float32)],
        compiler_params=pltpu.CompilerParams(
            dimension_semantics=("parallel", "parallel", "arbitrary"), vmem_limit_bytes=V7X_VMEM_LIMIT),
        name="projection_matmul",
    )(x, w)


FFT_N1 = 128
FFT_N2 = 65
FFT_N = FFT_N1 * FFT_N2
N2P = 80
SLAB = 264
TC = 128
MID_GROUP = 4


def _dft_constants():
    a = np.arange(FFT_N1)
    c = np.arange(FFT_N1)
    w1 = np.exp(-2j * np.pi * np.outer(c, a) / FFT_N1)
    w1r, w1i = w1.real, w1.imag
    half = FFT_N1 // 2
    f1 = np.block([[w1r[:, :half], -w1i[:, :half]], [w1i[:, :half], w1r[:, :half]]])
    f1_real = np.concatenate([w1r, w1i], axis=0)
    g = np.conj(w1).T
    gr, gi = g.real[:half], g.imag[:half]
    i2 = np.block([[gr, -gi], [gi, gr]])
    b = np.arange(FFT_N2)
    d = np.arange(FFT_N2)
    w2 = np.exp(-2j * np.pi * np.outer(d, b) / FFT_N2)
    tw = np.exp(-2j * np.pi * np.outer(c, b) / FFT_N)
    m = w2[None, :, :] * tw[:, None, :]
    ab = np.zeros((FFT_N1, 2 * N2P, 128), np.float64)
    ab[:, :FFT_N2, :FFT_N2] = m.real
    ab[:, N2P:N2P + FFT_N2, :FFT_N2] = m.imag
    bf = jnp.bfloat16
    return (jnp.asarray(f1, bf), jnp.asarray(f1_real, bf), jnp.asarray(i2, bf), jnp.asarray(ab, bf))


def _middle_stage_inputs(y_ref, c):
    yr = y_ref[pl.ds(c, N2P, stride=SLAB), :]
    yi = y_ref[pl.ds(FFT_N1 + c, N2P, stride=SLAB), :]
    return jnp.concatenate([yr, yi], axis=1).astype(jnp.bfloat16)


def _forward_middle(ab, y2):
    y2 = jnp.concatenate([y2, jnp.zeros((128 - N2P, 2 * TC), jnp.bfloat16)], axis=0)
    q = jnp.dot(ab, y2, preferred_element_type=jnp.float32)
    zr = q[:N2P, :TC] - q[N2P:, TC:]
    zi = q[N2P:, :TC] + q[:N2P, TC:]
    return zr, zi


def _spectrum_kernel(h_ref, w3f_ref, w3b_ref, delta_ref, f1_ref, ab_ref, k_ref, kt_ref, y_ref):
    w3 = jnp.concatenate([w3f_ref[...], w3b_ref[...]], axis=0).astype(jnp.bfloat16)
    delta = delta_ref[...]
    rows = FFT_N // 13

    def fill(i, carry):
        r0 = pl.multiple_of(i * rows, 8)
        kt = jnp.dot(h_ref[pl.ds(r0, rows), :], w3, preferred_element_type=jnp.float32)
        j = lax.broadcasted_iota(jnp.int32, (rows, TC), 0) + r0
        lag = jnp.where(j < L_TOK, j, FFT_N - j).astype(jnp.float32)
        kt_ref[pl.ds(r0, rows), :] = kt * jnp.exp(-(lag * (1.0 / (L_TOK - 1))) * delta)
        return carry

    lax.fori_loop(0, 13, fill, 0)
    y_ref[pl.ds(FFT_N2 * SLAB, (N2P - FFT_N2) * SLAB), :] = jnp.zeros(((N2P - FFT_N2) * SLAB, TC), jnp.float32)

    def stage1(b, carry):
        x = kt_ref[pl.ds(b, FFT_N1, stride=FFT_N2), :].astype(jnp.bfloat16)
        y = jnp.dot(f1_ref[...], x, preferred_element_type=jnp.float32)
        y_ref[pl.ds(pl.multiple_of(b * SLAB, 8), 2 * FFT_N1), :] = y
        return carry

    lax.fori_loop(0, FFT_N2, stage1, 0, unroll=5)

    def stage2(c, carry):
        zr, zi = _forward_middle(ab_ref[c], _middle_stage_inputs(y_ref, c))
        k_ref[c] = (jnp.concatenate([zr, zi], axis=1) * (1.0 / FFT_N)).astype(jnp.bfloat16)
        return carry

    lax.fori_loop(0, FFT_N1, stage2, 0, unroll=4)


def _filter_spectrum(h2, w3, deltas, consts):
    _, f1_real, _, ab = consts
    nt = HY_WIDTH // TC
    return pl.pallas_call(
        _spectrum_kernel,
        grid=(HY_ORDER, nt),
        in_specs=[
            pl.BlockSpec((FFT_N, 128), lambda n, i: (0, 0)),
            pl.BlockSpec((64, TC), lambda n, i: (0, n * nt + i)),
            pl.BlockSpec((64, TC), lambda n, i: (0, (HY_ORDER + n) * nt + i)),
            pl.BlockSpec((1, TC), lambda n, i: (0, i)),
            pl.BlockSpec((2 * FFT_N1, FFT_N1), lambda n, i: (0, 0)),
            pl.BlockSpec((FFT_N1, 2 * N2P, 128), lambda n, i: (0, 0, 0)),
        ],
        out_specs=pl.BlockSpec((None, None, FFT_N1, N2P, 2 * TC), lambda n, i: (n, i, 0, 0, 0)),
        out_shape=jax.ShapeDtypeStruct((HY_ORDER, nt, FFT_N1, N2P, 2 * TC), jnp.bfloat16),
        scratch_shapes=[pltpu.VMEM((FFT_N, TC), jnp.float32), pltpu.VMEM((N2P * SLAB, TC), jnp.float32)],
        compiler_params=pltpu.CompilerParams(
            dimension_semantics=("parallel", "parallel"), vmem_limit_bytes=V7X_VMEM_LIMIT),
        name="hyena_filter_spectrum",
    )(h2, w3, w3, deltas, f1_real, ab)


def _conv_kernel(u_ref, k_ref, f1_ref, i2_ref, ab_ref, o_ref, y_ref):
    y_ref[pl.ds(FFT_N2 * SLAB, (N2P - FFT_N2) * SLAB), :] = jnp.zeros(((N2P - FFT_N2) * SLAB, TC), jnp.float32)
    half = FFT_N1 // 2

    def stage1(b, carry):
        x0 = u_ref.at[0][pl.ds(b, half, stride=FFT_N2), :]
        x1 = u_ref.at[1][pl.ds(b, half, stride=FFT_N2), :]
        x = jnp.concatenate([x0, x1], axis=0).astype(jnp.bfloat16)
        y = jnp.dot(f1_ref[...], x, preferred_element_type=jnp.float32)
        y_ref[pl.ds(pl.multiple_of(b * SLAB, 8), 2 * FFT_N1), :] = y
        return carry

    lax.fori_loop(0, FFT_N2, stage1, 0, unroll=5)

    def middle_one(c, y2):
        ab = ab_ref[c]
        zr, zi = _forward_middle(ab, y2)
        kk = k_ref[c].astype(jnp.float32)
        kr, ki = kk[:, :TC], kk[:, TC:]
        pr = zr * kr - zi * ki
        pi = zr * ki + zi * kr
        rhs = jnp.concatenate([jnp.concatenate([pr, pi], axis=1), jnp.concatenate([pi, -pr], axis=1)], axis=0)
        return lax.dot_general(ab, rhs.astype(jnp.bfloat16), (((0,), (0,)), ((), ())),
                               preferred_element_type=jnp.float32)

    def middle(g, carry):
        cs = [g * MID_GROUP + i for i in range(MID_GROUP)]
        ins = [_middle_stage_inputs(y_ref, c) for c in cs]
        outs = [middle_one(c, y2) for c, y2 in zip(cs, ins)]
        for c, v in zip(cs, outs):
            y_ref[pl.ds(c, N2P, stride=SLAB), :] = v[:N2P, :TC]
            y_ref[pl.ds(FFT_N1 + c, N2P, stride=SLAB), :] = v[:N2P, TC:]
        return carry

    lax.fori_loop(0, FFT_N1 // MID_GROUP, middle, 0)

    def stage4(b, carry):
        v = y_ref[pl.ds(pl.multiple_of(b * SLAB, 8), 2 * FFT_N1), :].astype(jnp.bfloat16)
        x = jnp.dot(i2_ref[...], v, preferred_element_type=jnp.float32)
        o_ref.at[0][pl.ds(b, half, stride=FFT_N2), :] = x[:half]
        o_ref.at[1][pl.ds(b, half, stride=FFT_N2), :] = x[half:]
        return carry

    lax.fori_loop(0, FFT_N2, stage4, 0, unroll=5)


def _long_conv(u, col0, kspec, order, consts):
    f1, _, i2, ab = consts
    bsz, lp, _ = u.shape
    assert bsz == 2 and lp == LP and col0 % TC == 0
    nt = HY_WIDTH // TC
    off = col0 // TC
    return pl.pallas_call(
        _conv_kernel,
        grid=(nt,),
        in_specs=[
            pl.BlockSpec((bsz, lp, TC), lambda i: (0, 0, off + i)),
            pl.BlockSpec((None, None, FFT_N1, N2P, 2 * TC), lambda i: (order, i, 0, 0, 0)),
            pl.BlockSpec((2 * FFT_N1, FFT_N1), lambda i: (0, 0)),
            pl.BlockSpec((FFT_N1, 2 * FFT_N1), lambda i: (0, 0)),
            pl.BlockSpec((FFT_N1, 2 * N2P, 128), lambda i: (0, 0, 0)),
        ],
        out_specs=pl.BlockSpec((bsz, lp, TC), lambda i: (0, 0, i)),
        out_shape=jax.ShapeDtypeStruct((bsz, lp, HY_WIDTH), jnp.float32),
        scratch_shapes=[pltpu.VMEM((N2P * SLAB, TC), jnp.float32)],
        compiler_params=pltpu.CompilerParams(dimension_semantics=("parallel",), vmem_limit_bytes=V7X_VMEM_LIMIT),
        name="hyena_long_conv",
    )(u, kspec, f1, i2, ab)


def _short_conv_kernel(u_ref, w_ref, o_ref):
    u = u_ref[...]
    w = w_ref[...]
    y = pltpu.roll(u, 1, 0) * w[0:1] + u * w[1:2] + pltpu.roll(u, LP - 1, 0) * w[2:3]
    row = lax.broadcasted_iota(jnp.int32, y.shape, 0)
    o_ref[...] = jnp.where(row >= PAD, y, 0.0)


def _short_conv(p3, w_short):
    bsz = p3.shape[0]
    width = 3 * HY_WIDTH
    return pl.pallas_call(
        _short_conv_kernel,
        grid=(bsz, width // TC),
        in_specs=[pl.BlockSpec((None, LP, TC), lambda b, i: (b, 0, i)), pl.BlockSpec((3, TC), lambda b, i: (0, i))],
        out_specs=pl.BlockSpec((None, LP, TC), lambda b, i: (b, 0, i)),
        out_shape=jax.ShapeDtypeStruct((bsz, LP, width), jnp.float32),
        compiler_params=pltpu.CompilerParams(
            dimension_semantics=("parallel", "parallel"), vmem_limit_bytes=V7X_VMEM_LIMIT),
        name="hyena_short_conv",
    )(p3, w_short.T)


GATE_ROWS = 520
GATE_COLS = 512


def _gate1_kernel(v_ref, x_ref, y_ref, b_ref, o_ref):
    o_ref[...] = x_ref[...] * (y_ref[...] + v_ref[...] * b_ref[...])


def _gate2_kernel(z_ref, x_ref, y_ref, b_ref, g_ref, o_ref):
    g = g_ref[...]
    z = x_ref[...] * (y_ref[...] + z_ref[...] * b_ref[...])
    o_ref[...] = (z * (g * jax.nn.sigmoid(g))).astype(o_ref.dtype)


def _hyena_gate1(uc, y, bias):
    bsz = uc.shape[0]
    nc = HY_WIDTH // GATE_COLS
    blk = lambda off: pl.BlockSpec((None, GATE_ROWS, GATE_COLS), lambda b, r, c: (b, r, off + c))
    return pl.pallas_call(
        _gate1_kernel,
        grid=(bsz, LP // GATE_ROWS, nc),
        in_specs=[blk(0), blk(nc), blk(0), pl.BlockSpec((1, GATE_COLS), lambda b, r, c: (0, c))],
        out_specs=blk(0),
        out_shape=jax.ShapeDtypeStruct((bsz, LP, HY_WIDTH), jnp.float32),
        compiler_params=pltpu.CompilerParams(
            dimension_semantics=("parallel", "parallel", "parallel"), vmem_limit_bytes=V7X_VMEM_LIMIT),
        name="hyena_gate1",
    )(uc, uc, y, bias.reshape(1, HY_WIDTH))


def _hyena_gate2(z1, uc, y, bias, p3):
    bsz = uc.shape[0]
    nc = HY_WIDTH // GATE_COLS
    blk = lambda off: pl.BlockSpec((None, GATE_ROWS, GATE_COLS), lambda b, r, c: (b, r, off + c))
    return pl.pallas_call(
        _gate2_kernel,
        grid=(bsz, LP // GATE_ROWS, nc),
        in_specs=[blk(0), blk(2 * nc), blk(0), pl.BlockSpec((1, GATE_COLS), lambda b, r, c: (0, c)), blk(3 * nc)],
        out_specs=blk(0),
        out_shape=jax.ShapeDtypeStruct((bsz, LP, HY_WIDTH), jnp.bfloat16),
        compiler_params=pltpu.CompilerParams(
            dimension_semantics=("parallel", "parallel", "parallel"), vmem_limit_bytes=V7X_VMEM_LIMIT),
        name="hyena_gate2",
    )(z1, uc, y, bias.reshape(1, HY_WIDTH), p3)


def _hyena_hidden(w1, b1, fr1, w2, b2, fr2):
    f32 = jnp.float32
    l = L_TOK
    hi = lax.Precision.HIGHEST
    t = jnp.linspace(0.0, 1.0, l, dtype=f32)[:, None]
    w = 2.0 * math.pi * jnp.arange(l, dtype=f32)[:, None] / l
    bands = jnp.linspace(1e-4, HY_BANDS - 1, HY_BANDS, dtype=f32)[None, :]
    z = jnp.concatenate([t, jnp.cos(bands * w), -jnp.sin(bands * w)], axis=-1)
    hid = jnp.sin(fr1 * (jnp.dot(z, w1, precision=hi) + b1))
    hid = jnp.sin(fr2 * (jnp.dot(hid, w2, precision=hi) + b2))
    j = np.arange(FFT_N)
    lag = np.minimum(np.where(j < l, j, FFT_N - j), l - 1)
    fwd = (j < l).astype(np.float32)[:, None]
    bwd = ((j > FFT_N - l) | (j == 0)).astype(np.float32)[:, None]
    rows = hid[lag]
    return jnp.concatenate([rows * fwd, rows * bwd], axis=1).astype(jnp.bfloat16)


def _odd_mix(p3, w_short, w1, b1, fr1, w2, b2, fr2, w3, bias, consts):
    min_decay = math.log(HY_TARGET) / HY_SLOW_DECAY
    max_decay = math.log(HY_TARGET) / HY_FAST_DECAY
    deltas = jnp.abs(jnp.linspace(min_decay, max_decay, HY_WIDTH, dtype=jnp.float32)).reshape(1, HY_WIDTH)
    kspec = _filter_spectrum(_hyena_hidden(w1, b1, fr1, w2, b2, fr2), w3, deltas, consts)
    uc = _short_conv(p3, w_short)
    y1 = _long_conv(uc, 0, kspec, 0, consts)
    z1 = _hyena_gate1(uc, y1, bias[0])
    y2 = _long_conv(z1, 0, kspec, 1, consts)
    return _hyena_gate2(z1, uc, y2, bias[1], p3)


HG_HB = 2
HG_LEVELS = (32, 16, 8, 4, 2, 1)


def _cumsum_rows(g, row, rev):
    for sh in (1, 2, 4, 8, 16, 32):
        if rev:
            g = g + jnp.where(row < HG_CHUNK - sh, pltpu.roll(g, HG_CHUNK - sh, 0), 0.0)
        else:
            g = g + jnp.where(row >= sh, pltpu.roll(g, sh, 0), 0.0)
    return g


def _level_reference(gc, row, m, rev):
    if m >= 4:
        g3 = gc.reshape(HG_CHUNK // (2 * m), 2 * m, HEAD_DIM)
        ref = g3[:, m:m + 1, :] if rev else g3[:, m - 1:m, :]
        return jnp.broadcast_to(ref, g3.shape).reshape(HG_CHUNK, HEAD_DIM)
    r = row & (2 * m - 1)
    target = m if rev else m - 1
    out = gc
    for src in range(2 * m):
        if src != target:
            out = jnp.where(r == src, pltpu.roll(gc, (src - target) % HG_CHUNK, 0), out)
    return out


def _hgrn_chunk(q_raw, v, f_raw, la, lc, omlb, s_ref, rev):
    f32, bf = jnp.float32, jnp.bfloat16
    row = lax.broadcasted_iota(jnp.int32, (HG_CHUNK, HEAD_DIM), 0)
    q = q_raw * jax.nn.sigmoid(q_raw) * (HEAD_DIM ** -0.5)
    e = jnp.exp(-jnp.abs(f_raw))
    log_sig = jnp.minimum(f_raw, 0.0) - jnp.log1p(e)
    k = omlb * (jnp.where(f_raw >= 0.0, e, 1.0) / (1.0 + e))
    b = lc + log_sig
    g = jnp.maximum(la, b) + jnp.log1p(jnp.exp(-jnp.abs(la - b)))
    gc = _cumsum_rows(g, row, rev)

    t_idx = lax.broadcasted_iota(jnp.int32, (HG_CHUNK, HG_CHUNK), 0)
    s_idx = lax.broadcasted_iota(jnp.int32, (HG_CHUNK, HG_CHUNK), 1)
    nt = (((1,), (1,)), ((), ()))
    attn = jnp.where(t_idx == s_idx, lax.dot_general(q.astype(bf), k.astype(bf), nt, preferred_element_type=f32), 0.0)
    for m in HG_LEVELS:
        gref = _level_reference(gc, row, m, rev)
        upper = (row & m) != 0
        is_query = jnp.logical_not(upper) if rev else upper
        scale = jnp.exp(jnp.where(is_query, gc - gref, gref - gc))
        qm = jnp.where(is_query, q * scale, 0.0).astype(bf)
        km = jnp.where(is_query, 0.0, k * scale).astype(bf)
        same_block = ((t_idx ^ s_idx) & ~(2 * m - 1)) == 0
        attn = attn + jnp.where(same_block, lax.dot_general(qm, km, nt, preferred_element_type=f32), 0.0)

    state_t = s_ref[...]
    vb = v.astype(bf)
    o = jnp.dot(attn.astype(bf), vb, preferred_element_type=f32)
    o = o + lax.dot_general((q * jnp.exp(gc)).astype(bf), state_t.astype(bf), nt, preferred_element_type=f32)
    g_end = gc[0:1, :] if rev else gc[HG_CHUNK - 1:HG_CHUNK, :]
    kd = (k * jnp.exp(g_end - gc)).astype(bf)
    tn = (((0,), (0,)), ((), ()))
    s_ref[...] = jnp.exp(g_end) * state_t + lax.dot_general(vb, kd, tn, preferred_element_type=f32)
    return o


def _hgrn_kernel(qf_ref, if_ref, ff_ref, qb_ref, ib_ref, fb_ref, la_ref, lc_ref, om_ref, of_ref, ob_ref, s_ref):
    @pl.when(pl.program_id(2) == 0)
    def _():
        s_ref[...] = jnp.zeros_like(s_ref)

    for d, (q_ref, i_ref, f_ref, o_ref) in enumerate(((qf_ref, if_ref, ff_ref, of_ref), (qb_ref, ib_ref, fb_ref, ob_ref))):
        for h in range(HG_HB):
            sl = slice(h * HEAD_DIM, (h + 1) * HEAD_DIM)
            o_ref[:, sl] = _hgrn_chunk(q_ref[:, sl], i_ref[:, sl], f_ref[:, sl], la_ref[d:d + 1, sl], lc_ref[d:d + 1, sl],
                                       om_ref[d:d + 1, sl], s_ref.at[d, h], rev=(d == 1))


def _hgrn2(p, col0, lb):
    bsz, lp, _ = p.shape
    nchunk = lp // HG_CHUNK
    cw = HG_HB * HEAD_DIM
    ng = HG_WIDTH // cw
    off = col0 // cw
    la = jnp.log(lb)
    lc = jnp.log1p(-lb)
    om = 1.0 - lb
    fwd = lambda seg: pl.BlockSpec((None, HG_CHUNK, cw), lambda b, g, c: (b, c, off + seg * ng + g))
    bwd = lambda seg: pl.BlockSpec((None, HG_CHUNK, cw), lambda b, g, c: (b, nchunk - 1 - c, off + seg * ng + g))
    par = pl.BlockSpec((2, cw), lambda b, g, c: (0, g))
    return pl.pallas_call(
        _hgrn_kernel,
        grid=(bsz, ng, nchunk),
        in_specs=[fwd(0), fwd(1), fwd(2), bwd(0), bwd(1), bwd(3), par, par, par],
        out_specs=[pl.BlockSpec((None, HG_CHUNK, cw), lambda b, g, c: (b, c, g)),
                   pl.BlockSpec((None, HG_CHUNK, cw), lambda b, g, c: (b, nchunk - 1 - c, g))],
        out_shape=[jax.ShapeDtypeStruct((bsz, lp, HG_WIDTH), jnp.float32)] * 2,
        scratch_shapes=[pltpu.VMEM((2, HG_HB, HEAD_DIM, HEAD_DIM), jnp.float32)],
        compiler_params=pltpu.CompilerParams(
            dimension_semantics=("parallel", "parallel", "arbitrary"), vmem_limit_bytes=V7X_VMEM_LIMIT),
        name="hgrn2_bidirectional",
    )(p, p, p, p, p, p, la, lc, om)


EG_ROWS = 320


def _even_gate_kernel(oa_ref, of_ref, ob_ref, ga_ref, gb_ref, gain_ref, y_ref):
    ga = ga_ref[...]
    y_ref[:, :NA_WIDTH] = (oa_ref[...] * (ga * jax.nn.sigmoid(ga))).astype(y_ref.dtype)
    for h in range(HG_HEADS):
        sl = slice(h * HEAD_DIM, (h + 1) * HEAD_DIM)
        oh = of_ref[:, sl] + ob_ref[:, sl]
        oh = oh * lax.rsqrt(jnp.mean(oh * oh, axis=-1, keepdims=True) + EPS) * gain_ref[:, sl]
        gb = gb_ref[:, sl]
        y_ref[:, NA_WIDTH + h * HEAD_DIM:NA_WIDTH + (h + 1) * HEAD_DIM] = (
            oh * (gb * jax.nn.sigmoid(gb))).astype(y_ref.dtype)


def _even_gate(oa, o_fwd, o_bwd, p, gate_col0, norm_gain):
    bsz, lp, _ = oa.shape
    assert gate_col0 % NA_WIDTH == 0
    goff = gate_col0 // NA_WIDTH
    half = lambda off: pl.BlockSpec((None, EG_ROWS, NA_WIDTH), lambda b, r: (b, r, off))
    return pl.pallas_call(
        _even_gate_kernel,
        grid=(bsz, lp // EG_ROWS),
        in_specs=[half(0), half(0), half(0), half(goff), half(goff + 1),
                  pl.BlockSpec((1, HG_WIDTH), lambda b, r: (0, 0))],
        out_specs=pl.BlockSpec((None, EG_ROWS, D_MODEL), lambda b, r: (b, r, 0)),
        out_shape=jax.ShapeDtypeStruct((bsz, lp, D_MODEL), jnp.bfloat16),
        compiler_params=pltpu.CompilerParams(
            dimension_semantics=("parallel", "parallel"), vmem_limit_bytes=V7X_VMEM_LIMIT),
        name="even_gate",
    )(oa, o_fwd, o_bwd, p, p, norm_gain.reshape(1, HG_WIDTH))


def _split_heads(a):
    b, l, w = a.shape
    return a.reshape(b, l, w // HEAD_DIM, HEAD_DIM).transpose(0, 2, 1, 3)


def _merge_heads(a):
    b, h, l, d = a.shape
    return a.transpose(0, 2, 1, 3).reshape(b, l, h * d)


def _natten(q, k, v, rpb, meta_bias):
    b, h, l, dh = q.shape
    t = l - N_META
    rows = t // GRID_W
    kh = min(NA_KH, rows)
    scale = dh ** -0.5
    f32 = jnp.float32
    qm, km, vm = q[:, :, :N_META], k[:, :, :N_META], v[:, :, :N_META]
    qg = q[:, :, N_META:].reshape(b, h, rows, GRID_W, dh)
    kg = k[:, :, N_META:].reshape(b, h, rows, GRID_W, dh)
    vg = v[:, :, N_META:].reshape(b, h, rows, GRID_W, dh)
    mbias = meta_bias.astype(f32)[None, :, None, :]
    s_mm = jnp.einsum('bhqd,bhkd->bhqk', qm, km).astype(f32) * scale + mbias
    o_meta = jnp.einsum('bhqk,bhkd->bhqd', jax.nn.softmax(s_mm, axis=-1).astype(v.dtype), vm)
    cols = np.arange(GRID_W)
    c0 = np.clip(cols - NA_KW // 2, 0, GRID_W - NA_KW)
    col_ok = (cols[None, :] >= c0[:, None]) & (cols[None, :] < c0[:, None] + NA_KW)
    col_idx = np.clip(cols[None, :] - cols[:, None] + NA_KW - 1, 0, 2 * NA_KW - 2)
    rpb_cols = rpb.astype(f32)[:, :, col_idx]

    def row_block(r):
        r0 = jnp.clip(r - kh // 2, 0, rows - kh)
        kb = lax.dynamic_slice_in_dim(kg, r0, kh, axis=2)
        vb = lax.dynamic_slice_in_dim(vg, r0, kh, axis=2)
        qr = lax.dynamic_index_in_dim(qg, r, axis=2, keepdims=False)
        s_win = jnp.einsum('bhqd,bhikd->bhqik', qr, kb).astype(f32) * scale
        row_off = r0 + jnp.arange(kh) - r + NA_KH - 1
        bias = jnp.take(rpb_cols, row_off, axis=1).transpose(0, 2, 1, 3)
        s_win = jnp.where(col_ok[None, None, :, None, :], s_win + bias[None], -jnp.inf)
        s_meta = jnp.einsum('bhqd,bhmd->bhqm', qr, km).astype(f32) * scale + mbias
        s = jnp.concatenate([s_meta, s_win.reshape(b, h, GRID_W, kh * GRID_W)], axis=-1)
        p = jax.nn.softmax(s, axis=-1).astype(v.dtype)
        p_win = p[..., N_META:].reshape(b, h, GRID_W, kh, GRID_W)
        return (jnp.einsum('bhqm,bhmd->bhqd', p[..., :N_META], vm)
                + jnp.einsum('bhqik,bhikd->bhqd', p_win, vb))

    o_grid = lax.map(row_block, jnp.arange(rows))
    o_grid = o_grid.transpose(1, 2, 0, 3, 4).reshape(b, h, t, dh)
    return jnp.concatenate([o_meta, o_grid], axis=2)


def _even_mix(p, rpb, meta_bias, lb, norm_gain):
    nw = NA_WIDTH
    pt = p[:, PAD:]
    qa, ka, va = pt[..., :nw], pt[..., nw:2 * nw], pt[..., 2 * nw:3 * nw]
    oa = _merge_heads(_natten(_split_heads(qa), _split_heads(ka), _split_heads(va), rpb, meta_bias))
    oa = jnp.pad(oa, ((0, 0), (PAD, 0), (0, 0)))
    o_fwd, o_bwd = _hgrn2(p, 3 * nw, lb)
    return _even_gate(oa, o_fwd, o_bwd, p, 3 * nw + 4 * HG_WIDTH, norm_gain)


def kernel(x, meta_tokens, norm_pre, norm_post, ev_w_in, ev_w_out, na_rpb, na_meta_bias, hg_lower, hg_norm,
           od_w_in, od_w_out, hy_short, hy_ffn_w1, hy_ffn_b1, hy_ffn_freq1, hy_ffn_w2, hy_ffn_b2, hy_ffn_freq2,
           hy_ffn_w3, hy_bias):
    b = x.shape[0]
    depth = norm_pre.shape[0]
    f32 = jnp.float32
    lb_all = jnp.cumsum(jax.nn.softmax(hg_lower.astype(f32), axis=0), axis=0)
    lb_all = lb_all - lb_all[:1]
    meta = jnp.broadcast_to(meta_tokens.astype(f32)[None], (b, N_META, D_MODEL))
    h = jnp.concatenate([jnp.zeros((b, PAD, D_MODEL), f32), meta, x.astype(f32)], axis=1)
    h = h.reshape(b * LP, D_MODEL)
    consts = _dft_constants()
    for layer in range(depth):
        j = layer // 2
        hn = _rmsnorm(h, norm_pre[layer], jnp.bfloat16)
        if layer % 2 == 0:
            p = _matmul(hn, ev_w_in[j]).reshape(b, LP, -1)
            y = _even_mix(p, na_rpb[j], na_meta_bias[j], lb_all[j], hg_norm[j]).reshape(b * LP, D_MODEL)
            w_out = ev_w_out[j]
        else:
            p3 = _matmul(hn, od_w_in[j]).reshape(b, LP, -1)
            y = _odd_mix(p3, hy_short[j], hy_ffn_w1[j], hy_ffn_b1[j], hy_ffn_freq1[j], hy_ffn_w2[j],
                         hy_ffn_b2[j], hy_ffn_freq2[j], hy_ffn_w3[j], hy_bias[j], consts)
            y = y.reshape(b * LP, D_MODEL)
            w_out = od_w_out[j]
        out = _matmul(y, w_out)
        h = _residual_rmsnorm(h, out, norm_post[layer])
    return h.reshape(b, LP, D_MODEL)[:, PAD + N_META:].astype(x.dtype)
```

```python
import functools
import math

import jax
import jax.numpy as jnp
import numpy as np
from jax import lax
from jax.experimental import pallas as pl
from jax.experimental.pallas import tpu as pltpu

D_MODEL = 4096
SEQ = 4096
N_META = 16
GRID_W = 64
HEAD_DIM = 128
NA_WIDTH = D_MODEL // 2
NA_KH = 8
NA_KW = 16
HG_WIDTH = D_MODEL // 2
HG_HEADS = HG_WIDTH // HEAD_DIM
HG_CHUNK = 64
HY_WIDTH = D_MODEL
HY_ORDER = 2
HY_EMB = 33
HY_BANDS = (HY_EMB - 1) // 2
HY_FAST_DECAY = 0.3
HY_SLOW_DECAY = 1.5
HY_TARGET = 1e-2
EPS = 1e-6

PAD = HG_CHUNK - N_META
LP = PAD + N_META + SEQ
L_TOK = N_META + SEQ

V7X_VMEM_LIMIT = 56 * 1024 * 1024


def _rmsnorm_kernel(x_ref, g_ref, o_ref):
    x = x_ref[...]
    r = lax.rsqrt(jnp.mean(x * x, axis=-1, keepdims=True) + EPS)
    o_ref[...] = (x * r * g_ref[...]).astype(o_ref.dtype)


def _rmsnorm(x2d, g, out_dtype, tr=320):
    m, d = x2d.shape
    return pl.pallas_call(
        _rmsnorm_kernel,
        grid=(m // tr,),
        in_specs=[pl.BlockSpec((tr, d), lambda i: (i, 0)), pl.BlockSpec((1, d), lambda i: (0, 0))],
        out_specs=pl.BlockSpec((tr, d), lambda i: (i, 0)),
        out_shape=jax.ShapeDtypeStruct((m, d), out_dtype),
        compiler_params=pltpu.CompilerParams(vmem_limit_bytes=V7X_VMEM_LIMIT),
        name="rmsnorm_pre",
    )(x2d, g.reshape(1, d))


def _post_kernel(h_ref, y_ref, g_ref, o_ref):
    y = y_ref[...]
    r = lax.rsqrt(jnp.mean(y * y, axis=-1, keepdims=True) + EPS)
    o_ref[...] = h_ref[...] + y * r * g_ref[...]


def _residual_rmsnorm(h2d, y2d, g, tr=320):
    m, d = h2d.shape
    return pl.pallas_call(
        _post_kernel,
        grid=(m // tr,),
        in_specs=[pl.BlockSpec((tr, d), lambda i: (i, 0)), pl.BlockSpec((tr, d), lambda i: (i, 0)),
                  pl.BlockSpec((1, d), lambda i: (0, 0))],
        out_specs=pl.BlockSpec((tr, d), lambda i: (i, 0)),
        out_shape=jax.ShapeDtypeStruct((m, d), jnp.float32),
        compiler_params=pltpu.CompilerParams(vmem_limit_bytes=V7X_VMEM_LIMIT),
        name="residual_rmsnorm_post",
    )(h2d, y2d, g.reshape(1, d))


def _matmul_kernel(x_ref, w_ref, o_ref, acc_ref):
    k = pl.program_id(2)

    @pl.when(k == 0)
    def _():
        acc_ref[...] = jnp.zeros_like(acc_ref)

    acc_ref[...] += jnp.dot(x_ref[...].astype(jnp.bfloat16), w_ref[...].astype(jnp.bfloat16),
                            preferred_element_type=jnp.float32)

    @pl.when(k == pl.num_programs(2) - 1)
    def _():
        o_ref[...] = acc_ref[...].astype(o_ref.dtype)


def _matmul(x, w, tm=2080, tn=1024, tk=512, out_dtype=jnp.float32):
    m, kd = x.shape
    _, n = w.shape
    assert m % tm == 0 and n % tn == 0 and kd % tk == 0
    return pl.pallas_call(
        _matmul_kernel,
        grid=(m // tm, n // tn, kd // tk),
        in_specs=[pl.BlockSpec((tm, tk), lambda i, j, k: (i, k)), pl.BlockSpec((tk, tn), lambda i, j, k: (k, j))],
        out_specs=pl.BlockSpec((tm, tn), lambda i, j, k: (i, j)),
        out_shape=jax.ShapeDtypeStruct((m, n), out_dtype),
        scratch_shapes=[pltpu.VMEM((tm, tn), jnp.float32)],
        compiler_params=pltpu.CompilerParams(
            dimension_semantics=("parallel", "parallel", "arbitrary"), vmem_limit_bytes=V7X_VMEM_LIMIT),
        name="projection_matmul",
    )(x, w)


FFT_N1 = 128
FFT_N2 = 65
FFT_N = FFT_N1 * FFT_N2
N2P = 80
SLAB = 264
TC = 128
MID_GROUP = 4


def _dft_constants():
    a = np.arange(FFT_N1)
    c = np.arange(FFT_N1)
    w1 = np.exp(-2j * np.pi * np.outer(c, a) / FFT_N1)
    w1r, w1i = w1.real, w1.imag
    half = FFT_N1 // 2
    f1 = np.block([[w1r[:, :half], -w1i[:, :half]], [w1i[:, :half], w1r[:, :half]]])
    f1_real = np.concatenate([w1r, w1i], axis=0)
    g = np.conj(w1).T
    gr, gi = g.real[:half], g.imag[:half]
    i2 = np.block([[gr, -gi], [gi, gr]])
    b = np.arange(FFT_N2)
    d = np.arange(FFT_N2)
    w2 = np.exp(-2j * np.pi * np.outer(d, b) / FFT_N2)
    tw = np.exp(-2j * np.pi * np.outer(c, b) / FFT_N)
    m = w2[None, :, :] * tw[:, None, :]
    ab = np.zeros((FFT_N1, 2 * N2P, 128), np.float64)
    ab[:, :FFT_N2, :FFT_N2] = m.real
    ab[:, N2P:N2P + FFT_N2, :FFT_N2] = m.imag
    bf = jnp.bfloat16
    return (jnp.asarray(f1, bf), jnp.asarray(f1_real, bf), jnp.asarray(i2, bf), jnp.asarray(ab, bf))


def _middle_stage_inputs(y_ref, c):
    yr = y_ref[pl.ds(c, N2P, stride=SLAB), :]
    yi = y_ref[pl.ds(FFT_N1 + c, N2P, stride=SLAB), :]
    return jnp.concatenate([yr, yi], axis=1).astype(jnp.bfloat16)


def _forward_middle(ab, y2):
    y2 = jnp.concatenate([y2, jnp.zeros((128 - N2P, 2 * TC), jnp.bfloat16)], axis=0)
    q = jnp.dot(ab, y2, preferred_element_type=jnp.float32)
    zr = q[:N2P, :TC] - q[N2P:, TC:]
    zi = q[N2P:, :TC] + q[:N2P, TC:]
    return zr, zi


def _spectrum_kernel(h_ref, w3f_ref, w3b_ref, delta_ref, f1_ref, ab_ref, k_ref, kt_ref, y_ref):
    w3 = jnp.concatenate([w3f_ref[...], w3b_ref[...]], axis=0).astype(jnp.bfloat16)
    delta = delta_ref[...]
    rows = FFT_N // 13

    def fill(i, carry):
        r0 = pl.multiple_of(i * rows, 8)
        kt = jnp.dot(h_ref[pl.ds(r0, rows), :], w3, preferred_element_type=jnp.float32)
        j = lax.broadcasted_iota(jnp.int32, (rows, TC), 0) + r0
        lag = jnp.where(j < L_TOK, j, FFT_N - j).astype(jnp.float32)
        kt_ref[pl.ds(r0, rows), :] = kt * jnp.exp(-(lag * (1.0 / (L_TOK - 1))) * delta)
        return carry

    lax.fori_loop(0, 13, fill, 0)
    y_ref[pl.ds(FFT_N2 * SLAB, (N2P - FFT_N2) * SLAB), :] = jnp.zeros(((N2P - FFT_N2) * SLAB, TC), jnp.float32)

    def stage1(b, carry):
        x = kt_ref[pl.ds(b, FFT_N1, stride=FFT_N2), :].astype(jnp.bfloat16)
        y = jnp.dot(f1_ref[...], x, preferred_element_type=jnp.float32)
        y_ref[pl.ds(pl.multiple_of(b * SLAB, 8), 2 * FFT_N1), :] = y
        return carry

    lax.fori_loop(0, FFT_N2, stage1, 0, unroll=5)

    def stage2(c, carry):
        zr, zi = _forward_middle(ab_ref[c], _middle_stage_inputs(y_ref, c))
        k_ref[c] = (jnp.concatenate([zr, zi], axis=1) * (1.0 / FFT_N)).astype(jnp.bfloat16)
        return carry

    lax.fori_loop(0, FFT_N1, stage2, 0, unroll=4)


def _filter_spectrum(h2, w3, deltas, consts):
    _, f1_real, _, ab = consts
    nt = HY_WIDTH // TC
    return pl.pallas_call(
        _spectrum_kernel,
        grid=(HY_ORDER, nt),
        in_specs=[
            pl.BlockSpec((FFT_N, 128), lambda n, i: (0, 0)),
            pl.BlockSpec((64, TC), lambda n, i: (0, n * nt + i)),
            pl.BlockSpec((64, TC), lambda n, i: (0, (HY_ORDER + n) * nt + i)),
            pl.BlockSpec((1, TC), lambda n, i: (0, i)),
            pl.BlockSpec((2 * FFT_N1, FFT_N1), lambda n, i: (0, 0)),
            pl.BlockSpec((FFT_N1, 2 * N2P, 128), lambda n, i: (0, 0, 0)),
        ],
        out_specs=pl.BlockSpec((None, None, FFT_N1, N2P, 2 * TC), lambda n, i: (n, i, 0, 0, 0)),
        out_shape=jax.ShapeDtypeStruct((HY_ORDER, nt, FFT_N1, N2P, 2 * TC), jnp.bfloat16),
        scratch_shapes=[pltpu.VMEM((FFT_N, TC), jnp.float32), pltpu.VMEM((N2P * SLAB, TC), jnp.float32)],
        compiler_params=pltpu.CompilerParams(
            dimension_semantics=("parallel", "parallel"), vmem_limit_bytes=V7X_VMEM_LIMIT),
        name="hyena_filter_spectrum",
    )(h2, w3, w3, deltas, f1_real, ab)


def _conv_kernel(u_ref, k_ref, f1_ref, i2_ref, ab_ref, o_ref, y_ref):
    y_ref[pl.ds(FFT_N2 * SLAB, (N2P - FFT_N2) * SLAB), :] = jnp.zeros(((N2P - FFT_N2) * SLAB, TC), jnp.float32)
    half = FFT_N1 // 2

    def stage1(b, carry):
        x0 = u_ref.at[0][pl.ds(b, half, stride=FFT_N2), :]
        x1 = u_ref.at[1][pl.ds(b, half, stride=FFT_N2), :]
        x = jnp.concatenate([x0, x1], axis=0).astype(jnp.bfloat16)
        y = jnp.dot(f1_ref[...], x, preferred_element_type=jnp.float32)
        y_ref[pl.ds(pl.multiple_of(b * SLAB, 8), 2 * FFT_N1), :] = y
        return carry

    lax.fori_loop(0, FFT_N2, stage1, 0, unroll=5)

    def middle_one(c, y2):
        ab = ab_ref[c]
        zr, zi = _forward_middle(ab, y2)
        kk = k_ref[c].astype(jnp.float32)
        kr, ki = kk[:, :TC], kk[:, TC:]
        pr = zr * kr - zi * ki
        pi = zr * ki + zi * kr
        rhs = jnp.concatenate([jnp.concatenate([pr, pi], axis=1), jnp.concatenate([pi, -pr], axis=1)], axis=0)
        return lax.dot_general(ab, rhs.astype(jnp.bfloat16), (((0,), (0,)), ((), ())),
                               preferred_element_type=jnp.float32)

    def middle(g, carry):
        cs = [g * MID_GROUP + i for i in range(MID_GROUP)]
        ins = [_middle_stage_inputs(y_ref, c) for c in cs]
        outs = [middle_one(c, y2) for c, y2 in zip(cs, ins)]
        for c, v in zip(cs, outs):
            y_ref[pl.ds(c, N2P, stride=SLAB), :] = v[:N2P, :TC]
            y_ref[pl.ds(FFT_N1 + c, N2P, stride=SLAB), :] = v[:N2P, TC:]
        return carry

    lax.fori_loop(0, FFT_N1 // MID_GROUP, middle, 0)

    def stage4(b, carry):
        v = y_ref[pl.ds(pl.multiple_of(b * SLAB, 8), 2 * FFT_N1), :].astype(jnp.bfloat16)
        x = jnp.dot(i2_ref[...], v, preferred_element_type=jnp.float32)
        o_ref.at[0][pl.ds(b, half, stride=FFT_N2), :] = x[:half]
        o_ref.at[1][pl.ds(b, half, stride=FFT_N2), :] = x[half:]
        return carry

    lax.fori_loop(0, FFT_N2, stage4, 0, unroll=5)


def _long_conv(u, col0, kspec, order, consts):
    f1, _, i2, ab = consts
    bsz, lp, _ = u.shape
    assert bsz == 2 and lp == LP and col0 % TC == 0
    nt = HY_WIDTH // TC
    off = col0 // TC
    return pl.pallas_call(
        _conv_kernel,
        grid=(nt,),
        in_specs=[
            pl.BlockSpec((bsz, lp, TC), lambda i: (0, 0, off + i)),
            pl.BlockSpec((None, None, FFT_N1, N2P, 2 * TC), lambda i: (order, i, 0, 0, 0)),
            pl.BlockSpec((2 * FFT_N1, FFT_N1), lambda i: (0, 0)),
            pl.BlockSpec((FFT_N1, 2 * FFT_N1), lambda i: (0, 0)),
            pl.BlockSpec((FFT_N1, 2 * N2P, 128), lambda i: (0, 0, 0)),
        ],
        out_specs=pl.BlockSpec((bsz, lp, TC), lambda i: (0, 0, i)),
        out_shape=jax.ShapeDtypeStruct((bsz, lp, HY_WIDTH), jnp.float32),
        scratch_shapes=[pltpu.VMEM((N2P * SLAB, TC), jnp.float32)],
        compiler_params=pltpu.CompilerParams(dimension_semantics=("parallel",), vmem_limit_bytes=V7X_VMEM_LIMIT),
        name="hyena_long_conv",
    )(u, kspec, f1, i2, ab)


def _short_conv_kernel(u_ref, w_ref, o_ref):
    u = u_ref[...]
    w = w_ref[...]
    y = pltpu.roll(u, 1, 0) * w[0:1] + u * w[1:2] + pltpu.roll(u, LP - 1, 0) * w[2:3]
    row = lax.broadcasted_iota(jnp.int32, y.shape, 0)
    o_ref[...] = jnp.where(row >= PAD, y, 0.0)


def _short_conv(p3, w_short):
    bsz = p3.shape[0]
    width = 3 * HY_WIDTH
    return pl.pallas_call(
        _short_conv_kernel,
        grid=(bsz, width // TC),
        in_specs=[pl.BlockSpec((None, LP, TC), lambda b, i: (b, 0, i)), pl.BlockSpec((3, TC), lambda b, i: (0, i))],
        out_specs=pl.BlockSpec((None, LP, TC), lambda b, i: (b, 0, i)),
        out_shape=jax.ShapeDtypeStruct((bsz, LP, width), jnp.float32),
        compiler_params=pltpu.CompilerParams(
            dimension_semantics=("parallel", "parallel"), vmem_limit_bytes=V7X_VMEM_LIMIT),
        name="hyena_short_conv",
    )(p3, w_short.T)


GATE_ROWS = 520
GATE_COLS = 512


def _gate1_kernel(v_ref, x_ref, y_ref, b_ref, o_ref):
    o_ref[...] = x_ref[...] * (y_ref[...] + v_ref[...] * b_ref[...])


def _gate2_kernel(z_ref, x_ref, y_ref, b_ref, g_ref, o_ref):
    g = g_ref[...]
    z = x_ref[...] * (y_ref[...] + z_ref[...] * b_ref[...])
    o_ref[...] = (z * (g * jax.nn.sigmoid(g))).astype(o_ref.dtype)


def _hyena_gate1(uc, y, bias):
    bsz = uc.shape[0]
    nc = HY_WIDTH // GATE_COLS
    blk = lambda off: pl.BlockSpec((None, GATE_ROWS, GATE_COLS), lambda b, r, c: (b, r, off + c))
    return pl.pallas_call(
        _gate1_kernel,
        grid=(bsz, LP // GATE_ROWS, nc),
        in_specs=[blk(0), blk(nc), blk(0), pl.BlockSpec((1, GATE_COLS), lambda b, r, c: (0, c))],
        out_specs=blk(0),
        out_shape=jax.ShapeDtypeStruct((bsz, LP, HY_WIDTH), jnp.float32),
        compiler_params=pltpu.CompilerParams(
            dimension_semantics=("parallel", "parallel", "parallel"), vmem_limit_bytes=V7X_VMEM_LIMIT),
        name="hyena_gate1",
    )(uc, uc, y, bias.reshape(1, HY_WIDTH))


def _hyena_gate2(z1, uc, y, bias, p3):
    bsz = uc.shape[0]
    nc = HY_WIDTH // GATE_COLS
    blk = lambda off: pl.BlockSpec((None, GATE_ROWS, GATE_COLS), lambda b, r, c: (b, r, off + c))
    return pl.pallas_call(
        _gate2_kernel,
        grid=(bsz, LP // GATE_ROWS, nc),
        in_specs=[blk(0), blk(2 * nc), blk(0), pl.BlockSpec((1, GATE_COLS), lambda b, r, c: (0, c)), blk(3 * nc)],
        out_specs=blk(0),
        out_shape=jax.ShapeDtypeStruct((bsz, LP, HY_WIDTH), jnp.bfloat16),
        compiler_params=pltpu.CompilerParams(
            dimension_semantics=("parallel", "parallel", "parallel"), vmem_limit_bytes=V7X_VMEM_LIMIT),
        name="hyena_gate2",
    )(z1, uc, y, bias.reshape(1, HY_WIDTH), p3)


def _hyena_hidden(w1, b1, fr1, w2, b2, fr2):
    f32 = jnp.float32
    l = L_TOK
    hi = lax.Precision.HIGHEST
    t = jnp.linspace(0.0, 1.0, l, dtype=f32)[:, None]
    w = 2.0 * math.pi * jnp.arange(l, dtype=f32)[:, None] / l
    bands = jnp.linspace(1e-4, HY_BANDS - 1, HY_BANDS, dtype=f32)[None, :]
    z = jnp.concatenate([t, jnp.cos(bands * w), -jnp.sin(bands * w)], axis=-1)
    hid = jnp.sin(fr1 * (jnp.dot(z, w1, precision=hi) + b1))
    hid = jnp.sin(fr2 * (jnp.dot(hid, w2, precision=hi) + b2))
    j = np.arange(FFT_N)
    lag = np.minimum(np.where(j < l, j, FFT_N - j), l - 1)
    fwd = (j < l).astype(np.float32)[:, None]
    bwd = ((j > FFT_N - l) | (j == 0)).astype(np.float32)[:, None]
    rows = hid[lag]
    return jnp.concatenate([rows * fwd, rows * bwd], axis=1).astype(jnp.bfloat16)


def _odd_mix(p3, w_short, w1, b1, fr1, w2, b2, fr2, w3, bias, consts):
    min_decay = math.log(HY_TARGET) / HY_SLOW_DECAY
    max_decay = math.log(HY_TARGET) / HY_FAST_DECAY
    deltas = jnp.abs(jnp.linspace(min_decay, max_decay, HY_WIDTH, dtype=jnp.float32)).reshape(1, HY_WIDTH)
    kspec = _filter_spectrum(_hyena_hidden(w1, b1, fr1, w2, b2, fr2), w3, deltas, consts)
    uc = _short_conv(p3, w_short)
    y1 = _long_conv(uc, 0, kspec, 0, consts)
    z1 = _hyena_gate1(uc, y1, bias[0])
    y2 = _long_conv(z1, 0, kspec, 1, consts)
    return _hyena_gate2(z1, uc, y2, bias[1], p3)


HG_HB = 2
HG_LEVELS = (32, 16, 8, 4, 2, 1)


def _cumsum_rows(g, row, rev):
    for sh in (1, 2, 4, 8, 16, 32):
        if rev:
            g = g + jnp.where(row < HG_CHUNK - sh, pltpu.roll(g, HG_CHUNK - sh, 0), 0.0)
        else:
            g = g + jnp.where(row >= sh, pltpu.roll(g, sh, 0), 0.0)
    return g


def _level_reference(gc, row, m, rev):
    if m >= 4:
        g3 = gc.reshape(HG_CHUNK // (2 * m), 2 * m, HEAD_DIM)
        ref = g3[:, m:m + 1, :] if rev else g3[:, m - 1:m, :]
        return jnp.broadcast_to(ref, g3.shape).reshape(HG_CHUNK, HEAD_DIM)
    r = row & (2 * m - 1)
    target = m if rev else m - 1
    out = gc
    for src in range(2 * m):
        if src != target:
            out = jnp.where(r == src, pltpu.roll(gc, (src - target) % HG_CHUNK, 0), out)
    return out


def _hgrn_chunk(q_raw, v, f_raw, la, lc, omlb, s_ref, rev):
    f32, bf = jnp.float32, jnp.bfloat16
    row = lax.broadcasted_iota(jnp.int32, (HG_CHUNK, HEAD_DIM), 0)
    q = q_raw * jax.nn.sigmoid(q_raw) * (HEAD_DIM ** -0.5)
    e = jnp.exp(-jnp.abs(f_raw))
    log_sig = jnp.minimum(f_raw, 0.0) - jnp.log1p(e)
    k = omlb * (jnp.where(f_raw >= 0.0, e, 1.0) / (1.0 + e))
    b = lc + log_sig
    g = jnp.maximum(la, b) + jnp.log1p(jnp.exp(-jnp.abs(la - b)))
    gc = _cumsum_rows(g, row, rev)

    t_idx = lax.broadcasted_iota(jnp.int32, (HG_CHUNK, HG_CHUNK), 0)
    s_idx = lax.broadcasted_iota(jnp.int32, (HG_CHUNK, HG_CHUNK), 1)
    nt = (((1,), (1,)), ((), ()))
    attn = jnp.where(t_idx == s_idx, lax.dot_general(q.astype(bf), k.astype(bf), nt, preferred_element_type=f32), 0.0)
    for m in HG_LEVELS:
        gref = _level_reference(gc, row, m, rev)
        upper = (row & m) != 0
        is_query = jnp.logical_not(upper) if rev else upper
        scale = jnp.exp(jnp.where(is_query, gc - gref, gref - gc))
        qm = jnp.where(is_query, q * scale, 0.0).astype(bf)
        km = jnp.where(is_query, 0.0, k * scale).astype(bf)
        same_block = ((t_idx ^ s_idx) & ~(2 * m - 1)) == 0
        attn = attn + jnp.where(same_block, lax.dot_general(qm, km, nt, preferred_element_type=f32), 0.0)

    state_t = s_ref[...]
    vb = v.astype(bf)
    o = jnp.dot(attn.astype(bf), vb, preferred_element_type=f32)
    o = o + lax.dot_general((q * jnp.exp(gc)).astype(bf), state_t.astype(bf), nt, preferred_element_type=f32)
    g_end = gc[0:1, :] if rev else gc[HG_CHUNK - 1:HG_CHUNK, :]
    kd = (k * jnp.exp(g_end - gc)).astype(bf)
    tn = (((0,), (0,)), ((), ()))
    s_ref[...] = jnp.exp(g_end) * state_t + lax.dot_general(vb, kd, tn, preferred_element_type=f32)
    return o


def _hgrn_kernel(qf_ref, if_ref, ff_ref, qb_ref, ib_ref, fb_ref, la_ref, lc_ref, om_ref, of_ref, ob_ref, s_ref):
    @pl.when(pl.program_id(2) == 0)
    def _():
        s_ref[...] = jnp.zeros_like(s_ref)

    for d, (q_ref, i_ref, f_ref, o_ref) in enumerate(((qf_ref, if_ref, ff_ref, of_ref), (qb_ref, ib_ref, fb_ref, ob_ref))):
        for h in range(HG_HB):
            sl = slice(h * HEAD_DIM, (h + 1) * HEAD_DIM)
            o_ref[:, sl] = _hgrn_chunk(q_ref[:, sl], i_ref[:, sl], f_ref[:, sl], la_ref[d:d + 1, sl], lc_ref[d:d + 1, sl],
                                       om_ref[d:d + 1, sl], s_ref.at[d, h], rev=(d == 1))


def _hgrn2(p, col0, lb):
    bsz, lp, _ = p.shape
    nchunk = lp // HG_CHUNK
    cw = HG_HB * HEAD_DIM
    ng = HG_WIDTH // cw
    off = col0 // cw
    la = jnp.log(lb)
    lc = jnp.log1p(-lb)
    om = 1.0 - lb
    fwd = lambda seg: pl.BlockSpec((None, HG_CHUNK, cw), lambda b, g, c: (b, c, off + seg * ng + g))
    bwd = lambda seg: pl.BlockSpec((None, HG_CHUNK, cw), lambda b, g, c: (b, nchunk - 1 - c, off + seg * ng + g))
    par = pl.BlockSpec((2, cw), lambda b, g, c: (0, g))
    return pl.pallas_call(
        _hgrn_kernel,
        grid=(bsz, ng, nchunk),
        in_specs=[fwd(0), fwd(1), fwd(2), bwd(0), bwd(1), bwd(3), par, par, par],
        out_specs=[pl.BlockSpec((None, HG_CHUNK, cw), lambda b, g, c: (b, c, g)),
                   pl.BlockSpec((None, HG_CHUNK, cw), lambda b, g, c: (b, nchunk - 1 - c, g))],
        out_shape=[jax.ShapeDtypeStruct((bsz, lp, HG_WIDTH), jnp.float32)] * 2,
        scratch_shapes=[pltpu.VMEM((2, HG_HB, HEAD_DIM, HEAD_DIM), jnp.float32)],
        compiler_params=pltpu.CompilerParams(
            dimension_semantics=("parallel", "parallel", "arbitrary"), vmem_limit_bytes=V7X_VMEM_LIMIT),
        name="hgrn2_bidirectional",
    )(p, p, p, p, p, p, la, lc, om)


EG_ROWS = 320


def _even_gate_kernel(oa_ref, of_ref, ob_ref, ga_ref, gb_ref, gain_ref, y_ref):
    ga = ga_ref[...]
    y_ref[:, :NA_WIDTH] = (oa_ref[...] * (ga * jax.nn.sigmoid(ga))).astype(y_ref.dtype)
    for h in range(HG_HEADS):
        sl = slice(h * HEAD_DIM, (h + 1) * HEAD_DIM)
        oh = of_ref[:, sl] + ob_ref[:, sl]
        oh = oh * lax.rsqrt(jnp.mean(oh * oh, axis=-1, keepdims=True) + EPS) * gain_ref[:, sl]
        gb = gb_ref[:, sl]
        y_ref[:, NA_WIDTH + h * HEAD_DIM:NA_WIDTH + (h + 1) * HEAD_DIM] = (
            oh * (gb * jax.nn.sigmoid(gb))).astype(y_ref.dtype)


def _even_gate(oa, o_fwd, o_bwd, p, gate_col0, norm_gain):
    bsz, lp, _ = oa.shape
    assert gate_col0 % NA_WIDTH == 0
    goff = gate_col0 // NA_WIDTH
    half = lambda off: pl.BlockSpec((None, EG_ROWS, NA_WIDTH), lambda b, r: (b, r, off))
    return pl.pallas_call(
        _even_gate_kernel,
        grid=(bsz, lp // EG_ROWS),
        in_specs=[half(0), half(0), half(0), half(goff), half(goff + 1),
                  pl.BlockSpec((1, HG_WIDTH), lambda b, r: (0, 0))],
        out_specs=pl.BlockSpec((None, EG_ROWS, D_MODEL), lambda b, r: (b, r, 0)),
        out_shape=jax.ShapeDtypeStruct((bsz, lp, D_MODEL), jnp.bfloat16),
        compiler_params=pltpu.CompilerParams(
            dimension_semantics=("parallel", "parallel"), vmem_limit_bytes=V7X_VMEM_LIMIT),
        name="even_gate",
    )(oa, o_fwd, o_bwd, p, p, norm_gain.reshape(1, HG_WIDTH))


NA_MASKED = -1e30


def _na_bias_table(rpb):
    cols = np.arange(GRID_W)
    c0 = np.clip(cols - NA_KW // 2, 0, GRID_W - NA_KW)
    col_ok = (cols[None, :] >= c0[:, None]) & (cols[None, :] < c0[:, None] + NA_KW)
    col_idx = np.clip(cols[None, :] - cols[:, None] + NA_KW - 1, 0, 2 * NA_KW - 2)
    return jnp.where(col_ok[None, None], rpb.astype(jnp.float32)[:, :, col_idx], NA_MASKED)


def _na_kernel(q_ref, k_ref, v_ref, t_ref, mb_ref, o_ref):
    f32, bf = jnp.float32, jnp.bfloat16
    rows = (q_ref.shape[0] - HG_CHUNK) // GRID_W
    scale = HEAD_DIM ** -0.5
    nt = (((1,), (1,)), ((), ()))
    km = k_ref[PAD:PAD + N_META, :].astype(bf)
    vm = v_ref[PAD:PAD + N_META, :].astype(bf)
    mb = mb_ref[...]
    o_ref[0:PAD, :] = jnp.zeros((PAD, HEAD_DIM), f32)

    qm = q_ref[PAD:PAD + N_META, :].astype(bf)
    s = lax.dot_general(qm, km, nt, preferred_element_type=f32) * scale + mb
    e = jnp.exp(s - jnp.max(s, axis=-1, keepdims=True))
    o_ref[PAD:PAD + N_META, :] = (jnp.dot(e.astype(bf), vm, preferred_element_type=f32)
                                  / jnp.sum(e, axis=-1, keepdims=True))

    def grid_row(r, carry):
        r0 = jnp.clip(r - NA_KH // 2, 0, rows - NA_KH)
        q = q_ref[pl.ds(pl.multiple_of(HG_CHUNK + GRID_W * r, GRID_W), GRID_W), :].astype(bf)
        k0 = pl.multiple_of(HG_CHUNK + GRID_W * r0, GRID_W)
        kw = k_ref[pl.ds(k0, NA_KH * GRID_W), :].astype(bf)
        vw = v_ref[pl.ds(k0, NA_KH * GRID_W), :].astype(bf)
        bias = jnp.concatenate([t_ref[r0 + i - r + NA_KH - 1] for i in range(NA_KH)], axis=1)
        s_win = lax.dot_general(q, kw, nt, preferred_element_type=f32) * scale + bias
        s_meta = lax.dot_general(q, km, nt, preferred_element_type=f32) * scale + mb
        m = jnp.maximum(jnp.max(s_win, axis=-1, keepdims=True), jnp.max(s_meta, axis=-1, keepdims=True))
        pw = jnp.exp(s_win - m)
        pm = jnp.exp(s_meta - m)
        den = jnp.sum(pw, axis=-1, keepdims=True) + jnp.sum(pm, axis=-1, keepdims=True)
        o = jnp.dot(pw.astype(bf), vw, preferred_element_type=f32) + jnp.dot(pm.astype(bf), vm, preferred_element_type=f32)
        o_ref[pl.ds(pl.multiple_of(HG_CHUNK + GRID_W * r, GRID_W), GRID_W), :] = o / den
        return carry

    lax.fori_loop(0, rows, grid_row, 0, unroll=2)


def _natten(p, rpb, meta_bias):
    bsz, lp, _ = p.shape
    heads = NA_WIDTH // HEAD_DIM
    blk = lambda seg: pl.BlockSpec((None, lp, HEAD_DIM), lambda b, h: (b, 0, seg * heads + h))
    table = _na_bias_table(rpb)
    return pl.pallas_call(
        _na_kernel,
        grid=(bsz, heads),
        in_specs=[blk(0), blk(1), blk(2),
                  pl.BlockSpec((None, 2 * NA_KH - 1, GRID_W, GRID_W), lambda b, h: (h, 0, 0, 0)),
                  pl.BlockSpec((None, 1, N_META), lambda b, h: (h, 0, 0))],
        out_specs=pl.BlockSpec((None, lp, HEAD_DIM), lambda b, h: (b, 0, h)),
        out_shape=jax.ShapeDtypeStruct((bsz, lp, NA_WIDTH), jnp.float32),
        compiler_params=pltpu.CompilerParams(
            dimension_semantics=("parallel", "parallel"), vmem_limit_bytes=V7X_VMEM_LIMIT),
        name="neighbourhood_attention",
    )(p, p, p, table, meta_bias.astype(jnp.float32).reshape(heads, 1, N_META))


def _even_mix(p, rpb, meta_bias, lb, norm_gain):
    oa = _natten(p, rpb, meta_bias)
    o_fwd, o_bwd = _hgrn2(p, 3 * NA_WIDTH, lb)
    return _even_gate(oa, o_fwd, o_bwd, p, 3 * NA_WIDTH + 4 * HG_WIDTH, norm_gain)


def kernel(x, meta_tokens, norm_pre, norm_post, ev_w_in, ev_w_out, na_rpb, na_meta_bias, hg_lower, hg_norm,
           od_w_in, od_w_out, hy_short, hy_ffn_w1, hy_ffn_b1, hy_ffn_freq1, hy_ffn_w2, hy_ffn_b2, hy_ffn_freq2,
           hy_ffn_w3, hy_bias):
    b = x.shape[0]
    depth = norm_pre.shape[0]
    f32 = jnp.float32
    lb_all = jnp.cumsum(jax.nn.softmax(hg_lower.astype(f32), axis=0), axis=0)
    lb_all = lb_all - lb_all[:1]
    meta = jnp.broadcast_to(meta_tokens.astype(f32)[None], (b, N_META, D_MODEL))
    h = jnp.concatenate([jnp.zeros((b, PAD, D_MODEL), f32), meta, x.astype(f32)], axis=1)
    h = h.reshape(b * LP, D_MODEL)
    consts = _dft_constants()
    for layer in range(depth):
        j = layer // 2
        hn = _rmsnorm(h, norm_pre[layer], jnp.bfloat16)
        if layer % 2 == 0:
            p = _matmul(hn, ev_w_in[j]).reshape(b, LP, -1)
            y = _even_mix(p, na_rpb[j], na_meta_bias[j], lb_all[j], hg_norm[j]).reshape(b * LP, D_MODEL)
            w_out = ev_w_out[j]
        else:
            p3 = _matmul(hn, od_w_in[j]).reshape(b, LP, -1)
            y = _odd_mix(p3, hy_short[j], hy_ffn_w1[j], hy_ffn_b1[j], hy_ffn_freq1[j], hy_ffn_w2[j],
                         hy_ffn_b2[j], hy_ffn_freq2[j], hy_ffn_w3[j], hy_bias[j], consts)
            y = y.reshape(b * LP, D_MODEL)
            w_out = od_w_out[j]
        out = _matmul(y, w_out)
        h = _residual_rmsnorm(h, out, norm_post[layer])
    return h.reshape(b, LP, D_MODEL)[:, PAD + N_META:].astype(x.dtype)
```

```python
import functools
import math

import jax
import jax.numpy as jnp
import numpy as np
from jax import lax
from jax.experimental import pallas as pl
from jax.experimental.pallas import tpu as pltpu

D_MODEL = 4096
SEQ = 4096
N_META = 16
GRID_W = 64
HEAD_DIM = 128
NA_WIDTH = D_MODEL // 2
NA_KH = 8
NA_KW = 16
HG_WIDTH = D_MODEL // 2
HG_HEADS = HG_WIDTH // HEAD_DIM
HG_CHUNK = 64
HY_WIDTH = D_MODEL
HY_ORDER = 2
HY_EMB = 33
HY_BANDS = (HY_EMB - 1) // 2
HY_FAST_DECAY = 0.3
HY_SLOW_DECAY = 1.5
HY_TARGET = 1e-2
EPS = 1e-6

PAD = HG_CHUNK - N_META
LP = PAD + N_META + SEQ
L_TOK = N_META + SEQ

V7X_VMEM_LIMIT = 56 * 1024 * 1024


def _rmsnorm_kernel(x_ref, g_ref, o_ref):
    x = x_ref[...]
    r = lax.rsqrt(jnp.mean(x * x, axis=-1, keepdims=True) + EPS)
    o_ref[...] = (x * r * g_ref[...]).astype(o_ref.dtype)


def _rmsnorm(x2d, g, out_dtype, tr=320):
    m, d = x2d.shape
    return pl.pallas_call(
        _rmsnorm_kernel,
        grid=(m // tr,),
        in_specs=[pl.BlockSpec((tr, d), lambda i: (i, 0)), pl.BlockSpec((1, d), lambda i: (0, 0))],
        out_specs=pl.BlockSpec((tr, d), lambda i: (i, 0)),
        out_shape=jax.ShapeDtypeStruct((m, d), out_dtype),
        compiler_params=pltpu.CompilerParams(vmem_limit_bytes=V7X_VMEM_LIMIT),
        name="rmsnorm_pre",
    )(x2d, g.reshape(1, d))


def _post_kernel(h_ref, y_ref, g_ref, o_ref):
    y = y_ref[...]
    r = lax.rsqrt(jnp.mean(y * y, axis=-1, keepdims=True) + EPS)
    o_ref[...] = h_ref[...] + y * r * g_ref[...]


def _residual_rmsnorm(h2d, y2d, g, tr=320):
    m, d = h2d.shape
    return pl.pallas_call(
        _post_kernel,
        grid=(m // tr,),
        in_specs=[pl.BlockSpec((tr, d), lambda i: (i, 0)), pl.BlockSpec((tr, d), lambda i: (i, 0)),
                  pl.BlockSpec((1, d), lambda i: (0, 0))],
        out_specs=pl.BlockSpec((tr, d), lambda i: (i, 0)),
        out_shape=jax.ShapeDtypeStruct((m, d), jnp.float32),
        compiler_params=pltpu.CompilerParams(vmem_limit_bytes=V7X_VMEM_LIMIT),
        name="residual_rmsnorm_post",
    )(h2d, y2d, g.reshape(1, d))


def _matmul_kernel(x_ref, w_ref, o_ref):
    o_ref[...] = jnp.dot(x_ref[...], w_ref[...].astype(jnp.bfloat16),
                         preferred_element_type=jnp.float32).astype(o_ref.dtype)


def _matmul(x, w_stack, layer, tm=1040, tn=512, out_dtype=jnp.float32):
    m, kd = x.shape
    _, _, n = w_stack.shape
    assert m % tm == 0 and n % tn == 0
    return pl.pallas_call(
        _matmul_kernel,
        grid=(m // tm, n // tn),
        in_specs=[pl.BlockSpec((tm, kd), lambda i, j: (i, 0)),
                  pl.BlockSpec((None, kd, tn), lambda i, j: (layer, 0, j))],
        out_specs=pl.BlockSpec((tm, tn), lambda i, j: (i, j)),
        out_shape=jax.ShapeDtypeStruct((m, n), out_dtype),
        compiler_params=pltpu.CompilerParams(
            dimension_semantics=("parallel", "parallel"), vmem_limit_bytes=V7X_VMEM_LIMIT),
        name="projection_matmul",
    )(x, w_stack)


FFT_N1 = 128
FFT_N2 = 65
FFT_N = FFT_N1 * FFT_N2
N2P = 80
SLAB = 264
TC = 128
MID_GROUP = 16
B_GROUP = 13


def _dft_constants():
    a = np.arange(FFT_N1)
    c = np.arange(FFT_N1)
    w1 = np.exp(-2j * np.pi * np.outer(c, a) / FFT_N1)
    w1r, w1i = w1.real, w1.imag
    half = FFT_N1 // 2
    f1 = np.block([[w1r[:, :half], -w1i[:, :half]], [w1i[:, :half], w1r[:, :half]]])
    f1_real = np.concatenate([w1r, w1i], axis=0)
    g = np.conj(w1).T
    gr, gi = g.real[:half], g.imag[:half]
    i2 = np.block([[gr, -gi], [gi, gr]])
    b = np.arange(FFT_N2)
    d = np.arange(FFT_N2)
    w2 = np.exp(-2j * np.pi * np.outer(d, b) / FFT_N2)
    tw = np.exp(-2j * np.pi * np.outer(c, b) / FFT_N)
    m = w2[None, :, :] * tw[:, None, :]
    ab = np.zeros((FFT_N1, 2 * N2P, 128), np.float64)
    ab[:, :FFT_N2, :FFT_N2] = m.real
    ab[:, N2P:N2P + FFT_N2, :FFT_N2] = m.imag
    bf = jnp.bfloat16
    return (jnp.asarray(f1, bf), jnp.asarray(f1_real, bf), jnp.asarray(i2, bf), jnp.asarray(ab, bf))


def _middle_stage_inputs(y_ref, c):
    yr = y_ref[pl.ds(c, N2P, stride=SLAB), :]
    yi = y_ref[pl.ds(FFT_N1 + c, N2P, stride=SLAB), :]
    return jnp.concatenate([yr, yi], axis=1).astype(jnp.bfloat16)


def _forward_middle(ab, y2):
    y2 = jnp.concatenate([y2, jnp.zeros((128 - N2P, 2 * TC), jnp.bfloat16)], axis=0)
    q = jnp.dot(ab, y2, preferred_element_type=jnp.float32)
    zr = q[:N2P, :TC] - q[N2P:, TC:]
    zi = q[N2P:, :TC] + q[:N2P, TC:]
    return zr, zi


def _spectrum_kernel(h_ref, w3f_ref, w3b_ref, delta_ref, f1_ref, ab_ref, k_ref, kt_ref, y_ref):
    w3 = jnp.concatenate([w3f_ref[...], w3b_ref[...]], axis=0).astype(jnp.bfloat16)
    delta = delta_ref[...]
    rows = FFT_N // 13

    def fill(i, carry):
        r0 = pl.multiple_of(i * rows, 8)
        kt = jnp.dot(h_ref[pl.ds(r0, rows), :], w3, preferred_element_type=jnp.float32)
        j = lax.broadcasted_iota(jnp.int32, (rows, TC), 0) + r0
        lag = jnp.where(j < L_TOK, j, FFT_N - j).astype(jnp.float32)
        kt_ref[pl.ds(r0, rows), :] = kt * jnp.exp(-(lag * (1.0 / (L_TOK - 1))) * delta)
        return carry

    lax.fori_loop(0, 13, fill, 0)
    y_ref[pl.ds(FFT_N2 * SLAB, (N2P - FFT_N2) * SLAB), :] = jnp.zeros(((N2P - FFT_N2) * SLAB, TC), jnp.float32)

    def stage1(g, carry):
        bs = [g * B_GROUP + i for i in range(B_GROUP)]
        x = jnp.concatenate([kt_ref[pl.ds(b, FFT_N1, stride=FFT_N2), :] for b in bs], axis=1)
        y = jnp.dot(f1_ref[...], x.astype(jnp.bfloat16), preferred_element_type=jnp.float32)
        for i, b in enumerate(bs):
            y_ref[pl.ds(pl.multiple_of(b * SLAB, 8), 2 * FFT_N1), :] = y[:, i * TC:(i + 1) * TC]
        return carry

    lax.fori_loop(0, FFT_N2 // B_GROUP, stage1, 0)

    def stage2(g, carry):
        cs = [g * MID_GROUP + i for i in range(MID_GROUP)]
        zs = [_forward_middle(ab_ref[c], _middle_stage_inputs(y_ref, c)) for c in cs]
        for c, (zr, zi) in zip(cs, zs):
            k_ref[c] = (jnp.concatenate([zr, zi], axis=1) * (1.0 / FFT_N)).astype(jnp.bfloat16)
        return carry

    lax.fori_loop(0, FFT_N1 // MID_GROUP, stage2, 0)


def _filter_spectrum(h2, w3, deltas, consts):
    _, f1_real, _, ab = consts
    nt = HY_WIDTH // TC
    return pl.pallas_call(
        _spectrum_kernel,
        grid=(HY_ORDER, nt),
        in_specs=[
            pl.BlockSpec((FFT_N, 128), lambda n, i: (0, 0)),
            pl.BlockSpec((64, TC), lambda n, i: (0, n * nt + i)),
            pl.BlockSpec((64, TC), lambda n, i: (0, (HY_ORDER + n) * nt + i)),
            pl.BlockSpec((1, TC), lambda n, i: (0, i)),
            pl.BlockSpec((2 * FFT_N1, FFT_N1), lambda n, i: (0, 0)),
            pl.BlockSpec((FFT_N1, 2 * N2P, 128), lambda n, i: (0, 0, 0)),
        ],
        out_specs=pl.BlockSpec((None, None, FFT_N1, N2P, 2 * TC), lambda n, i: (n, i, 0, 0, 0)),
        out_shape=jax.ShapeDtypeStruct((HY_ORDER, nt, FFT_N1, N2P, 2 * TC), jnp.bfloat16),
        scratch_shapes=[pltpu.VMEM((FFT_N, TC), jnp.float32), pltpu.VMEM((N2P * SLAB, TC), jnp.float32)],
        compiler_params=pltpu.CompilerParams(
            dimension_semantics=("parallel", "parallel"), vmem_limit_bytes=V7X_VMEM_LIMIT),
        name="hyena_filter_spectrum",
    )(h2, w3, w3, deltas, f1_real, ab)


def _conv_kernel(u_ref, k_ref, f1_ref, i2_ref, ab_ref, o_ref, y_ref):
    y_ref[pl.ds(FFT_N2 * SLAB, (N2P - FFT_N2) * SLAB), :] = jnp.zeros(((N2P - FFT_N2) * SLAB, TC), jnp.float32)
    half = FFT_N1 // 2

    def stage1(g, carry):
        bs = [g * B_GROUP + i for i in range(B_GROUP)]
        x = jnp.concatenate(
            [jnp.concatenate([u_ref.at[ri][pl.ds(b, half, stride=FFT_N2), :] for b in bs], axis=1) for ri in range(2)],
            axis=0)
        y = jnp.dot(f1_ref[...], x.astype(jnp.bfloat16), preferred_element_type=jnp.float32)
        for i, b in enumerate(bs):
            y_ref[pl.ds(pl.multiple_of(b * SLAB, 8), 2 * FFT_N1), :] = y[:, i * TC:(i + 1) * TC]
        return carry

    lax.fori_loop(0, FFT_N2 // B_GROUP, stage1, 0)

    def inverse_middle(c, z):
        ab = ab_ref[c]
        zr, zi = z
        kk = k_ref[c].astype(jnp.float32)
        kr, ki = kk[:, :TC], kk[:, TC:]
        pr = zr * kr - zi * ki
        pi = zr * ki + zi * kr
        rhs = jnp.concatenate([jnp.concatenate([pr, pi], axis=1), jnp.concatenate([pi, -pr], axis=1)], axis=0)
        return lax.dot_general(ab, rhs.astype(jnp.bfloat16), (((0,), (0,)), ((), ())),
                               preferred_element_type=jnp.float32)

    def middle(g, carry):
        cs = [g * MID_GROUP + i for i in range(MID_GROUP)]
        zs = [_forward_middle(ab_ref[c], _middle_stage_inputs(y_ref, c)) for c in cs]
        outs = [inverse_middle(c, z) for c, z in zip(cs, zs)]
        for c, v in zip(cs, outs):
            y_ref[pl.ds(c, N2P, stride=SLAB), :] = v[:N2P, :TC]
            y_ref[pl.ds(FFT_N1 + c, N2P, stride=SLAB), :] = v[:N2P, TC:]
        return carry

    lax.fori_loop(0, FFT_N1 // MID_GROUP, middle, 0)

    def stage4(g, carry):
        bs = [g * B_GROUP + i for i in range(B_GROUP)]
        v = jnp.concatenate([y_ref[pl.ds(pl.multiple_of(b * SLAB, 8), 2 * FFT_N1), :] for b in bs], axis=1)
        x = jnp.dot(i2_ref[...], v.astype(jnp.bfloat16), preferred_element_type=jnp.float32)
        for i, b in enumerate(bs):
            o_ref.at[0][pl.ds(b, half, stride=FFT_N2), :] = x[:half, i * TC:(i + 1) * TC]
            o_ref.at[1][pl.ds(b, half, stride=FFT_N2), :] = x[half:, i * TC:(i + 1) * TC]
        return carry

    lax.fori_loop(0, FFT_N2 // B_GROUP, stage4, 0)


def _long_conv(u, col0, kspec, order, consts):
    f1, _, i2, ab = consts
    bsz, lp, _ = u.shape
    assert bsz == 2 and lp == LP and col0 % TC == 0
    nt = HY_WIDTH // TC
    off = col0 // TC
    return pl.pallas_call(
        _conv_kernel,
        grid=(nt,),
        in_specs=[
            pl.BlockSpec((bsz, lp, TC), lambda i: (0, 0, off + i)),
            pl.BlockSpec((None, None, FFT_N1, N2P, 2 * TC), lambda i: (order, i, 0, 0, 0)),
            pl.BlockSpec((2 * FFT_N1, FFT_N1), lambda i: (0, 0)),
            pl.BlockSpec((FFT_N1, 2 * FFT_N1), lambda i: (0, 0)),
            pl.BlockSpec((FFT_N1, 2 * N2P, 128), lambda i: (0, 0, 0)),
        ],
        out_specs=pl.BlockSpec((bsz, lp, TC), lambda i: (0, 0, i)),
        out_shape=jax.ShapeDtypeStruct((bsz, lp, HY_WIDTH), jnp.float32),
        scratch_shapes=[pltpu.VMEM((N2P * SLAB, TC), jnp.float32)],
        compiler_params=pltpu.CompilerParams(dimension_semantics=("parallel",), vmem_limit_bytes=V7X_VMEM_LIMIT),
        name="hyena_long_conv",
    )(u, kspec, f1, i2, ab)


def _short_conv_kernel(u_ref, w_ref, o_ref):
    u = u_ref[...]
    w = w_ref[...]
    y = pltpu.roll(u, 1, 0) * w[0:1] + u * w[1:2] + pltpu.roll(u, LP - 1, 0) * w[2:3]
    row = lax.broadcasted_iota(jnp.int32, y.shape, 0)
    o_ref[...] = jnp.where(row >= PAD, y, 0.0)


def _short_conv(p3, w_short):
    bsz = p3.shape[0]
    width = 3 * HY_WIDTH
    return pl.pallas_call(
        _short_conv_kernel,
        grid=(bsz, width // TC),
        in_specs=[pl.BlockSpec((None, LP, TC), lambda b, i: (b, 0, i)), pl.BlockSpec((3, TC), lambda b, i: (0, i))],
        out_specs=pl.BlockSpec((None, LP, TC), lambda b, i: (b, 0, i)),
        out_shape=jax.ShapeDtypeStruct((bsz, LP, width), jnp.float32),
        compiler_params=pltpu.CompilerParams(
            dimension_semantics=("parallel", "parallel"), vmem_limit_bytes=V7X_VMEM_LIMIT),
        name="hyena_short_conv",
    )(p3, w_short.T)


GATE_ROWS = 520
GATE_COLS = 512


def _gate1_kernel(v_ref, x_ref, y_ref, b_ref, o_ref):
    o_ref[...] = x_ref[...] * (y_ref[...] + v_ref[...] * b_ref[...])


def _gate2_kernel(z_ref, x_ref, y_ref, b_ref, g_ref, o_ref):
    g = g_ref[...]
    z = x_ref[...] * (y_ref[...] + z_ref[...] * b_ref[...])
    o_ref[...] = (z * (g * jax.nn.sigmoid(g))).astype(o_ref.dtype)


def _hyena_gate1(uc, y, bias):
    bsz = uc.shape[0]
    nc = HY_WIDTH // GATE_COLS
    blk = lambda off: pl.BlockSpec((None, GATE_ROWS, GATE_COLS), lambda b, r, c: (b, r, off + c))
    return pl.pallas_call(
        _gate1_kernel,
        grid=(bsz, LP // GATE_ROWS, nc),
        in_specs=[blk(0), blk(nc), blk(0), pl.BlockSpec((1, GATE_COLS), lambda b, r, c: (0, c))],
        out_specs=blk(0),
        out_shape=jax.ShapeDtypeStruct((bsz, LP, HY_WIDTH), jnp.float32),
        compiler_params=pltpu.CompilerParams(
            dimension_semantics=("parallel", "parallel", "parallel"), vmem_limit_bytes=V7X_VMEM_LIMIT),
        name="hyena_gate1",
    )(uc, uc, y, bias.reshape(1, HY_WIDTH))


def _hyena_gate2(z1, uc, y, bias, p3):
    bsz = uc.shape[0]
    nc = HY_WIDTH // GATE_COLS
    blk = lambda off: pl.BlockSpec((None, GATE_ROWS, GATE_COLS), lambda b, r, c: (b, r, off + c))
    return pl.pallas_call(
        _gate2_kernel,
        grid=(bsz, LP // GATE_ROWS, nc),
        in_specs=[blk(0), blk(2 * nc), blk(0), pl.BlockSpec((1, GATE_COLS), lambda b, r, c: (0, c)), blk(3 * nc)],
        out_specs=blk(0),
        out_shape=jax.ShapeDtypeStruct((bsz, LP, HY_WIDTH), jnp.bfloat16),
        compiler_params=pltpu.CompilerParams(
            dimension_semantics=("parallel", "parallel", "parallel"), vmem_limit_bytes=V7X_VMEM_LIMIT),
        name="hyena_gate2",
    )(z1, uc, y, bias.reshape(1, HY_WIDTH), p3)


def _hyena_hidden(w1, b1, fr1, w2, b2, fr2):
    f32 = jnp.float32
    l = L_TOK
    hi = lax.Precision.HIGHEST
    t = jnp.linspace(0.0, 1.0, l, dtype=f32)[:, None]
    w = 2.0 * math.pi * jnp.arange(l, dtype=f32)[:, None] / l
    bands = jnp.linspace(1e-4, HY_BANDS - 1, HY_BANDS, dtype=f32)[None, :]
    z = jnp.concatenate([t, jnp.cos(bands * w), -jnp.sin(bands * w)], axis=-1)
    hid = jnp.sin(fr1 * (jnp.dot(z, w1, precision=hi) + b1))
    hid = jnp.sin(fr2 * (jnp.dot(hid, w2, precision=hi) + b2))
    j = np.arange(FFT_N)
    lag = np.minimum(np.where(j < l, j, FFT_N - j), l - 1)
    fwd = (j < l).astype(np.float32)[:, None]
    bwd = ((j > FFT_N - l) | (j == 0)).astype(np.float32)[:, None]
    rows = hid[lag]
    return jnp.concatenate([rows * fwd, rows * bwd], axis=1).astype(jnp.bfloat16)


def _odd_mix(p3, w_short, w1, b1, fr1, w2, b2, fr2, w3, bias, consts):
    min_decay = math.log(HY_TARGET) / HY_SLOW_DECAY
    max_decay = math.log(HY_TARGET) / HY_FAST_DECAY
    deltas = jnp.abs(jnp.linspace(min_decay, max_decay, HY_WIDTH, dtype=jnp.float32)).reshape(1, HY_WIDTH)
    kspec = _filter_spectrum(_hyena_hidden(w1, b1, fr1, w2, b2, fr2), w3, deltas, consts)
    uc = _short_conv(p3, w_short)
    y1 = _long_conv(uc, 0, kspec, 0, consts)
    z1 = _hyena_gate1(uc, y1, bias[0])
    y2 = _long_conv(z1, 0, kspec, 1, consts)
    return _hyena_gate2(z1, uc, y2, bias[1], p3)


HG_HB = 4
HG_LEVELS = (32, 16, 8, 4, 2, 1)


def _cumsum_rows(g, row, rev):
    for sh in (1, 2, 4, 8, 16, 32):
        if rev:
            g = g + jnp.where(row < HG_CHUNK - sh, pltpu.roll(g, HG_CHUNK - sh, 0), 0.0)
        else:
            g = g + jnp.where(row >= sh, pltpu.roll(g, sh, 0), 0.0)
    return g


def _level_reference(gc, row, m, rev):
    if m >= 4:
        g3 = gc.reshape(HG_CHUNK // (2 * m), 2 * m, HEAD_DIM)
        ref = g3[:, m:m + 1, :] if rev else g3[:, m - 1:m, :]
        return jnp.broadcast_to(ref, g3.shape).reshape(HG_CHUNK, HEAD_DIM)
    r = row & (2 * m - 1)
    target = m if rev else m - 1
    out = gc
    for src in range(2 * m):
        if src != target:
            out = jnp.where(r == src, pltpu.roll(gc, (src - target) % HG_CHUNK, 0), out)
    return out


def _hgrn_gates(q_raw, f_raw, la, lc, omlb, row, rev):
    q = q_raw * jax.nn.sigmoid(q_raw) * (HEAD_DIM ** -0.5)
    e = jnp.exp(-jnp.abs(f_raw))
    log_sig = jnp.minimum(f_raw, 0.0) - jnp.log1p(e)
    k = omlb * (jnp.where(f_raw >= 0.0, e, 1.0) / (1.0 + e))
    b = lc + log_sig
    g = jnp.maximum(la, b) + jnp.log1p(jnp.exp(-jnp.abs(la - b)))
    return q, k, _cumsum_rows(g, row, rev)


def _hgrn_level_operands(q, k, gc, row, rev):
    ops = []
    for m in HG_LEVELS:
        gref = _level_reference(gc, row, m, rev)
        upper = (row & m) != 0
        is_query = jnp.logical_not(upper) if rev else upper
        d = gc - gref
        ops.append((jnp.where(is_query, q, k) * jnp.exp(jnp.where(is_query, d, -d))).astype(jnp.bfloat16))
    return ops


def _hgrn_level_id(rev):
    t = lax.broadcasted_iota(jnp.int32, (HG_CHUNK, HG_CHUNK), 0)
    s = lax.broadcasted_iota(jnp.int32, (HG_CHUNK, HG_CHUNK), 1)
    vis = (s >= t) if rev else (s <= t)
    x = t ^ s
    lvl = jnp.full((HG_CHUNK, HG_CHUNK), len(HG_LEVELS), jnp.int32)
    for i, m in enumerate(HG_LEVELS):
        lvl = jnp.where((x & m) != 0, jnp.minimum(lvl, i), lvl)
    return jnp.where(vis, lvl, -1)


def _hgrn_kernel(qf_ref, if_ref, ff_ref, qb_ref, ib_ref, fb_ref, la_ref, lc_ref, om_ref, of_ref, ob_ref, s_ref):
    f32, bf = jnp.float32, jnp.bfloat16
    nt = (((1,), (1,)), ((), ()))
    tn = (((0,), (0,)), ((), ()))

    @pl.when(pl.program_id(2) == 0)
    def _():
        s_ref[...] = jnp.zeros_like(s_ref)

    row = lax.broadcasted_iota(jnp.int32, (HG_CHUNK, HEAD_DIM), 0)
    refs = ((qf_ref, if_ref, ff_ref, of_ref), (qb_ref, ib_ref, fb_ref, ob_ref))
    inst = [(d, h) for d in range(2) for h in range(HG_HB)]
    lvl = [_hgrn_level_id(False), _hgrn_level_id(True)]

    pre = []
    for d, h in inst:
        q_ref, i_ref, f_ref, _ = refs[d]
        sl = slice(h * HEAD_DIM, (h + 1) * HEAD_DIM)
        q, k, gc = _hgrn_gates(q_ref[:, sl], f_ref[:, sl], la_ref[d:d + 1, sl], lc_ref[d:d + 1, sl],
                               om_ref[d:d + 1, sl], row, d == 1)
        g_end = gc[0:1, :] if d == 1 else gc[HG_CHUNK - 1:HG_CHUNK, :]
        pre.append(dict(ops=_hgrn_level_operands(q, k, gc, row, d == 1), qk=(q.astype(bf), k.astype(bf)),
                        qe=(q * jnp.exp(gc)).astype(bf), kd=(k * jnp.exp(g_end - gc)).astype(bf),
                        decay=jnp.exp(g_end), v=i_ref[:, sl].astype(bf)))

    attn = []
    for (d, h), pr in zip(inst, pre):
        prods = [lax.dot_general(x, x, nt, preferred_element_type=f32) for x in pr["ops"]]
        prods.append(lax.dot_general(pr["qk"][0], pr["qk"][1], nt, preferred_element_type=f32))
        a = jnp.zeros((HG_CHUNK, HG_CHUNK), f32)
        for i, pmat in enumerate(prods):
            a = jnp.where(lvl[d] == i, pmat, a)
        attn.append(a.astype(bf))

    for (d, h), pr, a in zip(inst, pre, attn):
        o_ref = refs[d][3]
        sl = slice(h * HEAD_DIM, (h + 1) * HEAD_DIM)
        state_t = s_ref[d, h]
        o = jnp.dot(a, pr["v"], preferred_element_type=f32)
        o_ref[:, sl] = o + lax.dot_general(pr["qe"], state_t.astype(bf), nt, preferred_element_type=f32)
        s_ref[d, h] = pr["decay"] * state_t + lax.dot_general(pr["v"], pr["kd"], tn, preferred_element_type=f32)


def _hgrn2(p, col0, lb):
    bsz, lp, _ = p.shape
    nchunk = lp // HG_CHUNK
    cw = HG_HB * HEAD_DIM
    ng = HG_WIDTH // cw
    off = col0 // cw
    la = jnp.log(lb)
    lc = jnp.log1p(-lb)
    om = 1.0 - lb
    fwd = lambda seg: pl.BlockSpec((None, HG_CHUNK, cw), lambda b, g, c: (b, c, off + seg * ng + g))
    bwd = lambda seg: pl.BlockSpec((None, HG_CHUNK, cw), lambda b, g, c: (b, nchunk - 1 - c, off + seg * ng + g))
    par = pl.BlockSpec((2, cw), lambda b, g, c: (0, g))
    return pl.pallas_call(
        _hgrn_kernel,
        grid=(bsz, ng, nchunk),
        in_specs=[fwd(0), fwd(1), fwd(2), bwd(0), bwd(1), bwd(3), par, par, par],
        out_specs=[pl.BlockSpec((None, HG_CHUNK, cw), lambda b, g, c: (b, c, g)),
                   pl.BlockSpec((None, HG_CHUNK, cw), lambda b, g, c: (b, nchunk - 1 - c, g))],
        out_shape=[jax.ShapeDtypeStruct((bsz, lp, HG_WIDTH), jnp.float32)] * 2,
        scratch_shapes=[pltpu.VMEM((2, HG_HB, HEAD_DIM, HEAD_DIM), jnp.float32)],
        compiler_params=pltpu.CompilerParams(
            dimension_semantics=("parallel", "parallel", "arbitrary"), vmem_limit_bytes=V7X_VMEM_LIMIT),
        name="hgrn2_bidirectional",
    )(p, p, p, p, p, p, la, lc, om)


EG_ROWS = 320


def _even_gate_kernel(oa_ref, of_ref, ob_ref, ga_ref, gb_ref, gain_ref, y_ref):
    ga = ga_ref[...]
    y_ref[:, :NA_WIDTH] = (oa_ref[...] * (ga * jax.nn.sigmoid(ga))).astype(y_ref.dtype)
    for h in range(HG_HEADS):
        sl = slice(h * HEAD_DIM, (h + 1) * HEAD_DIM)
        oh = of_ref[:, sl] + ob_ref[:, sl]
        oh = oh * lax.rsqrt(jnp.mean(oh * oh, axis=-1, keepdims=True) + EPS) * gain_ref[:, sl]
        gb = gb_ref[:, sl]
        y_ref[:, NA_WIDTH + h * HEAD_DIM:NA_WIDTH + (h + 1) * HEAD_DIM] = (
            oh * (gb * jax.nn.sigmoid(gb))).astype(y_ref.dtype)


def _even_gate(oa, o_fwd, o_bwd, p, gate_col0, norm_gain):
    bsz, lp, _ = oa.shape
    assert gate_col0 % NA_WIDTH == 0
    goff = gate_col0 // NA_WIDTH
    half = lambda off: pl.BlockSpec((None, EG_ROWS, NA_WIDTH), lambda b, r: (b, r, off))
    return pl.pallas_call(
        _even_gate_kernel,
        grid=(bsz, lp // EG_ROWS),
        in_specs=[half(0), half(0), half(0), half(goff), half(goff + 1),
                  pl.BlockSpec((1, HG_WIDTH), lambda b, r: (0, 0))],
        out_specs=pl.BlockSpec((None, EG_ROWS, D_MODEL), lambda b, r: (b, r, 0)),
        out_shape=jax.ShapeDtypeStruct((bsz, lp, D_MODEL), jnp.bfloat16),
        compiler_params=pltpu.CompilerParams(
            dimension_semantics=("parallel", "parallel"), vmem_limit_bytes=V7X_VMEM_LIMIT),
        name="even_gate",
    )(oa, o_fwd, o_bwd, p, p, norm_gain.reshape(1, HG_WIDTH))


NA_MASKED = -1e30
NA_GROUP = 4


def _na_bias_table(rpb):
    cols = np.arange(GRID_W)
    c0 = np.clip(cols - NA_KW // 2, 0, GRID_W - NA_KW)
    col_ok = (cols[None, :] >= c0[:, None]) & (cols[None, :] < c0[:, None] + NA_KW)
    col_idx = np.clip(cols[None, :] - cols[:, None] + NA_KW - 1, 0, 2 * NA_KW - 2)
    return jnp.where(col_ok[None, None], rpb.astype(jnp.float32)[:, :, col_idx], NA_MASKED)


def _na_kernel(q_ref, k_ref, v_ref, t_ref, mb_ref, o_ref):
    f32, bf = jnp.float32, jnp.bfloat16
    rows = (q_ref.shape[0] - HG_CHUNK) // GRID_W
    scale = HEAD_DIM ** -0.5
    nt = (((1,), (1,)), ((), ()))
    km = k_ref[PAD:PAD + N_META, :].astype(bf)
    vm = v_ref[PAD:PAD + N_META, :].astype(bf)
    mb = mb_ref[...]
    o_ref[0:PAD, :] = jnp.zeros((PAD, HEAD_DIM), f32)

    qm = q_ref[PAD:PAD + N_META, :].astype(bf)
    s = lax.dot_general(qm, km, nt, preferred_element_type=f32) * scale + mb
    e = jnp.exp(s - jnp.max(s, axis=-1, keepdims=True))
    o_ref[PAD:PAD + N_META, :] = (jnp.dot(e.astype(bf), vm, preferred_element_type=f32)
                                  / jnp.sum(e, axis=-1, keepdims=True))

    def row_group(g, carry):
        rs = [g * NA_GROUP + i for i in range(NA_GROUP)]
        scores = []
        for r in rs:
            r0 = jnp.clip(r - NA_KH // 2, 0, rows - NA_KH)
            q = q_ref[pl.ds(pl.multiple_of(HG_CHUNK + GRID_W * r, GRID_W), GRID_W), :].astype(bf)
            k0 = pl.multiple_of(HG_CHUNK + GRID_W * r0, GRID_W)
            kw = k_ref[pl.ds(k0, NA_KH * GRID_W), :].astype(bf)
            s_win = lax.dot_general(q, kw, nt, preferred_element_type=f32)
            s_meta = lax.dot_general(q, km, nt, preferred_element_type=f32)
            scores.append((r, r0, k0, s_win, s_meta))
        probs = []
        for r, r0, k0, s_win, s_meta in scores:
            bias = jnp.concatenate([t_ref[r0 + i - r + NA_KH - 1] for i in range(NA_KH)], axis=1)
            s_win = s_win * scale + bias
            s_meta = s_meta * scale + mb
            m = jnp.maximum(jnp.max(s_win, axis=-1, keepdims=True), jnp.max(s_meta, axis=-1, keepdims=True))
            pw = jnp.exp(s_win - m)
            pm = jnp.exp(s_meta - m)
            den = jnp.sum(pw, axis=-1, keepdims=True) + jnp.sum(pm, axis=-1, keepdims=True)
            probs.append((r, k0, pw.astype(bf), pm.astype(bf), den))
        for r, k0, pw, pm, den in probs:
            vw = v_ref[pl.ds(k0, NA_KH * GRID_W), :].astype(bf)
            o = jnp.dot(pw, vw, preferred_element_type=f32) + jnp.dot(pm, vm, preferred_element_type=f32)
            o_ref[pl.ds(pl.multiple_of(HG_CHUNK + GRID_W * r, GRID_W), GRID_W), :] = o / den
        return carry

    lax.fori_loop(0, rows // NA_GROUP, row_group, 0)


def _natten(p, rpb, meta_bias):
    bsz, lp, _ = p.shape
    heads = NA_WIDTH // HEAD_DIM
    blk = lambda seg: pl.BlockSpec((None, lp, HEAD_DIM), lambda b, h: (b, 0, seg * heads + h))
    table = _na_bias_table(rpb)
    return pl.pallas_call(
        _na_kernel,
        grid=(bsz, heads),
        in_specs=[blk(0), blk(1), blk(2),
                  pl.BlockSpec((None, 2 * NA_KH - 1, GRID_W, GRID_W), lambda b, h: (h, 0, 0, 0)),
                  pl.BlockSpec((None, 1, N_META), lambda b, h: (h, 0, 0))],
        out_specs=pl.BlockSpec((None, lp, HEAD_DIM), lambda b, h: (b, 0, h)),
        out_shape=jax.ShapeDtypeStruct((bsz, lp, NA_WIDTH), jnp.float32),
        compiler_params=pltpu.CompilerParams(
            dimension_semantics=("parallel", "parallel"), vmem_limit_bytes=V7X_VMEM_LIMIT),
        name="neighbourhood_attention",
    )(p, p, p, table, meta_bias.astype(jnp.float32).reshape(heads, 1, N_META))


def _even_mix(p, rpb, meta_bias, lb, norm_gain):
    oa = _natten(p, rpb, meta_bias)
    o_fwd, o_bwd = _hgrn2(p, 3 * NA_WIDTH, lb)
    return _even_gate(oa, o_fwd, o_bwd, p, 3 * NA_WIDTH + 4 * HG_WIDTH, norm_gain)


def kernel(x, meta_tokens, norm_pre, norm_post, ev_w_in, ev_w_out, na_rpb, na_meta_bias, hg_lower, hg_norm,
           od_w_in, od_w_out, hy_short, hy_ffn_w1, hy_ffn_b1, hy_ffn_freq1, hy_ffn_w2, hy_ffn_b2, hy_ffn_freq2,
           hy_ffn_w3, hy_bias):
    b = x.shape[0]
    depth = norm_pre.shape[0]
    f32 = jnp.float32
    lb_all = jnp.cumsum(jax.nn.softmax(hg_lower.astype(f32), axis=0), axis=0)
    lb_all = lb_all - lb_all[:1]
    meta = jnp.broadcast_to(meta_tokens.astype(f32)[None], (b, N_META, D_MODEL))
    h = jnp.concatenate([jnp.zeros((b, PAD, D_MODEL), f32), meta, x.astype(f32)], axis=1)
    h = h.reshape(b * LP, D_MODEL)
    consts = _dft_constants()
    for layer in range(depth):
        j = layer // 2
        hn = _rmsnorm(h, norm_pre[layer], jnp.bfloat16)
        if layer % 2 == 0:
            p = _matmul(hn, ev_w_in, j).reshape(b, LP, -1)
            y = _even_mix(p, na_rpb[j], na_meta_bias[j], lb_all[j], hg_norm[j]).reshape(b * LP, D_MODEL)
            w_out = ev_w_out
        else:
            p3 = _matmul(hn, od_w_in, j).reshape(b, LP, -1)
            y = _odd_mix(p3, hy_short[j], hy_ffn_w1[j], hy_ffn_b1[j], hy_ffn_freq1[j], hy_ffn_w2[j],
                         hy_ffn_b2[j], hy_ffn_freq2[j], hy_ffn_w3[j], hy_bias[j], consts)
            y = y.reshape(b * LP, D_MODEL)
            w_out = od_w_out
        out = _matmul(y, w_out, j)
        h = _residual_rmsnorm(h, out, norm_post[layer])
    return h.reshape(b, LP, D_MODEL)[:, PAD + N_META:].astype(x.dtype)
```

```python
import functools
import math

import jax
import jax.numpy as jnp
import numpy as np
from jax import lax
from jax.experimental import pallas as pl
from jax.experimental.pallas import tpu as pltpu

D_MODEL = 4096
SEQ = 4096
N_META = 16
GRID_W = 64
HEAD_DIM = 128
NA_WIDTH = D_MODEL // 2
NA_KH = 8
NA_KW = 16
HG_WIDTH = D_MODEL // 2
HG_HEADS = HG_WIDTH // HEAD_DIM
HG_CHUNK = 64
HY_WIDTH = D_MODEL
HY_ORDER = 2
HY_EMB = 33
HY_BANDS = (HY_EMB - 1) // 2
HY_FAST_DECAY = 0.3
HY_SLOW_DECAY = 1.5
HY_TARGET = 1e-2
EPS = 1e-6

PAD = HG_CHUNK - N_META
LP = PAD + N_META + SEQ
L_TOK = N_META + SEQ

V7X_VMEM_LIMIT = 56 * 1024 * 1024


def _rmsnorm_kernel(x_ref, g_ref, o_ref):
    x = x_ref[...]
    r = lax.rsqrt(jnp.mean(x * x, axis=-1, keepdims=True) + EPS)
    o_ref[...] = (x * r * g_ref[...]).astype(o_ref.dtype)


def _rmsnorm(x2d, g, out_dtype, tr=320):
    m, d = x2d.shape
    return pl.pallas_call(
        _rmsnorm_kernel,
        grid=(m // tr,),
        in_specs=[pl.BlockSpec((tr, d), lambda i: (i, 0)), pl.BlockSpec((1, d), lambda i: (0, 0))],
        out_specs=pl.BlockSpec((tr, d), lambda i: (i, 0)),
        out_shape=jax.ShapeDtypeStruct((m, d), out_dtype),
        compiler_params=pltpu.CompilerParams(vmem_limit_bytes=V7X_VMEM_LIMIT),
        name="rmsnorm_pre",
    )(x2d, g.reshape(1, d))


def _post_kernel(h_ref, y_ref, g_ref, o_ref):
    y = y_ref[...]
    r = lax.rsqrt(jnp.mean(y * y, axis=-1, keepdims=True) + EPS)
    o_ref[...] = h_ref[...] + y * r * g_ref[...]


def _post_pre_kernel(h_ref, y_ref, g_ref, gn_ref, o_ref, n_ref):
    y = y_ref[...]
    r = lax.rsqrt(jnp.mean(y * y, axis=-1, keepdims=True) + EPS)
    h = h_ref[...] + y * r * g_ref[...]
    o_ref[...] = h
    rn = lax.rsqrt(jnp.mean(h * h, axis=-1, keepdims=True) + EPS)
    n_ref[...] = (h * rn * gn_ref[...]).astype(n_ref.dtype)


def _residual_rmsnorm(h2d, y2d, g, g_next=None, tr=320):
    m, d = h2d.shape
    row = pl.BlockSpec((tr, d), lambda i: (i, 0))
    vec = pl.BlockSpec((1, d), lambda i: (0, 0))
    params = pltpu.CompilerParams(dimension_semantics=("parallel",), vmem_limit_bytes=V7X_VMEM_LIMIT)
    if g_next is None:
        return pl.pallas_call(
            _post_kernel, grid=(m // tr,), in_specs=[row, row, vec], out_specs=row,
            out_shape=jax.ShapeDtypeStruct((m, d), jnp.float32), compiler_params=params,
            name="residual_rmsnorm_post",
        )(h2d, y2d, g.reshape(1, d))
    return pl.pallas_call(
        _post_pre_kernel, grid=(m // tr,), in_specs=[row, row, vec, vec], out_specs=[row, row],
        out_shape=[jax.ShapeDtypeStruct((m, d), jnp.float32), jax.ShapeDtypeStruct((m, d), jnp.bfloat16)],
        compiler_params=params, name="residual_rmsnorm_post_pre",
    )(h2d, y2d, g.reshape(1, d), g_next.reshape(1, d))


def _matmul_kernel(x_ref, w_ref, o_ref):
    o_ref[...] = jnp.dot(x_ref[...], w_ref[...].astype(jnp.bfloat16),
                         preferred_element_type=jnp.float32).astype(o_ref.dtype)


def _matmul(x, w_stack, layer, tm=1040, tn=512, out_dtype=jnp.float32):
    m, kd = x.shape
    _, _, n = w_stack.shape
    assert m % tm == 0 and n % tn == 0
    return pl.pallas_call(
        _matmul_kernel,
        grid=(m // tm, n // tn),
        in_specs=[pl.BlockSpec((tm, kd), lambda i, j: (i, 0)),
                  pl.BlockSpec((None, kd, tn), lambda i, j: (layer, 0, j))],
        out_specs=pl.BlockSpec((tm, tn), lambda i, j: (i, j)),
        out_shape=jax.ShapeDtypeStruct((m, n), out_dtype),
        compiler_params=pltpu.CompilerParams(
            dimension_semantics=("parallel", "parallel"), vmem_limit_bytes=V7X_VMEM_LIMIT),
        name="projection_matmul",
    )(x, w_stack)


FFT_N1 = 128
FFT_N2 = 65
FFT_N = FFT_N1 * FFT_N2
N2P = 80
SLAB = 264
TC = 128
MID_GROUP = 16
B_GROUP = 13


def _dft_constants():
    a = np.arange(FFT_N1)
    c = np.arange(FFT_N1)
    w1 = np.exp(-2j * np.pi * np.outer(c, a) / FFT_N1)
    w1r, w1i = w1.real, w1.imag
    half = FFT_N1 // 2
    f1 = np.block([[w1r[:, :half], -w1i[:, :half]], [w1i[:, :half], w1r[:, :half]]])
    f1_real = np.concatenate([w1r, w1i], axis=0)
    g = np.conj(w1).T
    gr, gi = g.real[:half], g.imag[:half]
    i2 = np.block([[gr, -gi], [gi, gr]])
    b = np.arange(FFT_N2)
    d = np.arange(FFT_N2)
    w2 = np.exp(-2j * np.pi * np.outer(d, b) / FFT_N2)
    tw = np.exp(-2j * np.pi * np.outer(c, b) / FFT_N)
    m = w2[None, :, :] * tw[:, None, :]
    ab = np.zeros((FFT_N1, 2 * N2P, 128), np.float64)
    ab[:, :FFT_N2, :FFT_N2] = m.real
    ab[:, N2P:N2P + FFT_N2, :FFT_N2] = m.imag
    bf = jnp.bfloat16
    return (jnp.asarray(f1, bf), jnp.asarray(f1_real, bf), jnp.asarray(i2, bf), jnp.asarray(ab, bf))


def _middle_stage_inputs(y_ref, c):
    yr = y_ref[pl.ds(c, N2P, stride=SLAB), :]
    yi = y_ref[pl.ds(FFT_N1 + c, N2P, stride=SLAB), :]
    return jnp.concatenate([yr, yi], axis=1).astype(jnp.bfloat16)


def _forward_middle(ab, y2):
    y2 = jnp.concatenate([y2, jnp.zeros((128 - N2P, 2 * TC), jnp.bfloat16)], axis=0)
    q = jnp.dot(ab, y2, preferred_element_type=jnp.float32)
    zr = q[:N2P, :TC] - q[N2P:, TC:]
    zi = q[N2P:, :TC] + q[:N2P, TC:]
    return zr, zi


def _spectrum_kernel(h_ref, w3f_ref, w3b_ref, delta_ref, f1_ref, ab_ref, k_ref, kt_ref, y_ref):
    w3 = jnp.concatenate([w3f_ref[...], w3b_ref[...]], axis=0).astype(jnp.bfloat16)
    delta = delta_ref[...]
    rows = FFT_N // 13

    def fill(i, carry):
        r0 = pl.multiple_of(i * rows, 8)
        kt = jnp.dot(h_ref[pl.ds(r0, rows), :], w3, preferred_element_type=jnp.float32)
        j = lax.broadcasted_iota(jnp.int32, (rows, TC), 0) + r0
        lag = jnp.where(j < L_TOK, j, FFT_N - j).astype(jnp.float32)
        kt_ref[pl.ds(r0, rows), :] = kt * jnp.exp(-(lag * (1.0 / (L_TOK - 1))) * delta)
        return carry

    lax.fori_loop(0, 13, fill, 0)
    y_ref[pl.ds(FFT_N2 * SLAB, (N2P - FFT_N2) * SLAB), :] = jnp.zeros(((N2P - FFT_N2) * SLAB, TC), jnp.float32)

    def stage1(g, carry):
        bs = [g * B_GROUP + i for i in range(B_GROUP)]
        x = jnp.concatenate([kt_ref[pl.ds(b, FFT_N1, stride=FFT_N2), :] for b in bs], axis=1)
        y = jnp.dot(f1_ref[...], x.astype(jnp.bfloat16), preferred_element_type=jnp.float32)
        for i, b in enumerate(bs):
            y_ref[pl.ds(pl.multiple_of(b * SLAB, 8), 2 * FFT_N1), :] = y[:, i * TC:(i + 1) * TC]
        return carry

    lax.fori_loop(0, FFT_N2 // B_GROUP, stage1, 0)

    def stage2(g, carry):
        cs = [g * MID_GROUP + i for i in range(MID_GROUP)]
        zs = [_forward_middle(ab_ref[c], _middle_stage_inputs(y_ref, c)) for c in cs]
        for c, (zr, zi) in zip(cs, zs):
            k_ref[c] = (jnp.concatenate([zr, zi], axis=1) * (1.0 / FFT_N)).astype(jnp.bfloat16)
        return carry

    lax.fori_loop(0, FFT_N1 // MID_GROUP, stage2, 0)


def _filter_spectrum(h2, w3, deltas, consts):
    _, f1_real, _, ab = consts
    nt = HY_WIDTH // TC
    return pl.pallas_call(
        _spectrum_kernel,
        grid=(HY_ORDER, nt),
        in_specs=[
            pl.BlockSpec((FFT_N, 128), lambda n, i: (0, 0)),
            pl.BlockSpec((64, TC), lambda n, i: (0, n * nt + i)),
            pl.BlockSpec((64, TC), lambda n, i: (0, (HY_ORDER + n) * nt + i)),
            pl.BlockSpec((1, TC), lambda n, i: (0, i)),
            pl.BlockSpec((2 * FFT_N1, FFT_N1), lambda n, i: (0, 0)),
            pl.BlockSpec((FFT_N1, 2 * N2P, 128), lambda n, i: (0, 0, 0)),
        ],
        out_specs=pl.BlockSpec((None, None, FFT_N1, N2P, 2 * TC), lambda n, i: (n, i, 0, 0, 0)),
        out_shape=jax.ShapeDtypeStruct((HY_ORDER, nt, FFT_N1, N2P, 2 * TC), jnp.bfloat16),
        scratch_shapes=[pltpu.VMEM((FFT_N, TC), jnp.float32), pltpu.VMEM((N2P * SLAB, TC), jnp.float32)],
        compiler_params=pltpu.CompilerParams(
            dimension_semantics=("parallel", "parallel"), vmem_limit_bytes=V7X_VMEM_LIMIT),
        name="hyena_filter_spectrum",
    )(h2, w3, w3, deltas, f1_real, ab)


def _conv_kernel(u_ref, k_ref, f1_ref, i2_ref, ab_ref, o_ref, y_ref):
    y_ref[pl.ds(FFT_N2 * SLAB, (N2P - FFT_N2) * SLAB), :] = jnp.zeros(((N2P - FFT_N2) * SLAB, TC), jnp.float32)
    half = FFT_N1 // 2

    def stage1(g, carry):
        bs = [g * B_GROUP + i for i in range(B_GROUP)]
        x = jnp.concatenate(
            [jnp.concatenate([u_ref.at[ri][pl.ds(b, half, stride=FFT_N2), :] for b in bs], axis=1) for ri in range(2)],
            axis=0)
        y = jnp.dot(f1_ref[...], x.astype(jnp.bfloat16), preferred_element_type=jnp.float32)
        for i, b in enumerate(bs):
            y_ref[pl.ds(pl.multiple_of(b * SLAB, 8), 2 * FFT_N1), :] = y[:, i * TC:(i + 1) * TC]
        return carry

    lax.fori_loop(0, FFT_N2 // B_GROUP, stage1, 0)

    def inverse_middle(c, z):
        ab = ab_ref[c]
        zr, zi = z
        kk = k_ref[c].astype(jnp.float32)
        kr, ki = kk[:, :TC], kk[:, TC:]
        pr = zr * kr - zi * ki
        pi = zr * ki + zi * kr
        rhs = jnp.concatenate([jnp.concatenate([pr, pi], axis=1), jnp.concatenate([pi, -pr], axis=1)], axis=0)
        return lax.dot_general(ab, rhs.astype(jnp.bfloat16), (((0,), (0,)), ((), ())),
                               preferred_element_type=jnp.float32)

    def middle(g, carry):
        cs = [g * MID_GROUP + i for i in range(MID_GROUP)]
        zs = [_forward_middle(ab_ref[c], _middle_stage_inputs(y_ref, c)) for c in cs]
        outs = [inverse_middle(c, z) for c, z in zip(cs, zs)]
        for c, v in zip(cs, outs):
            y_ref[pl.ds(c, N2P, stride=SLAB), :] = v[:N2P, :TC]
            y_ref[pl.ds(FFT_N1 + c, N2P, stride=SLAB), :] = v[:N2P, TC:]
        return carry

    lax.fori_loop(0, FFT_N1 // MID_GROUP, middle, 0)

    def stage4(g, carry):
        bs = [g * B_GROUP + i for i in range(B_GROUP)]
        v = jnp.concatenate([y_ref[pl.ds(pl.multiple_of(b * SLAB, 8), 2 * FFT_N1), :] for b in bs], axis=1)
        x = jnp.dot(i2_ref[...], v.astype(jnp.bfloat16), preferred_element_type=jnp.float32)
        for i, b in enumerate(bs):
            o_ref.at[0][pl.ds(b, half, stride=FFT_N2), :] = x[:half, i * TC:(i + 1) * TC]
            o_ref.at[1][pl.ds(b, half, stride=FFT_N2), :] = x[half:, i * TC:(i + 1) * TC]
        return carry

    lax.fori_loop(0, FFT_N2 // B_GROUP, stage4, 0)


def _long_conv(u, col0, kspec, order, consts):
    f1, _, i2, ab = consts
    bsz, lp, _ = u.shape
    assert bsz == 2 and lp == LP and col0 % TC == 0
    nt = HY_WIDTH // TC
    off = col0 // TC
    return pl.pallas_call(
        _conv_kernel,
        grid=(nt,),
        in_specs=[
            pl.BlockSpec((bsz, lp, TC), lambda i: (0, 0, off + i)),
            pl.BlockSpec((None, None, FFT_N1, N2P, 2 * TC), lambda i: (order, i, 0, 0, 0)),
            pl.BlockSpec((2 * FFT_N1, FFT_N1), lambda i: (0, 0)),
            pl.BlockSpec((FFT_N1, 2 * FFT_N1), lambda i: (0, 0)),
            pl.BlockSpec((FFT_N1, 2 * N2P, 128), lambda i: (0, 0, 0)),
        ],
        out_specs=pl.BlockSpec((bsz, lp, TC), lambda i: (0, 0, i)),
        out_shape=jax.ShapeDtypeStruct((bsz, lp, HY_WIDTH), jnp.float32),
        scratch_shapes=[pltpu.VMEM((N2P * SLAB, TC), jnp.float32)],
        compiler_params=pltpu.CompilerParams(dimension_semantics=("parallel",), vmem_limit_bytes=V7X_VMEM_LIMIT),
        name="hyena_long_conv",
    )(u, kspec, f1, i2, ab)


def _short_conv_kernel(u_ref, w_ref, o_ref):
    u = u_ref[...]
    w = w_ref[...]
    y = pltpu.roll(u, 1, 0) * w[0:1] + u * w[1:2] + pltpu.roll(u, LP - 1, 0) * w[2:3]
    row = lax.broadcasted_iota(jnp.int32, y.shape, 0)
    o_ref[...] = jnp.where(row >= PAD, y, 0.0)


def _short_conv(p3, w_short):
    bsz = p3.shape[0]
    width = 3 * HY_WIDTH
    return pl.pallas_call(
        _short_conv_kernel,
        grid=(bsz, width // TC),
        in_specs=[pl.BlockSpec((None, LP, TC), lambda b, i: (b, 0, i)), pl.BlockSpec((3, TC), lambda b, i: (0, i))],
        out_specs=pl.BlockSpec((None, LP, TC), lambda b, i: (b, 0, i)),
        out_shape=jax.ShapeDtypeStruct((bsz, LP, width), jnp.float32),
        compiler_params=pltpu.CompilerParams(
            dimension_semantics=("parallel", "parallel"), vmem_limit_bytes=V7X_VMEM_LIMIT),
        name="hyena_short_conv",
    )(p3, w_short.T)


GATE_ROWS = 520
GATE_COLS = 512


def _gate1_kernel(v_ref, x_ref, y_ref, b_ref, o_ref):
    o_ref[...] = x_ref[...] * (y_ref[...] + v_ref[...] * b_ref[...])


def _gate2_kernel(z_ref, x_ref, y_ref, b_ref, g_ref, o_ref):
    g = g_ref[...]
    z = x_ref[...] * (y_ref[...] + z_ref[...] * b_ref[...])
    o_ref[...] = (z * (g * jax.nn.sigmoid(g))).astype(o_ref.dtype)


def _hyena_gate1(uc, y, bias):
    bsz = uc.shape[0]
    nc = HY_WIDTH // GATE_COLS
    blk = lambda off: pl.BlockSpec((None, GATE_ROWS, GATE_COLS), lambda b, r, c: (b, r, off + c))
    return pl.pallas_call(
        _gate1_kernel,
        grid=(bsz, LP // GATE_ROWS, nc),
        in_specs=[blk(0), blk(nc), blk(0), pl.BlockSpec((1, GATE_COLS), lambda b, r, c: (0, c))],
        out_specs=blk(0),
        out_shape=jax.ShapeDtypeStruct((bsz, LP, HY_WIDTH), jnp.float32),
        compiler_params=pltpu.CompilerParams(
            dimension_semantics=("parallel", "parallel", "parallel"), vmem_limit_bytes=V7X_VMEM_LIMIT),
        name="hyena_gate1",
    )(uc, uc, y, bias.reshape(1, HY_WIDTH))


def _hyena_gate2(z1, uc, y, bias, p3):
    bsz = uc.shape[0]
    nc = HY_WIDTH // GATE_COLS
    blk = lambda off: pl.BlockSpec((None, GATE_ROWS, GATE_COLS), lambda b, r, c: (b, r, off + c))
    return pl.pallas_call(
        _gate2_kernel,
        grid=(bsz, LP // GATE_ROWS, nc),
        in_specs=[blk(0), blk(2 * nc), blk(0), pl.BlockSpec((1, GATE_COLS), lambda b, r, c: (0, c)), blk(3 * nc)],
        out_specs=blk(0),
        out_shape=jax.ShapeDtypeStruct((bsz, LP, HY_WIDTH), jnp.bfloat16),
        compiler_params=pltpu.CompilerParams(
            dimension_semantics=("parallel", "parallel", "parallel"), vmem_limit_bytes=V7X_VMEM_LIMIT),
        name="hyena_gate2",
    )(z1, uc, y, bias.reshape(1, HY_WIDTH), p3)


def _hyena_hidden(w1, b1, fr1, w2, b2, fr2):
    f32 = jnp.float32
    l = L_TOK
    hi = lax.Precision.HIGHEST
    t = jnp.linspace(0.0, 1.0, l, dtype=f32)[:, None]
    w = 2.0 * math.pi * jnp.arange(l, dtype=f32)[:, None] / l
    bands = jnp.linspace(1e-4, HY_BANDS - 1, HY_BANDS, dtype=f32)[None, :]
    z = jnp.concatenate([t, jnp.cos(bands * w), -jnp.sin(bands * w)], axis=-1)
    hid = jnp.sin(fr1 * (jnp.dot(z, w1, precision=hi) + b1))
    hid = jnp.sin(fr2 * (jnp.dot(hid, w2, precision=hi) + b2))
    j = np.arange(FFT_N)
    lag = np.minimum(np.where(j < l, j, FFT_N - j), l - 1)
    fwd = (j < l).astype(np.float32)[:, None]
    bwd = ((j > FFT_N - l) | (j == 0)).astype(np.float32)[:, None]
    rows = hid[lag]
    return jnp.concatenate([rows * fwd, rows * bwd], axis=1).astype(jnp.bfloat16)


def _odd_mix(p3, w_short, w1, b1, fr1, w2, b2, fr2, w3, bias, consts):
    min_decay = math.log(HY_TARGET) / HY_SLOW_DECAY
    max_decay = math.log(HY_TARGET) / HY_FAST_DECAY
    deltas = jnp.abs(jnp.linspace(min_decay, max_decay, HY_WIDTH, dtype=jnp.float32)).reshape(1, HY_WIDTH)
    kspec = _filter_spectrum(_hyena_hidden(w1, b1, fr1, w2, b2, fr2), w3, deltas, consts)
    uc = _short_conv(p3, w_short)
    y1 = _long_conv(uc, 0, kspec, 0, consts)
    z1 = _hyena_gate1(uc, y1, bias[0])
    y2 = _long_conv(z1, 0, kspec, 1, consts)
    return _hyena_gate2(z1, uc, y2, bias[1], p3)


HG_HB = 4
HG_LEVELS = (32, 16, 8, 4, 2, 1)


def _cumsum_rows(g, row, rev):
    for sh in (1, 2, 4, 8, 16, 32):
        if rev:
            g = g + jnp.where(row < HG_CHUNK - sh, pltpu.roll(g, HG_CHUNK - sh, 0), 0.0)
        else:
            g = g + jnp.where(row >= sh, pltpu.roll(g, sh, 0), 0.0)
    return g


def _level_reference(gc, row, m, rev):
    if m >= 4:
        g3 = gc.reshape(HG_CHUNK // (2 * m), 2 * m, HEAD_DIM)
        ref = g3[:, m:m + 1, :] if rev else g3[:, m - 1:m, :]
        return jnp.broadcast_to(ref, g3.shape).reshape(HG_CHUNK, HEAD_DIM)
    r = row & (2 * m - 1)
    target = m if rev else m - 1
    out = gc
    for src in range(2 * m):
        if src != target:
            out = jnp.where(r == src, pltpu.roll(gc, (src - target) % HG_CHUNK, 0), out)
    return out


def _hgrn_gates(q_raw, f_raw, la, lc, omlb, row, rev):
    q = q_raw * jax.nn.sigmoid(q_raw) * (HEAD_DIM ** -0.5)
    e = jnp.exp(-jnp.abs(f_raw))
    ope = 1.0 + e
    log_sig = jnp.minimum(f_raw, 0.0) - jnp.log(ope)
    k = omlb * (jnp.where(f_raw >= 0.0, e, 1.0) / ope)
    b = lc + log_sig
    g = jnp.maximum(la, b) + jnp.log(1.0 + jnp.exp(-jnp.abs(la - b)))
    return q, k, _cumsum_rows(g, row, rev)


def _hgrn_level_operands(q, k, gc, row, rev):
    ops = []
    for m in HG_LEVELS:
        gref = _level_reference(gc, row, m, rev)
        upper = (row & m) != 0
        is_query = jnp.logical_not(upper) if rev else upper
        d = gc - gref
        ops.append((jnp.where(is_query, q, k) * jnp.exp(jnp.where(is_query, d, -d))).astype(jnp.bfloat16))
    return ops


def _hgrn_level_id(rev):
    t = lax.broadcasted_iota(jnp.int32, (HG_CHUNK, HG_CHUNK), 0)
    s = lax.broadcasted_iota(jnp.int32, (HG_CHUNK, HG_CHUNK), 1)
    vis = (s >= t) if rev else (s <= t)
    x = t ^ s
    lvl = jnp.full((HG_CHUNK, HG_CHUNK), len(HG_LEVELS), jnp.int32)
    for i, m in enumerate(HG_LEVELS):
        lvl = jnp.where((x & m) != 0, jnp.minimum(lvl, i), lvl)
    return jnp.where(vis, lvl, -1)


def _hgrn_kernel(qf_ref, if_ref, ff_ref, qb_ref, ib_ref, fb_ref, la_ref, lc_ref, om_ref, of_ref, ob_ref, s_ref):
    f32, bf = jnp.float32, jnp.bfloat16
    nt = (((1,), (1,)), ((), ()))
    tn = (((0,), (0,)), ((), ()))

    @pl.when(pl.program_id(2) == 0)
    def _():
        s_ref[...] = jnp.zeros_like(s_ref)

    row = lax.broadcasted_iota(jnp.int32, (HG_CHUNK, HEAD_DIM), 0)
    refs = ((qf_ref, if_ref, ff_ref, of_ref), (qb_ref, ib_ref, fb_ref, ob_ref))
    inst = [(d, h) for d in range(2) for h in range(HG_HB)]
    level_masks = [[lvl == i for i in range(len(HG_LEVELS) + 1)]
                   for lvl in (_hgrn_level_id(False), _hgrn_level_id(True))]

    pre = []
    for d, h in inst:
        q_ref, i_ref, f_ref, _ = refs[d]
        sl = slice(h * HEAD_DIM, (h + 1) * HEAD_DIM)
        q, k, gc = _hgrn_gates(q_ref[:, sl], f_ref[:, sl], la_ref[d:d + 1, sl], lc_ref[d:d + 1, sl],
                               om_ref[d:d + 1, sl], row, d == 1)
        g_end = gc[0:1, :] if d == 1 else gc[HG_CHUNK - 1:HG_CHUNK, :]
        pre.append(dict(ops=_hgrn_level_operands(q, k, gc, row, d == 1), qk=(q.astype(bf), k.astype(bf)),
                        qe=(q * jnp.exp(gc)).astype(bf), kd=(k * jnp.exp(g_end - gc)).astype(bf),
                        decay=jnp.exp(g_end), v=i_ref[:, sl].astype(bf)))

    attn = []
    for (d, h), pr in zip(inst, pre):
        prods = [lax.dot_general(x, x, nt, preferred_element_type=f32) for x in pr["ops"]]
        prods.append(lax.dot_general(pr["qk"][0], pr["qk"][1], nt, preferred_element_type=f32))
        a = jnp.zeros((HG_CHUNK, HG_CHUNK), f32)
        for i, pmat in enumerate(prods):
            a = jnp.where(level_masks[d][i], pmat, a)
        attn.append(a.astype(bf))

    for (d, h), pr, a in zip(inst, pre, attn):
        o_ref = refs[d][3]
        sl = slice(h * HEAD_DIM, (h + 1) * HEAD_DIM)
        state_t = s_ref[d, h]
        o = jnp.dot(a, pr["v"], preferred_element_type=f32)
        o_ref[:, sl] = o + lax.dot_general(pr["qe"], state_t.astype(bf), nt, preferred_element_type=f32)
        s_ref[d, h] = pr["decay"] * state_t + lax.dot_general(pr["v"], pr["kd"], tn, preferred_element_type=f32)


def _hgrn2(p, col0, lb):
    bsz, lp, _ = p.shape
    nchunk = lp // HG_CHUNK
    cw = HG_HB * HEAD_DIM
    ng = HG_WIDTH // cw
    off = col0 // cw
    la = jnp.log(lb)
    lc = jnp.log1p(-lb)
    om = 1.0 - lb
    fwd = lambda seg: pl.BlockSpec((None, HG_CHUNK, cw), lambda b, g, c: (b, c, off + seg * ng + g))
    bwd = lambda seg: pl.BlockSpec((None, HG_CHUNK, cw), lambda b, g, c: (b, nchunk - 1 - c, off + seg * ng + g))
    par = pl.BlockSpec((2, cw), lambda b, g, c: (0, g))
    return pl.pallas_call(
        _hgrn_kernel,
        grid=(bsz, ng, nchunk),
        in_specs=[fwd(0), fwd(1), fwd(2), bwd(0), bwd(1), bwd(3), par, par, par],
        out_specs=[pl.BlockSpec((None, HG_CHUNK, cw), lambda b, g, c: (b, c, g)),
                   pl.BlockSpec((None, HG_CHUNK, cw), lambda b, g, c: (b, nchunk - 1 - c, g))],
        out_shape=[jax.ShapeDtypeStruct((bsz, lp, HG_WIDTH), jnp.float32)] * 2,
        scratch_shapes=[pltpu.VMEM((2, HG_HB, HEAD_DIM, HEAD_DIM), jnp.float32)],
        compiler_params=pltpu.CompilerParams(
            dimension_semantics=("parallel", "parallel", "arbitrary"), vmem_limit_bytes=V7X_VMEM_LIMIT),
        name="hgrn2_bidirectional",
    )(p, p, p, p, p, p, la, lc, om)


EG_ROWS = 320


def _even_gate_kernel(oa_ref, of_ref, ob_ref, ga_ref, gb_ref, gain_ref, y_ref):
    ga = ga_ref[...]
    y_ref[:, :NA_WIDTH] = (oa_ref[...] * (ga * jax.nn.sigmoid(ga))).astype(y_ref.dtype)
    for h in range(HG_HEADS):
        sl = slice(h * HEAD_DIM, (h + 1) * HEAD_DIM)
        oh = of_ref[:, sl] + ob_ref[:, sl]
        oh = oh * lax.rsqrt(jnp.mean(oh * oh, axis=-1, keepdims=True) + EPS) * gain_ref[:, sl]
        gb = gb_ref[:, sl]
        y_ref[:, NA_WIDTH + h * HEAD_DIM:NA_WIDTH + (h + 1) * HEAD_DIM] = (
            oh * (gb * jax.nn.sigmoid(gb))).astype(y_ref.dtype)


def _even_gate(oa, o_fwd, o_bwd, p, gate_col0, norm_gain):
    bsz, lp, _ = oa.shape
    assert gate_col0 % NA_WIDTH == 0
    goff = gate_col0 // NA_WIDTH
    half = lambda off: pl.BlockSpec((None, EG_ROWS, NA_WIDTH), lambda b, r: (b, r, off))
    return pl.pallas_call(
        _even_gate_kernel,
        grid=(bsz, lp // EG_ROWS),
        in_specs=[half(0), half(0), half(0), half(goff), half(goff + 1),
                  pl.BlockSpec((1, HG_WIDTH), lambda b, r: (0, 0))],
        out_specs=pl.BlockSpec((None, EG_ROWS, D_MODEL), lambda b, r: (b, r, 0)),
        out_shape=jax.ShapeDtypeStruct((bsz, lp, D_MODEL), jnp.bfloat16),
        compiler_params=pltpu.CompilerParams(
            dimension_semantics=("parallel", "parallel"), vmem_limit_bytes=V7X_VMEM_LIMIT),
        name="even_gate",
    )(oa, o_fwd, o_bwd, p, p, norm_gain.reshape(1, HG_WIDTH))


NA_MASKED = -1e30
NA_GROUP = 4


def _na_bias_table(rpb):
    cols = np.arange(GRID_W)
    c0 = np.clip(cols - NA_KW // 2, 0, GRID_W - NA_KW)
    col_ok = (cols[None, :] >= c0[:, None]) & (cols[None, :] < c0[:, None] + NA_KW)
    col_idx = np.clip(cols[None, :] - cols[:, None] + NA_KW - 1, 0, 2 * NA_KW - 2)
    table = jnp.where(col_ok[None, None], rpb.astype(jnp.float32)[:, :, col_idx], NA_MASKED)
    return jnp.concatenate([table[:, :-1], table[:, 1:]], axis=-1)


def _na_kernel(q_ref, k_ref, v_ref, t_ref, mb_ref, o_ref):
    f32, bf = jnp.float32, jnp.bfloat16
    rows = (q_ref.shape[0] - HG_CHUNK) // GRID_W
    scale = HEAD_DIM ** -0.5
    nt = (((1,), (1,)), ((), ()))
    km = k_ref[PAD:PAD + N_META, :].astype(bf)
    vm = v_ref[PAD:PAD + N_META, :].astype(bf)
    mb = mb_ref[...]
    o_ref[0:PAD, :] = jnp.zeros((PAD, HEAD_DIM), f32)

    qm = q_ref[PAD:PAD + N_META, :].astype(bf)
    s = lax.dot_general(qm, km, nt, preferred_element_type=f32) * scale + mb
    e = jnp.exp(s - jnp.max(s, axis=-1, keepdims=True))
    o_ref[PAD:PAD + N_META, :] = (jnp.dot(e.astype(bf), vm, preferred_element_type=f32)
                                  / jnp.sum(e, axis=-1, keepdims=True))

    def row_group(g, carry):
        rs = [g * NA_GROUP + i for i in range(NA_GROUP)]
        scores = []
        for r in rs:
            r0 = jnp.clip(r - NA_KH // 2, 0, rows - NA_KH)
            q = q_ref[pl.ds(pl.multiple_of(HG_CHUNK + GRID_W * r, GRID_W), GRID_W), :].astype(bf)
            k0 = pl.multiple_of(HG_CHUNK + GRID_W * r0, GRID_W)
            kw = k_ref[pl.ds(k0, NA_KH * GRID_W), :].astype(bf)
            s_win = lax.dot_general(q, kw, nt, preferred_element_type=f32)
            s_meta = lax.dot_general(q, km, nt, preferred_element_type=f32)
            scores.append((r, r0, k0, s_win, s_meta))
        probs = []
        for r, r0, k0, s_win, s_meta in scores:
            bias = jnp.concatenate([t_ref[r0 + i - r + NA_KH - 1] for i in range(0, NA_KH, 2)], axis=1)
            s_win = s_win * scale + bias
            s_meta = s_meta * scale + mb
            m = jnp.maximum(jnp.max(s_win, axis=-1, keepdims=True), jnp.max(s_meta, axis=-1, keepdims=True))
            pw = jnp.exp(s_win - m)
            pm = jnp.exp(s_meta - m)
            den = jnp.sum(pw, axis=-1, keepdims=True) + jnp.sum(pm, axis=-1, keepdims=True)
            probs.append((r, k0, pw.astype(bf), pm.astype(bf), den))
        for r, k0, pw, pm, den in probs:
            vw = v_ref[pl.ds(k0, NA_KH * GRID_W), :].astype(bf)
            o = jnp.dot(pw, vw, preferred_element_type=f32) + jnp.dot(pm, vm, preferred_element_type=f32)
            o_ref[pl.ds(pl.multiple_of(HG_CHUNK + GRID_W * r, GRID_W), GRID_W), :] = o / den
        return carry

    lax.fori_loop(0, rows // NA_GROUP, row_group, 0)


def _natten(p, rpb, meta_bias):
    bsz, lp, _ = p.shape
    heads = NA_WIDTH // HEAD_DIM
    blk = lambda seg: pl.BlockSpec((None, lp, HEAD_DIM), lambda b, h: (b, 0, seg * heads + h))
    table = _na_bias_table(rpb)
    return pl.pallas_call(
        _na_kernel,
        grid=(bsz, heads),
        in_specs=[blk(0), blk(1), blk(2),
                  pl.BlockSpec((None, 2 * NA_KH - 2, GRID_W, 2 * GRID_W), lambda b, h: (h, 0, 0, 0)),
                  pl.BlockSpec((None, 1, N_META), lambda b, h: (h, 0, 0))],
        out_specs=pl.BlockSpec((None, lp, HEAD_DIM), lambda b, h: (b, 0, h)),
        out_shape=jax.ShapeDtypeStruct((bsz, lp, NA_WIDTH), jnp.float32),
        compiler_params=pltpu.CompilerParams(
            dimension_semantics=("parallel", "parallel"), vmem_limit_bytes=V7X_VMEM_LIMIT),
        name="neighbourhood_attention",
    )(p, p, p, table, meta_bias.astype(jnp.float32).reshape(heads, 1, N_META))


def _even_mix(p, rpb, meta_bias, lb, norm_gain):
    oa = _natten(p, rpb, meta_bias)
    o_fwd, o_bwd = _hgrn2(p, 3 * NA_WIDTH, lb)
    return _even_gate(oa, o_fwd, o_bwd, p, 3 * NA_WIDTH + 4 * HG_WIDTH, norm_gain)


def kernel(x, meta_tokens, norm_pre, norm_post, ev_w_in, ev_w_out, na_rpb, na_meta_bias, hg_lower, hg_norm,
           od_w_in, od_w_out, hy_short, hy_ffn_w1, hy_ffn_b1, hy_ffn_freq1, hy_ffn_w2, hy_ffn_b2, hy_ffn_freq2,
           hy_ffn_w3, hy_bias):
    b = x.shape[0]
    depth = norm_pre.shape[0]
    f32 = jnp.float32
    lb_all = jnp.cumsum(jax.nn.softmax(hg_lower.astype(f32), axis=0), axis=0)
    lb_all = lb_all - lb_all[:1]
    meta = jnp.broadcast_to(meta_tokens.astype(f32)[None], (b, N_META, D_MODEL))
    h = jnp.concatenate([jnp.zeros((b, PAD, D_MODEL), f32), meta, x.astype(f32)], axis=1)
    h = h.reshape(b * LP, D_MODEL)
    consts = _dft_constants()
    hn = _rmsnorm(h, norm_pre[0], jnp.bfloat16)
    for layer in range(depth):
        j = layer // 2
        if layer % 2 == 0:
            p = _matmul(hn, ev_w_in, j).reshape(b, LP, -1)
            y = _even_mix(p, na_rpb[j], na_meta_bias[j], lb_all[j], hg_norm[j]).reshape(b * LP, D_MODEL)
            w_out = ev_w_out
        else:
            p3 = _matmul(hn, od_w_in, j).reshape(b, LP, -1)
            y = _odd_mix(p3, hy_short[j], hy_ffn_w1[j], hy_ffn_b1[j], hy_ffn_freq1[j], hy_ffn_w2[j],
                         hy_ffn_b2[j], hy_ffn_freq2[j], hy_ffn_w3[j], hy_bias[j], consts)
            y = y.reshape(b * LP, D_MODEL)
            w_out = od_w_out
        out = _matmul(y, w_out, j)
        if layer + 1 < depth:
            h, hn = _residual_rmsnorm(h, out, norm_post[layer], norm_pre[layer + 1])
        else:
            h = _residual_rmsnorm(h, out, norm_post[layer])
    return h.reshape(b, LP, D_MODEL)[:, PAD + N_META:].astype(x.dtype)
```

```python
import functools
import math

import jax
import jax.numpy as jnp
import numpy as np
from jax import lax
from jax.experimental import pallas as pl
from jax.experimental.pallas import tpu as pltpu

D_MODEL = 4096
SEQ = 4096
N_META = 16
GRID_W = 64
HEAD_DIM = 128
NA_WIDTH = D_MODEL // 2
NA_KH = 8
NA_KW = 16
HG_WIDTH = D_MODEL // 2
HG_HEADS = HG_WIDTH // HEAD_DIM
HG_CHUNK = 64
HY_WIDTH = D_MODEL
HY_ORDER = 2
HY_EMB = 33
HY_BANDS = (HY_EMB - 1) // 2
HY_FAST_DECAY = 0.3
HY_SLOW_DECAY = 1.5
HY_TARGET = 1e-2
EPS = 1e-6

PAD = HG_CHUNK - N_META
LP = PAD + N_META + SEQ
L_TOK = N_META + SEQ

V7X_VMEM_LIMIT = 56 * 1024 * 1024


def _rmsnorm_kernel(x_ref, g_ref, o_ref):
    x = x_ref[...]
    r = lax.rsqrt(jnp.mean(x * x, axis=-1, keepdims=True) + EPS)
    o_ref[...] = (x * r * g_ref[...]).astype(o_ref.dtype)


def _rmsnorm(x2d, g, out_dtype, tr=320):
    m, d = x2d.shape
    return pl.pallas_call(
        _rmsnorm_kernel,
        grid=(m // tr,),
        in_specs=[pl.BlockSpec((tr, d), lambda i: (i, 0)), pl.BlockSpec((1, d), lambda i: (0, 0))],
        out_specs=pl.BlockSpec((tr, d), lambda i: (i, 0)),
        out_shape=jax.ShapeDtypeStruct((m, d), out_dtype),
        compiler_params=pltpu.CompilerParams(vmem_limit_bytes=V7X_VMEM_LIMIT),
        name="rmsnorm_pre",
    )(x2d, g.reshape(1, d))


def _post_kernel(h_ref, y_ref, g_ref, o_ref):
    y = y_ref[...]
    r = lax.rsqrt(jnp.mean(y * y, axis=-1, keepdims=True) + EPS)
    o_ref[...] = h_ref[...] + y * r * g_ref[...]


def _post_pre_kernel(h_ref, y_ref, g_ref, gn_ref, o_ref, n_ref):
    y = y_ref[...]
    r = lax.rsqrt(jnp.mean(y * y, axis=-1, keepdims=True) + EPS)
    h = h_ref[...] + y * r * g_ref[...]
    o_ref[...] = h
    rn = lax.rsqrt(jnp.mean(h * h, axis=-1, keepdims=True) + EPS)
    n_ref[...] = (h * rn * gn_ref[...]).astype(n_ref.dtype)


def _residual_rmsnorm(h2d, y2d, g, g_next=None, tr=320):
    m, d = h2d.shape
    row = pl.BlockSpec((tr, d), lambda i: (i, 0))
    vec = pl.BlockSpec((1, d), lambda i: (0, 0))
    params = pltpu.CompilerParams(dimension_semantics=("parallel",), vmem_limit_bytes=V7X_VMEM_LIMIT)
    if g_next is None:
        return pl.pallas_call(
            _post_kernel, grid=(m // tr,), in_specs=[row, row, vec], out_specs=row,
            out_shape=jax.ShapeDtypeStruct((m, d), jnp.float32), compiler_params=params,
            name="residual_rmsnorm_post",
        )(h2d, y2d, g.reshape(1, d))
    return pl.pallas_call(
        _post_pre_kernel, grid=(m // tr,), in_specs=[row, row, vec, vec], out_specs=[row, row],
        out_shape=[jax.ShapeDtypeStruct((m, d), jnp.float32), jax.ShapeDtypeStruct((m, d), jnp.bfloat16)],
        compiler_params=params, name="residual_rmsnorm_post_pre",
    )(h2d, y2d, g.reshape(1, d), g_next.reshape(1, d))


def _matmul_kernel(x_ref, w_ref, o_ref):
    o_ref[...] = jnp.dot(x_ref[...], w_ref[...].astype(jnp.bfloat16),
                         preferred_element_type=jnp.float32).astype(o_ref.dtype)


def _matmul(x, w_stack, layer, tm=1040, tn=512):
    m, kd = x.shape
    _, _, n = w_stack.shape
    assert m % tm == 0 and n % tn == 0
    return pl.pallas_call(
        _matmul_kernel,
        grid=(m // tm, n // tn),
        in_specs=[pl.BlockSpec((tm, kd), lambda i, j: (i, 0)),
                  pl.BlockSpec((None, kd, tn), lambda i, j: (layer, 0, j))],
        out_specs=pl.BlockSpec((tm, tn), lambda i, j: (i, j)),
        out_shape=jax.ShapeDtypeStruct((m, n), jnp.float32),
        compiler_params=pltpu.CompilerParams(
            dimension_semantics=("parallel", "parallel"), vmem_limit_bytes=V7X_VMEM_LIMIT),
        name="projection_matmul",
    )(x, w_stack)


FFT_N1 = 128
FFT_N2 = 65
FFT_N = FFT_N1 * FFT_N2
N2P = 80
SLAB = 264
TC = 128
MID_GROUP = 16
FILL_CHUNKS = 13
B_GROUP = 13


def _dft_constants():
    a = np.arange(FFT_N1)
    c = np.arange(FFT_N1)
    w1 = np.exp(-2j * np.pi * np.outer(c, a) / FFT_N1)
    w1r, w1i = w1.real, w1.imag
    half = FFT_N1 // 2
    f1 = np.block([[w1r[:, :half], -w1i[:, :half]], [w1i[:, :half], w1r[:, :half]]])
    f1_real = np.concatenate([w1r, w1i], axis=0)
    g = np.conj(w1).T
    gr, gi = g.real[:half], g.imag[:half]
    i2 = np.block([[gr, -gi], [gi, gr]])
    b = np.arange(FFT_N2)
    d = np.arange(FFT_N2)
    w2 = np.exp(-2j * np.pi * np.outer(d, b) / FFT_N2)
    tw = np.exp(-2j * np.pi * np.outer(c, b) / FFT_N)
    m = w2[None, :, :] * tw[:, None, :]
    ab = np.zeros((FFT_N1, 2 * N2P, 128), np.float64)
    ab[:, :FFT_N2, :FFT_N2] = m.real
    ab[:, N2P:N2P + FFT_N2, :FFT_N2] = m.imag
    bf = jnp.bfloat16
    return (jnp.asarray(f1, bf), jnp.asarray(f1_real, bf), jnp.asarray(i2, bf), jnp.asarray(ab, bf))


def _middle_stage_inputs(y_ref, c):
    yr = y_ref[pl.ds(c, N2P, stride=SLAB), :]
    yi = y_ref[pl.ds(FFT_N1 + c, N2P, stride=SLAB), :]
    return jnp.concatenate([yr, yi], axis=1).astype(jnp.bfloat16)


def _forward_middle(ab, y2):
    y2 = jnp.concatenate([y2, jnp.zeros((128 - N2P, 2 * TC), jnp.bfloat16)], axis=0)
    q = jnp.dot(ab, y2, preferred_element_type=jnp.float32)
    zr = q[:N2P, :TC] - q[N2P:, TC:]
    zi = q[N2P:, :TC] + q[:N2P, TC:]
    return zr, zi


def _spectrum_kernel(h_ref, t_ref, w3f_ref, w3b_ref, delta_ref, f1_ref, ab_ref, k_ref, kt_ref, y_ref):
    w3 = jnp.concatenate([w3f_ref[...], w3b_ref[...]], axis=0).astype(jnp.bfloat16)
    delta = delta_ref[...]
    rows = FFT_N // FILL_CHUNKS

    def fill(i, carry):
        r0 = pl.multiple_of(i * rows, 8)
        kt = jnp.dot(h_ref[pl.ds(r0, rows), :], w3, preferred_element_type=jnp.float32)
        kt_ref[pl.ds(r0, rows), :] = kt * jnp.exp(t_ref[pl.ds(r0, rows), :] * delta)
        return carry

    lax.fori_loop(0, FILL_CHUNKS, fill, 0)
    y_ref[pl.ds(FFT_N2 * SLAB, (N2P - FFT_N2) * SLAB), :] = jnp.zeros(((N2P - FFT_N2) * SLAB, TC), jnp.float32)

    def stage1(g, carry):
        bs = [g * B_GROUP + i for i in range(B_GROUP)]
        x = jnp.concatenate([kt_ref[pl.ds(b, FFT_N1, stride=FFT_N2), :] for b in bs], axis=1)
        y = jnp.dot(f1_ref[...], x.astype(jnp.bfloat16), preferred_element_type=jnp.float32)
        for i, b in enumerate(bs):
            y_ref[pl.ds(pl.multiple_of(b * SLAB, 8), 2 * FFT_N1), :] = y[:, i * TC:(i + 1) * TC]
        return carry

    lax.fori_loop(0, FFT_N2 // B_GROUP, stage1, 0)

    def stage2(g, carry):
        cs = [g * MID_GROUP + i for i in range(MID_GROUP)]
        zs = [_forward_middle(ab_ref[c], _middle_stage_inputs(y_ref, c)) for c in cs]
        for c, (zr, zi) in zip(cs, zs):
            k_ref[c] = (jnp.concatenate([zr, zi], axis=1) * (1.0 / FFT_N)).astype(jnp.bfloat16)
        return carry

    lax.fori_loop(0, FFT_N1 // MID_GROUP, stage2, 0)


def _filter_spectrum(h2, w3, deltas, consts):
    _, f1_real, _, ab = consts
    nt = HY_WIDTH // TC
    j = np.arange(FFT_N)
    lag = np.where(j < L_TOK, j, FFT_N - j).astype(np.float32)
    neg_t = jnp.asarray(np.broadcast_to((-lag / np.float32(L_TOK - 1))[:, None], (FFT_N, TC)))
    return pl.pallas_call(
        _spectrum_kernel,
        grid=(HY_ORDER, nt),
        in_specs=[
            pl.BlockSpec((FFT_N, 128), lambda n, i: (0, 0)),
            pl.BlockSpec((FFT_N, TC), lambda n, i: (0, 0)),
            pl.BlockSpec((64, TC), lambda n, i: (0, n * nt + i)),
            pl.BlockSpec((64, TC), lambda n, i: (0, (HY_ORDER + n) * nt + i)),
            pl.BlockSpec((1, TC), lambda n, i: (0, i)),
            pl.BlockSpec((2 * FFT_N1, FFT_N1), lambda n, i: (0, 0)),
            pl.BlockSpec((FFT_N1, 2 * N2P, 128), lambda n, i: (0, 0, 0)),
        ],
        out_specs=pl.BlockSpec((None, None, FFT_N1, N2P, 2 * TC), lambda n, i: (n, i, 0, 0, 0)),
        out_shape=jax.ShapeDtypeStruct((HY_ORDER, nt, FFT_N1, N2P, 2 * TC), jnp.bfloat16),
        scratch_shapes=[pltpu.VMEM((FFT_N, TC), jnp.float32), pltpu.VMEM((N2P * SLAB, TC), jnp.float32)],
        compiler_params=pltpu.CompilerParams(
            dimension_semantics=("parallel", "parallel"), vmem_limit_bytes=V7X_VMEM_LIMIT),
        name="hyena_filter_spectrum",
    )(h2, neg_t, w3, w3, deltas, f1_real, ab)


def _conv_kernel(u_ref, k_ref, f1_ref, i2_ref, ab_ref, o_ref, y_ref):
    y_ref[pl.ds(FFT_N2 * SLAB, (N2P - FFT_N2) * SLAB), :] = jnp.zeros(((N2P - FFT_N2) * SLAB, TC), jnp.float32)
    half = FFT_N1 // 2

    def stage1(g, carry):
        bs = [g * B_GROUP + i for i in range(B_GROUP)]
        x = jnp.concatenate(
            [jnp.concatenate([u_ref.at[ri][pl.ds(b, half, stride=FFT_N2), :] for b in bs], axis=1) for ri in range(2)],
            axis=0)
        y = jnp.dot(f1_ref[...], x.astype(jnp.bfloat16), preferred_element_type=jnp.float32)
        for i, b in enumerate(bs):
            y_ref[pl.ds(pl.multiple_of(b * SLAB, 8), 2 * FFT_N1), :] = y[:, i * TC:(i + 1) * TC]
        return carry

    lax.fori_loop(0, FFT_N2 // B_GROUP, stage1, 0)

    def inverse_middle(c, z):
        ab = ab_ref[c]
        zr, zi = z
        kk = k_ref[c].astype(jnp.float32)
        kr, ki = kk[:, :TC], kk[:, TC:]
        pr = zr * kr - zi * ki
        pi = zr * ki + zi * kr
        rhs = jnp.concatenate([jnp.concatenate([pr, pi], axis=1), jnp.concatenate([pi, -pr], axis=1)], axis=0)
        return lax.dot_general(ab, rhs.astype(jnp.bfloat16), (((0,), (0,)), ((), ())),
                               preferred_element_type=jnp.float32)

    def middle(g, carry):
        cs = [g * MID_GROUP + i for i in range(MID_GROUP)]
        zs = [_forward_middle(ab_ref[c], _middle_stage_inputs(y_ref, c)) for c in cs]
        outs = [inverse_middle(c, z) for c, z in zip(cs, zs)]
        for c, v in zip(cs, outs):
            y_ref[pl.ds(c, N2P, stride=SLAB), :] = v[:N2P, :TC]
            y_ref[pl.ds(FFT_N1 + c, N2P, stride=SLAB), :] = v[:N2P, TC:]
        return carry

    lax.fori_loop(0, FFT_N1 // MID_GROUP, middle, 0)

    def stage4(g, carry):
        bs = [g * B_GROUP + i for i in range(B_GROUP)]
        v = jnp.concatenate([y_ref[pl.ds(pl.multiple_of(b * SLAB, 8), 2 * FFT_N1), :] for b in bs], axis=1)
        x = jnp.dot(i2_ref[...], v.astype(jnp.bfloat16), preferred_element_type=jnp.float32)
        for i, b in enumerate(bs):
            o_ref.at[0][pl.ds(b, half, stride=FFT_N2), :] = x[:half, i * TC:(i + 1) * TC]
            o_ref.at[1][pl.ds(b, half, stride=FFT_N2), :] = x[half:, i * TC:(i + 1) * TC]
        return carry

    lax.fori_loop(0, FFT_N2 // B_GROUP, stage4, 0)


def _long_conv(u, kspec, order, consts):
    f1, _, i2, ab = consts
    bsz, lp, _ = u.shape
    assert bsz == 2 and lp == LP
    nt = HY_WIDTH // TC
    return pl.pallas_call(
        _conv_kernel,
        grid=(nt,),
        in_specs=[
            pl.BlockSpec((bsz, lp, TC), lambda i: (0, 0, i)),
            pl.BlockSpec((None, None, FFT_N1, N2P, 2 * TC), lambda i: (order, i, 0, 0, 0)),
            pl.BlockSpec((2 * FFT_N1, FFT_N1), lambda i: (0, 0)),
            pl.BlockSpec((FFT_N1, 2 * FFT_N1), lambda i: (0, 0)),
            pl.BlockSpec((FFT_N1, 2 * N2P, 128), lambda i: (0, 0, 0)),
        ],
        out_specs=pl.BlockSpec((bsz, lp, TC), lambda i: (0, 0, i)),
        out_shape=jax.ShapeDtypeStruct((bsz, lp, HY_WIDTH), jnp.float32),
        scratch_shapes=[pltpu.VMEM((N2P * SLAB, TC), jnp.float32)],
        compiler_params=pltpu.CompilerParams(dimension_semantics=("parallel",), vmem_limit_bytes=V7X_VMEM_LIMIT),
        name="hyena_long_conv",
    )(u, kspec, f1, i2, ab)


def _short_conv_rows(u, w):
    y = pltpu.roll(u, 1, 0) * w[0:1] + u * w[1:2] + pltpu.roll(u, LP - 1, 0) * w[2:3]
    row = lax.broadcasted_iota(jnp.int32, y.shape, 0)
    return jnp.where(row >= PAD, y, 0.0)


def _short_conv_kernel(u_ref, w_ref, o_ref):
    o_ref[...] = _short_conv_rows(u_ref[...], w_ref[...])


def _gate1_kernel(v_ref, u_ref, w_ref, y_ref, b_ref, o_ref):
    o_ref[...] = _short_conv_rows(u_ref[...], w_ref[...]) * (y_ref[...] + v_ref[...] * b_ref[...])


def _gate2_kernel(z_ref, u_ref, w_ref, y_ref, b_ref, g_ref, o_ref):
    g = g_ref[...]
    z = _short_conv_rows(u_ref[...], w_ref[...]) * (y_ref[...] + z_ref[...] * b_ref[...])
    o_ref[...] = (z * (g * jax.nn.sigmoid(g))).astype(o_ref.dtype)


def _seq_tile(col_block):
    return pl.BlockSpec((None, LP, TC), lambda b, i: (b, 0, col_block + i))


def _vec_tile(rows, col_block):
    return pl.BlockSpec((rows, TC), lambda b, i: (0, col_block + i))


def _short_conv(p3, w_short_t):
    bsz = p3.shape[0]
    return pl.pallas_call(
        _short_conv_kernel,
        grid=(bsz, HY_WIDTH // TC),
        in_specs=[_seq_tile(0), _vec_tile(3, 0)],
        out_specs=_seq_tile(0),
        out_shape=jax.ShapeDtypeStruct((bsz, LP, HY_WIDTH), jnp.float32),
        compiler_params=pltpu.CompilerParams(
            dimension_semantics=("parallel", "parallel"), vmem_limit_bytes=V7X_VMEM_LIMIT),
        name="hyena_short_conv",
    )(p3, w_short_t)


def _hyena_gate1(v, p3, w_short_t, y, bias):
    bsz = p3.shape[0]
    nt = HY_WIDTH // TC
    return pl.pallas_call(
        _gate1_kernel,
        grid=(bsz, nt),
        in_specs=[_seq_tile(0), _seq_tile(nt), _vec_tile(3, nt), _seq_tile(0), _vec_tile(1, 0)],
        out_specs=_seq_tile(0),
        out_shape=jax.ShapeDtypeStruct((bsz, LP, HY_WIDTH), jnp.float32),
        compiler_params=pltpu.CompilerParams(
            dimension_semantics=("parallel", "parallel"), vmem_limit_bytes=V7X_VMEM_LIMIT),
        name="hyena_gate1",
    )(v, p3, w_short_t, y, bias.reshape(1, HY_WIDTH))


def _hyena_gate2(z1, p3, w_short_t, y, bias):
    bsz = p3.shape[0]
    nt = HY_WIDTH // TC
    return pl.pallas_call(
        _gate2_kernel,
        grid=(bsz, nt),
        in_specs=[_seq_tile(0), _seq_tile(2 * nt), _vec_tile(3, 2 * nt), _seq_tile(0), _vec_tile(1, 0),
                  _seq_tile(3 * nt)],
        out_specs=_seq_tile(0),
        out_shape=jax.ShapeDtypeStruct((bsz, LP, HY_WIDTH), jnp.bfloat16),
        compiler_params=pltpu.CompilerParams(
            dimension_semantics=("parallel", "parallel"), vmem_limit_bytes=V7X_VMEM_LIMIT),
        name="hyena_gate2",
    )(z1, p3, w_short_t, y, bias.reshape(1, HY_WIDTH), p3)


def _hyena_hidden(w1, b1, fr1, w2, b2, fr2):
    f32 = jnp.float32
    l = L_TOK
    hi = lax.Precision.HIGHEST
    t = jnp.linspace(0.0, 1.0, l, dtype=f32)[:, None]
    w = 2.0 * math.pi * jnp.arange(l, dtype=f32)[:, None] / l
    bands = jnp.linspace(1e-4, HY_BANDS - 1, HY_BANDS, dtype=f32)[None, :]
    z = jnp.concatenate([t, jnp.cos(bands * w), -jnp.sin(bands * w)], axis=-1)
    hid = jnp.sin(fr1 * (jnp.dot(z, w1, precision=hi) + b1))
    hid = jnp.sin(fr2 * (jnp.dot(hid, w2, precision=hi) + b2))
    j = np.arange(FFT_N)
    lag = np.minimum(np.where(j < l, j, FFT_N - j), l - 1)
    fwd = (j < l).astype(np.float32)[:, None]
    bwd = ((j > FFT_N - l) | (j == 0)).astype(np.float32)[:, None]
    rows = hid[lag]
    return jnp.concatenate([rows * fwd, rows * bwd], axis=1).astype(jnp.bfloat16)


def _odd_mix(p3, w_short, w1, b1, fr1, w2, b2, fr2, w3, bias, consts):
    min_decay = math.log(HY_TARGET) / HY_SLOW_DECAY
    max_decay = math.log(HY_TARGET) / HY_FAST_DECAY
    deltas = jnp.abs(jnp.linspace(min_decay, max_decay, HY_WIDTH, dtype=jnp.float32)).reshape(1, HY_WIDTH)
    kspec = _filter_spectrum(_hyena_hidden(w1, b1, fr1, w2, b2, fr2), w3, deltas, consts)
    w_short_t = w_short.T
    v = _short_conv(p3, w_short_t)
    y1 = _long_conv(v, kspec, 0, consts)
    z1 = _hyena_gate1(v, p3, w_short_t, y1, bias[0])
    y2 = _long_conv(z1, kspec, 1, consts)
    return _hyena_gate2(z1, p3, w_short_t, y2, bias[1])


HG_HB = 4
HG_LEVELS = (32, 16, 8, 4, 2, 1)


def _cumsum_rows(g, row, rev):
    for sh in (1, 2, 4, 8, 16, 32):
        if rev:
            g = g + jnp.where(row < HG_CHUNK - sh, pltpu.roll(g, HG_CHUNK - sh, 0), 0.0)
        else:
            g = g + jnp.where(row >= sh, pltpu.roll(g, sh, 0), 0.0)
    return g


def _level_reference(gc, row, m, rev):
    if m >= 4:
        g3 = gc.reshape(HG_CHUNK // (2 * m), 2 * m, HEAD_DIM)
        ref = g3[:, m:m + 1, :] if rev else g3[:, m - 1:m, :]
        return jnp.broadcast_to(ref, g3.shape).reshape(HG_CHUNK, HEAD_DIM)
    r = row & (2 * m - 1)
    target = m if rev else m - 1
    out = gc
    for src in range(2 * m):
        if src != target:
            out = jnp.where(r == src, pltpu.roll(gc, (src - target) % HG_CHUNK, 0), out)
    return out


def _hgrn_gates(q_raw, f_raw, la, lc, omlb, row, rev):
    q = q_raw * jax.nn.sigmoid(q_raw) * (HEAD_DIM ** -0.5)
    e = jnp.exp(-jnp.abs(f_raw))
    ope = 1.0 + e
    log_sig = jnp.minimum(f_raw, 0.0) - jnp.log(ope)
    k = omlb * (jnp.where(f_raw >= 0.0, e, 1.0) / ope)
    b = lc + log_sig
    g = jnp.maximum(la, b) + jnp.log(1.0 + jnp.exp(-jnp.abs(la - b)))
    return q, k, _cumsum_rows(g, row, rev)


def _hgrn_level_operands(q, k, gc, row, rev):
    ops = []
    for m in HG_LEVELS:
        gref = _level_reference(gc, row, m, rev)
        upper = (row & m) != 0
        is_query = jnp.logical_not(upper) if rev else upper
        d = gc - gref
        ops.append((jnp.where(is_query, q, k) * jnp.exp(jnp.where(is_query, d, -d))).astype(jnp.bfloat16))
    return ops


def _hgrn_level_id(rev):
    t = lax.broadcasted_iota(jnp.int32, (HG_CHUNK, HG_CHUNK), 0)
    s = lax.broadcasted_iota(jnp.int32, (HG_CHUNK, HG_CHUNK), 1)
    vis = (s >= t) if rev else (s <= t)
    x = t ^ s
    lvl = jnp.full((HG_CHUNK, HG_CHUNK), len(HG_LEVELS), jnp.int32)
    for i, m in enumerate(HG_LEVELS):
        lvl = jnp.where((x & m) != 0, jnp.minimum(lvl, i), lvl)
    return jnp.where(vis, lvl, -1)


def _hgrn_kernel(qf_ref, if_ref, ff_ref, qb_ref, ib_ref, fb_ref, la_ref, lc_ref, om_ref, of_ref, ob_ref, s_ref):
    f32, bf = jnp.float32, jnp.bfloat16
    nt = (((1,), (1,)), ((), ()))
    tn = (((0,), (0,)), ((), ()))

    @pl.when(pl.program_id(2) == 0)
    def _():
        s_ref[...] = jnp.zeros_like(s_ref)

    row = lax.broadcasted_iota(jnp.int32, (HG_CHUNK, HEAD_DIM), 0)
    refs = ((qf_ref, if_ref, ff_ref, of_ref), (qb_ref, ib_ref, fb_ref, ob_ref))
    inst = [(d, h) for d in range(2) for h in range(HG_HB)]
    level_masks = [[lvl == i for i in range(len(HG_LEVELS) + 1)]
                   for lvl in (_hgrn_level_id(False), _hgrn_level_id(True))]

    pre = []
    for d, h in inst:
        q_ref, i_ref, f_ref, _ = refs[d]
        sl = slice(h * HEAD_DIM, (h + 1) * HEAD_DIM)
        q, k, gc = _hgrn_gates(q_ref[:, sl], f_ref[:, sl], la_ref[d:d + 1, sl], lc_ref[d:d + 1, sl],
                               om_ref[d:d + 1, sl], row, d == 1)
        g_end = gc[0:1, :] if d == 1 else gc[HG_CHUNK - 1:HG_CHUNK, :]
        pre.append(dict(ops=_hgrn_level_operands(q, k, gc, row, d == 1), qk=(q.astype(bf), k.astype(bf)),
                        qe=(q * jnp.exp(gc)).astype(bf), kd=(k * jnp.exp(g_end - gc)).astype(bf),
                        decay=jnp.exp(g_end), v=i_ref[:, sl].astype(bf)))

    attn = []
    for (d, h), pr in zip(inst, pre):
        prods = [lax.dot_general(x, x, nt, preferred_element_type=f32) for x in pr["ops"]]
        prods.append(lax.dot_general(pr["qk"][0], pr["qk"][1], nt, preferred_element_type=f32))
        a = jnp.zeros((HG_CHUNK, HG_CHUNK), f32)
        for i, pmat in enumerate(prods):
            a = jnp.where(level_masks[d][i], pmat, a)
        attn.append(a.astype(bf))

    for (d, h), pr, a in zip(inst, pre, attn):
        o_ref = refs[d][3]
        sl = slice(h * HEAD_DIM, (h + 1) * HEAD_DIM)
        state_t = s_ref[d, h]
        o = jnp.dot(a, pr["v"], preferred_element_type=f32)
        o_ref[:, sl] = o + lax.dot_general(pr["qe"], state_t.astype(bf), nt, preferred_element_type=f32)
        s_ref[d, h] = pr["decay"] * state_t + lax.dot_general(pr["v"], pr["kd"], tn, preferred_element_type=f32)


def _hgrn2(p, col0, lb):
    bsz, lp, _ = p.shape
    nchunk = lp // HG_CHUNK
    cw = HG_HB * HEAD_DIM
    ng = HG_WIDTH // cw
    off = col0 // cw
    la = jnp.log(lb)
    lc = jnp.log1p(-lb)
    om = 1.0 - lb
    fwd = lambda seg: pl.BlockSpec((None, HG_CHUNK, cw), lambda b, g, c: (b, c, off + seg * ng + g))
    bwd = lambda seg: pl.BlockSpec((None, HG_CHUNK, cw), lambda b, g, c: (b, nchunk - 1 - c, off + seg * ng + g))
    par = pl.BlockSpec((2, cw), lambda b, g, c: (0, g))
    return pl.pallas_call(
        _hgrn_kernel,
        grid=(bsz, ng, nchunk),
        in_specs=[fwd(0), fwd(1), fwd(2), bwd(0), bwd(1), bwd(3), par, par, par],
        out_specs=[pl.BlockSpec((None, HG_CHUNK, cw), lambda b, g, c: (b, c, g)),
                   pl.BlockSpec((None, HG_CHUNK, cw), lambda b, g, c: (b, nchunk - 1 - c, g))],
        out_shape=[jax.ShapeDtypeStruct((bsz, lp, HG_WIDTH), jnp.float32)] * 2,
        scratch_shapes=[pltpu.VMEM((2, HG_HB, HEAD_DIM, HEAD_DIM), jnp.float32)],
        compiler_params=pltpu.CompilerParams(
            dimension_semantics=("parallel", "parallel", "arbitrary"), vmem_limit_bytes=V7X_VMEM_LIMIT),
        name="hgrn2_bidirectional",
    )(p, p, p, p, p, p, la, lc, om)


EG_ROWS = 320


def _even_gate_kernel(oa_ref, of_ref, ob_ref, ga_ref, gb_ref, gain_ref, y_ref):
    ga = ga_ref[...]
    y_ref[:, :NA_WIDTH] = (oa_ref[...] * (ga * jax.nn.sigmoid(ga))).astype(y_ref.dtype)
    for h in range(HG_HEADS):
        sl = slice(h * HEAD_DIM, (h + 1) * HEAD_DIM)
        oh = of_ref[:, sl] + ob_ref[:, sl]
        oh = oh * lax.rsqrt(jnp.mean(oh * oh, axis=-1, keepdims=True) + EPS) * gain_ref[:, sl]
        gb = gb_ref[:, sl]
        y_ref[:, NA_WIDTH + h * HEAD_DIM:NA_WIDTH + (h + 1) * HEAD_DIM] = (
            oh * (gb * jax.nn.sigmoid(gb))).astype(y_ref.dtype)


def _even_gate(oa, o_fwd, o_bwd, p, gate_col0, norm_gain):
    bsz, lp, _ = oa.shape
    assert gate_col0 % NA_WIDTH == 0
    goff = gate_col0 // NA_WIDTH
    half = lambda off: pl.BlockSpec((None, EG_ROWS, NA_WIDTH), lambda b, r: (b, r, off))
    return pl.pallas_call(
        _even_gate_kernel,
        grid=(bsz, lp // EG_ROWS),
        in_specs=[half(0), half(0), half(0), half(goff), half(goff + 1),
                  pl.BlockSpec((1, HG_WIDTH), lambda b, r: (0, 0))],
        out_specs=pl.BlockSpec((None, EG_ROWS, D_MODEL), lambda b, r: (b, r, 0)),
        out_shape=jax.ShapeDtypeStruct((bsz, lp, D_MODEL), jnp.bfloat16),
        compiler_params=pltpu.CompilerParams(
            dimension_semantics=("parallel", "parallel"), vmem_limit_bytes=V7X_VMEM_LIMIT),
        name="even_gate",
    )(oa, o_fwd, o_bwd, p, p, norm_gain.reshape(1, HG_WIDTH))


NA_MASKED = -1e30
NA_GROUP = 4


def _na_bias_table(rpb):
    cols = np.arange(GRID_W)
    c0 = np.clip(cols - NA_KW // 2, 0, GRID_W - NA_KW)
    col_ok = (cols[None, :] >= c0[:, None]) & (cols[None, :] < c0[:, None] + NA_KW)
    col_idx = np.clip(cols[None, :] - cols[:, None] + NA_KW - 1, 0, 2 * NA_KW - 2)
    table = jnp.where(col_ok[None, None], rpb.astype(jnp.float32)[:, :, col_idx], NA_MASKED)
    return jnp.concatenate([table[:, :-1], table[:, 1:]], axis=-1)


def _na_kernel(q_ref, k_ref, v_ref, t_ref, mb_ref, o_ref):
    f32, bf = jnp.float32, jnp.bfloat16
    rows = (q_ref.shape[0] - HG_CHUNK) // GRID_W
    scale = HEAD_DIM ** -0.5
    nt = (((1,), (1,)), ((), ()))
    km = k_ref[PAD:PAD + N_META, :].astype(bf)
    vm = v_ref[PAD:PAD + N_META, :].astype(bf)
    mb = mb_ref[...]
    o_ref[0:PAD, :] = jnp.zeros((PAD, HEAD_DIM), f32)

    qm = q_ref[PAD:PAD + N_META, :].astype(bf)
    s = lax.dot_general(qm, km, nt, preferred_element_type=f32) * scale + mb
    e = jnp.exp(s - jnp.max(s, axis=-1, keepdims=True))
    o_ref[PAD:PAD + N_META, :] = (jnp.dot(e.astype(bf), vm, preferred_element_type=f32)
                                  / jnp.sum(e, axis=-1, keepdims=True))

    def row_group(g, carry):
        rs = [g * NA_GROUP + i for i in range(NA_GROUP)]
        scores = []
        for r in rs:
            r0 = jnp.clip(r - NA_KH // 2, 0, rows - NA_KH)
            q = q_ref[pl.ds(pl.multiple_of(HG_CHUNK + GRID_W * r, GRID_W), GRID_W), :].astype(bf)
            k0 = pl.multiple_of(HG_CHUNK + GRID_W * r0, GRID_W)
            kw = k_ref[pl.ds(k0, NA_KH * GRID_W), :].astype(bf)
            s_win = lax.dot_general(q, kw, nt, preferred_element_type=f32)
            s_meta = lax.dot_general(q, km, nt, preferred_element_type=f32)
            scores.append((r, r0, k0, s_win, s_meta))
        probs = []
        for r, r0, k0, s_win, s_meta in scores:
            bias = jnp.concatenate([t_ref[r0 + i - r + NA_KH - 1] for i in range(0, NA_KH, 2)], axis=1)
            s_win = s_win * scale + bias
            s_meta = s_meta * scale + mb
            m = jnp.maximum(jnp.max(s_win, axis=-1, keepdims=True), jnp.max(s_meta, axis=-1, keepdims=True))
            pw = jnp.exp(s_win - m)
            pm = jnp.exp(s_meta - m)
            den = jnp.sum(pw, axis=-1, keepdims=True) + jnp.sum(pm, axis=-1, keepdims=True)
            probs.append((r, k0, pw.astype(bf), pm.astype(bf), den))
        for r, k0, pw, pm, den in probs:
            vw = v_ref[pl.ds(k0, NA_KH * GRID_W), :].astype(bf)
            o = jnp.dot(pw, vw, preferred_element_type=f32) + jnp.dot(pm, vm, preferred_element_type=f32)
            o_ref[pl.ds(pl.multiple_of(HG_CHUNK + GRID_W * r, GRID_W), GRID_W), :] = o / den
        return carry

    lax.fori_loop(0, rows // NA_GROUP, row_group, 0)


def _natten(p, rpb, meta_bias):
    bsz, lp, _ = p.shape
    heads = NA_WIDTH // HEAD_DIM
    blk = lambda seg: pl.BlockSpec((None, lp, HEAD_DIM), lambda b, h: (b, 0, seg * heads + h))
    table = _na_bias_table(rpb)
    return pl.pallas_call(
        _na_kernel,
        grid=(bsz, heads),
        in_specs=[blk(0), blk(1), blk(2),
                  pl.BlockSpec((None, 2 * NA_KH - 2, GRID_W, 2 * GRID_W), lambda b, h: (h, 0, 0, 0)),
                  pl.BlockSpec((None, 1, N_META), lambda b, h: (h, 0, 0))],
        out_specs=pl.BlockSpec((None, lp, HEAD_DIM), lambda b, h: (b, 0, h)),
        out_shape=jax.ShapeDtypeStruct((bsz, lp, NA_WIDTH), jnp.float32),
        compiler_params=pltpu.CompilerParams(
            dimension_semantics=("parallel", "parallel"), vmem_limit_bytes=V7X_VMEM_LIMIT),
        name="neighbourhood_attention",
    )(p, p, p, table, meta_bias.astype(jnp.float32).reshape(heads, 1, N_META))


def _even_mix(p, rpb, meta_bias, lb, norm_gain):
    oa = _natten(p, rpb, meta_bias)
    o_fwd, o_bwd = _hgrn2(p, 3 * NA_WIDTH, lb)
    return _even_gate(oa, o_fwd, o_bwd, p, 3 * NA_WIDTH + 4 * HG_WIDTH, norm_gain)


def kernel(x, meta_tokens, norm_pre, norm_post, ev_w_in, ev_w_out, na_rpb, na_meta_bias, hg_lower, hg_norm,
           od_w_in, od_w_out, hy_short, hy_ffn_w1, hy_ffn_b1, hy_ffn_freq1, hy_ffn_w2, hy_ffn_b2, hy_ffn_freq2,
           hy_ffn_w3, hy_bias):
    b = x.shape[0]
    depth = norm_pre.shape[0]
    f32 = jnp.float32
    lb_all = jnp.cumsum(jax.nn.softmax(hg_lower.astype(f32), axis=0), axis=0)
    lb_all = lb_all - lb_all[:1]
    meta = jnp.broadcast_to(meta_tokens.astype(f32)[None], (b, N_META, D_MODEL))
    h = jnp.concatenate([jnp.zeros((b, PAD, D_MODEL), f32), meta, x.astype(f32)], axis=1)
    h = h.reshape(b * LP, D_MODEL)
    consts = _dft_constants()
    hn = _rmsnorm(h, norm_pre[0], jnp.bfloat16)
    for layer in range(depth):
        j = layer // 2
        if layer % 2 == 0:
            p = _matmul(hn, ev_w_in, j).reshape(b, LP, -1)
            y = _even_mix(p, na_rpb[j], na_meta_bias[j], lb_all[j], hg_norm[j]).reshape(b * LP, D_MODEL)
            w_out = ev_w_out
        else:
            p3 = _matmul(hn, od_w_in, j).reshape(b, LP, -1)
            y = _odd_mix(p3, hy_short[j], hy_ffn_w1[j], hy_ffn_b1[j], hy_ffn_freq1[j], hy_ffn_w2[j],
                         hy_ffn_b2[j], hy_ffn_freq2[j], hy_ffn_w3[j], hy_bias[j], consts)
            y = y.reshape(b * LP, D_MODEL)
            w_out = od_w_out
        out = _matmul(y, w_out, j)
        if layer + 1 < depth:
            h, hn = _residual_rmsnorm(h, out, norm_post[layer], norm_pre[layer + 1])
        else:
            h = _residual_rmsnorm(h, out, norm_post[layer])
    return h.reshape(b, LP, D_MODEL)[:, PAD + N_META:].astype(x.dtype)
```

```python
import functools
import math

import jax
import jax.numpy as jnp
import numpy as np
from jax import lax
from jax.experimental import pallas as pl
from jax.experimental.pallas import tpu as pltpu

D_MODEL = 4096
SEQ = 4096
N_META = 16
GRID_W = 64
HEAD_DIM = 128
NA_WIDTH = D_MODEL // 2
NA_KH = 8
NA_KW = 16
HG_WIDTH = D_MODEL // 2
HG_HEADS = HG_WIDTH // HEAD_DIM
HG_CHUNK = 64
HY_WIDTH = D_MODEL
HY_ORDER = 2
HY_EMB = 33
HY_BANDS = (HY_EMB - 1) // 2
HY_FAST_DECAY = 0.3
HY_SLOW_DECAY = 1.5
HY_TARGET = 1e-2
EPS = 1e-6

PAD = HG_CHUNK - N_META
LP = PAD + N_META + SEQ
L_TOK = N_META + SEQ

V7X_VMEM_LIMIT = 56 * 1024 * 1024


def _rmsnorm_kernel(x_ref, g_ref, o_ref):
    x = x_ref[...]
    r = lax.rsqrt(jnp.mean(x * x, axis=-1, keepdims=True) + EPS)
    o_ref[...] = (x * r * g_ref[...]).astype(o_ref.dtype)


def _rmsnorm(x2d, g, out_dtype, tr=320):
    m, d = x2d.shape
    return pl.pallas_call(
        _rmsnorm_kernel,
        grid=(m // tr,),
        in_specs=[pl.BlockSpec((tr, d), lambda i: (i, 0)), pl.BlockSpec((1, d), lambda i: (0, 0))],
        out_specs=pl.BlockSpec((tr, d), lambda i: (i, 0)),
        out_shape=jax.ShapeDtypeStruct((m, d), out_dtype),
        compiler_params=pltpu.CompilerParams(vmem_limit_bytes=V7X_VMEM_LIMIT),
        name="rmsnorm_pre",
    )(x2d, g.reshape(1, d))


def _post_kernel(h_ref, y_ref, g_ref, o_ref):
    y = y_ref[...]
    r = lax.rsqrt(jnp.mean(y * y, axis=-1, keepdims=True) + EPS)
    o_ref[...] = h_ref[...] + y * r * g_ref[...]


def _post_pre_kernel(h_ref, y_ref, g_ref, gn_ref, o_ref, n_ref):
    y = y_ref[...]
    r = lax.rsqrt(jnp.mean(y * y, axis=-1, keepdims=True) + EPS)
    h = h_ref[...] + y * r * g_ref[...]
    o_ref[...] = h
    rn = lax.rsqrt(jnp.mean(h * h, axis=-1, keepdims=True) + EPS)
    n_ref[...] = (h * rn * gn_ref[...]).astype(n_ref.dtype)


def _residual_rmsnorm(h2d, y2d, g, g_next=None, tr=320):
    m, d = h2d.shape
    row = pl.BlockSpec((tr, d), lambda i: (i, 0))
    vec = pl.BlockSpec((1, d), lambda i: (0, 0))
    params = pltpu.CompilerParams(dimension_semantics=("parallel",), vmem_limit_bytes=V7X_VMEM_LIMIT)
    if g_next is None:
        return pl.pallas_call(
            _post_kernel, grid=(m // tr,), in_specs=[row, row, vec], out_specs=row,
            out_shape=jax.ShapeDtypeStruct((m, d), jnp.float32), compiler_params=params,
            name="residual_rmsnorm_post",
        )(h2d, y2d, g.reshape(1, d))
    return pl.pallas_call(
        _post_pre_kernel, grid=(m // tr,), in_specs=[row, row, vec, vec], out_specs=[row, row],
        out_shape=[jax.ShapeDtypeStruct((m, d), jnp.float32), jax.ShapeDtypeStruct((m, d), jnp.bfloat16)],
        compiler_params=params, name="residual_rmsnorm_post_pre",
    )(h2d, y2d, g.reshape(1, d), g_next.reshape(1, d))


def _matmul_kernel(x_ref, w_ref, o_ref):
    o_ref[...] = jnp.dot(x_ref[...], w_ref[...].astype(jnp.bfloat16),
                         preferred_element_type=jnp.float32).astype(o_ref.dtype)


def _matmul(x, w_stack, layer, tm=1040, tn=512):
    m, kd = x.shape
    _, _, n = w_stack.shape
    assert m % tm == 0 and n % tn == 0
    return pl.pallas_call(
        _matmul_kernel,
        grid=(m // tm, n // tn),
        in_specs=[pl.BlockSpec((tm, kd), lambda i, j: (i, 0)),
                  pl.BlockSpec((None, kd, tn), lambda i, j: (layer, 0, j))],
        out_specs=pl.BlockSpec((tm, tn), lambda i, j: (i, j)),
        out_shape=jax.ShapeDtypeStruct((m, n), jnp.float32),
        compiler_params=pltpu.CompilerParams(
            dimension_semantics=("parallel", "parallel"), vmem_limit_bytes=V7X_VMEM_LIMIT),
        name="projection_matmul",
    )(x, w_stack)


FFT_N1 = 128
FFT_N2 = 65
FFT_N = FFT_N1 * FFT_N2
N2P = 80
SLAB = 264
TC = 128
MID_GROUP = 16
NC_HALF = FFT_N1 // 2 + 1
HALF_P = 72
SLAB_S = 152
C_GROUP = 13
FILL_CHUNKS = 13
B_GROUP = 13


def _dft_constants():
    a = np.arange(FFT_N1)
    c = np.arange(FFT_N1)
    w1 = np.exp(-2j * np.pi * np.outer(c, a) / FFT_N1)
    w1r, w1i = w1.real, w1.imag
    half = FFT_N1 // 2
    f1 = np.block([[w1r[:, :half], -w1i[:, :half]], [w1i[:, :half], w1r[:, :half]]])
    f1_real = np.zeros((2 * HALF_P, FFT_N1), np.float64)
    f1_real[:NC_HALF] = w1r[:NC_HALF]
    f1_real[HALF_P:HALF_P + NC_HALF] = w1i[:NC_HALF]
    g = np.conj(w1).T
    gr, gi = g.real[:half], g.imag[:half]
    i2 = np.block([[gr, -gi], [gi, gr]])
    b = np.arange(FFT_N2)
    d = np.arange(FFT_N2)
    w2 = np.exp(-2j * np.pi * np.outer(d, b) / FFT_N2)
    tw = np.exp(-2j * np.pi * np.outer(c, b) / FFT_N)
    m = w2[None, :, :] * tw[:, None, :]
    ab = np.zeros((FFT_N1, 2 * N2P, 128), np.float64)
    m[NC_HALF:] = m[NC_HALF:, ::-1, :]
    ab[:, :FFT_N2, :FFT_N2] = m.real
    ab[:, N2P:N2P + FFT_N2, :FFT_N2] = m.imag
    bf = jnp.bfloat16
    return (jnp.asarray(f1, bf), jnp.asarray(f1_real, bf), jnp.asarray(i2, bf), jnp.asarray(ab, bf))


def _middle_stage_inputs(y_ref, c, imag_row=FFT_N1, pitch=SLAB):
    yr = y_ref[pl.ds(c, N2P, stride=pitch), :]
    yi = y_ref[pl.ds(imag_row + c, N2P, stride=pitch), :]
    return jnp.concatenate([yr, yi], axis=1).astype(jnp.bfloat16)


def _forward_middle(ab, y2):
    y2 = jnp.concatenate([y2, jnp.zeros((128 - N2P, 2 * TC), jnp.bfloat16)], axis=0)
    q = jnp.dot(ab, y2, preferred_element_type=jnp.float32)
    zr = q[:N2P, :TC] - q[N2P:, TC:]
    zi = q[N2P:, :TC] + q[:N2P, TC:]
    return zr, zi


def _spectrum_kernel(h_ref, t_ref, w3f0_ref, w3f1_ref, w3b0_ref, w3b1_ref, delta_ref, f1_ref, ab_ref, k_ref,
                     kt_ref, y_ref):
    w3 = jnp.concatenate([jnp.concatenate([w3f0_ref[...], w3f1_ref[...]], axis=1),
                          jnp.concatenate([w3b0_ref[...], w3b1_ref[...]], axis=1)], axis=0).astype(jnp.bfloat16)
    delta = delta_ref[...]
    rows = FFT_N // FILL_CHUNKS

    def fill(i, carry):
        r0 = pl.multiple_of(i * rows, 8)
        kt = jnp.dot(h_ref[pl.ds(r0, rows), :], w3, preferred_element_type=jnp.float32)
        window = jnp.exp(t_ref[pl.ds(r0, rows), :] * delta)
        for n in range(HY_ORDER):
            kt_ref[n, pl.ds(r0, rows), :] = kt[:, n * TC:(n + 1) * TC] * window
        return carry

    lax.fori_loop(0, FILL_CHUNKS, fill, 0)
    y_ref[pl.ds(FFT_N2 * SLAB_S, (N2P - FFT_N2) * SLAB_S), :] = jnp.zeros(((N2P - FFT_N2) * SLAB_S, TC), jnp.float32)

    for n in range(HY_ORDER):
        def stage1(g, carry):
            bs = [g * B_GROUP + i for i in range(B_GROUP)]
            x = jnp.concatenate([kt_ref.at[n][pl.ds(b, FFT_N1, stride=FFT_N2), :] for b in bs], axis=1)
            y = jnp.dot(f1_ref[...], x.astype(jnp.bfloat16), preferred_element_type=jnp.float32)
            for i, b in enumerate(bs):
                y_ref[pl.ds(pl.multiple_of(b * SLAB_S, 8), 2 * HALF_P), :] = y[:, i * TC:(i + 1) * TC]
            return carry

        lax.fori_loop(0, FFT_N2 // B_GROUP, stage1, 0)

        def stage2(g, carry):
            cs = [g * C_GROUP + i for i in range(C_GROUP)]
            zs = [_forward_middle(ab_ref[c], _middle_stage_inputs(y_ref, c, HALF_P, SLAB_S)) for c in cs]
            for c, (zr, zi) in zip(cs, zs):
                k_ref[n, c] = (jnp.concatenate([zr, zi], axis=1) * (1.0 / FFT_N)).astype(jnp.bfloat16)
            return carry

        lax.fori_loop(0, NC_HALF // C_GROUP, stage2, 0)


def _filter_spectrum(h2, w3, deltas, consts):
    _, f1_real, _, ab = consts
    nt = HY_WIDTH // TC
    j = np.arange(FFT_N)
    lag = np.where(j < L_TOK, j, FFT_N - j).astype(np.float32)
    neg_t = jnp.asarray(np.broadcast_to((-lag / np.float32(L_TOK - 1))[:, None], (FFT_N, TC)))
    w3_cols = lambda direction, order: pl.BlockSpec((64, TC), lambda i: (0, (direction * HY_ORDER + order) * nt + i))
    return pl.pallas_call(
        _spectrum_kernel,
        grid=(nt,),
        in_specs=[
            pl.BlockSpec((FFT_N, 128), lambda i: (0, 0)),
            pl.BlockSpec((FFT_N, TC), lambda i: (0, 0)),
            w3_cols(0, 0), w3_cols(0, 1), w3_cols(1, 0), w3_cols(1, 1),
            pl.BlockSpec((1, TC), lambda i: (0, i)),
            pl.BlockSpec((2 * HALF_P, FFT_N1), lambda i: (0, 0)),
            pl.BlockSpec((FFT_N1, 2 * N2P, 128), lambda i: (0, 0, 0)),
        ],
        out_specs=pl.BlockSpec((HY_ORDER, None, NC_HALF, N2P, 2 * TC), lambda i: (0, i, 0, 0, 0)),
        out_shape=jax.ShapeDtypeStruct((HY_ORDER, nt, NC_HALF, N2P, 2 * TC), jnp.bfloat16),
        scratch_shapes=[pltpu.VMEM((HY_ORDER, FFT_N, TC), jnp.float32), pltpu.VMEM((N2P * SLAB_S, TC), jnp.float32)],
        compiler_params=pltpu.CompilerParams(dimension_semantics=("parallel",), vmem_limit_bytes=V7X_VMEM_LIMIT),
        name="hyena_filter_spectrum",
    )(h2, neg_t, w3, w3, w3, w3, deltas, f1_real, ab)


def _conv_kernel(u_ref, k_ref, f1_ref, i2_ref, ab_ref, o_ref, y_ref):
    y_ref[pl.ds(FFT_N2 * SLAB, (N2P - FFT_N2) * SLAB), :] = jnp.zeros(((N2P - FFT_N2) * SLAB, TC), jnp.float32)
    half = FFT_N1 // 2

    def stage1(g, carry):
        bs = [g * B_GROUP + i for i in range(B_GROUP)]
        x = jnp.concatenate(
            [jnp.concatenate([u_ref.at[ri][pl.ds(b, half, stride=FFT_N2), :] for b in bs], axis=1) for ri in range(2)],
            axis=0)
        y = jnp.dot(f1_ref[...], x.astype(jnp.bfloat16), preferred_element_type=jnp.float32)
        for i, b in enumerate(bs):
            y_ref[pl.ds(pl.multiple_of(b * SLAB, 8), 2 * FFT_N1), :] = y[:, i * TC:(i + 1) * TC]
        return carry

    lax.fori_loop(0, FFT_N2 // B_GROUP, stage1, 0)

    def inverse_middle(c, z):
        ab = ab_ref[c]
        zr, zi = z
        mirrored = c >= NC_HALF
        kk = k_ref[jnp.where(mirrored, FFT_N1 - c, c)].astype(jnp.float32)
        kr, ki = kk[:, :TC], kk[:, TC:] * jnp.where(mirrored, -1.0, 1.0)
        pr = zr * kr - zi * ki
        pi = zr * ki + zi * kr
        rhs = jnp.concatenate([jnp.concatenate([pr, pi], axis=1), jnp.concatenate([pi, -pr], axis=1)], axis=0)
        return lax.dot_general(ab, rhs.astype(jnp.bfloat16), (((0,), (0,)), ((), ())),
                               preferred_element_type=jnp.float32)

    def middle(g, carry):
        cs = [g * MID_GROUP + i for i in range(MID_GROUP)]
        zs = [_forward_middle(ab_ref[c], _middle_stage_inputs(y_ref, c)) for c in cs]
        outs = [inverse_middle(c, z) for c, z in zip(cs, zs)]
        for c, v in zip(cs, outs):
            y_ref[pl.ds(c, N2P, stride=SLAB), :] = v[:N2P, :TC]
            y_ref[pl.ds(FFT_N1 + c, N2P, stride=SLAB), :] = v[:N2P, TC:]
        return carry

    lax.fori_loop(0, FFT_N1 // MID_GROUP, middle, 0)

    def stage4(g, carry):
        bs = [g * B_GROUP + i for i in range(B_GROUP)]
        v = jnp.concatenate([y_ref[pl.ds(pl.multiple_of(b * SLAB, 8), 2 * FFT_N1), :] for b in bs], axis=1)
        x = jnp.dot(i2_ref[...], v.astype(jnp.bfloat16), preferred_element_type=jnp.float32)
        for i, b in enumerate(bs):
            o_ref.at[0][pl.ds(b, half, stride=FFT_N2), :] = x[:half, i * TC:(i + 1) * TC]
            o_ref.at[1][pl.ds(b, half, stride=FFT_N2), :] = x[half:, i * TC:(i + 1) * TC]
        return carry

    lax.fori_loop(0, FFT_N2 // B_GROUP, stage4, 0)


def _long_conv(u, kspec, order, consts):
    f1, _, i2, ab = consts
    bsz, lp, _ = u.shape
    assert bsz == 2 and lp == LP
    nt = HY_WIDTH // TC
    return pl.pallas_call(
        _conv_kernel,
        grid=(nt,),
        in_specs=[
            pl.BlockSpec((bsz, lp, TC), lambda i: (0, 0, i)),
            pl.BlockSpec((None, None, NC_HALF, N2P, 2 * TC), lambda i: (order, i, 0, 0, 0)),
            pl.BlockSpec((2 * FFT_N1, FFT_N1), lambda i: (0, 0)),
            pl.BlockSpec((FFT_N1, 2 * FFT_N1), lambda i: (0, 0)),
            pl.BlockSpec((FFT_N1, 2 * N2P, 128), lambda i: (0, 0, 0)),
        ],
        out_specs=pl.BlockSpec((bsz, lp, TC), lambda i: (0, 0, i)),
        out_shape=jax.ShapeDtypeStruct((bsz, lp, HY_WIDTH), jnp.float32),
        scratch_shapes=[pltpu.VMEM((N2P * SLAB, TC), jnp.float32)],
        compiler_params=pltpu.CompilerParams(dimension_semantics=("parallel",), vmem_limit_bytes=V7X_VMEM_LIMIT),
        name="hyena_long_conv",
    )(u, kspec, f1, i2, ab)


def _short_conv_rows(u, w):
    y = pltpu.roll(u, 1, 0) * w[0:1] + u * w[1:2] + pltpu.roll(u, LP - 1, 0) * w[2:3]
    row = lax.broadcasted_iota(jnp.int32, y.shape, 0)
    return jnp.where(row >= PAD, y, 0.0)


def _short_conv_kernel(u_ref, w_ref, o_ref):
    o_ref[...] = _short_conv_rows(u_ref[...], w_ref[...])


def _gate1_kernel(v_ref, u_ref, w_ref, y_ref, b_ref, o_ref):
    o_ref[...] = _short_conv_rows(u_ref[...], w_ref[...]) * (y_ref[...] + v_ref[...] * b_ref[...])


def _gate2_kernel(z_ref, u_ref, w_ref, y_ref, b_ref, g_ref, o_ref):
    g = g_ref[...]
    z = _short_conv_rows(u_ref[...], w_ref[...]) * (y_ref[...] + z_ref[...] * b_ref[...])
    o_ref[...] = (z * (g * jax.nn.sigmoid(g))).astype(o_ref.dtype)


def _seq_tile(col_block):
    return pl.BlockSpec((None, LP, TC), lambda b, i: (b, 0, col_block + i))


def _vec_tile(rows, col_block):
    return pl.BlockSpec((rows, TC), lambda b, i: (0, col_block + i))


def _short_conv(p3, w_short_t):
    bsz = p3.shape[0]
    return pl.pallas_call(
        _short_conv_kernel,
        grid=(bsz, HY_WIDTH // TC),
        in_specs=[_seq_tile(0), _vec_tile(3, 0)],
        out_specs=_seq_tile(0),
        out_shape=jax.ShapeDtypeStruct((bsz, LP, HY_WIDTH), jnp.float32),
        compiler_params=pltpu.CompilerParams(
            dimension_semantics=("parallel", "parallel"), vmem_limit_bytes=V7X_VMEM_LIMIT),
        name="hyena_short_conv",
    )(p3, w_short_t)


def _hyena_gate1(v, p3, w_short_t, y, bias):
    bsz = p3.shape[0]
    nt = HY_WIDTH // TC
    return pl.pallas_call(
        _gate1_kernel,
        grid=(bsz, nt),
        in_specs=[_seq_tile(0), _seq_tile(nt), _vec_tile(3, nt), _seq_tile(0), _vec_tile(1, 0)],
        out_specs=_seq_tile(0),
        out_shape=jax.ShapeDtypeStruct((bsz, LP, HY_WIDTH), jnp.float32),
        compiler_params=pltpu.CompilerParams(
            dimension_semantics=("parallel", "parallel"), vmem_limit_bytes=V7X_VMEM_LIMIT),
        name="hyena_gate1",
    )(v, p3, w_short_t, y, bias.reshape(1, HY_WIDTH))


def _hyena_gate2(z1, p3, w_short_t, y, bias):
    bsz = p3.shape[0]
    nt = HY_WIDTH // TC
    return pl.pallas_call(
        _gate2_kernel,
        grid=(bsz, nt),
        in_specs=[_seq_tile(0), _seq_tile(2 * nt), _vec_tile(3, 2 * nt), _seq_tile(0), _vec_tile(1, 0),
                  _seq_tile(3 * nt)],
        out_specs=_seq_tile(0),
        out_shape=jax.ShapeDtypeStruct((bsz, LP, HY_WIDTH), jnp.bfloat16),
        compiler_params=pltpu.CompilerParams(
            dimension_semantics=("parallel", "parallel"), vmem_limit_bytes=V7X_VMEM_LIMIT),
        name="hyena_gate2",
    )(z1, p3, w_short_t, y, bias.reshape(1, HY_WIDTH), p3)


def _hyena_hidden(w1, b1, fr1, w2, b2, fr2):
    f32 = jnp.float32
    l = L_TOK
    hi = lax.Precision.HIGHEST
    t = jnp.linspace(0.0, 1.0, l, dtype=f32)[:, None]
    w = 2.0 * math.pi * jnp.arange(l, dtype=f32)[:, None] / l
    bands = jnp.linspace(1e-4, HY_BANDS - 1, HY_BANDS, dtype=f32)[None, :]
    z = jnp.concatenate([t, jnp.cos(bands * w), -jnp.sin(bands * w)], axis=-1)
    hid = jnp.sin(fr1 * (jnp.dot(z, w1, precision=hi) + b1))
    hid = jnp.sin(fr2 * (jnp.dot(hid, w2, precision=hi) + b2))
    j = np.arange(FFT_N)
    lag = np.minimum(np.where(j < l, j, FFT_N - j), l - 1)
    fwd = (j < l).astype(np.float32)[:, None]
    bwd = ((j > FFT_N - l) | (j == 0)).astype(np.float32)[:, None]
    rows = hid[lag]
    return jnp.concatenate([rows * fwd, rows * bwd], axis=1).astype(jnp.bfloat16)


def _odd_mix(p3, w_short, w1, b1, fr1, w2, b2, fr2, w3, bias, consts):
    min_decay = math.log(HY_TARGET) / HY_SLOW_DECAY
    max_decay = math.log(HY_TARGET) / HY_FAST_DECAY
    deltas = jnp.abs(jnp.linspace(min_decay, max_decay, HY_WIDTH, dtype=jnp.float32)).reshape(1, HY_WIDTH)
    kspec = _filter_spectrum(_hyena_hidden(w1, b1, fr1, w2, b2, fr2), w3, deltas, consts)
    w_short_t = w_short.T
    v = _short_conv(p3, w_short_t)
    y1 = _long_conv(v, kspec, 0, consts)
    z1 = _hyena_gate1(v, p3, w_short_t, y1, bias[0])
    y2 = _long_conv(z1, kspec, 1, consts)
    return _hyena_gate2(z1, p3, w_short_t, y2, bias[1])


HG_HB = 4
HG_LEVELS = (32, 16, 8, 4, 2, 1)


def _cumsum_rows(g, rev):
    f32, bf = jnp.float32, jnp.bfloat16
    t = lax.broadcasted_iota(jnp.int32, (HG_CHUNK, 3 * HG_CHUNK), 0)
    s = lax.broadcasted_iota(jnp.int32, (HG_CHUNK, 3 * HG_CHUNK), 1) & (HG_CHUNK - 1)
    tri = jnp.where((s >= t) if rev else (s <= t), 1.0, 0.0).astype(bf)
    hi = g.astype(bf)
    r1 = g - hi.astype(f32)
    mid = r1.astype(bf)
    lo = (r1 - mid.astype(f32)).astype(bf)
    return jnp.dot(tri, jnp.concatenate([hi, mid, lo], axis=0), preferred_element_type=f32)


def _level_reference(gc, row, m, rev):
    if m >= 4:
        g3 = gc.reshape(HG_CHUNK // (2 * m), 2 * m, HEAD_DIM)
        ref = g3[:, m:m + 1, :] if rev else g3[:, m - 1:m, :]
        return jnp.broadcast_to(ref, g3.shape).reshape(HG_CHUNK, HEAD_DIM)
    r = row & (2 * m - 1)
    target = m if rev else m - 1
    out = gc
    for src in range(2 * m):
        if src != target:
            out = jnp.where(r == src, pltpu.roll(gc, (src - target) % HG_CHUNK, 0), out)
    return out


def _hgrn_gates(q_raw, f_raw, la, lc, omlb, row, rev):
    q = q_raw * jax.nn.sigmoid(q_raw) * (HEAD_DIM ** -0.5)
    e = jnp.exp(-jnp.abs(f_raw))
    ope = 1.0 + e
    log_sig = jnp.minimum(f_raw, 0.0) - jnp.log(ope)
    k = omlb * (jnp.where(f_raw >= 0.0, e, 1.0) / ope)
    b = lc + log_sig
    g = jnp.maximum(la, b) + jnp.log(1.0 + jnp.exp(-jnp.abs(la - b)))
    return q, k, _cumsum_rows(g, rev)


def _hgrn_level_operands(q, k, gc, row, rev):
    ops = []
    for m in HG_LEVELS:
        gref = _level_reference(gc, row, m, rev)
        upper = (row & m) != 0
        is_query = jnp.logical_not(upper) if rev else upper
        d = gc - gref
        ops.append((jnp.where(is_query, q, k) * jnp.exp(jnp.where(is_query, d, -d))).astype(jnp.bfloat16))
    return ops


def _hgrn_level_id(rev):
    t = lax.broadcasted_iota(jnp.int32, (HG_CHUNK, HG_CHUNK), 0)
    s = lax.broadcasted_iota(jnp.int32, (HG_CHUNK, HG_CHUNK), 1)
    vis = (s >= t) if rev else (s <= t)
    x = t ^ s
    lvl = jnp.full((HG_CHUNK, HG_CHUNK), len(HG_LEVELS), jnp.int32)
    for i, m in enumerate(HG_LEVELS):
        lvl = jnp.where((x & m) != 0, jnp.minimum(lvl, i), lvl)
    return jnp.where(vis, lvl, -1)


def _hgrn_kernel(qf_ref, if_ref, ff_ref, qb_ref, ib_ref, fb_ref, la_ref, lc_ref, om_ref, of_ref, ob_ref, s_ref):
    f32, bf = jnp.float32, jnp.bfloat16
    nt = (((1,), (1,)), ((), ()))
    tn = (((0,), (0,)), ((), ()))

    @pl.when(pl.program_id(2) == 0)
    def _():
        s_ref[...] = jnp.zeros_like(s_ref)

    row = lax.broadcasted_iota(jnp.int32, (HG_CHUNK, HEAD_DIM), 0)
    refs = ((qf_ref, if_ref, ff_ref, of_ref), (qb_ref, ib_ref, fb_ref, ob_ref))
    inst = [(d, h) for d in range(2) for h in range(HG_HB)]
    level_masks = [[lvl == i for i in range(len(HG_LEVELS) + 1)]
                   for lvl in (_hgrn_level_id(False), _hgrn_level_id(True))]

    pre = []
    for d, h in inst:
        q_ref, i_ref, f_ref, _ = refs[d]
        sl = slice(h * HEAD_DIM, (h + 1) * HEAD_DIM)
        q, k, gc = _hgrn_gates(q_ref[:, sl], f_ref[:, sl], la_ref[d:d + 1, sl], lc_ref[d:d + 1, sl],
                               om_ref[d:d + 1, sl], row, d == 1)
        g_end = gc[0:1, :] if d == 1 else gc[HG_CHUNK - 1:HG_CHUNK, :]
        pre.append(dict(ops=_hgrn_level_operands(q, k, gc, row, d == 1), qk=(q.astype(bf), k.astype(bf)),
                        qe=(q * jnp.exp(gc)).astype(bf), kd=(k * jnp.exp(g_end - gc)).astype(bf),
                        decay=jnp.exp(g_end), v=i_ref[:, sl].astype(bf)))

    attn = []
    for (d, h), pr in zip(inst, pre):
        prods = [lax.dot_general(x, x, nt, preferred_element_type=f32) for x in pr["ops"]]
        prods.append(lax.dot_general(pr["qk"][0], pr["qk"][1], nt, preferred_element_type=f32))
        a = jnp.zeros((HG_CHUNK, HG_CHUNK), f32)
        for i, pmat in enumerate(prods):
            a = jnp.where(level_masks[d][i], pmat, a)
        attn.append(a.astype(bf))

    for (d, h), pr, a in zip(inst, pre, attn):
        o_ref = refs[d][3]
        sl = slice(h * HEAD_DIM, (h + 1) * HEAD_DIM)
        state_t = s_ref[d, h]
        o = jnp.dot(a, pr["v"], preferred_element_type=f32)
        o_ref[:, sl] = o + lax.dot_general(pr["qe"], state_t.astype(bf), nt, preferred_element_type=f32)
        s_ref[d, h] = pr["decay"] * state_t + lax.dot_general(pr["v"], pr["kd"], tn, preferred_element_type=f32)


def _hgrn2(p, col0, lb):
    bsz, lp, _ = p.shape
    nchunk = lp // HG_CHUNK
    cw = HG_HB * HEAD_DIM
    ng = HG_WIDTH // cw
    off = col0 // cw
    la = jnp.log(lb)
    lc = jnp.log1p(-lb)
    om = 1.0 - lb
    fwd = lambda seg: pl.BlockSpec((None, HG_CHUNK, cw), lambda b, g, c: (b, c, off + seg * ng + g))
    bwd = lambda seg: pl.BlockSpec((None, HG_CHUNK, cw), lambda b, g, c: (b, nchunk - 1 - c, off + seg * ng + g))
    par = pl.BlockSpec((2, cw), lambda b, g, c: (0, g))
    return pl.pallas_call(
        _hgrn_kernel,
        grid=(bsz, ng, nchunk),
        in_specs=[fwd(0), fwd(1), fwd(2), bwd(0), bwd(1), bwd(3), par, par, par],
        out_specs=[pl.BlockSpec((None, HG_CHUNK, cw), lambda b, g, c: (b, c, g)),
                   pl.BlockSpec((None, HG_CHUNK, cw), lambda b, g, c: (b, nchunk - 1 - c, g))],
        out_shape=[jax.ShapeDtypeStruct((bsz, lp, HG_WIDTH), jnp.float32)] * 2,
        scratch_shapes=[pltpu.VMEM((2, HG_HB, HEAD_DIM, HEAD_DIM), jnp.float32)],
        compiler_params=pltpu.CompilerParams(
            dimension_semantics=("parallel", "parallel", "arbitrary"), vmem_limit_bytes=V7X_VMEM_LIMIT),
        name="hgrn2_bidirectional",
    )(p, p, p, p, p, p, la, lc, om)


EG_ROWS = 320


def _even_gate_kernel(oa_ref, of_ref, ob_ref, ga_ref, gb_ref, gain_ref, y_ref):
    ga = ga_ref[...]
    y_ref[:, :NA_WIDTH] = (oa_ref[...] * (ga * jax.nn.sigmoid(ga))).astype(y_ref.dtype)
    for h in range(HG_HEADS):
        sl = slice(h * HEAD_DIM, (h + 1) * HEAD_DIM)
        oh = of_ref[:, sl] + ob_ref[:, sl]
        oh = oh * lax.rsqrt(jnp.mean(oh * oh, axis=-1, keepdims=True) + EPS) * gain_ref[:, sl]
        gb = gb_ref[:, sl]
        y_ref[:, NA_WIDTH + h * HEAD_DIM:NA_WIDTH + (h + 1) * HEAD_DIM] = (
            oh * (gb * jax.nn.sigmoid(gb))).astype(y_ref.dtype)


def _even_gate(oa, o_fwd, o_bwd, p, gate_col0, norm_gain):
    bsz, lp, _ = oa.shape
    assert gate_col0 % NA_WIDTH == 0
    goff = gate_col0 // NA_WIDTH
    half = lambda off: pl.BlockSpec((None, EG_ROWS, NA_WIDTH), lambda b, r: (b, r, off))
    return pl.pallas_call(
        _even_gate_kernel,
        grid=(bsz, lp // EG_ROWS),
        in_specs=[half(0), half(0), half(0), half(goff), half(goff + 1),
                  pl.BlockSpec((1, HG_WIDTH), lambda b, r: (0, 0))],
        out_specs=pl.BlockSpec((None, EG_ROWS, D_MODEL), lambda b, r: (b, r, 0)),
        out_shape=jax.ShapeDtypeStruct((bsz, lp, D_MODEL), jnp.bfloat16),
        compiler_params=pltpu.CompilerParams(
            dimension_semantics=("parallel", "parallel"), vmem_limit_bytes=V7X_VMEM_LIMIT),
        name="even_gate",
    )(oa, o_fwd, o_bwd, p, p, norm_gain.reshape(1, HG_WIDTH))


NA_MASKED = -1e30
NA_GROUP = 8


def _na_bias_table(rpb):
    cols = np.arange(GRID_W)
    c0 = np.clip(cols - NA_KW // 2, 0, GRID_W - NA_KW)
    col_ok = (cols[None, :] >= c0[:, None]) & (cols[None, :] < c0[:, None] + NA_KW)
    col_idx = np.clip(cols[None, :] - cols[:, None] + NA_KW - 1, 0, 2 * NA_KW - 2)
    table = jnp.where(col_ok[None, None], rpb.astype(jnp.float32)[:, :, col_idx], NA_MASKED)
    return jnp.concatenate([table[:, :-1], table[:, 1:]], axis=-1)


def _na_kernel(q_ref, k_ref, v_ref, t_ref, mb_ref, o_ref):
    f32, bf = jnp.float32, jnp.bfloat16
    rows = (q_ref.shape[0] - HG_CHUNK) // GRID_W
    scale = HEAD_DIM ** -0.5
    nt = (((1,), (1,)), ((), ()))
    km = k_ref[PAD:PAD + N_META, :].astype(bf)
    vm = v_ref[PAD:PAD + N_META, :].astype(bf)
    mb = mb_ref[...]
    o_ref[0:PAD, :] = jnp.zeros((PAD, HEAD_DIM), f32)

    qm = q_ref[PAD:PAD + N_META, :].astype(bf)
    s = lax.dot_general(qm, km, nt, preferred_element_type=f32) * scale + mb
    e = jnp.exp(s - jnp.max(s, axis=-1, keepdims=True))
    o_ref[PAD:PAD + N_META, :] = (jnp.dot(e.astype(bf), vm, preferred_element_type=f32)
                                  / jnp.sum(e, axis=-1, keepdims=True))

    def row_group(g, carry):
        rs = [g * NA_GROUP + i for i in range(NA_GROUP)]
        scores = []
        for r in rs:
            r0 = jnp.clip(r - NA_KH // 2, 0, rows - NA_KH)
            q = q_ref[pl.ds(pl.multiple_of(HG_CHUNK + GRID_W * r, GRID_W), GRID_W), :].astype(bf)
            k0 = pl.multiple_of(HG_CHUNK + GRID_W * r0, GRID_W)
            kw = k_ref[pl.ds(k0, NA_KH * GRID_W), :].astype(bf)
            s_win = lax.dot_general(q, kw, nt, preferred_element_type=f32)
            s_meta = lax.dot_general(q, km, nt, preferred_element_type=f32)
            scores.append((r, r0, k0, s_win, s_meta))
        probs = []
        for r, r0, k0, s_win, s_meta in scores:
            bias = jnp.concatenate([t_ref[r0 + i - r + NA_KH - 1] for i in range(0, NA_KH, 2)], axis=1)
            s_win = s_win * scale + bias
            s_meta = s_meta * scale + mb
            m = jnp.maximum(jnp.max(s_win, axis=-1, keepdims=True), jnp.max(s_meta, axis=-1, keepdims=True))
            pw = jnp.exp(s_win - m)
            pm = jnp.exp(s_meta - m)
            den = jnp.sum(pw, axis=-1, keepdims=True) + jnp.sum(pm, axis=-1, keepdims=True)
            probs.append((r, k0, pw.astype(bf), pm.astype(bf), den))
        for r, k0, pw, pm, den in probs:
            vw = v_ref[pl.ds(k0, NA_KH * GRID_W), :].astype(bf)
            o = jnp.dot(pw, vw, preferred_element_type=f32) + jnp.dot(pm, vm, preferred_element_type=f32)
            o_ref[pl.ds(pl.multiple_of(HG_CHUNK + GRID_W * r, GRID_W), GRID_W), :] = o / den
        return carry

    lax.fori_loop(0, rows // NA_GROUP, row_group, 0)


def _natten(p, rpb, meta_bias):
    bsz, lp, _ = p.shape
    assert ((lp - HG_CHUNK) // GRID_W) % NA_GROUP == 0
    heads = NA_WIDTH // HEAD_DIM
    blk = lambda seg: pl.BlockSpec((None, lp, HEAD_DIM), lambda b, h: (b, 0, seg * heads + h))
    table = _na_bias_table(rpb)
    return pl.pallas_call(
        _na_kernel,
        grid=(bsz, heads),
        in_specs=[blk(0), blk(1), blk(2),
                  pl.BlockSpec((None, 2 * NA_KH - 2, GRID_W, 2 * GRID_W), lambda b, h: (h, 0, 0, 0)),
                  pl.BlockSpec((None, 1, N_META), lambda b, h: (h, 0, 0))],
        out_specs=pl.BlockSpec((None, lp, HEAD_DIM), lambda b, h: (b, 0, h)),
        out_shape=jax.ShapeDtypeStruct((bsz, lp, NA_WIDTH), jnp.float32),
        compiler_params=pltpu.CompilerParams(
            dimension_semantics=("parallel", "parallel"), vmem_limit_bytes=V7X_VMEM_LIMIT),
        name="neighbourhood_attention",
    )(p, p, p, table, meta_bias.astype(jnp.float32).reshape(heads, 1, N_META))


def _even_mix(p, rpb, meta_bias, lb, norm_gain):
    oa = _natten(p, rpb, meta_bias)
    o_fwd, o_bwd = _hgrn2(p, 3 * NA_WIDTH, lb)
    return _even_gate(oa, o_fwd, o_bwd, p, 3 * NA_WIDTH + 4 * HG_WIDTH, norm_gain)


def kernel(x, meta_tokens, norm_pre, norm_post, ev_w_in, ev_w_out, na_rpb, na_meta_bias, hg_lower, hg_norm,
           od_w_in, od_w_out, hy_short, hy_ffn_w1, hy_ffn_b1, hy_ffn_freq1, hy_ffn_w2, hy_ffn_b2, hy_ffn_freq2,
           hy_ffn_w3, hy_bias):
    b = x.shape[0]
    depth = norm_pre.shape[0]
    f32 = jnp.float32
    lb_all = jnp.cumsum(jax.nn.softmax(hg_lower.astype(f32), axis=0), axis=0)
    lb_all = lb_all - lb_all[:1]
    meta = jnp.broadcast_to(meta_tokens.astype(f32)[None], (b, N_META, D_MODEL))
    h = jnp.concatenate([jnp.zeros((b, PAD, D_MODEL), f32), meta, x.astype(f32)], axis=1)
    h = h.reshape(b * LP, D_MODEL)
    consts = _dft_constants()
    hn = _rmsnorm(h, norm_pre[0], jnp.bfloat16)
    for layer in range(depth):
        j = layer // 2
        if layer % 2 == 0:
            p = _matmul(hn, ev_w_in, j).reshape(b, LP, -1)
            y = _even_mix(p, na_rpb[j], na_meta_bias[j], lb_all[j], hg_norm[j]).reshape(b * LP, D_MODEL)
            w_out = ev_w_out
        else:
            p3 = _matmul(hn, od_w_in, j).reshape(b, LP, -1)
            y = _odd_mix(p3, hy_short[j], hy_ffn_w1[j], hy_ffn_b1[j], hy_ffn_freq1[j], hy_ffn_w2[j],
                         hy_ffn_b2[j], hy_ffn_freq2[j], hy_ffn_w3[j], hy_bias[j], consts)
            y = y.reshape(b * LP, D_MODEL)
            w_out = od_w_out
        out = _matmul(y, w_out, j)
        if layer + 1 < depth:
            h, hn = _residual_rmsnorm(h, out, norm_post[layer], norm_pre[layer + 1])
        else:
            h = _residual_rmsnorm(h, out, norm_post[layer])
    return h.reshape(b, LP, D_MODEL)[:, PAD + N_META:].astype(x.dtype)
```

```python
import functools
import math

import jax
import jax.numpy as jnp
import numpy as np
from jax import lax
from jax.experimental import pallas as pl
from jax.experimental.pallas import tpu as pltpu

D_MODEL = 4096
SEQ = 4096
N_META = 16
GRID_W = 64
HEAD_DIM = 128
NA_WIDTH = D_MODEL // 2
NA_KH = 8
NA_KW = 16
HG_WIDTH = D_MODEL // 2
HG_HEADS = HG_WIDTH // HEAD_DIM
HG_CHUNK = 64
HY_WIDTH = D_MODEL
HY_ORDER = 2
HY_EMB = 33
HY_BANDS = (HY_EMB - 1) // 2
HY_FAST_DECAY = 0.3
HY_SLOW_DECAY = 1.5
HY_TARGET = 1e-2
EPS = 1e-6

PAD = HG_CHUNK - N_META
LP = PAD + N_META + SEQ
L_TOK = N_META + SEQ

V7X_VMEM_LIMIT = 56 * 1024 * 1024


def _rmsnorm_kernel(x_ref, g_ref, o_ref):
    x = x_ref[...]
    r = lax.rsqrt(jnp.mean(x * x, axis=-1, keepdims=True) + EPS)
    o_ref[...] = (x * r * g_ref[...]).astype(o_ref.dtype)


def _rmsnorm(x2d, g, out_dtype, tr=320):
    m, d = x2d.shape
    return pl.pallas_call(
        _rmsnorm_kernel,
        grid=(m // tr,),
        in_specs=[pl.BlockSpec((tr, d), lambda i: (i, 0)), pl.BlockSpec((1, d), lambda i: (0, 0))],
        out_specs=pl.BlockSpec((tr, d), lambda i: (i, 0)),
        out_shape=jax.ShapeDtypeStruct((m, d), out_dtype),
        compiler_params=pltpu.CompilerParams(vmem_limit_bytes=V7X_VMEM_LIMIT),
        name="rmsnorm_pre",
    )(x2d, g.reshape(1, d))


def _post_kernel(h_ref, y_ref, g_ref, o_ref):
    y = y_ref[...]
    r = lax.rsqrt(jnp.mean(y * y, axis=-1, keepdims=True) + EPS)
    o_ref[...] = h_ref[...] + y * r * g_ref[...]


def _post_pre_kernel(h_ref, y_ref, g_ref, gn_ref, o_ref, n_ref):
    y = y_ref[...]
    r = lax.rsqrt(jnp.mean(y * y, axis=-1, keepdims=True) + EPS)
    h = h_ref[...] + y * r * g_ref[...]
    o_ref[...] = h
    rn = lax.rsqrt(jnp.mean(h * h, axis=-1, keepdims=True) + EPS)
    n_ref[...] = (h * rn * gn_ref[...]).astype(n_ref.dtype)


def _residual_rmsnorm(h2d, y2d, g, g_next=None, tr=320):
    m, d = h2d.shape
    row = pl.BlockSpec((tr, d), lambda i: (i, 0))
    vec = pl.BlockSpec((1, d), lambda i: (0, 0))
    params = pltpu.CompilerParams(dimension_semantics=("parallel",), vmem_limit_bytes=V7X_VMEM_LIMIT)
    if g_next is None:
        return pl.pallas_call(
            _post_kernel, grid=(m // tr,), in_specs=[row, row, vec], out_specs=row,
            out_shape=jax.ShapeDtypeStruct((m, d), jnp.float32), compiler_params=params,
            name="residual_rmsnorm_post",
        )(h2d, y2d, g.reshape(1, d))
    return pl.pallas_call(
        _post_pre_kernel, grid=(m // tr,), in_specs=[row, row, vec, vec], out_specs=[row, row],
        out_shape=[jax.ShapeDtypeStruct((m, d), jnp.float32), jax.ShapeDtypeStruct((m, d), jnp.bfloat16)],
        compiler_params=params, name="residual_rmsnorm_post_pre",
    )(h2d, y2d, g.reshape(1, d), g_next.reshape(1, d))


def _matmul_kernel(x_ref, w_ref, o_ref):
    o_ref[...] = jnp.dot(x_ref[...], w_ref[...].astype(jnp.bfloat16),
                         preferred_element_type=jnp.float32).astype(o_ref.dtype)


def _matmul(x, w_stack, layer, tm=1040, tn=512):
    m, kd = x.shape
    _, _, n = w_stack.shape
    assert m % tm == 0 and n % tn == 0
    return pl.pallas_call(
        _matmul_kernel,
        grid=(m // tm, n // tn),
        in_specs=[pl.BlockSpec((tm, kd), lambda i, j: (i, 0)),
                  pl.BlockSpec((None, kd, tn), lambda i, j: (layer, 0, j))],
        out_specs=pl.BlockSpec((tm, tn), lambda i, j: (i, j)),
        out_shape=jax.ShapeDtypeStruct((m, n), jnp.float32),
        compiler_params=pltpu.CompilerParams(
            dimension_semantics=("parallel", "parallel"), vmem_limit_bytes=V7X_VMEM_LIMIT),
        name="projection_matmul",
    )(x, w_stack)


FFT_N1 = 128
FFT_N2 = 65
FFT_N = FFT_N1 * FFT_N2
N2P = 80
SLAB = 264
TC = 128
MID_GROUP = 16
NC_HALF = FFT_N1 // 2 + 1
HALF_P = 72
SLAB_S = 152
C_GROUP = 13
FILL_CHUNKS = 13
B_GROUP = 13


def _dft_constants():
    a = np.arange(FFT_N1)
    c = np.arange(FFT_N1)
    w1 = np.exp(-2j * np.pi * np.outer(c, a) / FFT_N1)
    w1r, w1i = w1.real, w1.imag
    half = FFT_N1 // 2
    f1 = np.block([[w1r[:, :half], -w1i[:, :half]], [w1i[:, :half], w1r[:, :half]]])
    f1_real = np.zeros((2 * HALF_P, FFT_N1), np.float64)
    f1_real[:NC_HALF] = w1r[:NC_HALF]
    f1_real[HALF_P:HALF_P + NC_HALF] = w1i[:NC_HALF]
    g = np.conj(w1).T
    gr, gi = g.real[:half], g.imag[:half]
    i2 = np.block([[gr, -gi], [gi, gr]])
    b = np.arange(FFT_N2)
    d = np.arange(FFT_N2)
    w2 = np.exp(-2j * np.pi * np.outer(d, b) / FFT_N2)
    tw = np.exp(-2j * np.pi * np.outer(c, b) / FFT_N)
    m = w2[None, :, :] * tw[:, None, :]
    ab = np.zeros((FFT_N1, 2 * N2P, 128), np.float64)
    m[NC_HALF:] = m[NC_HALF:, ::-1, :]
    ab[:, :FFT_N2, :FFT_N2] = m.real
    ab[:, N2P:N2P + FFT_N2, :FFT_N2] = m.imag
    bf = jnp.bfloat16
    abt = np.zeros((FFT_N1, N2P, 256), np.float64)
    abt[:, :, :2 * N2P] = np.swapaxes(ab[:, :, :N2P], 1, 2)
    return (jnp.asarray(f1, bf), jnp.asarray(f1_real, bf), jnp.asarray(i2, bf), jnp.asarray(ab, bf),
            jnp.asarray(abt, bf))


def _middle_stage_inputs(y_ref, c, imag_row=FFT_N1, pitch=SLAB):
    yr = y_ref[pl.ds(c, N2P, stride=pitch), :]
    yi = y_ref[pl.ds(imag_row + c, N2P, stride=pitch), :]
    return jnp.concatenate([yr, yi], axis=1).astype(jnp.bfloat16)


def _forward_middle(ab, y2):
    y2 = jnp.concatenate([y2, jnp.zeros((128 - N2P, 2 * TC), jnp.bfloat16)], axis=0)
    q = jnp.dot(ab, y2, preferred_element_type=jnp.float32)
    zr = q[:N2P, :TC] - q[N2P:, TC:]
    zi = q[N2P:, :TC] + q[:N2P, TC:]
    return zr, zi


def _spectrum_kernel(h_ref, t_ref, w3f0_ref, w3f1_ref, w3b0_ref, w3b1_ref, delta_ref, f1_ref, ab_ref, k_ref,
                     kt_ref, y_ref):
    w3 = jnp.concatenate([jnp.concatenate([w3f0_ref[...], w3f1_ref[...]], axis=1),
                          jnp.concatenate([w3b0_ref[...], w3b1_ref[...]], axis=1)], axis=0).astype(jnp.bfloat16)
    delta = delta_ref[...]
    rows = FFT_N // FILL_CHUNKS

    def fill(i, carry):
        r0 = pl.multiple_of(i * rows, 8)
        kt = jnp.dot(h_ref[pl.ds(r0, rows), :], w3, preferred_element_type=jnp.float32)
        window = jnp.exp(t_ref[pl.ds(r0, rows), :] * delta)
        for n in range(HY_ORDER):
            kt_ref[n, pl.ds(r0, rows), :] = kt[:, n * TC:(n + 1) * TC] * window
        return carry

    lax.fori_loop(0, FILL_CHUNKS, fill, 0)
    y_ref[pl.ds(FFT_N2 * SLAB_S, (N2P - FFT_N2) * SLAB_S), :] = jnp.zeros(((N2P - FFT_N2) * SLAB_S, TC), jnp.float32)

    for n in range(HY_ORDER):
        def stage1(g, carry):
            bs = [g * B_GROUP + i for i in range(B_GROUP)]
            x = jnp.concatenate([kt_ref.at[n][pl.ds(b, FFT_N1, stride=FFT_N2), :] for b in bs], axis=1)
            y = jnp.dot(f1_ref[...], x.astype(jnp.bfloat16), preferred_element_type=jnp.float32)
            for i, b in enumerate(bs):
                y_ref[pl.ds(pl.multiple_of(b * SLAB_S, 8), 2 * HALF_P), :] = y[:, i * TC:(i + 1) * TC]
            return carry

        lax.fori_loop(0, FFT_N2 // B_GROUP, stage1, 0)

        def stage2(g, carry):
            cs = [g * C_GROUP + i for i in range(C_GROUP)]
            zs = [_forward_middle(ab_ref[c], _middle_stage_inputs(y_ref, c, HALF_P, SLAB_S)) for c in cs]
            for c, (zr, zi) in zip(cs, zs):
                k_ref[n, c] = (jnp.concatenate([zr, zi], axis=1) * (1.0 / FFT_N)).astype(jnp.bfloat16)
            return carry

        lax.fori_loop(0, NC_HALF // C_GROUP, stage2, 0)


def _filter_spectrum(h2, w3, deltas, consts):
    _, f1_real, _, ab, _ = consts
    nt = HY_WIDTH // TC
    j = np.arange(FFT_N)
    lag = np.where(j < L_TOK, j, FFT_N - j).astype(np.float32)
    neg_t = jnp.asarray(np.broadcast_to((-lag / np.float32(L_TOK - 1))[:, None], (FFT_N, TC)))
    w3_cols = lambda direction, order: pl.BlockSpec((64, TC), lambda i: (0, (direction * HY_ORDER + order) * nt + i))
    return pl.pallas_call(
        _spectrum_kernel,
        grid=(nt,),
        in_specs=[
            pl.BlockSpec((FFT_N, 128), lambda i: (0, 0)),
            pl.BlockSpec((FFT_N, TC), lambda i: (0, 0)),
            w3_cols(0, 0), w3_cols(0, 1), w3_cols(1, 0), w3_cols(1, 1),
            pl.BlockSpec((1, TC), lambda i: (0, i)),
            pl.BlockSpec((2 * HALF_P, FFT_N1), lambda i: (0, 0)),
            pl.BlockSpec((FFT_N1, 2 * N2P, 128), lambda i: (0, 0, 0)),
        ],
        out_specs=pl.BlockSpec((HY_ORDER, None, NC_HALF, N2P, 2 * TC), lambda i: (0, i, 0, 0, 0)),
        out_shape=jax.ShapeDtypeStruct((HY_ORDER, nt, NC_HALF, N2P, 2 * TC), jnp.bfloat16),
        scratch_shapes=[pltpu.VMEM((HY_ORDER, FFT_N, TC), jnp.float32), pltpu.VMEM((N2P * SLAB_S, TC), jnp.float32)],
        compiler_params=pltpu.CompilerParams(dimension_semantics=("parallel",), vmem_limit_bytes=V7X_VMEM_LIMIT),
        name="hyena_filter_spectrum",
    )(h2, neg_t, w3, w3, w3, w3, deltas, f1_real, ab)


def _conv_kernel(u_ref, k_ref, f1_ref, i2_ref, ab_ref, abt_ref, o_ref, y_ref):
    y_ref[pl.ds(FFT_N2 * SLAB, (N2P - FFT_N2) * SLAB), :] = jnp.zeros(((N2P - FFT_N2) * SLAB, TC), jnp.float32)
    half = FFT_N1 // 2

    def stage1(g, carry):
        bs = [g * B_GROUP + i for i in range(B_GROUP)]
        x = jnp.concatenate(
            [jnp.concatenate([u_ref.at[ri][pl.ds(b, half, stride=FFT_N2), :] for b in bs], axis=1) for ri in range(2)],
            axis=0)
        y = jnp.dot(f1_ref[...], x.astype(jnp.bfloat16), preferred_element_type=jnp.float32)
        for i, b in enumerate(bs):
            y_ref[pl.ds(pl.multiple_of(b * SLAB, 8), 2 * FFT_N1), :] = y[:, i * TC:(i + 1) * TC]
        return carry

    lax.fori_loop(0, FFT_N2 // B_GROUP, stage1, 0)

    def inverse_middle(c, z):
        zr, zi = z
        mirrored = c >= NC_HALF
        kk = k_ref[jnp.where(mirrored, FFT_N1 - c, c)].astype(jnp.float32)
        kr, ki = kk[:, :TC], kk[:, TC:] * jnp.where(mirrored, -1.0, 1.0)
        pr = zr * kr - zi * ki
        pi = zr * ki + zi * kr
        rhs = jnp.concatenate([jnp.concatenate([pr, pi], axis=1), jnp.concatenate([pi, -pr], axis=1),
                               jnp.zeros((256 - 2 * N2P, 2 * TC), jnp.float32)], axis=0)
        return jnp.dot(abt_ref[c], rhs.astype(jnp.bfloat16), preferred_element_type=jnp.float32)

    def middle(g, carry):
        cs = [g * MID_GROUP + i for i in range(MID_GROUP)]
        zs = [_forward_middle(ab_ref[c], _middle_stage_inputs(y_ref, c)) for c in cs]
        outs = [inverse_middle(c, z) for c, z in zip(cs, zs)]
        for c, v in zip(cs, outs):
            y_ref[pl.ds(c, N2P, stride=SLAB), :] = v[:N2P, :TC]
            y_ref[pl.ds(FFT_N1 + c, N2P, stride=SLAB), :] = v[:N2P, TC:]
        return carry

    lax.fori_loop(0, FFT_N1 // MID_GROUP, middle, 0)

    def stage4(g, carry):
        bs = [g * B_GROUP + i for i in range(B_GROUP)]
        v = jnp.concatenate([y_ref[pl.ds(pl.multiple_of(b * SLAB, 8), 2 * FFT_N1), :] for b in bs], axis=1)
        x = jnp.dot(i2_ref[...], v.astype(jnp.bfloat16), preferred_element_type=jnp.float32)
        for i, b in enumerate(bs):
            o_ref.at[0][pl.ds(b, half, stride=FFT_N2), :] = x[:half, i * TC:(i + 1) * TC]
            o_ref.at[1][pl.ds(b, half, stride=FFT_N2), :] = x[half:, i * TC:(i + 1) * TC]
        return carry

    lax.fori_loop(0, FFT_N2 // B_GROUP, stage4, 0)


def _long_conv(u, kspec, order, consts):
    f1, _, i2, ab, abt = consts
    bsz, lp, _ = u.shape
    assert bsz == 2 and lp == LP
    nt = HY_WIDTH // TC
    return pl.pallas_call(
        _conv_kernel,
        grid=(nt,),
        in_specs=[
            pl.BlockSpec((bsz, lp, TC), lambda i: (0, 0, i)),
            pl.BlockSpec((None, None, NC_HALF, N2P, 2 * TC), lambda i: (order, i, 0, 0, 0)),
            pl.BlockSpec((2 * FFT_N1, FFT_N1), lambda i: (0, 0)),
            pl.BlockSpec((FFT_N1, 2 * FFT_N1), lambda i: (0, 0)),
            pl.BlockSpec((FFT_N1, 2 * N2P, 128), lambda i: (0, 0, 0), pipeline_mode=pl.Buffered(1)),
            pl.BlockSpec((FFT_N1, N2P, 256), lambda i: (0, 0, 0), pipeline_mode=pl.Buffered(1)),
        ],
        out_specs=pl.BlockSpec((bsz, lp, TC), lambda i: (0, 0, i)),
        out_shape=jax.ShapeDtypeStruct((bsz, lp, HY_WIDTH), jnp.float32),
        scratch_shapes=[pltpu.VMEM((N2P * SLAB, TC), jnp.float32)],
        compiler_params=pltpu.CompilerParams(dimension_semantics=("parallel",), vmem_limit_bytes=V7X_VMEM_LIMIT),
        name="hyena_long_conv",
    )(u, kspec, f1, i2, ab, abt)


def _short_conv_rows(u, w):
    y = pltpu.roll(u, 1, 0) * w[0:1] + u * w[1:2] + pltpu.roll(u, LP - 1, 0) * w[2:3]
    row = lax.broadcasted_iota(jnp.int32, y.shape, 0)
    return jnp.where(row >= PAD, y, 0.0)


def _short_conv_kernel(u_ref, w_ref, o_ref):
    o_ref[...] = _short_conv_rows(u_ref[...], w_ref[...])


def _gate1_kernel(v_ref, u_ref, w_ref, y_ref, b_ref, o_ref):
    o_ref[...] = _short_conv_rows(u_ref[...], w_ref[...]) * (y_ref[...] + v_ref[...] * b_ref[...])


def _gate2_kernel(z_ref, u_ref, w_ref, y_ref, b_ref, g_ref, o_ref):
    g = g_ref[...]
    z = _short_conv_rows(u_ref[...], w_ref[...]) * (y_ref[...] + z_ref[...] * b_ref[...])
    o_ref[...] = (z * (g * jax.nn.sigmoid(g))).astype(o_ref.dtype)


def _seq_tile(col_block):
    return pl.BlockSpec((None, LP, TC), lambda b, i: (b, 0, col_block + i))


def _vec_tile(rows, col_block):
    return pl.BlockSpec((rows, TC), lambda b, i: (0, col_block + i))


def _short_conv(p3, w_short_t):
    bsz = p3.shape[0]
    return pl.pallas_call(
        _short_conv_kernel,
        grid=(bsz, HY_WIDTH // TC),
        in_specs=[_seq_tile(0), _vec_tile(3, 0)],
        out_specs=_seq_tile(0),
        out_shape=jax.ShapeDtypeStruct((bsz, LP, HY_WIDTH), jnp.float32),
        compiler_params=pltpu.CompilerParams(
            dimension_semantics=("parallel", "parallel"), vmem_limit_bytes=V7X_VMEM_LIMIT),
        name="hyena_short_conv",
    )(p3, w_short_t)


def _hyena_gate1(v, p3, w_short_t, y, bias):
    bsz = p3.shape[0]
    nt = HY_WIDTH // TC
    return pl.pallas_call(
        _gate1_kernel,
        grid=(bsz, nt),
        in_specs=[_seq_tile(0), _seq_tile(nt), _vec_tile(3, nt), _seq_tile(0), _vec_tile(1, 0)],
        out_specs=_seq_tile(0),
        out_shape=jax.ShapeDtypeStruct((bsz, LP, HY_WIDTH), jnp.float32),
        compiler_params=pltpu.CompilerParams(
            dimension_semantics=("parallel", "parallel"), vmem_limit_bytes=V7X_VMEM_LIMIT),
        name="hyena_gate1",
    )(v, p3, w_short_t, y, bias.reshape(1, HY_WIDTH))


def _hyena_gate2(z1, p3, w_short_t, y, bias):
    bsz = p3.shape[0]
    nt = HY_WIDTH // TC
    return pl.pallas_call(
        _gate2_kernel,
        grid=(bsz, nt),
        in_specs=[_seq_tile(0), _seq_tile(2 * nt), _vec_tile(3, 2 * nt), _seq_tile(0), _vec_tile(1, 0),
                  _seq_tile(3 * nt)],
        out_specs=_seq_tile(0),
        out_shape=jax.ShapeDtypeStruct((bsz, LP, HY_WIDTH), jnp.bfloat16),
        compiler_params=pltpu.CompilerParams(
            dimension_semantics=("parallel", "parallel"), vmem_limit_bytes=V7X_VMEM_LIMIT),
        name="hyena_gate2",
    )(z1, p3, w_short_t, y, bias.reshape(1, HY_WIDTH), p3)


def _filter_positions():
    f32 = np.float32
    l = L_TOK
    t = np.linspace(0.0, 1.0, l, dtype=f32)[:, None]
    w = f32(2.0 * math.pi) * np.arange(l, dtype=f32)[:, None] / f32(l)
    bands = np.linspace(1e-4, HY_BANDS - 1, HY_BANDS, dtype=f32)[None, :]
    z = np.concatenate([t, np.cos(bands * w), -np.sin(bands * w)], axis=-1).astype(f32)
    j = np.arange(FFT_N)
    lag = np.minimum(np.where(j < l, j, FFT_N - j), l - 1)
    zrows = np.zeros((FFT_N, 128), f32)
    zrows[:, :HY_EMB] = z[lag]
    mask = np.zeros((FFT_N, 128), f32)
    mask[:, :64] = (j < l)[:, None]
    mask[:, 64:] = ((j > FFT_N - l) | (j == 0))[:, None]
    return jnp.asarray(zrows), jnp.asarray(mask)


def _hidden_kernel(z_ref, m_ref, w1_ref, b1_ref, fr1_ref, w2_ref, b2_ref, fr2_ref, o_ref):
    hi = lax.Precision.HIGHEST
    a = jnp.dot(z_ref[...], w1_ref[...], precision=hi, preferred_element_type=jnp.float32) + b1_ref[...]
    hid = jnp.sin(fr1_ref[...] * a)
    a = jnp.dot(hid, w2_ref[...], precision=hi, preferred_element_type=jnp.float32) + b2_ref[...]
    hid = jnp.sin(fr2_ref[...] * a)
    o_ref[...] = (jnp.concatenate([hid, hid], axis=1) * m_ref[...]).astype(o_ref.dtype)


def _hyena_hidden(w1, b1, fr1, w2, b2, fr2):
    zrows, mask = _filter_positions()
    rows = FFT_N // FILL_CHUNKS
    hdim = w2.shape[0]
    w1p = jnp.zeros((128, hdim), jnp.float32).at[:HY_EMB].set(w1.astype(jnp.float32))
    tile = pl.BlockSpec((rows, 128), lambda i: (i, 0))
    full = lambda a: pl.BlockSpec(a.shape, lambda i: (0,) * a.ndim)
    vecs = [v.astype(jnp.float32).reshape(1, hdim) for v in (b1, fr1, b2, fr2)]
    w2f = w2.astype(jnp.float32)
    return pl.pallas_call(
        _hidden_kernel,
        grid=(FILL_CHUNKS,),
        in_specs=[tile, tile, full(w1p), full(vecs[0]), full(vecs[1]), full(w2f), full(vecs[2]), full(vecs[3])],
        out_specs=tile,
        out_shape=jax.ShapeDtypeStruct((FFT_N, 128), jnp.bfloat16),
        compiler_params=pltpu.CompilerParams(dimension_semantics=("parallel",), vmem_limit_bytes=V7X_VMEM_LIMIT),
        name="hyena_filter_hidden",
    )(zrows, mask, w1p, vecs[0], vecs[1], w2f, vecs[2], vecs[3])


def _odd_mix(p3, w_short, w1, b1, fr1, w2, b2, fr2, w3, bias, consts):
    min_decay = math.log(HY_TARGET) / HY_SLOW_DECAY
    max_decay = math.log(HY_TARGET) / HY_FAST_DECAY
    deltas = jnp.abs(jnp.linspace(min_decay, max_decay, HY_WIDTH, dtype=jnp.float32)).reshape(1, HY_WIDTH)
    kspec = _filter_spectrum(_hyena_hidden(w1, b1, fr1, w2, b2, fr2), w3, deltas, consts)
    w_short_t = w_short.T
    v = _short_conv(p3, w_short_t)
    y1 = _long_conv(v, kspec, 0, consts)
    z1 = _hyena_gate1(v, p3, w_short_t, y1, bias[0])
    y2 = _long_conv(z1, kspec, 1, consts)
    return _hyena_gate2(z1, p3, w_short_t, y2, bias[1])


HG_HB = 4
HG_LEVELS = (32, 16, 8, 4, 2, 1)
LOG2_E = 1.4426950408889634


def _cumsum_rows(g, rev):
    f32, bf = jnp.float32, jnp.bfloat16
    t = lax.broadcasted_iota(jnp.int32, (HG_CHUNK, 3 * HG_CHUNK), 0)
    s = lax.broadcasted_iota(jnp.int32, (HG_CHUNK, 3 * HG_CHUNK), 1) & (HG_CHUNK - 1)
    tri = jnp.where((s >= t) if rev else (s <= t), 1.0, 0.0).astype(bf)
    hi = g.astype(bf)
    r1 = g - hi.astype(f32)
    mid = r1.astype(bf)
    lo = (r1 - mid.astype(f32)).astype(bf)
    return jnp.dot(tri, jnp.concatenate([hi, mid, lo], axis=0), preferred_element_type=f32)


def _level_reference(gc, row, m, rev):
    if m >= 4:
        g3 = gc.reshape(HG_CHUNK // (2 * m), 2 * m, HEAD_DIM)
        ref = g3[:, m:m + 1, :] if rev else g3[:, m - 1:m, :]
        return jnp.broadcast_to(ref, g3.shape).reshape(HG_CHUNK, HEAD_DIM)
    r = row & (2 * m - 1)
    target = m if rev else m - 1
    out = gc
    for src in range(2 * m):
        if src != target:
            out = jnp.where(r == src, pltpu.roll(gc, (src - target) % HG_CHUNK, 0), out)
    return out


def _hgrn_gates(q_raw, f_raw, la, lc, omlb, row, rev):
    q = q_raw * jax.nn.sigmoid(q_raw) * (HEAD_DIM ** -0.5)
    e = jnp.exp(-jnp.abs(f_raw))
    ope = 1.0 + e
    log_sig = jnp.minimum(f_raw, 0.0) - jnp.log(ope)
    k = omlb * (jnp.where(f_raw >= 0.0, e, 1.0) / ope)
    b = lc + log_sig
    g = jnp.maximum(la, b) + jnp.log(1.0 + jnp.exp(-jnp.abs(la - b)))
    return q, k, _cumsum_rows(g, rev)


def _hgrn_level_operands(q, k, gc, row, rev):
    bf = jnp.bfloat16
    qb, kb = q.astype(bf), k.astype(bf)
    ops = []
    for m in HG_LEVELS:
        gref = _level_reference(gc, row, m, rev)
        upper = (row & m) != 0
        is_query = jnp.logical_not(upper) if rev else upper
        sign_log2e = jnp.where(is_query, LOG2_E, -LOG2_E)
        ops.append(jnp.where(is_query, qb, kb) * jnp.exp2((gc - gref) * sign_log2e).astype(bf))
    return ops


def _hgrn_level_id(rev):
    t = lax.broadcasted_iota(jnp.int32, (HG_CHUNK, HG_CHUNK), 0)
    s = lax.broadcasted_iota(jnp.int32, (HG_CHUNK, HG_CHUNK), 1)
    vis = (s >= t) if rev else (s <= t)
    x = t ^ s
    lvl = jnp.full((HG_CHUNK, HG_CHUNK), len(HG_LEVELS), jnp.int32)
    for i, m in enumerate(HG_LEVELS):
        lvl = jnp.where((x & m) != 0, jnp.minimum(lvl, i), lvl)
    return jnp.where(vis, lvl, -1)


def _hgrn_kernel(qf_ref, if_ref, ff_ref, qb_ref, ib_ref, fb_ref, la_ref, lc_ref, om_ref, of_ref, ob_ref, s_ref):
    f32, bf = jnp.float32, jnp.bfloat16
    nt = (((1,), (1,)), ((), ()))
    tn = (((0,), (0,)), ((), ()))

    @pl.when(pl.program_id(2) == 0)
    def _():
        s_ref[...] = jnp.zeros_like(s_ref)

    row = lax.broadcasted_iota(jnp.int32, (HG_CHUNK, HEAD_DIM), 0)
    refs = ((qf_ref, if_ref, ff_ref, of_ref), (qb_ref, ib_ref, fb_ref, ob_ref))
    inst = [(d, h) for d in range(2) for h in range(HG_HB)]
    level_masks = [[lvl == i for i in range(len(HG_LEVELS) + 1)]
                   for lvl in (_hgrn_level_id(False), _hgrn_level_id(True))]

    pre = []
    for d, h in inst:
        q_ref, i_ref, f_ref, _ = refs[d]
        sl = slice(h * HEAD_DIM, (h + 1) * HEAD_DIM)
        q, k, gc = _hgrn_gates(q_ref[:, sl], f_ref[:, sl], la_ref[d:d + 1, sl], lc_ref[d:d + 1, sl],
                               om_ref[d:d + 1, sl], row, d == 1)
        g_end = gc[0:1, :] if d == 1 else gc[HG_CHUNK - 1:HG_CHUNK, :]
        pre.append(dict(ops=_hgrn_level_operands(q, k, gc, row, d == 1), qk=(q.astype(bf), k.astype(bf)),
                        qe=(q * jnp.exp(gc)).astype(bf), kd=(k * jnp.exp(g_end - gc)).astype(bf),
                        decay=jnp.exp(g_end), v=i_ref[:, sl].astype(bf)))

    attn = []
    for (d, h), pr in zip(inst, pre):
        prods = [lax.dot_general(x, x, nt, preferred_element_type=f32) for x in pr["ops"]]
        prods.append(lax.dot_general(pr["qk"][0], pr["qk"][1], nt, preferred_element_type=f32))
        a = jnp.zeros((HG_CHUNK, HG_CHUNK), f32)
        for i, pmat in enumerate(prods):
            a = jnp.where(level_masks[d][i], pmat, a)
        attn.append(a.astype(bf))

    for (d, h), pr, a in zip(inst, pre, attn):
        o_ref = refs[d][3]
        sl = slice(h * HEAD_DIM, (h + 1) * HEAD_DIM)
        state_t = s_ref[d, h]
        o = jnp.dot(a, pr["v"], preferred_element_type=f32)
        o = o + lax.dot_general(pr["qe"], state_t.astype(bf), nt, preferred_element_type=f32)
        o_ref[:, sl] = o.astype(o_ref.dtype)
        s_ref[d, h] = pr["decay"] * state_t + lax.dot_general(pr["v"], pr["kd"], tn, preferred_element_type=f32)


def _hgrn2(p, col0, lb):
    bsz, lp, _ = p.shape
    nchunk = lp // HG_CHUNK
    cw = HG_HB * HEAD_DIM
    ng = HG_WIDTH // cw
    off = col0 // cw
    la = jnp.log(lb)
    lc = jnp.log1p(-lb)
    om = 1.0 - lb
    fwd = lambda seg: pl.BlockSpec((None, HG_CHUNK, cw), lambda b, g, c: (b, c, off + seg * ng + g))
    bwd = lambda seg: pl.BlockSpec((None, HG_CHUNK, cw), lambda b, g, c: (b, nchunk - 1 - c, off + seg * ng + g))
    par = pl.BlockSpec((2, cw), lambda b, g, c: (0, g))
    return pl.pallas_call(
        _hgrn_kernel,
        grid=(bsz, ng, nchunk),
        in_specs=[fwd(0), fwd(1), fwd(2), bwd(0), bwd(1), bwd(3), par, par, par],
        out_specs=[pl.BlockSpec((None, HG_CHUNK, cw), lambda b, g, c: (b, c, g)),
                   pl.BlockSpec((None, HG_CHUNK, cw), lambda b, g, c: (b, nchunk - 1 - c, g))],
        out_shape=[jax.ShapeDtypeStruct((bsz, lp, HG_WIDTH), jnp.bfloat16)] * 2,
        scratch_shapes=[pltpu.VMEM((2, HG_HB, HEAD_DIM, HEAD_DIM), jnp.float32)],
        compiler_params=pltpu.CompilerParams(
            dimension_semantics=("parallel", "parallel", "arbitrary"), vmem_limit_bytes=V7X_VMEM_LIMIT),
        name="hgrn2_bidirectional",
    )(p, p, p, p, p, p, la, lc, om)


EG_ROWS = 320


def _even_gate_kernel(oa_ref, of_ref, ob_ref, ga_ref, gb_ref, gain_ref, y_ref):
    ga = ga_ref[...]
    y_ref[:, :NA_WIDTH] = (oa_ref[...].astype(jnp.float32) * (ga * jax.nn.sigmoid(ga))).astype(y_ref.dtype)
    for h in range(HG_HEADS):
        sl = slice(h * HEAD_DIM, (h + 1) * HEAD_DIM)
        oh = of_ref[:, sl].astype(jnp.float32) + ob_ref[:, sl].astype(jnp.float32)
        oh = oh * lax.rsqrt(jnp.mean(oh * oh, axis=-1, keepdims=True) + EPS) * gain_ref[:, sl]
        gb = gb_ref[:, sl]
        y_ref[:, NA_WIDTH + h * HEAD_DIM:NA_WIDTH + (h + 1) * HEAD_DIM] = (
            oh * (gb * jax.nn.sigmoid(gb))).astype(y_ref.dtype)


def _even_gate(oa, o_fwd, o_bwd, p, gate_col0, norm_gain):
    bsz, lp, _ = oa.shape
    assert gate_col0 % NA_WIDTH == 0
    goff = gate_col0 // NA_WIDTH
    half = lambda off: pl.BlockSpec((None, EG_ROWS, NA_WIDTH), lambda b, r: (b, r, off))
    return pl.pallas_call(
        _even_gate_kernel,
        grid=(bsz, lp // EG_ROWS),
        in_specs=[half(0), half(0), half(0), half(goff), half(goff + 1),
                  pl.BlockSpec((1, HG_WIDTH), lambda b, r: (0, 0))],
        out_specs=pl.BlockSpec((None, EG_ROWS, D_MODEL), lambda b, r: (b, r, 0)),
        out_shape=jax.ShapeDtypeStruct((bsz, lp, D_MODEL), jnp.bfloat16),
        compiler_params=pltpu.CompilerParams(
            dimension_semantics=("parallel", "parallel"), vmem_limit_bytes=V7X_VMEM_LIMIT),
        name="even_gate",
    )(oa, o_fwd, o_bwd, p, p, norm_gain.reshape(1, HG_WIDTH))


NA_MASKED = -1e30
NA_GROUP = 8


def _na_bias_table(rpb):
    cols = np.arange(GRID_W)
    c0 = np.clip(cols - NA_KW // 2, 0, GRID_W - NA_KW)
    col_ok = (cols[None, :] >= c0[:, None]) & (cols[None, :] < c0[:, None] + NA_KW)
    col_idx = np.clip(cols[None, :] - cols[:, None] + NA_KW - 1, 0, 2 * NA_KW - 2)
    table = jnp.where(col_ok[None, None], rpb.astype(jnp.float32)[:, :, col_idx], NA_MASKED)
    return jnp.concatenate([table[:, :-1], table[:, 1:]], axis=-1)


def _na_kernel(q_ref, k_ref, v_ref, t_ref, mb_ref, o_ref):
    f32, bf = jnp.float32, jnp.bfloat16
    rows = (q_ref.shape[0] - HG_CHUNK) // GRID_W
    scale = HEAD_DIM ** -0.5
    nt = (((1,), (1,)), ((), ()))
    km = k_ref[PAD:PAD + N_META, :].astype(bf)
    vm = v_ref[PAD:PAD + N_META, :].astype(bf)
    mb = mb_ref[...]
    o_ref[0:PAD, :] = jnp.zeros((PAD, HEAD_DIM), o_ref.dtype)

    qm = q_ref[PAD:PAD + N_META, :].astype(bf)
    s = lax.dot_general(qm, km, nt, preferred_element_type=f32) * scale + mb
    e = jnp.exp(s - jnp.max(s, axis=-1, keepdims=True))
    o_ref[PAD:PAD + N_META, :] = (jnp.dot(e.astype(bf), vm, preferred_element_type=f32)
                                  / jnp.sum(e, axis=-1, keepdims=True)).astype(o_ref.dtype)

    def row_group(g, carry):
        rs = [g * NA_GROUP + i for i in range(NA_GROUP)]
        scores = []
        for r in rs:
            r0 = jnp.clip(r - NA_KH // 2, 0, rows - NA_KH)
            q = q_ref[pl.ds(pl.multiple_of(HG_CHUNK + GRID_W * r, GRID_W), GRID_W), :].astype(bf)
            k0 = pl.multiple_of(HG_CHUNK + GRID_W * r0, GRID_W)
            kw = k_ref[pl.ds(k0, NA_KH * GRID_W), :].astype(bf)
            s_win = lax.dot_general(q, kw, nt, preferred_element_type=f32)
            s_meta = lax.dot_general(q, km, nt, preferred_element_type=f32)
            scores.append((r, r0, k0, s_win, s_meta))
        probs = []
        for r, r0, k0, s_win, s_meta in scores:
            bias = jnp.concatenate([t_ref[r0 + i - r + NA_KH - 1] for i in range(0, NA_KH, 2)], axis=1)
            s_win = s_win * scale + bias
            s_meta = s_meta * scale + mb
            m = jnp.maximum(jnp.max(s_win, axis=-1, keepdims=True), jnp.max(s_meta, axis=-1, keepdims=True))
            pw = jnp.exp(s_win - m)
            pm = jnp.exp(s_meta - m)
            den = jnp.sum(pw, axis=-1, keepdims=True) + jnp.sum(pm, axis=-1, keepdims=True)
            probs.append((r, k0, pw.astype(bf), pm.astype(bf), den))
        for r, k0, pw, pm, den in probs:
            vw = v_ref[pl.ds(k0, NA_KH * GRID_W), :].astype(bf)
            o = jnp.dot(pw, vw, preferred_element_type=f32) + jnp.dot(pm, vm, preferred_element_type=f32)
            o_ref[pl.ds(pl.multiple_of(HG_CHUNK + GRID_W * r, GRID_W), GRID_W), :] = (o / den).astype(o_ref.dtype)
        return carry

    lax.fori_loop(0, rows // NA_GROUP, row_group, 0)


def _natten(p, rpb, meta_bias):
    bsz, lp, _ = p.shape
    assert ((lp - HG_CHUNK) // GRID_W) % NA_GROUP == 0
    heads = NA_WIDTH // HEAD_DIM
    blk = lambda seg: pl.BlockSpec((None, lp, HEAD_DIM), lambda b, h: (b, 0, seg * heads + h))
    table = _na_bias_table(rpb)
    return pl.pallas_call(
        _na_kernel,
        grid=(bsz, heads),
        in_specs=[blk(0), blk(1), blk(2),
                  pl.BlockSpec((None, 2 * NA_KH - 2, GRID_W, 2 * GRID_W), lambda b, h: (h, 0, 0, 0)),
                  pl.BlockSpec((None, 1, N_META), lambda b, h: (h, 0, 0))],
        out_specs=pl.BlockSpec((None, lp, HEAD_DIM), lambda b, h: (b, 0, h)),
        out_shape=jax.ShapeDtypeStruct((bsz, lp, NA_WIDTH), jnp.bfloat16),
        compiler_params=pltpu.CompilerParams(
            dimension_semantics=("parallel", "parallel"), vmem_limit_bytes=V7X_VMEM_LIMIT),
        name="neighbourhood_attention",
    )(p, p, p, table, meta_bias.astype(jnp.float32).reshape(heads, 1, N_META))


def _even_mix(p, rpb, meta_bias, lb, norm_gain):
    oa = _natten(p, rpb, meta_bias)
    o_fwd, o_bwd = _hgrn2(p, 3 * NA_WIDTH, lb)
    return _even_gate(oa, o_fwd, o_bwd, p, 3 * NA_WIDTH + 4 * HG_WIDTH, norm_gain)


def kernel(x, meta_tokens, norm_pre, norm_post, ev_w_in, ev_w_out, na_rpb, na_meta_bias, hg_lower, hg_norm,
           od_w_in, od_w_out, hy_short, hy_ffn_w1, hy_ffn_b1, hy_ffn_freq1, hy_ffn_w2, hy_ffn_b2, hy_ffn_freq2,
           hy_ffn_w3, hy_bias):
    b = x.shape[0]
    depth = norm_pre.shape[0]
    f32 = jnp.float32
    lb_all = jnp.cumsum(jax.nn.softmax(hg_lower.astype(f32), axis=0), axis=0)
    lb_all = lb_all - lb_all[:1]
    meta = jnp.broadcast_to(meta_tokens.astype(f32)[None], (b, N_META, D_MODEL))
    h = jnp.concatenate([jnp.zeros((b, PAD, D_MODEL), f32), meta, x.astype(f32)], axis=1)
    h = h.reshape(b * LP, D_MODEL)
    consts = _dft_constants()
    hn = _rmsnorm(h, norm_pre[0], jnp.bfloat16)
    for layer in range(depth):
        j = layer // 2
        if layer % 2 == 0:
            p = _matmul(hn, ev_w_in, j).reshape(b, LP, -1)
            y = _even_mix(p, na_rpb[j], na_meta_bias[j], lb_all[j], hg_norm[j]).reshape(b * LP, D_MODEL)
            w_out = ev_w_out
        else:
            p3 = _matmul(hn, od_w_in, j).reshape(b, LP, -1)
            y = _odd_mix(p3, hy_short[j], hy_ffn_w1[j], hy_ffn_b1[j], hy_ffn_freq1[j], hy_ffn_w2[j],
                         hy_ffn_b2[j], hy_ffn_freq2[j], hy_ffn_w3[j], hy_bias[j], consts)
            y = y.reshape(b * LP, D_MODEL)
            w_out = od_w_out
        out = _matmul(y, w_out, j)
        if layer + 1 < depth:
            h, hn = _residual_rmsnorm(h, out, norm_post[layer], norm_pre[layer + 1])
        else:
            h = _residual_rmsnorm(h, out, norm_post[layer])
    return h.reshape(b, LP, D_MODEL)[:, PAD + N_META:].astype(x.dtype)
```

```python
import functools
import math

import jax
import jax.numpy as jnp
import numpy as np
from jax import lax
from jax.experimental import pallas as pl
from jax.experimental.pallas import tpu as pltpu

D_MODEL = 4096
SEQ = 4096
N_META = 16
GRID_W = 64
HEAD_DIM = 128
NA_WIDTH = D_MODEL // 2
NA_KH = 8
NA_KW = 16
HG_WIDTH = D_MODEL // 2
HG_HEADS = HG_WIDTH // HEAD_DIM
HG_CHUNK = 64
HY_WIDTH = D_MODEL
HY_ORDER = 2
HY_EMB = 33
HY_BANDS = (HY_EMB - 1) // 2
HY_FAST_DECAY = 0.3
HY_SLOW_DECAY = 1.5
HY_TARGET = 1e-2
EPS = 1e-6

PAD = HG_CHUNK - N_META
LP = PAD + N_META + SEQ
L_TOK = N_META + SEQ

V7X_VMEM_LIMIT = 56 * 1024 * 1024


def _rmsnorm_kernel(x_ref, g_ref, o_ref):
    x = x_ref[...]
    r = lax.rsqrt(jnp.mean(x * x, axis=-1, keepdims=True) + EPS)
    o_ref[...] = (x * r * g_ref[...]).astype(o_ref.dtype)


def _rmsnorm(x2d, g, out_dtype, tr=320):
    m, d = x2d.shape
    return pl.pallas_call(
        _rmsnorm_kernel,
        grid=(m // tr,),
        in_specs=[pl.BlockSpec((tr, d), lambda i: (i, 0)), pl.BlockSpec((1, d), lambda i: (0, 0))],
        out_specs=pl.BlockSpec((tr, d), lambda i: (i, 0)),
        out_shape=jax.ShapeDtypeStruct((m, d), out_dtype),
        compiler_params=pltpu.CompilerParams(vmem_limit_bytes=V7X_VMEM_LIMIT),
        name="rmsnorm_pre",
    )(x2d, g.reshape(1, d))


def _post_kernel(h_ref, y_ref, g_ref, o_ref):
    y = y_ref[...].astype(jnp.float32)
    r = lax.rsqrt(jnp.mean(y * y, axis=-1, keepdims=True) + EPS)
    o_ref[...] = h_ref[...] + y * r * g_ref[...]


def _post_pre_kernel(h_ref, y_ref, g_ref, gn_ref, o_ref, n_ref):
    y = y_ref[...].astype(jnp.float32)
    r = lax.rsqrt(jnp.mean(y * y, axis=-1, keepdims=True) + EPS)
    h = h_ref[...] + y * r * g_ref[...]
    o_ref[...] = h
    rn = lax.rsqrt(jnp.mean(h * h, axis=-1, keepdims=True) + EPS)
    n_ref[...] = (h * rn * gn_ref[...]).astype(n_ref.dtype)


def _residual_rmsnorm(h2d, y2d, g, g_next=None, tr=320):
    m, d = h2d.shape
    row = pl.BlockSpec((tr, d), lambda i: (i, 0))
    vec = pl.BlockSpec((1, d), lambda i: (0, 0))
    params = pltpu.CompilerParams(dimension_semantics=("parallel",), vmem_limit_bytes=V7X_VMEM_LIMIT)
    if g_next is None:
        return pl.pallas_call(
            _post_kernel, grid=(m // tr,), in_specs=[row, row, vec], out_specs=row,
            out_shape=jax.ShapeDtypeStruct((m, d), jnp.float32), compiler_params=params,
            name="residual_rmsnorm_post",
        )(h2d, y2d, g.reshape(1, d))
    return pl.pallas_call(
        _post_pre_kernel, grid=(m // tr,), in_specs=[row, row, vec, vec], out_specs=[row, row],
        out_shape=[jax.ShapeDtypeStruct((m, d), jnp.float32), jax.ShapeDtypeStruct((m, d), jnp.bfloat16)],
        compiler_params=params, name="residual_rmsnorm_post_pre",
    )(h2d, y2d, g.reshape(1, d), g_next.reshape(1, d))


def _matmul_kernel(x_ref, w_ref, o_ref):
    o_ref[...] = jnp.dot(x_ref[...], w_ref[...].astype(jnp.bfloat16),
                         preferred_element_type=jnp.float32).astype(o_ref.dtype)


def _matmul(x, w_stack, layer, out_dtype, tm=1040, tn=512):
    m, kd = x.shape
    _, _, n = w_stack.shape
    assert m % tm == 0 and n % tn == 0
    return pl.pallas_call(
        _matmul_kernel,
        grid=(m // tm, n // tn),
        in_specs=[pl.BlockSpec((tm, kd), lambda i, j: (i, 0)),
                  pl.BlockSpec((None, kd, tn), lambda i, j: (layer, 0, j))],
        out_specs=pl.BlockSpec((tm, tn), lambda i, j: (i, j)),
        out_shape=jax.ShapeDtypeStruct((m, n), out_dtype),
        compiler_params=pltpu.CompilerParams(
            dimension_semantics=("parallel", "parallel"), vmem_limit_bytes=V7X_VMEM_LIMIT),
        name="projection_matmul",
    )(x, w_stack)


FFT_N1 = 128
FFT_N2 = 65
FFT_N = FFT_N1 * FFT_N2
N2P = 80
SLAB = 264
TC = 128
MID_GROUP = 16
NC_HALF = FFT_N1 // 2 + 1
HALF_P = 72
SLAB_S = 152
C_GROUP = 13
FILL_CHUNKS = 13
B_GROUP = 13


def _dft_constants():
    a = np.arange(FFT_N1)
    c = np.arange(FFT_N1)
    w1 = np.exp(-2j * np.pi * np.outer(c, a) / FFT_N1)
    w1r, w1i = w1.real, w1.imag
    half = FFT_N1 // 2
    f1 = np.block([[w1r[:, :half], -w1i[:, :half]], [w1i[:, :half], w1r[:, :half]]])
    f1_real = np.zeros((2 * HALF_P, FFT_N1), np.float64)
    f1_real[:NC_HALF] = w1r[:NC_HALF]
    f1_real[HALF_P:HALF_P + NC_HALF] = w1i[:NC_HALF]
    g = np.conj(w1).T
    gr, gi = g.real[:half], g.imag[:half]
    i2 = np.block([[gr, -gi], [gi, gr]])
    b = np.arange(FFT_N2)
    d = np.arange(FFT_N2)
    w2 = np.exp(-2j * np.pi * np.outer(d, b) / FFT_N2)
    tw = np.exp(-2j * np.pi * np.outer(c, b) / FFT_N)
    m = w2[None, :, :] * tw[:, None, :]
    ab = np.zeros((FFT_N1, 2 * N2P, 128), np.float64)
    m[NC_HALF:] = m[NC_HALF:, ::-1, :]
    ab[:, :FFT_N2, :FFT_N2] = m.real
    ab[:, N2P:N2P + FFT_N2, :FFT_N2] = m.imag
    bf = jnp.bfloat16
    abt = np.zeros((FFT_N1, N2P, 256), np.float64)
    abt[:, :, :2 * N2P] = np.swapaxes(ab[:, :, :N2P], 1, 2)
    return (jnp.asarray(f1, bf), jnp.asarray(f1_real, bf), jnp.asarray(i2, bf), jnp.asarray(ab, bf),
            jnp.asarray(abt, bf))


def _middle_stage_inputs(y_ref, c, imag_row=FFT_N1, pitch=SLAB):
    yr = y_ref[pl.ds(c, N2P, stride=pitch), :]
    yi = y_ref[pl.ds(imag_row + c, N2P, stride=pitch), :]
    return jnp.concatenate([yr, yi], axis=1).astype(jnp.bfloat16)


def _forward_middle(ab, y2):
    y2 = jnp.concatenate([y2, jnp.zeros((128 - N2P, 2 * TC), jnp.bfloat16)], axis=0)
    q = jnp.dot(ab, y2, preferred_element_type=jnp.float32)
    zr = q[:N2P, :TC] - q[N2P:, TC:]
    zi = q[N2P:, :TC] + q[:N2P, TC:]
    return zr, zi


def _spectrum_kernel(h_ref, t_ref, w3f0_ref, w3f1_ref, w3b0_ref, w3b1_ref, delta_ref, f1_ref, ab_ref, k_ref,
                     kt_ref, y_ref):
    w3 = jnp.concatenate([jnp.concatenate([w3f0_ref[...], w3f1_ref[...]], axis=1),
                          jnp.concatenate([w3b0_ref[...], w3b1_ref[...]], axis=1)], axis=0).astype(jnp.bfloat16)
    delta = delta_ref[...]
    rows = FFT_N // FILL_CHUNKS

    def fill(i, carry):
        r0 = pl.multiple_of(i * rows, 8)
        kt = jnp.dot(h_ref[pl.ds(r0, rows), :], w3, preferred_element_type=jnp.float32)
        window = jnp.exp(t_ref[pl.ds(r0, rows), :] * delta)
        for n in range(HY_ORDER):
            kt_ref[n, pl.ds(r0, rows), :] = kt[:, n * TC:(n + 1) * TC] * window
        return carry

    lax.fori_loop(0, FILL_CHUNKS, fill, 0)
    y_ref[pl.ds(FFT_N2 * SLAB_S, (N2P - FFT_N2) * SLAB_S), :] = jnp.zeros(((N2P - FFT_N2) * SLAB_S, TC), jnp.float32)

    for n in range(HY_ORDER):
        def stage1(g, carry):
            bs = [g * B_GROUP + i for i in range(B_GROUP)]
            x = jnp.concatenate([kt_ref.at[n][pl.ds(b, FFT_N1, stride=FFT_N2), :] for b in bs], axis=1)
            y = jnp.dot(f1_ref[...], x.astype(jnp.bfloat16), preferred_element_type=jnp.float32)
            for i, b in enumerate(bs):
                y_ref[pl.ds(pl.multiple_of(b * SLAB_S, 8), 2 * HALF_P), :] = y[:, i * TC:(i + 1) * TC]
            return carry

        lax.fori_loop(0, FFT_N2 // B_GROUP, stage1, 0)

        def stage2(g, carry):
            cs = [g * C_GROUP + i for i in range(C_GROUP)]
            zs = [_forward_middle(ab_ref[c], _middle_stage_inputs(y_ref, c, HALF_P, SLAB_S)) for c in cs]
            for c, (zr, zi) in zip(cs, zs):
                k_ref[n, c] = (jnp.concatenate([zr, zi], axis=1) * (1.0 / FFT_N)).astype(jnp.bfloat16)
            return carry

        lax.fori_loop(0, NC_HALF // C_GROUP, stage2, 0)


def _filter_spectrum(h2, w3, deltas, consts):
    _, f1_real, _, ab, _ = consts
    nt = HY_WIDTH // TC
    j = np.arange(FFT_N)
    lag = np.where(j < L_TOK, j, FFT_N - j).astype(np.float32)
    neg_t = jnp.asarray(np.broadcast_to((-lag / np.float32(L_TOK - 1))[:, None], (FFT_N, TC)))
    w3_cols = lambda direction, order: pl.BlockSpec((64, TC), lambda i: (0, (direction * HY_ORDER + order) * nt + i))
    return pl.pallas_call(
        _spectrum_kernel,
        grid=(nt,),
        in_specs=[
            pl.BlockSpec((FFT_N, 128), lambda i: (0, 0)),
            pl.BlockSpec((FFT_N, TC), lambda i: (0, 0)),
            w3_cols(0, 0), w3_cols(0, 1), w3_cols(1, 0), w3_cols(1, 1),
            pl.BlockSpec((1, TC), lambda i: (0, i)),
            pl.BlockSpec((2 * HALF_P, FFT_N1), lambda i: (0, 0)),
            pl.BlockSpec((FFT_N1, 2 * N2P, 128), lambda i: (0, 0, 0)),
        ],
        out_specs=pl.BlockSpec((HY_ORDER, None, NC_HALF, N2P, 2 * TC), lambda i: (0, i, 0, 0, 0)),
        out_shape=jax.ShapeDtypeStruct((HY_ORDER, nt, NC_HALF, N2P, 2 * TC), jnp.bfloat16),
        scratch_shapes=[pltpu.VMEM((HY_ORDER, FFT_N, TC), jnp.float32), pltpu.VMEM((N2P * SLAB_S, TC), jnp.float32)],
        compiler_params=pltpu.CompilerParams(dimension_semantics=("parallel",), vmem_limit_bytes=V7X_VMEM_LIMIT),
        name="hyena_filter_spectrum",
    )(h2, neg_t, w3, w3, w3, w3, deltas, f1_real, ab)


def _conv_kernel(u_ref, k_ref, f1_ref, i2_ref, ab_ref, abt_ref, o_ref, y_ref):
    y_ref[pl.ds(FFT_N2 * SLAB, (N2P - FFT_N2) * SLAB), :] = jnp.zeros(((N2P - FFT_N2) * SLAB, TC), jnp.float32)
    half = FFT_N1 // 2

    def stage1(g, carry):
        bs = [g * B_GROUP + i for i in range(B_GROUP)]
        x = jnp.concatenate(
            [jnp.concatenate([u_ref.at[ri][pl.ds(b, half, stride=FFT_N2), :] for b in bs], axis=1) for ri in range(2)],
            axis=0)
        y = jnp.dot(f1_ref[...], x.astype(jnp.bfloat16), preferred_element_type=jnp.float32)
        for i, b in enumerate(bs):
            y_ref[pl.ds(pl.multiple_of(b * SLAB, 8), 2 * FFT_N1), :] = y[:, i * TC:(i + 1) * TC]
        return carry

    lax.fori_loop(0, FFT_N2 // B_GROUP, stage1, 0)

    def inverse_middle(c, z):
        zr, zi = z
        mirrored = c >= NC_HALF
        kk = k_ref[jnp.where(mirrored, FFT_N1 - c, c)].astype(jnp.float32)
        kr, ki = kk[:, :TC], kk[:, TC:] * jnp.where(mirrored, -1.0, 1.0)
        pr = zr * kr - zi * ki
        pi = zr * ki + zi * kr
        rhs = jnp.concatenate([jnp.concatenate([pr, pi], axis=1), jnp.concatenate([pi, -pr], axis=1),
                               jnp.zeros((256 - 2 * N2P, 2 * TC), jnp.float32)], axis=0)
        return jnp.dot(abt_ref[c], rhs.astype(jnp.bfloat16), preferred_element_type=jnp.float32)

    def middle(g, carry):
        cs = [g * MID_GROUP + i for i in range(MID_GROUP)]
        zs = [_forward_middle(ab_ref[c], _middle_stage_inputs(y_ref, c)) for c in cs]
        outs = [inverse_middle(c, z) for c, z in zip(cs, zs)]
        for c, v in zip(cs, outs):
            y_ref[pl.ds(c, N2P, stride=SLAB), :] = v[:N2P, :TC]
            y_ref[pl.ds(FFT_N1 + c, N2P, stride=SLAB), :] = v[:N2P, TC:]
        return carry

    lax.fori_loop(0, FFT_N1 // MID_GROUP, middle, 0)

    def stage4(g, carry):
        bs = [g * B_GROUP + i for i in range(B_GROUP)]
        v = jnp.concatenate([y_ref[pl.ds(pl.multiple_of(b * SLAB, 8), 2 * FFT_N1), :] for b in bs], axis=1)
        x = jnp.dot(i2_ref[...], v.astype(jnp.bfloat16), preferred_element_type=jnp.float32)
        for i, b in enumerate(bs):
            o_ref.at[0][pl.ds(b, half, stride=FFT_N2), :] = x[:half, i * TC:(i + 1) * TC]
            o_ref.at[1][pl.ds(b, half, stride=FFT_N2), :] = x[half:, i * TC:(i + 1) * TC]
        return carry

    lax.fori_loop(0, FFT_N2 // B_GROUP, stage4, 0)


def _long_conv(u, kspec, order, consts):
    f1, _, i2, ab, abt = consts
    bsz, lp, _ = u.shape
    assert bsz == 2 and lp == LP
    nt = HY_WIDTH // TC
    return pl.pallas_call(
        _conv_kernel,
        grid=(nt,),
        in_specs=[
            pl.BlockSpec((bsz, lp, TC), lambda i: (0, 0, i)),
            pl.BlockSpec((None, None, NC_HALF, N2P, 2 * TC), lambda i: (order, i, 0, 0, 0)),
            pl.BlockSpec((2 * FFT_N1, FFT_N1), lambda i: (0, 0)),
            pl.BlockSpec((FFT_N1, 2 * FFT_N1), lambda i: (0, 0)),
            pl.BlockSpec((FFT_N1, 2 * N2P, 128), lambda i: (0, 0, 0), pipeline_mode=pl.Buffered(1)),
            pl.BlockSpec((FFT_N1, N2P, 256), lambda i: (0, 0, 0), pipeline_mode=pl.Buffered(1)),
        ],
        out_specs=pl.BlockSpec((bsz, lp, TC), lambda i: (0, 0, i)),
        out_shape=jax.ShapeDtypeStruct((bsz, lp, HY_WIDTH), jnp.float32),
        scratch_shapes=[pltpu.VMEM((N2P * SLAB, TC), jnp.float32)],
        compiler_params=pltpu.CompilerParams(dimension_semantics=("parallel",), vmem_limit_bytes=V7X_VMEM_LIMIT),
        name="hyena_long_conv",
    )(u, kspec, f1, i2, ab, abt)


def _short_conv_rows(u, w):
    y = pltpu.roll(u, 1, 0) * w[0:1] + u * w[1:2] + pltpu.roll(u, LP - 1, 0) * w[2:3]
    row = lax.broadcasted_iota(jnp.int32, y.shape, 0)
    return jnp.where(row >= PAD, y, 0.0)


def _short_conv_kernel(u_ref, w_ref, o_ref):
    o_ref[...] = _short_conv_rows(u_ref[...].astype(jnp.float32), w_ref[...])


def _gate1_kernel(v_ref, u_ref, w_ref, y_ref, b_ref, o_ref):
    x = _short_conv_rows(u_ref[...].astype(jnp.float32), w_ref[...])
    o_ref[...] = x * (y_ref[...] + v_ref[...] * b_ref[...])


def _gate2_kernel(z_ref, u_ref, w_ref, y_ref, b_ref, g_ref, o_ref):
    g = g_ref[...].astype(jnp.float32)
    z = _short_conv_rows(u_ref[...].astype(jnp.float32), w_ref[...]) * (y_ref[...] + z_ref[...] * b_ref[...])
    o_ref[...] = (z * (g * jax.nn.sigmoid(g))).astype(o_ref.dtype)


def _seq_tile(col_block):
    return pl.BlockSpec((None, LP, TC), lambda b, i: (b, 0, col_block + i))


def _vec_tile(rows, col_block):
    return pl.BlockSpec((rows, TC), lambda b, i: (0, col_block + i))


def _short_conv(p3, w_short_t):
    bsz = p3.shape[0]
    return pl.pallas_call(
        _short_conv_kernel,
        grid=(bsz, HY_WIDTH // TC),
        in_specs=[_seq_tile(0), _vec_tile(3, 0)],
        out_specs=_seq_tile(0),
        out_shape=jax.ShapeDtypeStruct((bsz, LP, HY_WIDTH), jnp.float32),
        compiler_params=pltpu.CompilerParams(
            dimension_semantics=("parallel", "parallel"), vmem_limit_bytes=V7X_VMEM_LIMIT),
        name="hyena_short_conv",
    )(p3, w_short_t)


def _hyena_gate1(v, p3, w_short_t, y, bias):
    bsz = p3.shape[0]
    nt = HY_WIDTH // TC
    return pl.pallas_call(
        _gate1_kernel,
        grid=(bsz, nt),
        in_specs=[_seq_tile(0), _seq_tile(nt), _vec_tile(3, nt), _seq_tile(0), _vec_tile(1, 0)],
        out_specs=_seq_tile(0),
        out_shape=jax.ShapeDtypeStruct((bsz, LP, HY_WIDTH), jnp.float32),
        compiler_params=pltpu.CompilerParams(
            dimension_semantics=("parallel", "parallel"), vmem_limit_bytes=V7X_VMEM_LIMIT),
        name="hyena_gate1",
    )(v, p3, w_short_t, y, bias.reshape(1, HY_WIDTH))


def _hyena_gate2(z1, p3, w_short_t, y, bias):
    bsz = p3.shape[0]
    nt = HY_WIDTH // TC
    return pl.pallas_call(
        _gate2_kernel,
        grid=(bsz, nt),
        in_specs=[_seq_tile(0), _seq_tile(2 * nt), _vec_tile(3, 2 * nt), _seq_tile(0), _vec_tile(1, 0),
                  _seq_tile(3 * nt)],
        out_specs=_seq_tile(0),
        out_shape=jax.ShapeDtypeStruct((bsz, LP, HY_WIDTH), jnp.bfloat16),
        compiler_params=pltpu.CompilerParams(
            dimension_semantics=("parallel", "parallel"), vmem_limit_bytes=V7X_VMEM_LIMIT),
        name="hyena_gate2",
    )(z1, p3, w_short_t, y, bias.reshape(1, HY_WIDTH), p3)


def _filter_positions():
    f32 = np.float32
    l = L_TOK
    t = np.linspace(0.0, 1.0, l, dtype=f32)[:, None]
    w = f32(2.0 * math.pi) * np.arange(l, dtype=f32)[:, None] / f32(l)
    bands = np.linspace(1e-4, HY_BANDS - 1, HY_BANDS, dtype=f32)[None, :]
    z = np.concatenate([t, np.cos(bands * w), -np.sin(bands * w)], axis=-1).astype(f32)
    j = np.arange(FFT_N)
    lag = np.minimum(np.where(j < l, j, FFT_N - j), l - 1)
    zrows = np.zeros((FFT_N, 128), f32)
    zrows[:, :HY_EMB] = z[lag]
    mask = np.zeros((FFT_N, 128), f32)
    mask[:, :64] = (j < l)[:, None]
    mask[:, 64:] = ((j > FFT_N - l) | (j == 0))[:, None]
    return jnp.asarray(zrows), jnp.asarray(mask)


def _hidden_kernel(z_ref, m_ref, w1_ref, b1_ref, fr1_ref, w2_ref, b2_ref, fr2_ref, o_ref):
    hi = lax.Precision.HIGHEST
    a = jnp.dot(z_ref[...], w1_ref[...], precision=hi, preferred_element_type=jnp.float32) + b1_ref[...]
    hid = jnp.sin(fr1_ref[...] * a)
    a = jnp.dot(hid, w2_ref[...], precision=hi, preferred_element_type=jnp.float32) + b2_ref[...]
    hid = jnp.sin(fr2_ref[...] * a)
    o_ref[...] = (jnp.concatenate([hid, hid], axis=1) * m_ref[...]).astype(o_ref.dtype)


def _hyena_hidden(w1, b1, fr1, w2, b2, fr2):
    zrows, mask = _filter_positions()
    rows = FFT_N // FILL_CHUNKS
    hdim = w2.shape[0]
    w1p = jnp.zeros((128, hdim), jnp.float32).at[:HY_EMB].set(w1.astype(jnp.float32))
    tile = pl.BlockSpec((rows, 128), lambda i: (i, 0))
    full = lambda a: pl.BlockSpec(a.shape, lambda i: (0,) * a.ndim)
    vecs = [v.astype(jnp.float32).reshape(1, hdim) for v in (b1, fr1, b2, fr2)]
    w2f = w2.astype(jnp.float32)
    return pl.pallas_call(
        _hidden_kernel,
        grid=(FILL_CHUNKS,),
        in_specs=[tile, tile, full(w1p), full(vecs[0]), full(vecs[1]), full(w2f), full(vecs[2]), full(vecs[3])],
        out_specs=tile,
        out_shape=jax.ShapeDtypeStruct((FFT_N, 128), jnp.bfloat16),
        compiler_params=pltpu.CompilerParams(dimension_semantics=("parallel",), vmem_limit_bytes=V7X_VMEM_LIMIT),
        name="hyena_filter_hidden",
    )(zrows, mask, w1p, vecs[0], vecs[1], w2f, vecs[2], vecs[3])


def _odd_mix(p3, w_short, w1, b1, fr1, w2, b2, fr2, w3, bias, consts):
    min_decay = math.log(HY_TARGET) / HY_SLOW_DECAY
    max_decay = math.log(HY_TARGET) / HY_FAST_DECAY
    deltas = jnp.abs(jnp.linspace(min_decay, max_decay, HY_WIDTH, dtype=jnp.float32)).reshape(1, HY_WIDTH)
    kspec = _filter_spectrum(_hyena_hidden(w1, b1, fr1, w2, b2, fr2), w3, deltas, consts)
    w_short_t = w_short.T
    v = _short_conv(p3, w_short_t)
    y1 = _long_conv(v, kspec, 0, consts)
    z1 = _hyena_gate1(v, p3, w_short_t, y1, bias[0])
    y2 = _long_conv(z1, kspec, 1, consts)
    return _hyena_gate2(z1, p3, w_short_t, y2, bias[1])


HG_HB = 4
HG_LEVELS = (32, 16, 8, 4, 2, 1)
LOG2_E = 1.4426950408889634


def _cumsum_rows(g, rev):
    f32, bf = jnp.float32, jnp.bfloat16
    t = lax.broadcasted_iota(jnp.int32, (HG_CHUNK, 3 * HG_CHUNK), 0)
    s = lax.broadcasted_iota(jnp.int32, (HG_CHUNK, 3 * HG_CHUNK), 1) & (HG_CHUNK - 1)
    tri = jnp.where((s >= t) if rev else (s <= t), 1.0, 0.0).astype(bf)
    hi = g.astype(bf)
    r1 = g - hi.astype(f32)
    mid = r1.astype(bf)
    lo = (r1 - mid.astype(f32)).astype(bf)
    return jnp.dot(tri, jnp.concatenate([hi, mid, lo], axis=0), preferred_element_type=f32)


def _level_reference(gc, row, m, rev):
    if m >= 4:
        g3 = gc.reshape(HG_CHUNK // (2 * m), 2 * m, HEAD_DIM)
        ref = g3[:, m:m + 1, :] if rev else g3[:, m - 1:m, :]
        return jnp.broadcast_to(ref, g3.shape).reshape(HG_CHUNK, HEAD_DIM)
    r = row & (2 * m - 1)
    target = m if rev else m - 1
    out = gc
    for src in range(2 * m):
        if src != target:
            out = jnp.where(r == src, pltpu.roll(gc, (src - target) % HG_CHUNK, 0), out)
    return out


def _hgrn_gates(q_raw, f_raw, la, lc, omlb, row, rev):
    q = q_raw * jax.nn.sigmoid(q_raw) * (HEAD_DIM ** -0.5)
    e = jnp.exp(-jnp.abs(f_raw))
    ope = 1.0 + e
    log_sig = jnp.minimum(f_raw, 0.0) - jnp.log(ope)
    k = omlb * (jnp.where(f_raw >= 0.0, e, 1.0) / ope)
    b = lc + log_sig
    g = jnp.maximum(la, b) + jnp.log(1.0 + jnp.exp(-jnp.abs(la - b)))
    return q, k, _cumsum_rows(g, rev)


def _hgrn_level_operands(q, k, gc, row, rev):
    bf = jnp.bfloat16
    qb, kb = q.astype(bf), k.astype(bf)
    ops = []
    for m in HG_LEVELS:
        gref = _level_reference(gc, row, m, rev)
        upper = (row & m) != 0
        is_query = jnp.logical_not(upper) if rev else upper
        sign_log2e = jnp.where(is_query, LOG2_E, -LOG2_E)
        ops.append(jnp.where(is_query, qb, kb) * jnp.exp2((gc - gref) * sign_log2e).astype(bf))
    return ops


def _hgrn_level_id(rev):
    t = lax.broadcasted_iota(jnp.int32, (HG_CHUNK, HG_CHUNK), 0)
    s = lax.broadcasted_iota(jnp.int32, (HG_CHUNK, HG_CHUNK), 1)
    vis = (s >= t) if rev else (s <= t)
    x = t ^ s
    lvl = jnp.full((HG_CHUNK, HG_CHUNK), len(HG_LEVELS), jnp.int32)
    for i, m in enumerate(HG_LEVELS):
        lvl = jnp.where((x & m) != 0, jnp.minimum(lvl, i), lvl)
    return jnp.where(vis, lvl, -1)


def _hgrn_kernel(qf_ref, if_ref, ff_ref, qb_ref, ib_ref, fb_ref, la_ref, lc_ref, om_ref, of_ref, ob_ref, s_ref):
    f32, bf = jnp.float32, jnp.bfloat16
    nt = (((1,), (1,)), ((), ()))
    tn = (((0,), (0,)), ((), ()))

    @pl.when(pl.program_id(2) == 0)
    def _():
        s_ref[...] = jnp.zeros_like(s_ref)

    row = lax.broadcasted_iota(jnp.int32, (HG_CHUNK, HEAD_DIM), 0)
    refs = ((qf_ref, if_ref, ff_ref, of_ref), (qb_ref, ib_ref, fb_ref, ob_ref))
    inst = [(d, h) for d in range(2) for h in range(HG_HB)]
    level_masks = [[lvl == i for i in range(len(HG_LEVELS) + 1)]
                   for lvl in (_hgrn_level_id(False), _hgrn_level_id(True))]

    pre = []
    for d, h in inst:
        q_ref, i_ref, f_ref, _ = refs[d]
        sl = slice(h * HEAD_DIM, (h + 1) * HEAD_DIM)
        q, k, gc = _hgrn_gates(q_ref[:, sl], f_ref[:, sl], la_ref[d:d + 1, sl], lc_ref[d:d + 1, sl],
                               om_ref[d:d + 1, sl], row, d == 1)
        g_end = gc[0:1, :] if d == 1 else gc[HG_CHUNK - 1:HG_CHUNK, :]
        pre.append(dict(ops=_hgrn_level_operands(q, k, gc, row, d == 1), qk=(q.astype(bf), k.astype(bf)),
                        qe=(q * jnp.exp(gc)).astype(bf), kd=(k * jnp.exp(g_end - gc)).astype(bf),
                        decay=jnp.exp(g_end), v=i_ref[:, sl].astype(bf)))

    attn = []
    for (d, h), pr in zip(inst, pre):
        prods = [lax.dot_general(x, x, nt, preferred_element_type=f32) for x in pr["ops"]]
        prods.append(lax.dot_general(pr["qk"][0], pr["qk"][1], nt, preferred_element_type=f32))
        a = jnp.zeros((HG_CHUNK, HG_CHUNK), f32)
        for i, pmat in enumerate(prods):
            a = jnp.where(level_masks[d][i], pmat, a)
        attn.append(a.astype(bf))

    for (d, h), pr, a in zip(inst, pre, attn):
        o_ref = refs[d][3]
        sl = slice(h * HEAD_DIM, (h + 1) * HEAD_DIM)
        state_t = s_ref[d, h]
        o = jnp.dot(a, pr["v"], preferred_element_type=f32)
        o = o + lax.dot_general(pr["qe"], state_t.astype(bf), nt, preferred_element_type=f32)
        o_ref[:, sl] = o.astype(o_ref.dtype)
        s_ref[d, h] = pr["decay"] * state_t + lax.dot_general(pr["v"], pr["kd"], tn, preferred_element_type=f32)


def _hgrn2(p, col0, lb):
    bsz, lp, _ = p.shape
    nchunk = lp // HG_CHUNK
    cw = HG_HB * HEAD_DIM
    ng = HG_WIDTH // cw
    off = col0 // cw
    la = jnp.log(lb)
    lc = jnp.log1p(-lb)
    om = 1.0 - lb
    fwd = lambda seg: pl.BlockSpec((None, HG_CHUNK, cw), lambda b, g, c: (b, c, off + seg * ng + g))
    bwd = lambda seg: pl.BlockSpec((None, HG_CHUNK, cw), lambda b, g, c: (b, nchunk - 1 - c, off + seg * ng + g))
    par = pl.BlockSpec((2, cw), lambda b, g, c: (0, g))
    return pl.pallas_call(
        _hgrn_kernel,
        grid=(bsz, ng, nchunk),
        in_specs=[fwd(0), fwd(1), fwd(2), bwd(0), bwd(1), bwd(3), par, par, par],
        out_specs=[pl.BlockSpec((None, HG_CHUNK, cw), lambda b, g, c: (b, c, g)),
                   pl.BlockSpec((None, HG_CHUNK, cw), lambda b, g, c: (b, nchunk - 1 - c, g))],
        out_shape=[jax.ShapeDtypeStruct((bsz, lp, HG_WIDTH), jnp.bfloat16)] * 2,
        scratch_shapes=[pltpu.VMEM((2, HG_HB, HEAD_DIM, HEAD_DIM), jnp.float32)],
        compiler_params=pltpu.CompilerParams(
            dimension_semantics=("parallel", "parallel", "arbitrary"), vmem_limit_bytes=V7X_VMEM_LIMIT),
        name="hgrn2_bidirectional",
    )(p, p, p, p, p, p, la, lc, om)


EG_ROWS = 320


def _even_gate_kernel(oa_ref, of_ref, ob_ref, ga_ref, gb_ref, gain_ref, y_ref):
    ga = ga_ref[...]
    y_ref[:, :NA_WIDTH] = (oa_ref[...].astype(jnp.float32) * (ga * jax.nn.sigmoid(ga))).astype(y_ref.dtype)
    for h in range(HG_HEADS):
        sl = slice(h * HEAD_DIM, (h + 1) * HEAD_DIM)
        oh = of_ref[:, sl].astype(jnp.float32) + ob_ref[:, sl].astype(jnp.float32)
        oh = oh * lax.rsqrt(jnp.mean(oh * oh, axis=-1, keepdims=True) + EPS) * gain_ref[:, sl]
        gb = gb_ref[:, sl]
        y_ref[:, NA_WIDTH + h * HEAD_DIM:NA_WIDTH + (h + 1) * HEAD_DIM] = (
            oh * (gb * jax.nn.sigmoid(gb))).astype(y_ref.dtype)


def _even_gate(oa, o_fwd, o_bwd, p, gate_col0, norm_gain):
    bsz, lp, _ = oa.shape
    assert gate_col0 % NA_WIDTH == 0
    goff = gate_col0 // NA_WIDTH
    half = lambda off: pl.BlockSpec((None, EG_ROWS, NA_WIDTH), lambda b, r: (b, r, off))
    return pl.pallas_call(
        _even_gate_kernel,
        grid=(bsz, lp // EG_ROWS),
        in_specs=[half(0), half(0), half(0), half(goff), half(goff + 1),
                  pl.BlockSpec((1, HG_WIDTH), lambda b, r: (0, 0))],
        out_specs=pl.BlockSpec((None, EG_ROWS, D_MODEL), lambda b, r: (b, r, 0)),
        out_shape=jax.ShapeDtypeStruct((bsz, lp, D_MODEL), jnp.bfloat16),
        compiler_params=pltpu.CompilerParams(
            dimension_semantics=("parallel", "parallel"), vmem_limit_bytes=V7X_VMEM_LIMIT),
        name="even_gate",
    )(oa, o_fwd, o_bwd, p, p, norm_gain.reshape(1, HG_WIDTH))


NA_MASKED = -1e30
NA_GROUP = 8


def _na_bias_table(rpb):
    cols = np.arange(GRID_W)
    c0 = np.clip(cols - NA_KW // 2, 0, GRID_W - NA_KW)
    col_ok = (cols[None, :] >= c0[:, None]) & (cols[None, :] < c0[:, None] + NA_KW)
    col_idx = np.clip(cols[None, :] - cols[:, None] + NA_KW - 1, 0, 2 * NA_KW - 2)
    table = jnp.where(col_ok[None, None], rpb.astype(jnp.float32)[:, :, col_idx], NA_MASKED)
    return jnp.concatenate([table[:, :-1], table[:, 1:]], axis=-1)


def _na_kernel(q_ref, k_ref, v_ref, t_ref, mb_ref, o_ref):
    f32, bf = jnp.float32, jnp.bfloat16
    rows = (q_ref.shape[0] - HG_CHUNK) // GRID_W
    scale = HEAD_DIM ** -0.5
    nt = (((1,), (1,)), ((), ()))
    km = k_ref[PAD:PAD + N_META, :].astype(bf)
    vm = v_ref[PAD:PAD + N_META, :].astype(bf)
    mb = mb_ref[...]
    o_ref[0:PAD, :] = jnp.zeros((PAD, HEAD_DIM), o_ref.dtype)

    qm = q_ref[PAD:PAD + N_META, :].astype(bf)
    s = lax.dot_general(qm, km, nt, preferred_element_type=f32) * scale + mb
    e = jnp.exp(s - jnp.max(s, axis=-1, keepdims=True))
    o_ref[PAD:PAD + N_META, :] = (jnp.dot(e.astype(bf), vm, preferred_element_type=f32)
                                  / jnp.sum(e, axis=-1, keepdims=True)).astype(o_ref.dtype)

    def row_group(g, carry):
        rs = [g * NA_GROUP + i for i in range(NA_GROUP)]
        scores = []
        for r in rs:
            r0 = jnp.clip(r - NA_KH // 2, 0, rows - NA_KH)
            q = q_ref[pl.ds(pl.multiple_of(HG_CHUNK + GRID_W * r, GRID_W), GRID_W), :].astype(bf)
            k0 = pl.multiple_of(HG_CHUNK + GRID_W * r0, GRID_W)
            kw = k_ref[pl.ds(k0, NA_KH * GRID_W), :].astype(bf)
            s_win = lax.dot_general(q, kw, nt, preferred_element_type=f32)
            s_meta = lax.dot_general(q, km, nt, preferred_element_type=f32)
            scores.append((r, r0, k0, s_win, s_meta))
        probs = []
        for r, r0, k0, s_win, s_meta in scores:
            bias = jnp.concatenate([t_ref[r0 + i - r + NA_KH - 1] for i in range(0, NA_KH, 2)], axis=1)
            s_win = s_win * scale + bias
            s_meta = s_meta * scale + mb
            m = jnp.maximum(jnp.max(s_win, axis=-1, keepdims=True), jnp.max(s_meta, axis=-1, keepdims=True))
            pw = jnp.exp(s_win - m)
            pm = jnp.exp(s_meta - m)
            den = jnp.sum(pw, axis=-1, keepdims=True) + jnp.sum(pm, axis=-1, keepdims=True)
            probs.append((r, k0, pw.astype(bf), pm.astype(bf), den))
        for r, k0, pw, pm, den in probs:
            vw = v_ref[pl.ds(k0, NA_KH * GRID_W), :].astype(bf)
            o = jnp.dot(pw, vw, preferred_element_type=f32) + jnp.dot(pm, vm, preferred_element_type=f32)
            o_ref[pl.ds(pl.multiple_of(HG_CHUNK + GRID_W * r, GRID_W), GRID_W), :] = (o / den).astype(o_ref.dtype)
        return carry

    lax.fori_loop(0, rows // NA_GROUP, row_group, 0)


def _natten(p, rpb, meta_bias):
    bsz, lp, _ = p.shape
    assert ((lp - HG_CHUNK) // GRID_W) % NA_GROUP == 0
    heads = NA_WIDTH // HEAD_DIM
    blk = lambda seg: pl.BlockSpec((None, lp, HEAD_DIM), lambda b, h: (b, 0, seg * heads + h))
    table = _na_bias_table(rpb)
    return pl.pallas_call(
        _na_kernel,
        grid=(bsz, heads),
        in_specs=[blk(0), blk(1), blk(2),
                  pl.BlockSpec((None, 2 * NA_KH - 2, GRID_W, 2 * GRID_W), lambda b, h: (h, 0, 0, 0)),
                  pl.BlockSpec((None, 1, N_META), lambda b, h: (h, 0, 0))],
        out_specs=pl.BlockSpec((None, lp, HEAD_DIM), lambda b, h: (b, 0, h)),
        out_shape=jax.ShapeDtypeStruct((bsz, lp, NA_WIDTH), jnp.bfloat16),
        compiler_params=pltpu.CompilerParams(
            dimension_semantics=("parallel", "parallel"), vmem_limit_bytes=V7X_VMEM_LIMIT),
        name="neighbourhood_attention",
    )(p, p, p, table, meta_bias.astype(jnp.float32).reshape(heads, 1, N_META))


def _even_mix(p, rpb, meta_bias, lb, norm_gain):
    oa = _natten(p, rpb, meta_bias)
    o_fwd, o_bwd = _hgrn2(p, 3 * NA_WIDTH, lb)
    return _even_gate(oa, o_fwd, o_bwd, p, 3 * NA_WIDTH + 4 * HG_WIDTH, norm_gain)


def kernel(x, meta_tokens, norm_pre, norm_post, ev_w_in, ev_w_out, na_rpb, na_meta_bias, hg_lower, hg_norm,
           od_w_in, od_w_out, hy_short, hy_ffn_w1, hy_ffn_b1, hy_ffn_freq1, hy_ffn_w2, hy_ffn_b2, hy_ffn_freq2,
           hy_ffn_w3, hy_bias):
    b = x.shape[0]
    depth = norm_pre.shape[0]
    f32 = jnp.float32
    lb_all = jnp.cumsum(jax.nn.softmax(hg_lower.astype(f32), axis=0), axis=0)
    lb_all = lb_all - lb_all[:1]
    meta = jnp.broadcast_to(meta_tokens.astype(f32)[None], (b, N_META, D_MODEL))
    h = jnp.concatenate([jnp.zeros((b, PAD, D_MODEL), f32), meta, x.astype(f32)], axis=1)
    h = h.reshape(b * LP, D_MODEL)
    consts = _dft_constants()
    hn = _rmsnorm(h, norm_pre[0], jnp.bfloat16)
    for layer in range(depth):
        j = layer // 2
        if layer % 2 == 0:
            p = _matmul(hn, ev_w_in, j, f32).reshape(b, LP, -1)
            y = _even_mix(p, na_rpb[j], na_meta_bias[j], lb_all[j], hg_norm[j]).reshape(b * LP, D_MODEL)
            w_out = ev_w_out
        else:
            p3 = _matmul(hn, od_w_in, j, jnp.bfloat16).reshape(b, LP, -1)
            y = _odd_mix(p3, hy_short[j], hy_ffn_w1[j], hy_ffn_b1[j], hy_ffn_freq1[j], hy_ffn_w2[j],
                         hy_ffn_b2[j], hy_ffn_freq2[j], hy_ffn_w3[j], hy_bias[j], consts)
            y = y.reshape(b * LP, D_MODEL)
            w_out = od_w_out
        out = _matmul(y, w_out, j, jnp.bfloat16)
        if layer + 1 < depth:
            h, hn = _residual_rmsnorm(h, out, norm_post[layer], norm_pre[layer + 1])
        else:
            h = _residual_rmsnorm(h, out, norm_post[layer])
    return h.reshape(b, LP, D_MODEL)[:, PAD + N_META:].astype(x.dtype)
```

```python
import functools
import math

import jax
import jax.numpy as jnp
import numpy as np
from jax import lax
from jax.experimental import pallas as pl
from jax.experimental.pallas import tpu as pltpu

D_MODEL = 4096
SEQ = 4096
N_META = 16
GRID_W = 64
HEAD_DIM = 128
NA_WIDTH = D_MODEL // 2
NA_KH = 8
NA_KW = 16
HG_WIDTH = D_MODEL // 2
HG_HEADS = HG_WIDTH // HEAD_DIM
HG_CHUNK = 64
HY_WIDTH = D_MODEL
HY_ORDER = 2
HY_EMB = 33
HY_BANDS = (HY_EMB - 1) // 2
HY_FAST_DECAY = 0.3
HY_SLOW_DECAY = 1.5
HY_TARGET = 1e-2
EPS = 1e-6

PAD = HG_CHUNK - N_META
LP = PAD + N_META + SEQ
L_TOK = N_META + SEQ

V7X_VMEM_LIMIT = 56 * 1024 * 1024


EMBED_PIECES = 5


def _embed_kernel(*refs):
    x_refs, meta_ref, g_ref, h_ref, n_ref = refs[:EMBED_PIECES], *refs[EMBED_PIECES:]
    first = jnp.concatenate([jnp.zeros((PAD, D_MODEL), jnp.float32), meta_ref[...].astype(jnp.float32)], axis=0)
    pieces = [jnp.where(pl.program_id(1) == 0, first, x_refs[0][...].astype(jnp.float32))]
    pieces += [x_ref[...].astype(jnp.float32) for x_ref in x_refs[1:]]
    h = jnp.concatenate(pieces, axis=0)
    h_ref[...] = h
    r = lax.rsqrt(jnp.mean(h * h, axis=-1, keepdims=True) + EPS)
    n_ref[...] = (h * r * g_ref[...]).astype(n_ref.dtype)


def _embed(x, meta_tokens, g):
    bsz, seq, d = x.shape
    assert seq == SEQ and LP % (EMBED_PIECES * HG_CHUNK) == 0
    tr = EMBED_PIECES * HG_CHUNK
    piece = lambda k: pl.BlockSpec((None, HG_CHUNK, d), lambda b, r: (b, jnp.maximum(EMBED_PIECES * r + k - 1, 0), 0))
    row = pl.BlockSpec((None, tr, d), lambda b, r: (b, r, 0))
    return pl.pallas_call(
        _embed_kernel,
        grid=(bsz, LP // tr),
        in_specs=[piece(k) for k in range(EMBED_PIECES)] + [pl.BlockSpec((N_META, d), lambda b, r: (0, 0)),
                                                           pl.BlockSpec((1, d), lambda b, r: (0, 0))],
        out_specs=[row, row],
        out_shape=[jax.ShapeDtypeStruct((bsz, LP, d), jnp.float32), jax.ShapeDtypeStruct((bsz, LP, d), jnp.bfloat16)],
        compiler_params=pltpu.CompilerParams(
            dimension_semantics=("parallel", "parallel"), vmem_limit_bytes=V7X_VMEM_LIMIT),
        name="embed_rmsnorm_pre",
    )(*([x] * EMBED_PIECES), meta_tokens, g.reshape(1, d))


def _post_kernel(h_ref, y_ref, g_ref, o_ref):
    y = y_ref[...].astype(jnp.float32)
    r = lax.rsqrt(jnp.mean(y * y, axis=-1, keepdims=True) + EPS)
    o_ref[...] = (h_ref[...] + y * r * g_ref[...]).astype(o_ref.dtype)


def _post_pre_kernel(h_ref, y_ref, g_ref, gn_ref, o_ref, n_ref):
    y = y_ref[...].astype(jnp.float32)
    r = lax.rsqrt(jnp.mean(y * y, axis=-1, keepdims=True) + EPS)
    h = h_ref[...] + y * r * g_ref[...]
    o_ref[...] = h
    rn = lax.rsqrt(jnp.mean(h * h, axis=-1, keepdims=True) + EPS)
    n_ref[...] = (h * rn * gn_ref[...]).astype(n_ref.dtype)


def _final_residual(h3, y3, g, out_dtype):
    bsz, lp, d = h3.shape
    skip = (PAD + N_META) // HG_CHUNK
    src = pl.BlockSpec((None, HG_CHUNK, d), lambda b, r: (b, r + skip, 0))
    return pl.pallas_call(
        _post_kernel,
        grid=(bsz, SEQ // HG_CHUNK),
        in_specs=[src, src, pl.BlockSpec((1, d), lambda b, r: (0, 0))],
        out_specs=pl.BlockSpec((None, HG_CHUNK, d), lambda b, r: (b, r, 0)),
        out_shape=jax.ShapeDtypeStruct((bsz, SEQ, d), out_dtype),
        compiler_params=pltpu.CompilerParams(
            dimension_semantics=("parallel", "parallel"), vmem_limit_bytes=V7X_VMEM_LIMIT),
        name="residual_rmsnorm_post_final",
    )(h3, y3, g.reshape(1, d))


def _residual_rmsnorm(h2d, y2d, g, g_next, tr=320):
    m, d = h2d.shape
    row = pl.BlockSpec((tr, d), lambda i: (i, 0))
    vec = pl.BlockSpec((1, d), lambda i: (0, 0))
    params = pltpu.CompilerParams(dimension_semantics=("parallel",), vmem_limit_bytes=V7X_VMEM_LIMIT)
    return pl.pallas_call(
        _post_pre_kernel, grid=(m // tr,), in_specs=[row, row, vec, vec], out_specs=[row, row],
        out_shape=[jax.ShapeDtypeStruct((m, d), jnp.float32), jax.ShapeDtypeStruct((m, d), jnp.bfloat16)],
        compiler_params=params, name="residual_rmsnorm_post_pre",
    )(h2d, y2d, g.reshape(1, d), g_next.reshape(1, d))


def _matmul_kernel(x_ref, w_ref, o_ref):
    o_ref[...] = jnp.dot(x_ref[...], w_ref[...].astype(jnp.bfloat16),
                         preferred_element_type=jnp.float32).astype(o_ref.dtype)


def _matmul(x, w_stack, layer, out_dtype, tm=1040, tn=512):
    m, kd = x.shape
    _, _, n = w_stack.shape
    assert m % tm == 0 and n % tn == 0
    return pl.pallas_call(
        _matmul_kernel,
        grid=(m // tm, n // tn),
        in_specs=[pl.BlockSpec((tm, kd), lambda i, j: (i, 0)),
                  pl.BlockSpec((None, kd, tn), lambda i, j: (layer, 0, j))],
        out_specs=pl.BlockSpec((tm, tn), lambda i, j: (i, j)),
        out_shape=jax.ShapeDtypeStruct((m, n), out_dtype),
        compiler_params=pltpu.CompilerParams(
            dimension_semantics=("parallel", "parallel"), vmem_limit_bytes=V7X_VMEM_LIMIT),
        name="projection_matmul",
    )(x, w_stack)


FFT_N1 = 128
FFT_N2 = 65
FFT_N = FFT_N1 * FFT_N2
N2P = 80
SLAB = 264
TC = 128
MID_GROUP = 16
NC_HALF = FFT_N1 // 2 + 1
HALF_P = 72
SLAB_S = 152
C_GROUP = 13
FILL_CHUNKS = 13
B_GROUP = 13


def _dft_constants():
    a = np.arange(FFT_N1)
    c = np.arange(FFT_N1)
    w1 = np.exp(-2j * np.pi * np.outer(c, a) / FFT_N1)
    w1r, w1i = w1.real, w1.imag
    half = FFT_N1 // 2
    f1 = np.block([[w1r[:, :half], -w1i[:, :half]], [w1i[:, :half], w1r[:, :half]]])
    f1_real = np.zeros((2 * HALF_P, FFT_N1), np.float64)
    f1_real[:NC_HALF] = w1r[:NC_HALF]
    f1_real[HALF_P:HALF_P + NC_HALF] = w1i[:NC_HALF]
    g = np.conj(w1).T
    gr, gi = g.real[:half], g.imag[:half]
    i2 = np.block([[gr, -gi], [gi, gr]])
    b = np.arange(FFT_N2)
    d = np.arange(FFT_N2)
    w2 = np.exp(-2j * np.pi * np.outer(d, b) / FFT_N2)
    tw = np.exp(-2j * np.pi * np.outer(c, b) / FFT_N)
    m = w2[None, :, :] * tw[:, None, :]
    ab = np.zeros((FFT_N1, 2 * N2P, 128), np.float64)
    m[NC_HALF:] = m[NC_HALF:, ::-1, :]
    ab[:, :FFT_N2, :FFT_N2] = m.real
    ab[:, N2P:N2P + FFT_N2, :FFT_N2] = m.imag
    bf = jnp.bfloat16
    abt = np.zeros((FFT_N1, N2P, 256), np.float64)
    abt[:, :, :2 * N2P] = np.swapaxes(ab[:, :, :N2P], 1, 2)
    return (jnp.asarray(f1, bf), jnp.asarray(f1_real, bf), jnp.asarray(i2, bf), jnp.asarray(ab, bf),
            jnp.asarray(abt, bf))


def _middle_stage_inputs(y_ref, c, imag_row=FFT_N1, pitch=SLAB):
    yr = y_ref[pl.ds(c, N2P, stride=pitch), :]
    yi = y_ref[pl.ds(imag_row + c, N2P, stride=pitch), :]
    return jnp.concatenate([yr, yi], axis=1).astype(jnp.bfloat16)


def _forward_middle(ab, y2):
    y2 = jnp.concatenate([y2, jnp.zeros((128 - N2P, 2 * TC), jnp.bfloat16)], axis=0)
    q = jnp.dot(ab, y2, preferred_element_type=jnp.float32)
    zr = q[:N2P, :TC] - q[N2P:, TC:]
    zi = q[N2P:, :TC] + q[:N2P, TC:]
    return zr, zi


def _spectrum_kernel(h_ref, t_ref, w3f0_ref, w3f1_ref, w3b0_ref, w3b1_ref, delta_ref, f1_ref, ab_ref, k_ref,
                     kt_ref, y_ref):
    w3 = jnp.concatenate([jnp.concatenate([w3f0_ref[...], w3f1_ref[...]], axis=1),
                          jnp.concatenate([w3b0_ref[...], w3b1_ref[...]], axis=1)], axis=0).astype(jnp.bfloat16)
    delta = delta_ref[...]
    rows = FFT_N // FILL_CHUNKS

    def fill(i, carry):
        r0 = pl.multiple_of(i * rows, 8)
        kt = jnp.dot(h_ref[pl.ds(r0, rows), :], w3, preferred_element_type=jnp.float32)
        window = jnp.exp(t_ref[pl.ds(r0, rows), :] * delta)
        for n in range(HY_ORDER):
            kt_ref[n, pl.ds(r0, rows), :] = kt[:, n * TC:(n + 1) * TC] * window
        return carry

    lax.fori_loop(0, FILL_CHUNKS, fill, 0)
    y_ref[pl.ds(FFT_N2 * SLAB_S, (N2P - FFT_N2) * SLAB_S), :] = jnp.zeros(((N2P - FFT_N2) * SLAB_S, TC), jnp.float32)

    for n in range(HY_ORDER):
        def stage1(g, carry):
            bs = [g * B_GROUP + i for i in range(B_GROUP)]
            x = jnp.concatenate([kt_ref.at[n][pl.ds(b, FFT_N1, stride=FFT_N2), :] for b in bs], axis=1)
            y = jnp.dot(f1_ref[...], x.astype(jnp.bfloat16), preferred_element_type=jnp.float32)
            for i, b in enumerate(bs):
                y_ref[pl.ds(pl.multiple_of(b * SLAB_S, 8), 2 * HALF_P), :] = y[:, i * TC:(i + 1) * TC]
            return carry

        lax.fori_loop(0, FFT_N2 // B_GROUP, stage1, 0)

        def stage2(g, carry):
            cs = [g * C_GROUP + i for i in range(C_GROUP)]
            zs = [_forward_middle(ab_ref[c], _middle_stage_inputs(y_ref, c, HALF_P, SLAB_S)) for c in cs]
            for c, (zr, zi) in zip(cs, zs):
                k_ref[n, c] = (jnp.concatenate([zr, zi], axis=1) * (1.0 / FFT_N)).astype(jnp.bfloat16)
            return carry

        lax.fori_loop(0, NC_HALF // C_GROUP, stage2, 0)


def _filter_spectrum(h2, w3, deltas, consts):
    _, f1_real, _, ab, _ = consts
    nt = HY_WIDTH // TC
    j = np.arange(FFT_N)
    lag = np.where(j < L_TOK, j, FFT_N - j).astype(np.float32)
    neg_t = jnp.asarray(np.broadcast_to((-lag / np.float32(L_TOK - 1))[:, None], (FFT_N, TC)))
    w3_cols = lambda direction, order: pl.BlockSpec((64, TC), lambda i: (0, (direction * HY_ORDER + order) * nt + i))
    return pl.pallas_call(
        _spectrum_kernel,
        grid=(nt,),
        in_specs=[
            pl.BlockSpec((FFT_N, 128), lambda i: (0, 0)),
            pl.BlockSpec((FFT_N, TC), lambda i: (0, 0)),
            w3_cols(0, 0), w3_cols(0, 1), w3_cols(1, 0), w3_cols(1, 1),
            pl.BlockSpec((1, TC), lambda i: (0, i)),
            pl.BlockSpec((2 * HALF_P, FFT_N1), lambda i: (0, 0)),
            pl.BlockSpec((FFT_N1, 2 * N2P, 128), lambda i: (0, 0, 0)),
        ],
        out_specs=pl.BlockSpec((HY_ORDER, None, NC_HALF, N2P, 2 * TC), lambda i: (0, i, 0, 0, 0)),
        out_shape=jax.ShapeDtypeStruct((HY_ORDER, nt, NC_HALF, N2P, 2 * TC), jnp.bfloat16),
        scratch_shapes=[pltpu.VMEM((HY_ORDER, FFT_N, TC), jnp.float32), pltpu.VMEM((N2P * SLAB_S, TC), jnp.float32)],
        compiler_params=pltpu.CompilerParams(dimension_semantics=("parallel",), vmem_limit_bytes=V7X_VMEM_LIMIT),
        name="hyena_filter_spectrum",
    )(h2, neg_t, w3, w3, w3, w3, deltas, f1_real, ab)


def _conv_kernel(u_ref, k_ref, f1_ref, i2_ref, ab_ref, abt_ref, o_ref, y_ref):
    y_ref[pl.ds(FFT_N2 * SLAB, (N2P - FFT_N2) * SLAB), :] = jnp.zeros(((N2P - FFT_N2) * SLAB, TC), jnp.float32)
    half = FFT_N1 // 2

    def stage1(g, carry):
        bs = [g * B_GROUP + i for i in range(B_GROUP)]
        x = jnp.concatenate(
            [jnp.concatenate([u_ref.at[ri][pl.ds(b, half, stride=FFT_N2), :] for b in bs], axis=1) for ri in range(2)],
            axis=0)
        y = jnp.dot(f1_ref[...], x.astype(jnp.bfloat16), preferred_element_type=jnp.float32)
        for i, b in enumerate(bs):
            y_ref[pl.ds(pl.multiple_of(b * SLAB, 8), 2 * FFT_N1), :] = y[:, i * TC:(i + 1) * TC]
        return carry

    lax.fori_loop(0, FFT_N2 // B_GROUP, stage1, 0)

    def inverse_middle(c, z):
        zr, zi = z
        mirrored = c >= NC_HALF
        kk = k_ref[jnp.where(mirrored, FFT_N1 - c, c)].astype(jnp.float32)
        kr, ki = kk[:, :TC], kk[:, TC:] * jnp.where(mirrored, -1.0, 1.0)
        pr = zr * kr - zi * ki
        pi = zr * ki + zi * kr
        rhs = jnp.concatenate([jnp.concatenate([pr, pi], axis=1), jnp.concatenate([pi, -pr], axis=1),
                               jnp.zeros((256 - 2 * N2P, 2 * TC), jnp.float32)], axis=0)
        return jnp.dot(abt_ref[c], rhs.astype(jnp.bfloat16), preferred_element_type=jnp.float32)

    def middle(g, carry):
        cs = [g * MID_GROUP + i for i in range(MID_GROUP)]
        zs = [_forward_middle(ab_ref[c], _middle_stage_inputs(y_ref, c)) for c in cs]
        outs = [inverse_middle(c, z) for c, z in zip(cs, zs)]
        for c, v in zip(cs, outs):
            y_ref[pl.ds(c, N2P, stride=SLAB), :] = v[:N2P, :TC]
            y_ref[pl.ds(FFT_N1 + c, N2P, stride=SLAB), :] = v[:N2P, TC:]
        return carry

    lax.fori_loop(0, FFT_N1 // MID_GROUP, middle, 0)

    def stage4(g, carry):
        bs = [g * B_GROUP + i for i in range(B_GROUP)]
        v = jnp.concatenate([y_ref[pl.ds(pl.multiple_of(b * SLAB, 8), 2 * FFT_N1), :] for b in bs], axis=1)
        x = jnp.dot(i2_ref[...], v.astype(jnp.bfloat16), preferred_element_type=jnp.float32)
        for i, b in enumerate(bs):
            o_ref.at[0][pl.ds(b, half, stride=FFT_N2), :] = x[:half, i * TC:(i + 1) * TC]
            o_ref.at[1][pl.ds(b, half, stride=FFT_N2), :] = x[half:, i * TC:(i + 1) * TC]
        return carry

    lax.fori_loop(0, FFT_N2 // B_GROUP, stage4, 0)


def _long_conv(u, kspec, order, consts):
    f1, _, i2, ab, abt = consts
    bsz, lp, _ = u.shape
    assert bsz == 2 and lp == LP
    nt = HY_WIDTH // TC
    return pl.pallas_call(
        _conv_kernel,
        grid=(nt,),
        in_specs=[
            pl.BlockSpec((bsz, lp, TC), lambda i: (0, 0, i)),
            pl.BlockSpec((None, None, NC_HALF, N2P, 2 * TC), lambda i: (order, i, 0, 0, 0)),
            pl.BlockSpec((2 * FFT_N1, FFT_N1), lambda i: (0, 0)),
            pl.BlockSpec((FFT_N1, 2 * FFT_N1), lambda i: (0, 0)),
            pl.BlockSpec((FFT_N1, 2 * N2P, 128), lambda i: (0, 0, 0), pipeline_mode=pl.Buffered(1)),
            pl.BlockSpec((FFT_N1, N2P, 256), lambda i: (0, 0, 0), pipeline_mode=pl.Buffered(1)),
        ],
        out_specs=pl.BlockSpec((bsz, lp, TC), lambda i: (0, 0, i)),
        out_shape=jax.ShapeDtypeStruct((bsz, lp, HY_WIDTH), jnp.float32),
        scratch_shapes=[pltpu.VMEM((N2P * SLAB, TC), jnp.float32)],
        compiler_params=pltpu.CompilerParams(dimension_semantics=("parallel",), vmem_limit_bytes=V7X_VMEM_LIMIT),
        name="hyena_long_conv",
    )(u, kspec, f1, i2, ab, abt)


def _short_conv_rows(u, w):
    y = pltpu.roll(u, 1, 0) * w[0:1] + u * w[1:2] + pltpu.roll(u, LP - 1, 0) * w[2:3]
    row = lax.broadcasted_iota(jnp.int32, y.shape, 0)
    return jnp.where(row >= PAD, y, 0.0)


def _short_conv_kernel(u_ref, w_ref, o_ref):
    o_ref[...] = _short_conv_rows(u_ref[...].astype(jnp.float32), w_ref[...])


def _gate1_kernel(v_ref, u_ref, w_ref, y_ref, b_ref, o_ref):
    x = _short_conv_rows(u_ref[...].astype(jnp.float32), w_ref[...])
    o_ref[...] = x * (y_ref[...] + v_ref[...] * b_ref[...])


def _gate2_kernel(z_ref, u_ref, w_ref, y_ref, b_ref, g_ref, o_ref):
    g = g_ref[...].astype(jnp.float32)
    z = _short_conv_rows(u_ref[...].astype(jnp.float32), w_ref[...]) * (y_ref[...] + z_ref[...] * b_ref[...])
    o_ref[...] = (z * (g * jax.nn.sigmoid(g))).astype(o_ref.dtype)


def _seq_tile(col_block):
    return pl.BlockSpec((None, LP, TC), lambda b, i: (b, 0, col_block + i))


def _vec_tile(rows, col_block):
    return pl.BlockSpec((rows, TC), lambda b, i: (0, col_block + i))


def _short_conv(p3, w_short_t):
    bsz = p3.shape[0]
    return pl.pallas_call(
        _short_conv_kernel,
        grid=(bsz, HY_WIDTH // TC),
        in_specs=[_seq_tile(0), _vec_tile(3, 0)],
        out_specs=_seq_tile(0),
        out_shape=jax.ShapeDtypeStruct((bsz, LP, HY_WIDTH), jnp.float32),
        compiler_params=pltpu.CompilerParams(
            dimension_semantics=("parallel", "parallel"), vmem_limit_bytes=V7X_VMEM_LIMIT),
        name="hyena_short_conv",
    )(p3, w_short_t)


def _hyena_gate1(v, p3, w_short_t, y, bias):
    bsz = p3.shape[0]
    nt = HY_WIDTH // TC
    return pl.pallas_call(
        _gate1_kernel,
        grid=(bsz, nt),
        in_specs=[_seq_tile(0), _seq_tile(nt), _vec_tile(3, nt), _seq_tile(0), _vec_tile(1, 0)],
        out_specs=_seq_tile(0),
        out_shape=jax.ShapeDtypeStruct((bsz, LP, HY_WIDTH), jnp.float32),
        compiler_params=pltpu.CompilerParams(
            dimension_semantics=("parallel", "parallel"), vmem_limit_bytes=V7X_VMEM_LIMIT),
        name="hyena_gate1",
    )(v, p3, w_short_t, y, bias.reshape(1, HY_WIDTH))


def _hyena_gate2(z1, p3, w_short_t, y, bias):
    bsz = p3.shape[0]
    nt = HY_WIDTH // TC
    return pl.pallas_call(
        _gate2_kernel,
        grid=(bsz, nt),
        in_specs=[_seq_tile(0), _seq_tile(2 * nt), _vec_tile(3, 2 * nt), _seq_tile(0), _vec_tile(1, 0),
                  _seq_tile(3 * nt)],
        out_specs=_seq_tile(0),
        out_shape=jax.ShapeDtypeStruct((bsz, LP, HY_WIDTH), jnp.bfloat16),
        compiler_params=pltpu.CompilerParams(
            dimension_semantics=("parallel", "parallel"), vmem_limit_bytes=V7X_VMEM_LIMIT),
        name="hyena_gate2",
    )(z1, p3, w_short_t, y, bias.reshape(1, HY_WIDTH), p3)


def _filter_positions():
    f32 = np.float32
    l = L_TOK
    t = np.linspace(0.0, 1.0, l, dtype=f32)[:, None]
    w = f32(2.0 * math.pi) * np.arange(l, dtype=f32)[:, None] / f32(l)
    bands = np.linspace(1e-4, HY_BANDS - 1, HY_BANDS, dtype=f32)[None, :]
    z = np.concatenate([t, np.cos(bands * w), -np.sin(bands * w)], axis=-1).astype(f32)
    j = np.arange(FFT_N)
    lag = np.minimum(np.where(j < l, j, FFT_N - j), l - 1)
    zrows = np.zeros((FFT_N, 128), f32)
    zrows[:, :HY_EMB] = z[lag]
    mask = np.zeros((FFT_N, 128), f32)
    mask[:, :64] = (j < l)[:, None]
    mask[:, 64:] = ((j > FFT_N - l) | (j == 0))[:, None]
    return jnp.asarray(zrows), jnp.asarray(mask)


def _hidden_kernel(z_ref, m_ref, w1_ref, b1_ref, fr1_ref, w2_ref, b2_ref, fr2_ref, o_ref):
    hi = lax.Precision.HIGHEST
    a = jnp.dot(z_ref[...], w1_ref[...], precision=hi, preferred_element_type=jnp.float32) + b1_ref[...]
    hid = jnp.sin(fr1_ref[...] * a)
    a = jnp.dot(hid, w2_ref[...], precision=hi, preferred_element_type=jnp.float32) + b2_ref[...]
    hid = jnp.sin(fr2_ref[...] * a)
    o_ref[...] = (jnp.concatenate([hid, hid], axis=1) * m_ref[...]).astype(o_ref.dtype)


def _hyena_hidden(w1, b1, fr1, w2, b2, fr2):
    zrows, mask = _filter_positions()
    rows = FFT_N // FILL_CHUNKS
    hdim = w2.shape[0]
    w1p = jnp.zeros((128, hdim), jnp.float32).at[:HY_EMB].set(w1.astype(jnp.float32))
    tile = pl.BlockSpec((rows, 128), lambda i: (i, 0))
    full = lambda a: pl.BlockSpec(a.shape, lambda i: (0,) * a.ndim)
    vecs = [v.astype(jnp.float32).reshape(1, hdim) for v in (b1, fr1, b2, fr2)]
    w2f = w2.astype(jnp.float32)
    return pl.pallas_call(
        _hidden_kernel,
        grid=(FILL_CHUNKS,),
        in_specs=[tile, tile, full(w1p), full(vecs[0]), full(vecs[1]), full(w2f), full(vecs[2]), full(vecs[3])],
        out_specs=tile,
        out_shape=jax.ShapeDtypeStruct((FFT_N, 128), jnp.bfloat16),
        compiler_params=pltpu.CompilerParams(dimension_semantics=("parallel",), vmem_limit_bytes=V7X_VMEM_LIMIT),
        name="hyena_filter_hidden",
    )(zrows, mask, w1p, vecs[0], vecs[1], w2f, vecs[2], vecs[3])


def _odd_mix(p3, w_short, w1, b1, fr1, w2, b2, fr2, w3, bias, consts):
    min_decay = math.log(HY_TARGET) / HY_SLOW_DECAY
    max_decay = math.log(HY_TARGET) / HY_FAST_DECAY
    deltas = jnp.abs(jnp.linspace(min_decay, max_decay, HY_WIDTH, dtype=jnp.float32)).reshape(1, HY_WIDTH)
    kspec = _filter_spectrum(_hyena_hidden(w1, b1, fr1, w2, b2, fr2), w3, deltas, consts)
    w_short_t = w_short.T
    v = _short_conv(p3, w_short_t)
    y1 = _long_conv(v, kspec, 0, consts)
    z1 = _hyena_gate1(v, p3, w_short_t, y1, bias[0])
    y2 = _long_conv(z1, kspec, 1, consts)
    return _hyena_gate2(z1, p3, w_short_t, y2, bias[1])


HG_HB = 4
HG_LEVELS = (32, 16, 8, 4, 2, 1)
LOG2_E = 1.4426950408889634


def _cumsum_rows(g, rev):
    f32, bf = jnp.float32, jnp.bfloat16
    t = lax.broadcasted_iota(jnp.int32, (HG_CHUNK, 3 * HG_CHUNK), 0)
    s = lax.broadcasted_iota(jnp.int32, (HG_CHUNK, 3 * HG_CHUNK), 1) & (HG_CHUNK - 1)
    tri = jnp.where((s >= t) if rev else (s <= t), 1.0, 0.0).astype(bf)
    hi = g.astype(bf)
    r1 = g - hi.astype(f32)
    mid = r1.astype(bf)
    lo = (r1 - mid.astype(f32)).astype(bf)
    return jnp.dot(tri, jnp.concatenate([hi, mid, lo], axis=0), preferred_element_type=f32)


def _hgrn_row_masks(rev):
    row = lax.broadcasted_iota(jnp.int32, (HG_CHUNK, HEAD_DIM), 0)
    masks = {}
    for m in HG_LEVELS:
        upper = (row & m) != 0
        is_query = jnp.logical_not(upper) if rev else upper
        target = m if rev else m - 1
        selectors = [(src, (row & (2 * m - 1)) == src) for src in range(2 * m) if src != target] if m == 2 else None
        masks[m] = (is_query, jnp.where(is_query, LOG2_E, -LOG2_E), selectors)
    return masks


def _level_reference(gc, m, rev, selectors):
    if m >= 4:
        g3 = gc.reshape(HG_CHUNK // (2 * m), 2 * m, HEAD_DIM)
        ref = g3[:, m:m + 1, :] if rev else g3[:, m - 1:m, :]
        return jnp.broadcast_to(ref, g3.shape).reshape(HG_CHUNK, HEAD_DIM)
    target = m if rev else m - 1
    out = gc
    for src, picks_src in selectors:
        out = jnp.where(picks_src, pltpu.roll(gc, (src - target) % HG_CHUNK, 0), out)
    return out


def _hgrn_gates(q_raw, f_raw, la, lc, omlb, rev):
    q = q_raw * (0.5 * HEAD_DIM ** -0.5) * (1.0 + jnp.tanh(0.5 * q_raw))
    k = (0.5 * omlb) * (1.0 - jnp.tanh(0.5 * f_raw))
    log_sig = jnp.minimum(f_raw, 0.0) - jnp.log(1.0 + jnp.exp(-jnp.abs(f_raw)))
    b = lc + log_sig
    g = jnp.maximum(la, b) + jnp.log(1.0 + jnp.exp(-jnp.abs(la - b)))
    return q, k, g, _cumsum_rows(g, rev)


def _hgrn_level_operands(q, k, g, gc, rev, masks):
    bf = jnp.bfloat16
    qb, kb = q.astype(bf), k.astype(bf)
    ops = []
    for m in HG_LEVELS:
        is_query, sign_log2e, selectors = masks[m]
        if m == 1:
            ops.append(jnp.where(is_query, qb * jnp.exp2(g * LOG2_E).astype(bf), kb))
        else:
            gref = _level_reference(gc, m, rev, selectors)
            ops.append(jnp.where(is_query, qb, kb) * jnp.exp2((gc - gref) * sign_log2e).astype(bf))
    return ops


def _hgrn_level_id(rev):
    t = lax.broadcasted_iota(jnp.int32, (HG_CHUNK, HG_CHUNK), 0)
    s = lax.broadcasted_iota(jnp.int32, (HG_CHUNK, HG_CHUNK), 1)
    vis = (s >= t) if rev else (s <= t)
    x = t ^ s
    lvl = jnp.full((HG_CHUNK, HG_CHUNK), len(HG_LEVELS), jnp.int32)
    for i, m in enumerate(HG_LEVELS):
        lvl = jnp.where((x & m) != 0, jnp.minimum(lvl, i), lvl)
    return jnp.where(vis, lvl, -1)


def _hgrn_kernel(qf_ref, if_ref, ff_ref, qb_ref, ib_ref, fb_ref, la_ref, lc_ref, om_ref, of_ref, ob_ref, s_ref):
    f32, bf = jnp.float32, jnp.bfloat16
    nt = (((1,), (1,)), ((), ()))
    tn = (((0,), (0,)), ((), ()))

    @pl.when(pl.program_id(2) == 0)
    def _():
        s_ref[...] = jnp.zeros_like(s_ref)

    row_masks = [_hgrn_row_masks(False), _hgrn_row_masks(True)]
    refs = ((qf_ref, if_ref, ff_ref, of_ref), (qb_ref, ib_ref, fb_ref, ob_ref))
    inst = [(d, h) for d in range(2) for h in range(HG_HB)]
    level_masks = [[lvl == i for i in range(len(HG_LEVELS) + 1)]
                   for lvl in (_hgrn_level_id(False), _hgrn_level_id(True))]

    pre = []
    for d, h in inst:
        q_ref, i_ref, f_ref, _ = refs[d]
        sl = slice(h * HEAD_DIM, (h + 1) * HEAD_DIM)
        q, k, g, gc = _hgrn_gates(q_ref[:, sl], f_ref[:, sl], la_ref[d:d + 1, sl], lc_ref[d:d + 1, sl],
                                  om_ref[d:d + 1, sl], d == 1)
        g_end = gc[0:1, :] if d == 1 else gc[HG_CHUNK - 1:HG_CHUNK, :]
        pre.append(dict(ops=_hgrn_level_operands(q, k, g, gc, d == 1, row_masks[d]), qk=(q.astype(bf), k.astype(bf)),
                        qe=(q * jnp.exp(gc)).astype(bf), kd=(k * jnp.exp(g_end - gc)).astype(bf),
                        decay=jnp.exp(g_end), v=i_ref[:, sl].astype(bf)))

    attn = []
    for (d, h), pr in zip(inst, pre):
        prods = [lax.dot_general(x, x, nt, preferred_element_type=f32) for x in pr["ops"]]
        prods.append(lax.dot_general(pr["qk"][0], pr["qk"][1], nt, preferred_element_type=f32))
        a = jnp.zeros((HG_CHUNK, HG_CHUNK), f32)
        for i, pmat in enumerate(prods):
            a = jnp.where(level_masks[d][i], pmat, a)
        attn.append(a.astype(bf))

    for (d, h), pr, a in zip(inst, pre, attn):
        o_ref = refs[d][3]
        sl = slice(h * HEAD_DIM, (h + 1) * HEAD_DIM)
        state_t = s_ref[d, h]
        o = jnp.dot(a, pr["v"], preferred_element_type=f32)
        o = o + lax.dot_general(pr["qe"], state_t.astype(bf), nt, preferred_element_type=f32)
        o_ref[:, sl] = o.astype(o_ref.dtype)
        s_ref[d, h] = pr["decay"] * state_t + lax.dot_general(pr["v"], pr["kd"], tn, preferred_element_type=f32)


def _hgrn2(p, col0, lb):
    bsz, lp, _ = p.shape
    nchunk = lp // HG_CHUNK
    cw = HG_HB * HEAD_DIM
    ng = HG_WIDTH // cw
    off = col0 // cw
    la = jnp.log(lb)
    lc = jnp.log1p(-lb)
    om = 1.0 - lb
    fwd = lambda seg: pl.BlockSpec((None, HG_CHUNK, cw), lambda b, g, c: (b, c, off + seg * ng + g))
    bwd = lambda seg: pl.BlockSpec((None, HG_CHUNK, cw), lambda b, g, c: (b, nchunk - 1 - c, off + seg * ng + g))
    par = pl.BlockSpec((2, cw), lambda b, g, c: (0, g))
    return pl.pallas_call(
        _hgrn_kernel,
        grid=(bsz, ng, nchunk),
        in_specs=[fwd(0), fwd(1), fwd(2), bwd(0), bwd(1), bwd(3), par, par, par],
        out_specs=[pl.BlockSpec((None, HG_CHUNK, cw), lambda b, g, c: (b, c, g)),
                   pl.BlockSpec((None, HG_CHUNK, cw), lambda b, g, c: (b, nchunk - 1 - c, g))],
        out_shape=[jax.ShapeDtypeStruct((bsz, lp, HG_WIDTH), jnp.bfloat16)] * 2,
        scratch_shapes=[pltpu.VMEM((2, HG_HB, HEAD_DIM, HEAD_DIM), jnp.float32)],
        compiler_params=pltpu.CompilerParams(
            dimension_semantics=("parallel", "parallel", "arbitrary"), vmem_limit_bytes=V7X_VMEM_LIMIT),
        name="hgrn2_bidirectional",
    )(p, p, p, p, p, p, la, lc, om)


EG_ROWS = 320


def _even_gate_kernel(oa_ref, of_ref, ob_ref, ga_ref, gb_ref, gain_ref, y_ref):
    ga = ga_ref[...]
    y_ref[:, :NA_WIDTH] = (oa_ref[...].astype(jnp.float32) * (ga * jax.nn.sigmoid(ga))).astype(y_ref.dtype)
    for h in range(HG_HEADS):
        sl = slice(h * HEAD_DIM, (h + 1) * HEAD_DIM)
        oh = of_ref[:, sl].astype(jnp.float32) + ob_ref[:, sl].astype(jnp.float32)
        oh = oh * lax.rsqrt(jnp.mean(oh * oh, axis=-1, keepdims=True) + EPS) * gain_ref[:, sl]
        gb = gb_ref[:, sl]
        y_ref[:, NA_WIDTH + h * HEAD_DIM:NA_WIDTH + (h + 1) * HEAD_DIM] = (
            oh * (gb * jax.nn.sigmoid(gb))).astype(y_ref.dtype)


def _even_gate(oa, o_fwd, o_bwd, p, gate_col0, norm_gain):
    bsz, lp, _ = oa.shape
    assert gate_col0 % NA_WIDTH == 0
    goff = gate_col0 // NA_WIDTH
    half = lambda off: pl.BlockSpec((None, EG_ROWS, NA_WIDTH), lambda b, r: (b, r, off))
    return pl.pallas_call(
        _even_gate_kernel,
        grid=(bsz, lp // EG_ROWS),
        in_specs=[half(0), half(0), half(0), half(goff), half(goff + 1),
                  pl.BlockSpec((1, HG_WIDTH), lambda b, r: (0, 0))],
        out_specs=pl.BlockSpec((None, EG_ROWS, D_MODEL), lambda b, r: (b, r, 0)),
        out_shape=jax.ShapeDtypeStruct((bsz, lp, D_MODEL), jnp.bfloat16),
        compiler_params=pltpu.CompilerParams(
            dimension_semantics=("parallel", "parallel"), vmem_limit_bytes=V7X_VMEM_LIMIT),
        name="even_gate",
    )(oa, o_fwd, o_bwd, p, p, norm_gain.reshape(1, HG_WIDTH))


NA_MASKED = -1e30
NA_GROUP = 8


def _na_bias_table(rpb):
    cols = np.arange(GRID_W)
    c0 = np.clip(cols - NA_KW // 2, 0, GRID_W - NA_KW)
    col_ok = (cols[None, :] >= c0[:, None]) & (cols[None, :] < c0[:, None] + NA_KW)
    col_idx = np.clip(cols[None, :] - cols[:, None] + NA_KW - 1, 0, 2 * NA_KW - 2)
    onehot = (col_idx[None] == np.arange(2 * NA_KW - 1)[:, None, None]).astype(np.float32)
    picked = jnp.einsum('hrj,jqk->hrqk', rpb.astype(jnp.float32), onehot, precision=lax.Precision.HIGHEST)
    table = jnp.where(col_ok[None, None], picked, NA_MASKED)
    return jnp.concatenate([table[:, :-1], table[:, 1:]], axis=-1)


def _na_kernel(q_ref, k_ref, v_ref, t_ref, mb_ref, o_ref):
    f32, bf = jnp.float32, jnp.bfloat16
    rows = (q_ref.shape[0] - HG_CHUNK) // GRID_W
    scale = HEAD_DIM ** -0.5
    nt = (((1,), (1,)), ((), ()))
    km = k_ref[PAD:PAD + N_META, :].astype(bf)
    vm = v_ref[PAD:PAD + N_META, :].astype(bf)
    mb = mb_ref[...]
    o_ref[0:PAD, :] = jnp.zeros((PAD, HEAD_DIM), o_ref.dtype)

    qm = q_ref[PAD:PAD + N_META, :].astype(bf)
    s = lax.dot_general(qm, km, nt, preferred_element_type=f32) * scale + mb
    e = jnp.exp(s - jnp.max(s, axis=-1, keepdims=True))
    o_ref[PAD:PAD + N_META, :] = (jnp.dot(e.astype(bf), vm, preferred_element_type=f32)
                                  / jnp.sum(e, axis=-1, keepdims=True)).astype(o_ref.dtype)

    def row_group(g, carry):
        rs = [g * NA_GROUP + i for i in range(NA_GROUP)]
        scores = []
        for r in rs:
            r0 = jnp.clip(r - NA_KH // 2, 0, rows - NA_KH)
            q = q_ref[pl.ds(pl.multiple_of(HG_CHUNK + GRID_W * r, GRID_W), GRID_W), :].astype(bf)
            k0 = pl.multiple_of(HG_CHUNK + GRID_W * r0, GRID_W)
            kw = k_ref[pl.ds(k0, NA_KH * GRID_W), :].astype(bf)
            s_win = lax.dot_general(q, kw, nt, preferred_element_type=f32)
            s_meta = lax.dot_general(q, km, nt, preferred_element_type=f32)
            scores.append((r, r0, k0, s_win, s_meta))
        probs = []
        for r, r0, k0, s_win, s_meta in scores:
            bias = jnp.concatenate([t_ref[r0 + i - r + NA_KH - 1] for i in range(0, NA_KH, 2)], axis=1)
            s_win = s_win * scale + bias
            s_meta = s_meta * scale + mb
            m = jnp.maximum(jnp.max(s_win, axis=-1, keepdims=True), jnp.max(s_meta, axis=-1, keepdims=True))
            pw = jnp.exp(s_win - m)
            pm = jnp.exp(s_meta - m)
            den = jnp.sum(pw, axis=-1, keepdims=True) + jnp.sum(pm, axis=-1, keepdims=True)
            probs.append((r, k0, pw.astype(bf), pm.astype(bf), den))
        for r, k0, pw, pm, den in probs:
            vw = v_ref[pl.ds(k0, NA_KH * GRID_W), :].astype(bf)
            o = jnp.dot(pw, vw, preferred_element_type=f32) + jnp.dot(pm, vm, preferred_element_type=f32)
            o_ref[pl.ds(pl.multiple_of(HG_CHUNK + GRID_W * r, GRID_W), GRID_W), :] = (o / den).astype(o_ref.dtype)
        return carry

    lax.fori_loop(0, rows // NA_GROUP, row_group, 0)


def _natten(p, rpb, meta_bias):
    bsz, lp, _ = p.shape
    assert ((lp - HG_CHUNK) // GRID_W) % NA_GROUP == 0
    heads = NA_WIDTH // HEAD_DIM
    blk = lambda seg: pl.BlockSpec((None, lp, HEAD_DIM), lambda b, h: (b, 0, seg * heads + h))
    table = _na_bias_table(rpb)
    return pl.pallas_call(
        _na_kernel,
        grid=(bsz, heads),
        in_specs=[blk(0), blk(1), blk(2),
                  pl.BlockSpec((None, 2 * NA_KH - 2, GRID_W, 2 * GRID_W), lambda b, h: (h, 0, 0, 0)),
                  pl.BlockSpec((None, 1, N_META), lambda b, h: (h, 0, 0))],
        out_specs=pl.BlockSpec((None, lp, HEAD_DIM), lambda b, h: (b, 0, h)),
        out_shape=jax.ShapeDtypeStruct((bsz, lp, NA_WIDTH), jnp.bfloat16),
        compiler_params=pltpu.CompilerParams(
            dimension_semantics=("parallel", "parallel"), vmem_limit_bytes=V7X_VMEM_LIMIT),
        name="neighbourhood_attention",
    )(p, p, p, table, meta_bias.astype(jnp.float32).reshape(heads, 1, N_META))


def _even_mix(p, rpb, meta_bias, lb, norm_gain):
    oa = _natten(p, rpb, meta_bias)
    o_fwd, o_bwd = _hgrn2(p, 3 * NA_WIDTH, lb)
    return _even_gate(oa, o_fwd, o_bwd, p, 3 * NA_WIDTH + 4 * HG_WIDTH, norm_gain)


def kernel(x, meta_tokens, norm_pre, norm_post, ev_w_in, ev_w_out, na_rpb, na_meta_bias, hg_lower, hg_norm,
           od_w_in, od_w_out, hy_short, hy_ffn_w1, hy_ffn_b1, hy_ffn_freq1, hy_ffn_w2, hy_ffn_b2, hy_ffn_freq2,
           hy_ffn_w3, hy_bias):
    b = x.shape[0]
    depth = norm_pre.shape[0]
    f32 = jnp.float32
    lb_all = jnp.cumsum(jax.nn.softmax(hg_lower.astype(f32), axis=0), axis=0)
    lb_all = lb_all - lb_all[:1]
    h, hn = _embed(x, meta_tokens, norm_pre[0])
    h = h.reshape(b * LP, D_MODEL)
    hn = hn.reshape(b * LP, D_MODEL)
    consts = _dft_constants()
    for layer in range(depth):
        j = layer // 2
        if layer % 2 == 0:
            p = _matmul(hn, ev_w_in, j, f32).reshape(b, LP, -1)
            y = _even_mix(p, na_rpb[j], na_meta_bias[j], lb_all[j], hg_norm[j]).reshape(b * LP, D_MODEL)
            w_out = ev_w_out
        else:
            p3 = _matmul(hn, od_w_in, j, jnp.bfloat16).reshape(b, LP, -1)
            y = _odd_mix(p3, hy_short[j], hy_ffn_w1[j], hy_ffn_b1[j], hy_ffn_freq1[j], hy_ffn_w2[j],
                         hy_ffn_b2[j], hy_ffn_freq2[j], hy_ffn_w3[j], hy_bias[j], consts)
            y = y.reshape(b * LP, D_MODEL)
            w_out = od_w_out
        out = _matmul(y, w_out, j, jnp.bfloat16)
        if layer + 1 < depth:
            h, hn = _residual_rmsnorm(h, out, norm_post[layer], norm_pre[layer + 1])
    return _final_residual(h.reshape(b, LP, D_MODEL), out.reshape(b, LP, D_MODEL), norm_post[depth - 1], x.dtype)
```

```python
import functools
import math

import jax
import jax.numpy as jnp
import numpy as np
from jax import lax
from jax.experimental import pallas as pl
from jax.experimental.pallas import tpu as pltpu

D_MODEL = 4096
SEQ = 4096
N_META = 16
GRID_W = 64
HEAD_DIM = 128
NA_WIDTH = D_MODEL // 2
NA_KH = 8
NA_KW = 16
HG_WIDTH = D_MODEL // 2
HG_HEADS = HG_WIDTH // HEAD_DIM
HG_CHUNK = 64
HY_WIDTH = D_MODEL
HY_ORDER = 2
HY_EMB = 33
HY_BANDS = (HY_EMB - 1) // 2
HY_FAST_DECAY = 0.3
HY_SLOW_DECAY = 1.5
HY_TARGET = 1e-2
EPS = 1e-6

PAD = HG_CHUNK - N_META
LP = PAD + N_META + SEQ
L_TOK = N_META + SEQ

V7X_VMEM_LIMIT = 56 * 1024 * 1024


EMBED_PIECES = 5


def _embed_kernel(*refs):
    x_refs, meta_ref, g_ref, h_ref, n_ref = refs[:EMBED_PIECES], *refs[EMBED_PIECES:]
    first = jnp.concatenate([jnp.zeros((PAD, D_MODEL), jnp.float32), meta_ref[...].astype(jnp.float32)], axis=0)
    pieces = [jnp.where(pl.program_id(1) == 0, first, x_refs[0][...].astype(jnp.float32))]
    pieces += [x_ref[...].astype(jnp.float32) for x_ref in x_refs[1:]]
    h = jnp.concatenate(pieces, axis=0)
    h_ref[...] = h
    r = lax.rsqrt(jnp.mean(h * h, axis=-1, keepdims=True) + EPS)
    n_ref[...] = (h * r * g_ref[...]).astype(n_ref.dtype)


def _embed(x, meta_tokens, g):
    bsz, seq, d = x.shape
    assert seq == SEQ and LP % (EMBED_PIECES * HG_CHUNK) == 0
    tr = EMBED_PIECES * HG_CHUNK
    piece = lambda k: pl.BlockSpec((None, HG_CHUNK, d), lambda b, r: (b, jnp.maximum(EMBED_PIECES * r + k - 1, 0), 0))
    row = pl.BlockSpec((None, tr, d), lambda b, r: (b, r, 0))
    return pl.pallas_call(
        _embed_kernel,
        grid=(bsz, LP // tr),
        in_specs=[piece(k) for k in range(EMBED_PIECES)] + [pl.BlockSpec((N_META, d), lambda b, r: (0, 0)),
                                                           pl.BlockSpec((1, d), lambda b, r: (0, 0))],
        out_specs=[row, row],
        out_shape=[jax.ShapeDtypeStruct((bsz, LP, d), jnp.float32), jax.ShapeDtypeStruct((bsz, LP, d), jnp.bfloat16)],
        compiler_params=pltpu.CompilerParams(
            dimension_semantics=("parallel", "parallel"), vmem_limit_bytes=V7X_VMEM_LIMIT),
        name="embed_rmsnorm_pre",
    )(*([x] * EMBED_PIECES), meta_tokens, g.reshape(1, d))


def _post_pre_kernel(h_ref, y_ref, g_ref, gn_ref, o_ref, n_ref):
    y = y_ref[...].astype(jnp.float32)
    r = lax.rsqrt(jnp.mean(y * y, axis=-1, keepdims=True) + EPS)
    h = h_ref[...] + y * r * g_ref[...]
    o_ref[...] = h
    rn = lax.rsqrt(jnp.mean(h * h, axis=-1, keepdims=True) + EPS)
    n_ref[...] = (h * rn * gn_ref[...]).astype(n_ref.dtype)


FINAL_PIECES = 4


def _final_kernel(*refs):
    h_refs, y_refs = refs[:FINAL_PIECES], refs[FINAL_PIECES:2 * FINAL_PIECES]
    g_ref, o_ref = refs[2 * FINAL_PIECES:]
    for k, (h_ref, y_ref) in enumerate(zip(h_refs, y_refs)):
        y = y_ref[...].astype(jnp.float32)
        r = lax.rsqrt(jnp.mean(y * y, axis=-1, keepdims=True) + EPS)
        o_ref[k * HG_CHUNK:(k + 1) * HG_CHUNK, :] = (h_ref[...] + y * r * g_ref[...]).astype(o_ref.dtype)


def _final_residual(h3, y3, g, out_dtype):
    bsz, lp, d = h3.shape
    assert PAD + N_META == HG_CHUNK and SEQ % (FINAL_PIECES * HG_CHUNK) == 0
    piece = lambda k: pl.BlockSpec((None, HG_CHUNK, d), lambda b, r: (b, FINAL_PIECES * r + k + 1, 0))
    pieces = [piece(k) for k in range(FINAL_PIECES)]
    return pl.pallas_call(
        _final_kernel,
        grid=(bsz, SEQ // (FINAL_PIECES * HG_CHUNK)),
        in_specs=pieces + pieces + [pl.BlockSpec((1, d), lambda b, r: (0, 0))],
        out_specs=pl.BlockSpec((None, FINAL_PIECES * HG_CHUNK, d), lambda b, r: (b, r, 0)),
        out_shape=jax.ShapeDtypeStruct((bsz, SEQ, d), out_dtype),
        compiler_params=pltpu.CompilerParams(
            dimension_semantics=("parallel", "parallel"), vmem_limit_bytes=V7X_VMEM_LIMIT),
        name="residual_rmsnorm_post_final",
    )(*([h3] * FINAL_PIECES), *([y3] * FINAL_PIECES), g.reshape(1, d))


def _residual_rmsnorm(h2d, y2d, g, g_next, tr=320):
    m, d = h2d.shape
    row = pl.BlockSpec((tr, d), lambda i: (i, 0))
    vec = pl.BlockSpec((1, d), lambda i: (0, 0))
    params = pltpu.CompilerParams(dimension_semantics=("parallel",), vmem_limit_bytes=V7X_VMEM_LIMIT)
    return pl.pallas_call(
        _post_pre_kernel, grid=(m // tr,), in_specs=[row, row, vec, vec], out_specs=[row, row],
        out_shape=[jax.ShapeDtypeStruct((m, d), jnp.float32), jax.ShapeDtypeStruct((m, d), jnp.bfloat16)],
        compiler_params=params, name="residual_rmsnorm_post_pre",
    )(h2d, y2d, g.reshape(1, d), g_next.reshape(1, d))


def _matmul_kernel(x_ref, w_ref, o_ref):
    o_ref[...] = jnp.dot(x_ref[...], w_ref[...].astype(jnp.bfloat16),
                         preferred_element_type=jnp.float32).astype(o_ref.dtype)


def _matmul(x, w_stack, layer, out_dtype, tm=1040, tn=512):
    m, kd = x.shape
    _, _, n = w_stack.shape
    assert m % tm == 0 and n % tn == 0
    return pl.pallas_call(
        _matmul_kernel,
        grid=(m // tm, n // tn),
        in_specs=[pl.BlockSpec((tm, kd), lambda i, j: (i, 0)),
                  pl.BlockSpec((None, kd, tn), lambda i, j: (layer, 0, j))],
        out_specs=pl.BlockSpec((tm, tn), lambda i, j: (i, j)),
        out_shape=jax.ShapeDtypeStruct((m, n), out_dtype),
        compiler_params=pltpu.CompilerParams(
            dimension_semantics=("parallel", "parallel"), vmem_limit_bytes=V7X_VMEM_LIMIT),
        name="projection_matmul",
    )(x, w_stack)


FFT_N1 = 128
FFT_N2 = 65
FFT_N = FFT_N1 * FFT_N2
N2P = 80
SLAB = 264
TC = 128
MID_GROUP = 16
NC_HALF = FFT_N1 // 2 + 1
HALF_P = 72
SLAB_S = 152
C_GROUP = 13
FILL_CHUNKS = 13
B_GROUP = 13


def _dft_constants():
    a = np.arange(FFT_N1)
    c = np.arange(FFT_N1)
    w1 = np.exp(-2j * np.pi * np.outer(c, a) / FFT_N1)
    w1r, w1i = w1.real, w1.imag
    half = FFT_N1 // 2
    f1 = np.block([[w1r[:, :half], -w1i[:, :half]], [w1i[:, :half], w1r[:, :half]]])
    f1_real = np.zeros((2 * HALF_P, FFT_N1), np.float64)
    f1_real[:NC_HALF] = w1r[:NC_HALF]
    f1_real[HALF_P:HALF_P + NC_HALF] = w1i[:NC_HALF]
    g = np.conj(w1).T
    gr, gi = g.real[:half], g.imag[:half]
    i2 = np.block([[gr, -gi], [gi, gr]])
    b = np.arange(FFT_N2)
    d = np.arange(FFT_N2)
    w2 = np.exp(-2j * np.pi * np.outer(d, b) / FFT_N2)
    tw = np.exp(-2j * np.pi * np.outer(c, b) / FFT_N)
    m = w2[None, :, :] * tw[:, None, :]
    ab = np.zeros((FFT_N1, 2 * N2P, 128), np.float64)
    m[NC_HALF:] = m[NC_HALF:, ::-1, :]
    ab[:, :FFT_N2, :FFT_N2] = m.real
    ab[:, N2P:N2P + FFT_N2, :FFT_N2] = m.imag
    bf = jnp.bfloat16
    abt = np.zeros((FFT_N1, N2P, 256), np.float64)
    abt[:, :, :2 * N2P] = np.swapaxes(ab[:, :, :N2P], 1, 2)
    return (jnp.asarray(f1, bf), jnp.asarray(f1_real, bf), jnp.asarray(i2, bf), jnp.asarray(ab, bf),
            jnp.asarray(abt, bf))


def _middle_stage_inputs(y_ref, c, imag_row=FFT_N1, pitch=SLAB):
    yr = y_ref[pl.ds(c, N2P, stride=pitch), :]
    yi = y_ref[pl.ds(imag_row + c, N2P, stride=pitch), :]
    return jnp.concatenate([yr, yi], axis=1).astype(jnp.bfloat16)


def _forward_middle(ab, y2):
    y2 = jnp.concatenate([y2, jnp.zeros((128 - N2P, 2 * TC), jnp.bfloat16)], axis=0)
    q = jnp.dot(ab, y2, preferred_element_type=jnp.float32)
    zr = q[:N2P, :TC] - q[N2P:, TC:]
    zi = q[N2P:, :TC] + q[:N2P, TC:]
    return zr, zi


def _spectrum_kernel(h_ref, t_ref, w3f0_ref, w3f1_ref, w3b0_ref, w3b1_ref, delta_ref, f1_ref, ab_ref, k_ref,
                     kt_ref, y_ref):
    w3 = jnp.concatenate([jnp.concatenate([w3f0_ref[...], w3f1_ref[...]], axis=1),
                          jnp.concatenate([w3b0_ref[...], w3b1_ref[...]], axis=1)], axis=0).astype(jnp.bfloat16)
    delta = delta_ref[...]
    rows = FFT_N // FILL_CHUNKS

    def fill(i, carry):
        r0 = pl.multiple_of(i * rows, 8)
        kt = jnp.dot(h_ref[pl.ds(r0, rows), :], w3, preferred_element_type=jnp.float32)
        window = jnp.exp(t_ref[pl.ds(r0, rows), :] * delta)
        for n in range(HY_ORDER):
            kt_ref[n, pl.ds(r0, rows), :] = kt[:, n * TC:(n + 1) * TC] * window
        return carry

    lax.fori_loop(0, FILL_CHUNKS, fill, 0)
    y_ref[pl.ds(FFT_N2 * SLAB_S, (N2P - FFT_N2) * SLAB_S), :] = jnp.zeros(((N2P - FFT_N2) * SLAB_S, TC), jnp.float32)

    for n in range(HY_ORDER):
        def stage1(g, carry):
            bs = [g * B_GROUP + i for i in range(B_GROUP)]
            x = jnp.concatenate([kt_ref.at[n][pl.ds(b, FFT_N1, stride=FFT_N2), :] for b in bs], axis=1)
            y = jnp.dot(f1_ref[...], x.astype(jnp.bfloat16), preferred_element_type=jnp.float32)
            for i, b in enumerate(bs):
                y_ref[pl.ds(pl.multiple_of(b * SLAB_S, 8), 2 * HALF_P), :] = y[:, i * TC:(i + 1) * TC]
            return carry

        lax.fori_loop(0, FFT_N2 // B_GROUP, stage1, 0)

        def stage2(g, carry):
            cs = [g * C_GROUP + i for i in range(C_GROUP)]
            zs = [_forward_middle(ab_ref[c], _middle_stage_inputs(y_ref, c, HALF_P, SLAB_S)) for c in cs]
            for c, (zr, zi) in zip(cs, zs):
                k_ref[n, c] = (jnp.concatenate([zr, zi], axis=1) * (1.0 / FFT_N)).astype(jnp.bfloat16)
            return carry

        lax.fori_loop(0, NC_HALF // C_GROUP, stage2, 0)


def _filter_spectrum(h2, w3, deltas, consts):
    _, f1_real, _, ab, _ = consts
    nt = HY_WIDTH // TC
    j = np.arange(FFT_N)
    lag = np.where(j < L_TOK, j, FFT_N - j).astype(np.float32)
    neg_t = jnp.asarray(np.broadcast_to((-lag / np.float32(L_TOK - 1))[:, None], (FFT_N, TC)))
    w3_cols = lambda direction, order: pl.BlockSpec((64, TC), lambda i: (0, (direction * HY_ORDER + order) * nt + i))
    return pl.pallas_call(
        _spectrum_kernel,
        grid=(nt,),
        in_specs=[
            pl.BlockSpec((FFT_N, 128), lambda i: (0, 0)),
            pl.BlockSpec((FFT_N, TC), lambda i: (0, 0)),
            w3_cols(0, 0), w3_cols(0, 1), w3_cols(1, 0), w3_cols(1, 1),
            pl.BlockSpec((1, TC), lambda i: (0, i)),
            pl.BlockSpec((2 * HALF_P, FFT_N1), lambda i: (0, 0)),
            pl.BlockSpec((FFT_N1, 2 * N2P, 128), lambda i: (0, 0, 0)),
        ],
        out_specs=pl.BlockSpec((HY_ORDER, None, NC_HALF, N2P, 2 * TC), lambda i: (0, i, 0, 0, 0)),
        out_shape=jax.ShapeDtypeStruct((HY_ORDER, nt, NC_HALF, N2P, 2 * TC), jnp.bfloat16),
        scratch_shapes=[pltpu.VMEM((HY_ORDER, FFT_N, TC), jnp.float32), pltpu.VMEM((N2P * SLAB_S, TC), jnp.float32)],
        compiler_params=pltpu.CompilerParams(dimension_semantics=("parallel",), vmem_limit_bytes=V7X_VMEM_LIMIT),
        name="hyena_filter_spectrum",
    )(h2, neg_t, w3, w3, w3, w3, deltas, f1_real, ab)


def _conv_kernel(u_ref, k_ref, f1_ref, i2_ref, ab_ref, abt_ref, o_ref, y_ref):
    y_ref[pl.ds(FFT_N2 * SLAB, (N2P - FFT_N2) * SLAB), :] = jnp.zeros(((N2P - FFT_N2) * SLAB, TC), jnp.float32)
    half = FFT_N1 // 2

    def stage1(g, carry):
        bs = [g * B_GROUP + i for i in range(B_GROUP)]
        x = jnp.concatenate(
            [jnp.concatenate([u_ref.at[ri][pl.ds(b, half, stride=FFT_N2), :] for b in bs], axis=1) for ri in range(2)],
            axis=0)
        y = jnp.dot(f1_ref[...], x.astype(jnp.bfloat16), preferred_element_type=jnp.float32)
        for i, b in enumerate(bs):
            y_ref[pl.ds(pl.multiple_of(b * SLAB, 8), 2 * FFT_N1), :] = y[:, i * TC:(i + 1) * TC]
        return carry

    lax.fori_loop(0, FFT_N2 // B_GROUP, stage1, 0)

    def inverse_middle(c, z):
        zr, zi = z
        mirrored = c >= NC_HALF
        kk = k_ref[jnp.where(mirrored, FFT_N1 - c, c)].astype(jnp.float32)
        kr, ki = kk[:, :TC], kk[:, TC:] * jnp.where(mirrored, -1.0, 1.0)
        pr = zr * kr - zi * ki
        pi = zr * ki + zi * kr
        rhs = jnp.concatenate([jnp.concatenate([pr, pi], axis=1), jnp.concatenate([pi, -pr], axis=1),
                               jnp.zeros((256 - 2 * N2P, 2 * TC), jnp.float32)], axis=0)
        return jnp.dot(abt_ref[c], rhs.astype(jnp.bfloat16), preferred_element_type=jnp.float32)

    def middle(g, carry):
        cs = [g * MID_GROUP + i for i in range(MID_GROUP)]
        zs = [_forward_middle(ab_ref[c], _middle_stage_inputs(y_ref, c)) for c in cs]
        outs = [inverse_middle(c, z) for c, z in zip(cs, zs)]
        for c, v in zip(cs, outs):
            y_ref[pl.ds(c, N2P, stride=SLAB), :] = v[:N2P, :TC]
            y_ref[pl.ds(FFT_N1 + c, N2P, stride=SLAB), :] = v[:N2P, TC:]
        return carry

    lax.fori_loop(0, FFT_N1 // MID_GROUP, middle, 0)

    def stage4(g, carry):
        bs = [g * B_GROUP + i for i in range(B_GROUP)]
        v = jnp.concatenate([y_ref[pl.ds(pl.multiple_of(b * SLAB, 8), 2 * FFT_N1), :] for b in bs], axis=1)
        x = jnp.dot(i2_ref[...], v.astype(jnp.bfloat16), preferred_element_type=jnp.float32)
        for i, b in enumerate(bs):
            o_ref.at[0][pl.ds(b, half, stride=FFT_N2), :] = x[:half, i * TC:(i + 1) * TC]
            o_ref.at[1][pl.ds(b, half, stride=FFT_N2), :] = x[half:, i * TC:(i + 1) * TC]
        return carry

    lax.fori_loop(0, FFT_N2 // B_GROUP, stage4, 0)


def _long_conv(u, kspec, order, consts):
    f1, _, i2, ab, abt = consts
    bsz, lp, _ = u.shape
    assert bsz == 2 and lp == LP
    nt = HY_WIDTH // TC
    return pl.pallas_call(
        _conv_kernel,
        grid=(nt,),
        in_specs=[
            pl.BlockSpec((bsz, lp, TC), lambda i: (0, 0, i)),
            pl.BlockSpec((None, None, NC_HALF, N2P, 2 * TC), lambda i: (order, i, 0, 0, 0)),
            pl.BlockSpec((2 * FFT_N1, FFT_N1), lambda i: (0, 0)),
            pl.BlockSpec((FFT_N1, 2 * FFT_N1), lambda i: (0, 0)),
            pl.BlockSpec((FFT_N1, 2 * N2P, 128), lambda i: (0, 0, 0), pipeline_mode=pl.Buffered(1)),
            pl.BlockSpec((FFT_N1, N2P, 256), lambda i: (0, 0, 0), pipeline_mode=pl.Buffered(1)),
        ],
        out_specs=pl.BlockSpec((bsz, lp, TC), lambda i: (0, 0, i)),
        out_shape=jax.ShapeDtypeStruct((bsz, lp, HY_WIDTH), jnp.float32),
        scratch_shapes=[pltpu.VMEM((N2P * SLAB, TC), jnp.float32)],
        compiler_params=pltpu.CompilerParams(dimension_semantics=("parallel",), vmem_limit_bytes=V7X_VMEM_LIMIT),
        name="hyena_long_conv",
    )(u, kspec, f1, i2, ab, abt)


def _short_conv_rows(u, w):
    y = pltpu.roll(u, 1, 0) * w[0:1] + u * w[1:2] + pltpu.roll(u, LP - 1, 0) * w[2:3]
    row = lax.broadcasted_iota(jnp.int32, y.shape, 0)
    return jnp.where(row >= PAD, y, 0.0)


def _short_conv_kernel(u_ref, w_ref, o_ref):
    o_ref[...] = _short_conv_rows(u_ref[...].astype(jnp.float32), w_ref[...])


def _gate1_kernel(v_ref, u_ref, w_ref, y_ref, b_ref, o_ref):
    x = _short_conv_rows(u_ref[...].astype(jnp.float32), w_ref[...])
    o_ref[...] = x * (y_ref[...] + v_ref[...] * b_ref[...])


def _gate2_kernel(z_ref, u_ref, w_ref, y_ref, b_ref, g_ref, o_ref):
    g = g_ref[...].astype(jnp.float32)
    z = _short_conv_rows(u_ref[...].astype(jnp.float32), w_ref[...]) * (y_ref[...] + z_ref[...] * b_ref[...])
    o_ref[...] = (z * (g * jax.nn.sigmoid(g))).astype(o_ref.dtype)


def _seq_tile(col_block):
    return pl.BlockSpec((None, LP, TC), lambda b, i: (b, 0, col_block + i))


def _vec_tile(rows, col_block):
    return pl.BlockSpec((rows, TC), lambda b, i: (0, col_block + i))


def _short_conv(p3, w_short_t):
    bsz = p3.shape[0]
    return pl.pallas_call(
        _short_conv_kernel,
        grid=(bsz, HY_WIDTH // TC),
        in_specs=[_seq_tile(0), _vec_tile(3, 0)],
        out_specs=_seq_tile(0),
        out_shape=jax.ShapeDtypeStruct((bsz, LP, HY_WIDTH), jnp.float32),
        compiler_params=pltpu.CompilerParams(
            dimension_semantics=("parallel", "parallel"), vmem_limit_bytes=V7X_VMEM_LIMIT),
        name="hyena_short_conv",
    )(p3, w_short_t)


def _hyena_gate1(v, p3, w_short_t, y, bias):
    bsz = p3.shape[0]
    nt = HY_WIDTH // TC
    return pl.pallas_call(
        _gate1_kernel,
        grid=(bsz, nt),
        in_specs=[_seq_tile(0), _seq_tile(nt), _vec_tile(3, nt), _seq_tile(0), _vec_tile(1, 0)],
        out_specs=_seq_tile(0),
        out_shape=jax.ShapeDtypeStruct((bsz, LP, HY_WIDTH), jnp.float32),
        compiler_params=pltpu.CompilerParams(
            dimension_semantics=("parallel", "parallel"), vmem_limit_bytes=V7X_VMEM_LIMIT),
        name="hyena_gate1",
    )(v, p3, w_short_t, y, bias.reshape(1, HY_WIDTH))


def _hyena_gate2(z1, p3, w_short_t, y, bias):
    bsz = p3.shape[0]
    nt = HY_WIDTH // TC
    return pl.pallas_call(
        _gate2_kernel,
        grid=(bsz, nt),
        in_specs=[_seq_tile(0), _seq_tile(2 * nt), _vec_tile(3, 2 * nt), _seq_tile(0), _vec_tile(1, 0),
                  _seq_tile(3 * nt)],
        out_specs=_seq_tile(0),
        out_shape=jax.ShapeDtypeStruct((bsz, LP, HY_WIDTH), jnp.bfloat16),
        compiler_params=pltpu.CompilerParams(
            dimension_semantics=("parallel", "parallel"), vmem_limit_bytes=V7X_VMEM_LIMIT),
        name="hyena_gate2",
    )(z1, p3, w_short_t, y, bias.reshape(1, HY_WIDTH), p3)


def _filter_positions():
    f32 = np.float32
    l = L_TOK
    t = np.linspace(0.0, 1.0, l, dtype=f32)[:, None]
    w = f32(2.0 * math.pi) * np.arange(l, dtype=f32)[:, None] / f32(l)
    bands = np.linspace(1e-4, HY_BANDS - 1, HY_BANDS, dtype=f32)[None, :]
    z = np.concatenate([t, np.cos(bands * w), -np.sin(bands * w)], axis=-1).astype(f32)
    j = np.arange(FFT_N)
    lag = np.minimum(np.where(j < l, j, FFT_N - j), l - 1)
    zrows = np.zeros((FFT_N, 128), f32)
    zrows[:, :HY_EMB] = z[lag]
    mask = np.zeros((FFT_N, 128), f32)
    mask[:, :64] = (j < l)[:, None]
    mask[:, 64:] = ((j > FFT_N - l) | (j == 0))[:, None]
    return jnp.asarray(zrows), jnp.asarray(mask)


def _hidden_kernel(z_ref, m_ref, w1_ref, b1_ref, fr1_ref, w2_ref, b2_ref, fr2_ref, o_ref):
    hi = lax.Precision.HIGHEST
    a = jnp.dot(z_ref[...], w1_ref[...], precision=hi, preferred_element_type=jnp.float32) + b1_ref[...]
    hid = jnp.sin(fr1_ref[...] * a)
    a = jnp.dot(hid, w2_ref[...], precision=hi, preferred_element_type=jnp.float32) + b2_ref[...]
    hid = jnp.sin(fr2_ref[...] * a)
    o_ref[...] = (jnp.concatenate([hid, hid], axis=1) * m_ref[...]).astype(o_ref.dtype)


def _hyena_hidden(w1, b1, fr1, w2, b2, fr2):
    zrows, mask = _filter_positions()
    rows = FFT_N // FILL_CHUNKS
    hdim = w2.shape[0]
    w1p = jnp.zeros((128, hdim), jnp.float32).at[:HY_EMB].set(w1.astype(jnp.float32))
    tile = pl.BlockSpec((rows, 128), lambda i: (i, 0))
    full = lambda a: pl.BlockSpec(a.shape, lambda i: (0,) * a.ndim)
    vecs = [v.astype(jnp.float32).reshape(1, hdim) for v in (b1, fr1, b2, fr2)]
    w2f = w2.astype(jnp.float32)
    return pl.pallas_call(
        _hidden_kernel,
        grid=(FILL_CHUNKS,),
        in_specs=[tile, tile, full(w1p), full(vecs[0]), full(vecs[1]), full(w2f), full(vecs[2]), full(vecs[3])],
        out_specs=tile,
        out_shape=jax.ShapeDtypeStruct((FFT_N, 128), jnp.bfloat16),
        compiler_params=pltpu.CompilerParams(dimension_semantics=("parallel",), vmem_limit_bytes=V7X_VMEM_LIMIT),
        name="hyena_filter_hidden",
    )(zrows, mask, w1p, vecs[0], vecs[1], w2f, vecs[2], vecs[3])


def _odd_mix(p3, w_short, w1, b1, fr1, w2, b2, fr2, w3, bias, consts):
    min_decay = math.log(HY_TARGET) / HY_SLOW_DECAY
    max_decay = math.log(HY_TARGET) / HY_FAST_DECAY
    deltas = jnp.abs(jnp.linspace(min_decay, max_decay, HY_WIDTH, dtype=jnp.float32)).reshape(1, HY_WIDTH)
    kspec = _filter_spectrum(_hyena_hidden(w1, b1, fr1, w2, b2, fr2), w3, deltas, consts)
    w_short_t = w_short.T
    v = _short_conv(p3, w_short_t)
    y1 = _long_conv(v, kspec, 0, consts)
    z1 = _hyena_gate1(v, p3, w_short_t, y1, bias[0])
    y2 = _long_conv(z1, kspec, 1, consts)
    return _hyena_gate2(z1, p3, w_short_t, y2, bias[1])


HG_HB = 16
HG_LEVELS = (32, 16, 8, 4, 2, 1)
LOG2_E = 1.4426950408889634


def _cumsum_rows(g, rev):
    f32, bf = jnp.float32, jnp.bfloat16
    t = lax.broadcasted_iota(jnp.int32, (HG_CHUNK, 3 * HG_CHUNK), 0)
    s = lax.broadcasted_iota(jnp.int32, (HG_CHUNK, 3 * HG_CHUNK), 1) & (HG_CHUNK - 1)
    tri = jnp.where((s >= t) if rev else (s <= t), 1.0, 0.0).astype(bf)
    hi = g.astype(bf)
    r1 = g - hi.astype(f32)
    mid = r1.astype(bf)
    lo = (r1 - mid.astype(f32)).astype(bf)
    return jnp.dot(tri, jnp.concatenate([hi, mid, lo], axis=0), preferred_element_type=f32)


def _hgrn_row_masks(rev):
    row = lax.broadcasted_iota(jnp.int32, (HG_CHUNK, HEAD_DIM), 0)
    masks = {}
    for m in HG_LEVELS:
        upper = (row & m) != 0
        is_query = jnp.logical_not(upper) if rev else upper
        target = m if rev else m - 1
        selectors = [(src, (row & (2 * m - 1)) == src) for src in range(2 * m) if src != target] if m == 2 else None
        masks[m] = (is_query, jnp.where(is_query, LOG2_E, -LOG2_E), selectors)
    return masks


def _level_reference(gc, m, rev, selectors):
    if m >= 4:
        g3 = gc.reshape(HG_CHUNK // (2 * m), 2 * m, HEAD_DIM)
        ref = g3[:, m:m + 1, :] if rev else g3[:, m - 1:m, :]
        return jnp.broadcast_to(ref, g3.shape).reshape(HG_CHUNK, HEAD_DIM)
    target = m if rev else m - 1
    out = gc
    for src, picks_src in selectors:
        out = jnp.where(picks_src, pltpu.roll(gc, (src - target) % HG_CHUNK, 0), out)
    return out


def _hgrn_gates(q_raw, f_raw, la, lc, omlb, rev):
    q = q_raw * (0.5 * HEAD_DIM ** -0.5) * (1.0 + jnp.tanh(0.5 * q_raw))
    k = (0.5 * omlb) * (1.0 - jnp.tanh(0.5 * f_raw))
    log_sig = jnp.minimum(f_raw, 0.0) - jnp.log(1.0 + jnp.exp(-jnp.abs(f_raw)))
    b = lc + log_sig
    g = jnp.maximum(la, b) + jnp.log(1.0 + jnp.exp(-jnp.abs(la - b)))
    return q, k, g, _cumsum_rows(g, rev)


def _hgrn_level_operands(q, k, g, gc, rev, masks):
    bf = jnp.bfloat16
    qb, kb = q.astype(bf), k.astype(bf)
    ops = []
    for m in HG_LEVELS:
        is_query, sign_log2e, selectors = masks[m]
        if m == 1:
            ops.append(jnp.where(is_query, qb * jnp.exp2(g * LOG2_E).astype(bf), kb))
        else:
            gref = _level_reference(gc, m, rev, selectors)
            ops.append(jnp.where(is_query, qb, kb) * jnp.exp2((gc - gref) * sign_log2e).astype(bf))
    return ops


def _hgrn_level_id(rev):
    t = lax.broadcasted_iota(jnp.int32, (HG_CHUNK, HG_CHUNK), 0)
    s = lax.broadcasted_iota(jnp.int32, (HG_CHUNK, HG_CHUNK), 1)
    vis = (s >= t) if rev else (s <= t)
    x = t ^ s
    lvl = jnp.full((HG_CHUNK, HG_CHUNK), len(HG_LEVELS), jnp.int32)
    for i, m in enumerate(HG_LEVELS):
        lvl = jnp.where((x & m) != 0, jnp.minimum(lvl, i), lvl)
    return jnp.where(vis, lvl, -1)


def _hgrn_kernel(qf_ref, if_ref, ff_ref, qb_ref, ib_ref, fb_ref, la_ref, lc_ref, om_ref, of_ref, ob_ref, s_ref):
    f32, bf = jnp.float32, jnp.bfloat16
    nt = (((1,), (1,)), ((), ()))
    tn = (((0,), (0,)), ((), ()))

    @pl.when(pl.program_id(2) == 0)
    def _():
        s_ref[...] = jnp.zeros_like(s_ref)

    row_masks = [_hgrn_row_masks(False), _hgrn_row_masks(True)]
    refs = ((qf_ref, if_ref, ff_ref, of_ref), (qb_ref, ib_ref, fb_ref, ob_ref))
    inst = [(d, h) for d in range(2) for h in range(HG_HB)]
    level_masks = [[lvl == i for i in range(len(HG_LEVELS) + 1)]
                   for lvl in (_hgrn_level_id(False), _hgrn_level_id(True))]

    pre = []
    for d, h in inst:
        q_ref, i_ref, f_ref, _ = refs[d]
        sl = slice(h * HEAD_DIM, (h + 1) * HEAD_DIM)
        q, k, g, gc = _hgrn_gates(q_ref[:, sl], f_ref[:, sl], la_ref[d:d + 1, sl], lc_ref[d:d + 1, sl],
                                  om_ref[d:d + 1, sl], d == 1)
        g_end = gc[0:1, :] if d == 1 else gc[HG_CHUNK - 1:HG_CHUNK, :]
        pre.append(dict(ops=_hgrn_level_operands(q, k, g, gc, d == 1, row_masks[d]), qk=(q.astype(bf), k.astype(bf)),
                        qe=(q * jnp.exp(gc)).astype(bf), kd=(k * jnp.exp(g_end - gc)).astype(bf),
                        decay=jnp.exp(g_end), v=i_ref[:, sl].astype(bf)))

    attn = []
    for (d, h), pr in zip(inst, pre):
        prods = [lax.dot_general(x, x, nt, preferred_element_type=f32) for x in pr["ops"]]
        prods.append(lax.dot_general(pr["qk"][0], pr["qk"][1], nt, preferred_element_type=f32))
        a = jnp.zeros((HG_CHUNK, HG_CHUNK), f32)
        for i, pmat in enumerate(prods):
            a = jnp.where(level_masks[d][i], pmat, a)
        attn.append(a.astype(bf))

    for (d, h), pr, a in zip(inst, pre, attn):
        o_ref = refs[d][3]
        sl = slice(h * HEAD_DIM, (h + 1) * HEAD_DIM)
        state_t = s_ref[d, h]
        o = jnp.dot(a, pr["v"], preferred_element_type=f32)
        o = o + lax.dot_general(pr["qe"], state_t.astype(bf), nt, preferred_element_type=f32)
        o_ref[:, sl] = o.astype(o_ref.dtype)
        s_ref[d, h] = pr["decay"] * state_t + lax.dot_general(pr["v"], pr["kd"], tn, preferred_element_type=f32)


def _hgrn2(p, col0, lb):
    bsz, lp, _ = p.shape
    nchunk = lp // HG_CHUNK
    cw = HG_HB * HEAD_DIM
    ng = HG_WIDTH // cw
    off = col0 // cw
    la = jnp.log(lb)
    lc = jnp.log1p(-lb)
    om = 1.0 - lb
    fwd = lambda seg: pl.BlockSpec((None, HG_CHUNK, cw), lambda b, g, c: (b, c, off + seg * ng + g))
    bwd = lambda seg: pl.BlockSpec((None, HG_CHUNK, cw), lambda b, g, c: (b, nchunk - 1 - c, off + seg * ng + g))
    par = pl.BlockSpec((2, cw), lambda b, g, c: (0, g))
    return pl.pallas_call(
        _hgrn_kernel,
        grid=(bsz, ng, nchunk),
        in_specs=[fwd(0), fwd(1), fwd(2), bwd(0), bwd(1), bwd(3), par, par, par],
        out_specs=[pl.BlockSpec((None, HG_CHUNK, cw), lambda b, g, c: (b, c, g)),
                   pl.BlockSpec((None, HG_CHUNK, cw), lambda b, g, c: (b, nchunk - 1 - c, g))],
        out_shape=[jax.ShapeDtypeStruct((bsz, lp, HG_WIDTH), jnp.bfloat16)] * 2,
        scratch_shapes=[pltpu.VMEM((2, HG_HB, HEAD_DIM, HEAD_DIM), jnp.float32)],
        compiler_params=pltpu.CompilerParams(
            dimension_semantics=("parallel", "parallel", "arbitrary"), vmem_limit_bytes=V7X_VMEM_LIMIT),
        name="hgrn2_bidirectional",
    )(p, p, p, p, p, p, la, lc, om)


EG_ROWS = 320


def _even_gate_kernel(oa_ref, of_ref, ob_ref, ga_ref, gb_ref, gain_ref, y_ref):
    ga = ga_ref[...]
    y_ref[:, :NA_WIDTH] = (oa_ref[...].astype(jnp.float32) * (ga * jax.nn.sigmoid(ga))).astype(y_ref.dtype)
    for h in range(HG_HEADS):
        sl = slice(h * HEAD_DIM, (h + 1) * HEAD_DIM)
        oh = of_ref[:, sl].astype(jnp.float32) + ob_ref[:, sl].astype(jnp.float32)
        oh = oh * lax.rsqrt(jnp.mean(oh * oh, axis=-1, keepdims=True) + EPS) * gain_ref[:, sl]
        gb = gb_ref[:, sl]
        y_ref[:, NA_WIDTH + h * HEAD_DIM:NA_WIDTH + (h + 1) * HEAD_DIM] = (
            oh * (gb * jax.nn.sigmoid(gb))).astype(y_ref.dtype)


def _even_gate(oa, o_fwd, o_bwd, p, gate_col0, norm_gain):
    bsz, lp, _ = oa.shape
    assert gate_col0 % NA_WIDTH == 0
    goff = gate_col0 // NA_WIDTH
    half = lambda off: pl.BlockSpec((None, EG_ROWS, NA_WIDTH), lambda b, r: (b, r, off))
    return pl.pallas_call(
        _even_gate_kernel,
        grid=(bsz, lp // EG_ROWS),
        in_specs=[half(0), half(0), half(0), half(goff), half(goff + 1),
                  pl.BlockSpec((1, HG_WIDTH), lambda b, r: (0, 0))],
        out_specs=pl.BlockSpec((None, EG_ROWS, D_MODEL), lambda b, r: (b, r, 0)),
        out_shape=jax.ShapeDtypeStruct((bsz, lp, D_MODEL), jnp.bfloat16),
        compiler_params=pltpu.CompilerParams(
            dimension_semantics=("parallel", "parallel"), vmem_limit_bytes=V7X_VMEM_LIMIT),
        name="even_gate",
    )(oa, o_fwd, o_bwd, p, p, norm_gain.reshape(1, HG_WIDTH))


NA_MASKED = -1e30
NA_GROUP = 8


def _na_bias_table(rpb):
    cols = np.arange(GRID_W)
    c0 = np.clip(cols - NA_KW // 2, 0, GRID_W - NA_KW)
    col_ok = (cols[None, :] >= c0[:, None]) & (cols[None, :] < c0[:, None] + NA_KW)
    col_idx = np.clip(cols[None, :] - cols[:, None] + NA_KW - 1, 0, 2 * NA_KW - 2)
    onehot = (col_idx[None] == np.arange(2 * NA_KW - 1)[:, None, None]).astype(np.float32)
    picked = jnp.einsum('hrj,jqk->hrqk', rpb.astype(jnp.float32), onehot, precision=lax.Precision.HIGHEST)
    table = jnp.where(col_ok[None, None], picked, NA_MASKED)
    return jnp.concatenate([table[:, :-1], table[:, 1:]], axis=-1)


def _na_kernel(q_ref, k_ref, v_ref, t_ref, mb_ref, o_ref):
    f32, bf = jnp.float32, jnp.bfloat16
    rows = (q_ref.shape[0] - HG_CHUNK) // GRID_W
    scale = HEAD_DIM ** -0.5
    nt = (((1,), (1,)), ((), ()))
    km = k_ref[PAD:PAD + N_META, :].astype(bf)
    vm = v_ref[PAD:PAD + N_META, :].astype(bf)
    mb = mb_ref[...]
    o_ref[0:PAD, :] = jnp.zeros((PAD, HEAD_DIM), o_ref.dtype)

    qm = q_ref[PAD:PAD + N_META, :].astype(bf)
    s = lax.dot_general(qm, km, nt, preferred_element_type=f32) * scale + mb
    e = jnp.exp(s - jnp.max(s, axis=-1, keepdims=True))
    o_ref[PAD:PAD + N_META, :] = (jnp.dot(e.astype(bf), vm, preferred_element_type=f32)
                                  / jnp.sum(e, axis=-1, keepdims=True)).astype(o_ref.dtype)

    def row_group(g, carry):
        rs = [g * NA_GROUP + i for i in range(NA_GROUP)]
        scores = []
        for r in rs:
            r0 = jnp.clip(r - NA_KH // 2, 0, rows - NA_KH)
            q = q_ref[pl.ds(pl.multiple_of(HG_CHUNK + GRID_W * r, GRID_W), GRID_W), :].astype(bf)
            k0 = pl.multiple_of(HG_CHUNK + GRID_W * r0, GRID_W)
            kw = k_ref[pl.ds(k0, NA_KH * GRID_W), :].astype(bf)
            s_win = lax.dot_general(q, kw, nt, preferred_element_type=f32)
            s_meta = lax.dot_general(q, km, nt, preferred_element_type=f32)
            scores.append((r, r0, k0, s_win, s_meta))
        probs = []
        for r, r0, k0, s_win, s_meta in scores:
            bias = jnp.concatenate([t_ref[r0 + i - r + NA_KH - 1] for i in range(0, NA_KH, 2)], axis=1)
            s_win = s_win * scale + bias
            s_meta = s_meta * scale + mb
            m = jnp.maximum(jnp.max(s_win, axis=-1, keepdims=True), jnp.max(s_meta, axis=-1, keepdims=True))
            pw = jnp.exp(s_win - m)
            pm = jnp.exp(s_meta - m)
            den = jnp.sum(pw, axis=-1, keepdims=True) + jnp.sum(pm, axis=-1, keepdims=True)
            probs.append((r, k0, pw.astype(bf), pm.astype(bf), den))
        for r, k0, pw, pm, den in probs:
            vw = v_ref[pl.ds(k0, NA_KH * GRID_W), :].astype(bf)
            o = jnp.dot(pw, vw, preferred_element_type=f32) + jnp.dot(pm, vm, preferred_element_type=f32)
            o_ref[pl.ds(pl.multiple_of(HG_CHUNK + GRID_W * r, GRID_W), GRID_W), :] = (o / den).astype(o_ref.dtype)
        return carry

    lax.fori_loop(0, rows // NA_GROUP, row_group, 0)


def _natten(p, rpb, meta_bias):
    bsz, lp, _ = p.shape
    assert ((lp - HG_CHUNK) // GRID_W) % NA_GROUP == 0
    heads = NA_WIDTH // HEAD_DIM
    blk = lambda seg: pl.BlockSpec((None, lp, HEAD_DIM), lambda b, h: (b, 0, seg * heads + h))
    table = _na_bias_table(rpb)
    return pl.pallas_call(
        _na_kernel,
        grid=(bsz, heads),
        in_specs=[blk(0), blk(1), blk(2),
                  pl.BlockSpec((None, 2 * NA_KH - 2, GRID_W, 2 * GRID_W), lambda b, h: (h, 0, 0, 0)),
                  pl.BlockSpec((None, 1, N_META), lambda b, h: (h, 0, 0))],
        out_specs=pl.BlockSpec((None, lp, HEAD_DIM), lambda b, h: (b, 0, h)),
        out_shape=jax.ShapeDtypeStruct((bsz, lp, NA_WIDTH), jnp.bfloat16),
        compiler_params=pltpu.CompilerParams(
            dimension_semantics=("parallel", "parallel"), vmem_limit_bytes=V7X_VMEM_LIMIT),
        name="neighbourhood_attention",
    )(p, p, p, table, meta_bias.astype(jnp.float32).reshape(heads, 1, N_META))


def _even_mix(p, rpb, meta_bias, lb, norm_gain):
    oa = _natten(p, rpb, meta_bias)
    o_fwd, o_bwd = _hgrn2(p, 3 * NA_WIDTH, lb)
    return _even_gate(oa, o_fwd, o_bwd, p, 3 * NA_WIDTH + 4 * HG_WIDTH, norm_gain)


def kernel(x, meta_tokens, norm_pre, norm_post, ev_w_in, ev_w_out, na_rpb, na_meta_bias, hg_lower, hg_norm,
           od_w_in, od_w_out, hy_short, hy_ffn_w1, hy_ffn_b1, hy_ffn_freq1, hy_ffn_w2, hy_ffn_b2, hy_ffn_freq2,
           hy_ffn_w3, hy_bias):
    b = x.shape[0]
    depth = norm_pre.shape[0]
    f32 = jnp.float32
    lb_all = jnp.cumsum(jax.nn.softmax(hg_lower.astype(f32), axis=0), axis=0)
    lb_all = lb_all - lb_all[:1]
    h, hn = _embed(x, meta_tokens, norm_pre[0])
    h = h.reshape(b * LP, D_MODEL)
    hn = hn.reshape(b * LP, D_MODEL)
    consts = _dft_constants()
    for layer in range(depth):
        j = layer // 2
        if layer % 2 == 0:
            p = _matmul(hn, ev_w_in, j, f32).reshape(b, LP, -1)
            y = _even_mix(p, na_rpb[j], na_meta_bias[j], lb_all[j], hg_norm[j]).reshape(b * LP, D_MODEL)
            w_out = ev_w_out
        else:
            p3 = _matmul(hn, od_w_in, j, jnp.bfloat16).reshape(b, LP, -1)
            y = _odd_mix(p3, hy_short[j], hy_ffn_w1[j], hy_ffn_b1[j], hy_ffn_freq1[j], hy_ffn_w2[j],
                         hy_ffn_b2[j], hy_ffn_freq2[j], hy_ffn_w3[j], hy_bias[j], consts)
            y = y.reshape(b * LP, D_MODEL)
            w_out = od_w_out
        out = _matmul(y, w_out, j, jnp.bfloat16)
        if layer + 1 < depth:
            h, hn = _residual_rmsnorm(h, out, norm_post[layer], norm_pre[layer + 1])
    return _final_residual(h.reshape(b, LP, D_MODEL), out.reshape(b, LP, D_MODEL), norm_post[depth - 1], x.dtype)
```

```python
import functools
import math

import jax
import jax.numpy as jnp
import numpy as np
from jax import lax
from jax.experimental import pallas as pl
from jax.experimental.pallas import tpu as pltpu

D_MODEL = 4096
SEQ = 4096
N_META = 16
GRID_W = 64
HEAD_DIM = 128
NA_WIDTH = D_MODEL // 2
NA_KH = 8
NA_KW = 16
HG_WIDTH = D_MODEL // 2
HG_HEADS = HG_WIDTH // HEAD_DIM
HG_CHUNK = 64
HY_WIDTH = D_MODEL
HY_ORDER = 2
HY_EMB = 33
HY_BANDS = (HY_EMB - 1) // 2
HY_FAST_DECAY = 0.3
HY_SLOW_DECAY = 1.5
HY_TARGET = 1e-2
EPS = 1e-6

PAD = HG_CHUNK - N_META
LP = PAD + N_META + SEQ
L_TOK = N_META + SEQ

V7X_VMEM_LIMIT = 56 * 1024 * 1024


EMBED_PIECES = 5


def _embed_kernel(*refs):
    x_refs, meta_ref, g_ref, h_ref, n_ref = refs[:EMBED_PIECES], *refs[EMBED_PIECES:]
    first = jnp.concatenate([jnp.zeros((PAD, D_MODEL), jnp.float32), meta_ref[...].astype(jnp.float32)], axis=0)
    pieces = [jnp.where(pl.program_id(1) == 0, first, x_refs[0][...].astype(jnp.float32))]
    pieces += [x_ref[...].astype(jnp.float32) for x_ref in x_refs[1:]]
    h = jnp.concatenate(pieces, axis=0)
    h_ref[...] = h
    r = lax.rsqrt(jnp.mean(h * h, axis=-1, keepdims=True) + EPS)
    n_ref[...] = (h * r * g_ref[...]).astype(n_ref.dtype)


def _embed(x, meta_tokens, g):
    bsz, seq, d = x.shape
    assert seq == SEQ and LP % (EMBED_PIECES * HG_CHUNK) == 0
    tr = EMBED_PIECES * HG_CHUNK
    piece = lambda k: pl.BlockSpec((None, HG_CHUNK, d), lambda b, r: (b, jnp.maximum(EMBED_PIECES * r + k - 1, 0), 0))
    row = pl.BlockSpec((None, tr, d), lambda b, r: (b, r, 0))
    return pl.pallas_call(
        _embed_kernel,
        grid=(bsz, LP // tr),
        in_specs=[piece(k) for k in range(EMBED_PIECES)] + [pl.BlockSpec((N_META, d), lambda b, r: (0, 0)),
                                                           pl.BlockSpec((1, d), lambda b, r: (0, 0))],
        out_specs=[row, row],
        out_shape=[jax.ShapeDtypeStruct((bsz, LP, d), jnp.float32), jax.ShapeDtypeStruct((bsz, LP, d), jnp.bfloat16)],
        compiler_params=pltpu.CompilerParams(
            dimension_semantics=("parallel", "parallel"), vmem_limit_bytes=V7X_VMEM_LIMIT),
        name="embed_rmsnorm_pre",
    )(*([x] * EMBED_PIECES), meta_tokens, g.reshape(1, d))


def _post_pre_kernel(h_ref, y_ref, g_ref, gn_ref, o_ref, n_ref):
    y = y_ref[...].astype(jnp.float32)
    r = lax.rsqrt(jnp.mean(y * y, axis=-1, keepdims=True) + EPS)
    h = h_ref[...] + y * r * g_ref[...]
    o_ref[...] = h
    rn = lax.rsqrt(jnp.mean(h * h, axis=-1, keepdims=True) + EPS)
    n_ref[...] = (h * rn * gn_ref[...]).astype(n_ref.dtype)


FINAL_PIECES = 4


def _final_kernel(*refs):
    h_refs, y_refs = refs[:FINAL_PIECES], refs[FINAL_PIECES:2 * FINAL_PIECES]
    g_ref, o_ref = refs[2 * FINAL_PIECES:]
    for k, (h_ref, y_ref) in enumerate(zip(h_refs, y_refs)):
        y = y_ref[...].astype(jnp.float32)
        r = lax.rsqrt(jnp.mean(y * y, axis=-1, keepdims=True) + EPS)
        o_ref[k * HG_CHUNK:(k + 1) * HG_CHUNK, :] = (h_ref[...] + y * r * g_ref[...]).astype(o_ref.dtype)


def _final_residual(h3, y3, g, out_dtype):
    bsz, lp, d = h3.shape
    assert PAD + N_META == HG_CHUNK and SEQ % (FINAL_PIECES * HG_CHUNK) == 0
    piece = lambda k: pl.BlockSpec((None, HG_CHUNK, d), lambda b, r: (b, FINAL_PIECES * r + k + 1, 0))
    pieces = [piece(k) for k in range(FINAL_PIECES)]
    return pl.pallas_call(
        _final_kernel,
        grid=(bsz, SEQ // (FINAL_PIECES * HG_CHUNK)),
        in_specs=pieces + pieces + [pl.BlockSpec((1, d), lambda b, r: (0, 0))],
        out_specs=pl.BlockSpec((None, FINAL_PIECES * HG_CHUNK, d), lambda b, r: (b, r, 0)),
        out_shape=jax.ShapeDtypeStruct((bsz, SEQ, d), out_dtype),
        compiler_params=pltpu.CompilerParams(
            dimension_semantics=("parallel", "parallel"), vmem_limit_bytes=V7X_VMEM_LIMIT),
        name="residual_rmsnorm_post_final",
    )(*([h3] * FINAL_PIECES), *([y3] * FINAL_PIECES), g.reshape(1, d))


def _residual_rmsnorm(h2d, y2d, g, g_next, tr=320):
    m, d = h2d.shape
    row = pl.BlockSpec((tr, d), lambda i: (i, 0))
    vec = pl.BlockSpec((1, d), lambda i: (0, 0))
    params = pltpu.CompilerParams(dimension_semantics=("parallel",), vmem_limit_bytes=V7X_VMEM_LIMIT)
    return pl.pallas_call(
        _post_pre_kernel, grid=(m // tr,), in_specs=[row, row, vec, vec], out_specs=[row, row],
        out_shape=[jax.ShapeDtypeStruct((m, d), jnp.float32), jax.ShapeDtypeStruct((m, d), jnp.bfloat16)],
        compiler_params=params, name="residual_rmsnorm_post_pre",
    )(h2d, y2d, g.reshape(1, d), g_next.reshape(1, d))


def _matmul_kernel(x_ref, w_ref, o_ref):
    o_ref[...] = jnp.dot(x_ref[...], w_ref[...].astype(jnp.bfloat16),
                         preferred_element_type=jnp.float32).astype(o_ref.dtype)


def _matmul(x, w_stack, layer, out_dtype, tm=1040, tn=512):
    m, kd = x.shape
    _, _, n = w_stack.shape
    assert m % tm == 0 and n % tn == 0
    return pl.pallas_call(
        _matmul_kernel,
        grid=(m // tm, n // tn),
        in_specs=[pl.BlockSpec((tm, kd), lambda i, j: (i, 0)),
                  pl.BlockSpec((None, kd, tn), lambda i, j: (layer, 0, j))],
        out_specs=pl.BlockSpec((tm, tn), lambda i, j: (i, j)),
        out_shape=jax.ShapeDtypeStruct((m, n), out_dtype),
        compiler_params=pltpu.CompilerParams(
            dimension_semantics=("parallel", "parallel"), vmem_limit_bytes=V7X_VMEM_LIMIT),
        name="projection_matmul",
    )(x, w_stack)


FFT_N1 = 128
FFT_N2 = 65
FFT_N = FFT_N1 * FFT_N2
N2P = 80
SLAB = 264
TC = 128
MID_GROUP = 16
NC_HALF = FFT_N1 // 2 + 1
HALF_P = 72
SLAB_S = 152
C_GROUP = 13
FILL_CHUNKS = 13
B_GROUP = 13


def _dft_constants():
    a = np.arange(FFT_N1)
    c = np.arange(FFT_N1)
    w1 = np.exp(-2j * np.pi * np.outer(c, a) / FFT_N1)
    w1r, w1i = w1.real, w1.imag
    half = FFT_N1 // 2
    f1 = np.block([[w1r[:, :half], -w1i[:, :half]], [w1i[:, :half], w1r[:, :half]]])
    f1_real = np.zeros((2 * HALF_P, FFT_N1), np.float64)
    f1_real[:NC_HALF] = w1r[:NC_HALF]
    f1_real[HALF_P:HALF_P + NC_HALF] = w1i[:NC_HALF]
    g = np.conj(w1).T
    gr, gi = g.real[:half], g.imag[:half]
    i2 = np.block([[gr, -gi], [gi, gr]])
    b = np.arange(FFT_N2)
    d = np.arange(FFT_N2)
    w2 = np.exp(-2j * np.pi * np.outer(d, b) / FFT_N2)
    tw = np.exp(-2j * np.pi * np.outer(c, b) / FFT_N)
    m = w2[None, :, :] * tw[:, None, :]
    ab = np.zeros((FFT_N1, 2 * N2P, 128), np.float64)
    m[NC_HALF:] = m[NC_HALF:, ::-1, :]
    ab[:, :FFT_N2, :FFT_N2] = m.real
    ab[:, N2P:N2P + FFT_N2, :FFT_N2] = m.imag
    bf = jnp.bfloat16
    abt = np.zeros((FFT_N1, N2P, 256), np.float64)
    abt[:, :, :2 * N2P] = np.swapaxes(ab[:, :, :N2P], 1, 2)
    return (jnp.asarray(f1, bf), jnp.asarray(f1_real, bf), jnp.asarray(i2, bf), jnp.asarray(ab, bf),
            jnp.asarray(abt, bf))


def _middle_stage_inputs(y_ref, c, imag_row=FFT_N1, pitch=SLAB):
    yr = y_ref[pl.ds(c, N2P, stride=pitch), :]
    yi = y_ref[pl.ds(imag_row + c, N2P, stride=pitch), :]
    return jnp.concatenate([yr, yi], axis=1).astype(jnp.bfloat16)


def _forward_middle(ab, y2):
    y2 = jnp.concatenate([y2, jnp.zeros((128 - N2P, 2 * TC), jnp.bfloat16)], axis=0)
    q = jnp.dot(ab, y2, preferred_element_type=jnp.float32)
    zr = q[:N2P, :TC] - q[N2P:, TC:]
    zi = q[N2P:, :TC] + q[:N2P, TC:]
    return zr, zi


def _spectrum_kernel(h_ref, t_ref, w3f0_ref, w3f1_ref, w3b0_ref, w3b1_ref, delta_ref, f1_ref, ab_ref, k_ref,
                     kt_ref, y_ref):
    w3 = jnp.concatenate([jnp.concatenate([w3f0_ref[...], w3f1_ref[...]], axis=1),
                          jnp.concatenate([w3b0_ref[...], w3b1_ref[...]], axis=1)], axis=0).astype(jnp.bfloat16)
    delta = delta_ref[...]
    rows = FFT_N // FILL_CHUNKS

    def fill(i, carry):
        r0 = pl.multiple_of(i * rows, 8)
        kt = jnp.dot(h_ref[pl.ds(r0, rows), :], w3, preferred_element_type=jnp.float32)
        window = jnp.exp(t_ref[pl.ds(r0, rows), :] * delta)
        for n in range(HY_ORDER):
            kt_ref[n, pl.ds(r0, rows), :] = kt[:, n * TC:(n + 1) * TC] * window
        return carry

    lax.fori_loop(0, FILL_CHUNKS, fill, 0)
    y_ref[pl.ds(FFT_N2 * SLAB_S, (N2P - FFT_N2) * SLAB_S), :] = jnp.zeros(((N2P - FFT_N2) * SLAB_S, TC), jnp.float32)

    for n in range(HY_ORDER):
        def stage1(g, carry):
            bs = [g * B_GROUP + i for i in range(B_GROUP)]
            x = jnp.concatenate([kt_ref.at[n][pl.ds(b, FFT_N1, stride=FFT_N2), :] for b in bs], axis=1)
            y = jnp.dot(f1_ref[...], x.astype(jnp.bfloat16), preferred_element_type=jnp.float32)
            for i, b in enumerate(bs):
                y_ref[pl.ds(pl.multiple_of(b * SLAB_S, 8), 2 * HALF_P), :] = y[:, i * TC:(i + 1) * TC]
            return carry

        lax.fori_loop(0, FFT_N2 // B_GROUP, stage1, 0)

        def stage2(g, carry):
            cs = [g * C_GROUP + i for i in range(C_GROUP)]
            zs = [_forward_middle(ab_ref[c], _middle_stage_inputs(y_ref, c, HALF_P, SLAB_S)) for c in cs]
            for c, (zr, zi) in zip(cs, zs):
                k_ref[n, c] = (jnp.concatenate([zr, zi], axis=1) * (1.0 / FFT_N)).astype(jnp.bfloat16)
            return carry

        lax.fori_loop(0, NC_HALF // C_GROUP, stage2, 0)


def _filter_spectrum(h2, w3, deltas, consts):
    _, f1_real, _, ab, _ = consts
    nt = HY_WIDTH // TC
    j = np.arange(FFT_N)
    lag = np.where(j < L_TOK, j, FFT_N - j).astype(np.float32)
    neg_t = jnp.asarray(np.broadcast_to((-lag / np.float32(L_TOK - 1))[:, None], (FFT_N, TC)))
    w3_cols = lambda direction, order: pl.BlockSpec((64, TC), lambda i: (0, (direction * HY_ORDER + order) * nt + i))
    return pl.pallas_call(
        _spectrum_kernel,
        grid=(nt,),
        in_specs=[
            pl.BlockSpec((FFT_N, 128), lambda i: (0, 0)),
            pl.BlockSpec((FFT_N, TC), lambda i: (0, 0)),
            w3_cols(0, 0), w3_cols(0, 1), w3_cols(1, 0), w3_cols(1, 1),
            pl.BlockSpec((1, TC), lambda i: (0, i)),
            pl.BlockSpec((2 * HALF_P, FFT_N1), lambda i: (0, 0)),
            pl.BlockSpec((FFT_N1, 2 * N2P, 128), lambda i: (0, 0, 0)),
        ],
        out_specs=pl.BlockSpec((HY_ORDER, None, NC_HALF, N2P, 2 * TC), lambda i: (0, i, 0, 0, 0)),
        out_shape=jax.ShapeDtypeStruct((HY_ORDER, nt, NC_HALF, N2P, 2 * TC), jnp.bfloat16),
        scratch_shapes=[pltpu.VMEM((HY_ORDER, FFT_N, TC), jnp.float32), pltpu.VMEM((N2P * SLAB_S, TC), jnp.float32)],
        compiler_params=pltpu.CompilerParams(dimension_semantics=("parallel",), vmem_limit_bytes=V7X_VMEM_LIMIT),
        name="hyena_filter_spectrum",
    )(h2, neg_t, w3, w3, w3, w3, deltas, f1_real, ab)


def _conv_kernel(u_ref, k_ref, f1_ref, i2_ref, ab_ref, abt_ref, o_ref, y_ref):
    y_ref[pl.ds(FFT_N2 * SLAB, (N2P - FFT_N2) * SLAB), :] = jnp.zeros(((N2P - FFT_N2) * SLAB, TC), jnp.float32)
    half = FFT_N1 // 2

    def stage1(g, carry):
        bs = [g * B_GROUP + i for i in range(B_GROUP)]
        x = jnp.concatenate(
            [jnp.concatenate([u_ref.at[ri][pl.ds(b, half, stride=FFT_N2), :] for b in bs], axis=1) for ri in range(2)],
            axis=0)
        y = jnp.dot(f1_ref[...], x.astype(jnp.bfloat16), preferred_element_type=jnp.float32)
        for i, b in enumerate(bs):
            y_ref[pl.ds(pl.multiple_of(b * SLAB, 8), 2 * FFT_N1), :] = y[:, i * TC:(i + 1) * TC]
        return carry

    lax.fori_loop(0, FFT_N2 // B_GROUP, stage1, 0)

    def inverse_middle(c, z):
        zr, zi = z
        mirrored = c >= NC_HALF
        kk = k_ref[jnp.where(mirrored, FFT_N1 - c, c)].astype(jnp.float32)
        kr, ki = kk[:, :TC], kk[:, TC:] * jnp.where(mirrored, -1.0, 1.0)
        pr = zr * kr - zi * ki
        pi = zr * ki + zi * kr
        rhs = jnp.concatenate([jnp.concatenate([pr, pi], axis=1), jnp.concatenate([pi, -pr], axis=1),
                               jnp.zeros((256 - 2 * N2P, 2 * TC), jnp.float32)], axis=0)
        return jnp.dot(abt_ref[c], rhs.astype(jnp.bfloat16), preferred_element_type=jnp.float32)

    def middle(g, carry):
        cs = [g * MID_GROUP + i for i in range(MID_GROUP)]
        zs = [_forward_middle(ab_ref[c], _middle_stage_inputs(y_ref, c)) for c in cs]
        outs = [inverse_middle(c, z) for c, z in zip(cs, zs)]
        for c, v in zip(cs, outs):
            y_ref[pl.ds(c, N2P, stride=SLAB), :] = v[:N2P, :TC]
            y_ref[pl.ds(FFT_N1 + c, N2P, stride=SLAB), :] = v[:N2P, TC:]
        return carry

    lax.fori_loop(0, FFT_N1 // MID_GROUP, middle, 0)

    def stage4(g, carry):
        bs = [g * B_GROUP + i for i in range(B_GROUP)]
        v = jnp.concatenate([y_ref[pl.ds(pl.multiple_of(b * SLAB, 8), 2 * FFT_N1), :] for b in bs], axis=1)
        x = jnp.dot(i2_ref[...], v.astype(jnp.bfloat16), preferred_element_type=jnp.float32)
        for i, b in enumerate(bs):
            o_ref.at[0][pl.ds(b, half, stride=FFT_N2), :] = x[:half, i * TC:(i + 1) * TC]
            o_ref.at[1][pl.ds(b, half, stride=FFT_N2), :] = x[half:, i * TC:(i + 1) * TC]
        return carry

    lax.fori_loop(0, FFT_N2 // B_GROUP, stage4, 0)


def _long_conv(u, kspec, order, consts):
    f1, _, i2, ab, abt = consts
    bsz, lp, _ = u.shape
    assert bsz == 2 and lp == LP
    nt = HY_WIDTH // TC
    return pl.pallas_call(
        _conv_kernel,
        grid=(nt,),
        in_specs=[
            pl.BlockSpec((bsz, lp, TC), lambda i: (0, 0, i)),
            pl.BlockSpec((None, None, NC_HALF, N2P, 2 * TC), lambda i: (order, i, 0, 0, 0)),
            pl.BlockSpec((2 * FFT_N1, FFT_N1), lambda i: (0, 0)),
            pl.BlockSpec((FFT_N1, 2 * FFT_N1), lambda i: (0, 0)),
            pl.BlockSpec((FFT_N1, 2 * N2P, 128), lambda i: (0, 0, 0), pipeline_mode=pl.Buffered(1)),
            pl.BlockSpec((FFT_N1, N2P, 256), lambda i: (0, 0, 0), pipeline_mode=pl.Buffered(1)),
        ],
        out_specs=pl.BlockSpec((bsz, lp, TC), lambda i: (0, 0, i)),
        out_shape=jax.ShapeDtypeStruct((bsz, lp, HY_WIDTH), jnp.float32),
        scratch_shapes=[pltpu.VMEM((N2P * SLAB, TC), jnp.float32)],
        compiler_params=pltpu.CompilerParams(dimension_semantics=("parallel",), vmem_limit_bytes=V7X_VMEM_LIMIT),
        name="hyena_long_conv",
    )(u, kspec, f1, i2, ab, abt)


def _short_conv_rows(u, w):
    y = pltpu.roll(u, 1, 0) * w[0:1] + u * w[1:2] + pltpu.roll(u, LP - 1, 0) * w[2:3]
    row = lax.broadcasted_iota(jnp.int32, y.shape, 0)
    return jnp.where(row >= PAD, y, 0.0)


def _short_conv_kernel(u_ref, w_ref, o_ref):
    o_ref[...] = _short_conv_rows(u_ref[...].astype(jnp.float32), w_ref[...])


def _gate1_kernel(v_ref, u_ref, w_ref, y_ref, b_ref, o_ref):
    x = _short_conv_rows(u_ref[...].astype(jnp.float32), w_ref[...])
    o_ref[...] = x * (y_ref[...] + v_ref[...] * b_ref[...])


def _gate2_kernel(z_ref, u_ref, w_ref, y_ref, b_ref, g_ref, o_ref):
    g = g_ref[...].astype(jnp.float32)
    z = _short_conv_rows(u_ref[...].astype(jnp.float32), w_ref[...]) * (y_ref[...] + z_ref[...] * b_ref[...])
    o_ref[...] = (z * (g * jax.nn.sigmoid(g))).astype(o_ref.dtype)


def _seq_tile(col_block):
    return pl.BlockSpec((None, LP, TC), lambda b, i: (b, 0, col_block + i))


def _vec_tile(rows, col_block):
    return pl.BlockSpec((rows, TC), lambda b, i: (0, col_block + i))


def _short_conv(p3, w_short_t):
    bsz = p3.shape[0]
    return pl.pallas_call(
        _short_conv_kernel,
        grid=(bsz, HY_WIDTH // TC),
        in_specs=[_seq_tile(0), _vec_tile(3, 0)],
        out_specs=_seq_tile(0),
        out_shape=jax.ShapeDtypeStruct((bsz, LP, HY_WIDTH), jnp.float32),
        compiler_params=pltpu.CompilerParams(
            dimension_semantics=("parallel", "parallel"), vmem_limit_bytes=V7X_VMEM_LIMIT),
        name="hyena_short_conv",
    )(p3, w_short_t)


def _hyena_gate1(v, p3, w_short_t, y, bias):
    bsz = p3.shape[0]
    nt = HY_WIDTH // TC
    return pl.pallas_call(
        _gate1_kernel,
        grid=(bsz, nt),
        in_specs=[_seq_tile(0), _seq_tile(nt), _vec_tile(3, nt), _seq_tile(0), _vec_tile(1, 0)],
        out_specs=_seq_tile(0),
        out_shape=jax.ShapeDtypeStruct((bsz, LP, HY_WIDTH), jnp.float32),
        compiler_params=pltpu.CompilerParams(
            dimension_semantics=("parallel", "parallel"), vmem_limit_bytes=V7X_VMEM_LIMIT),
        name="hyena_gate1",
    )(v, p3, w_short_t, y, bias.reshape(1, HY_WIDTH))


def _hyena_gate2(z1, p3, w_short_t, y, bias):
    bsz = p3.shape[0]
    nt = HY_WIDTH // TC
    return pl.pallas_call(
        _gate2_kernel,
        grid=(bsz, nt),
        in_specs=[_seq_tile(0), _seq_tile(2 * nt), _vec_tile(3, 2 * nt), _seq_tile(0), _vec_tile(1, 0),
                  _seq_tile(3 * nt)],
        out_specs=_seq_tile(0),
        out_shape=jax.ShapeDtypeStruct((bsz, LP, HY_WIDTH), jnp.bfloat16),
        compiler_params=pltpu.CompilerParams(
            dimension_semantics=("parallel", "parallel"), vmem_limit_bytes=V7X_VMEM_LIMIT),
        name="hyena_gate2",
    )(z1, p3, w_short_t, y, bias.reshape(1, HY_WIDTH), p3)


def _filter_positions():
    f32 = np.float32
    l = L_TOK
    t = np.linspace(0.0, 1.0, l, dtype=f32)[:, None]
    w = f32(2.0 * math.pi) * np.arange(l, dtype=f32)[:, None] / f32(l)
    bands = np.linspace(1e-4, HY_BANDS - 1, HY_BANDS, dtype=f32)[None, :]
    z = np.concatenate([t, np.cos(bands * w), -np.sin(bands * w)], axis=-1).astype(f32)
    j = np.arange(FFT_N)
    lag = np.minimum(np.where(j < l, j, FFT_N - j), l - 1)
    zrows = np.zeros((FFT_N, 128), f32)
    zrows[:, :HY_EMB] = z[lag]
    mask = np.zeros((FFT_N, 128), f32)
    mask[:, :64] = (j < l)[:, None]
    mask[:, 64:] = ((j > FFT_N - l) | (j == 0))[:, None]
    return jnp.asarray(zrows), jnp.asarray(mask)


def _hidden_kernel(z_ref, m_ref, w1_ref, b1_ref, fr1_ref, w2_ref, b2_ref, fr2_ref, o_ref):
    hi = lax.Precision.HIGHEST
    a = jnp.dot(z_ref[...], w1_ref[...], precision=hi, preferred_element_type=jnp.float32) + b1_ref[...]
    hid = jnp.sin(fr1_ref[...] * a)
    a = jnp.dot(hid, w2_ref[...], precision=hi, preferred_element_type=jnp.float32) + b2_ref[...]
    hid = jnp.sin(fr2_ref[...] * a)
    o_ref[...] = (jnp.concatenate([hid, hid], axis=1) * m_ref[...]).astype(o_ref.dtype)


def _hyena_hidden(w1, b1, fr1, w2, b2, fr2):
    zrows, mask = _filter_positions()
    rows = FFT_N // FILL_CHUNKS
    hdim = w2.shape[0]
    w1p = jnp.zeros((128, hdim), jnp.float32).at[:HY_EMB].set(w1.astype(jnp.float32))
    tile = pl.BlockSpec((rows, 128), lambda i: (i, 0))
    full = lambda a: pl.BlockSpec(a.shape, lambda i: (0,) * a.ndim)
    vecs = [v.astype(jnp.float32).reshape(1, hdim) for v in (b1, fr1, b2, fr2)]
    w2f = w2.astype(jnp.float32)
    return pl.pallas_call(
        _hidden_kernel,
        grid=(FILL_CHUNKS,),
        in_specs=[tile, tile, full(w1p), full(vecs[0]), full(vecs[1]), full(w2f), full(vecs[2]), full(vecs[3])],
        out_specs=tile,
        out_shape=jax.ShapeDtypeStruct((FFT_N, 128), jnp.bfloat16),
        compiler_params=pltpu.CompilerParams(dimension_semantics=("parallel",), vmem_limit_bytes=V7X_VMEM_LIMIT),
        name="hyena_filter_hidden",
    )(zrows, mask, w1p, vecs[0], vecs[1], w2f, vecs[2], vecs[3])


def _odd_mix(p3, w_short, w1, b1, fr1, w2, b2, fr2, w3, bias, consts):
    min_decay = math.log(HY_TARGET) / HY_SLOW_DECAY
    max_decay = math.log(HY_TARGET) / HY_FAST_DECAY
    deltas = jnp.abs(jnp.linspace(min_decay, max_decay, HY_WIDTH, dtype=jnp.float32)).reshape(1, HY_WIDTH)
    kspec = _filter_spectrum(_hyena_hidden(w1, b1, fr1, w2, b2, fr2), w3, deltas, consts)
    w_short_t = w_short.T
    v = _short_conv(p3, w_short_t)
    y1 = _long_conv(v, kspec, 0, consts)
    z1 = _hyena_gate1(v, p3, w_short_t, y1, bias[0])
    y2 = _long_conv(z1, kspec, 1, consts)
    return _hyena_gate2(z1, p3, w_short_t, y2, bias[1])


HG_HB = 16
HG_LEVELS = (32, 16, 8, 4, 2, 1)
LOG2_E = 1.4426950408889634


def _cumsum_rows(g, rev):
    f32, bf = jnp.float32, jnp.bfloat16
    t = lax.broadcasted_iota(jnp.int32, (HG_CHUNK, 3 * HG_CHUNK), 0)
    s = lax.broadcasted_iota(jnp.int32, (HG_CHUNK, 3 * HG_CHUNK), 1) & (HG_CHUNK - 1)
    tri = jnp.where((s >= t) if rev else (s <= t), 1.0, 0.0).astype(bf)
    hi = g.astype(bf)
    r1 = g - hi.astype(f32)
    mid = r1.astype(bf)
    lo = (r1 - mid.astype(f32)).astype(bf)
    return jnp.dot(tri, jnp.concatenate([hi, mid, lo], axis=0), preferred_element_type=f32)


def _hgrn_row_masks(rev):
    row = lax.broadcasted_iota(jnp.int32, (HG_CHUNK, HEAD_DIM), 0)
    masks = {}
    for m in HG_LEVELS:
        upper = (row & m) != 0
        is_query = jnp.logical_not(upper) if rev else upper
        target = m if rev else m - 1
        selectors = [(src, (row & (2 * m - 1)) == src) for src in range(2 * m) if src != target] if m == 2 else None
        masks[m] = (is_query, jnp.where(is_query, LOG2_E, -LOG2_E), selectors)
    return masks


def _level_reference(gc, m, rev, selectors):
    if m >= 4:
        g3 = gc.reshape(HG_CHUNK // (2 * m), 2 * m, HEAD_DIM)
        ref = g3[:, m:m + 1, :] if rev else g3[:, m - 1:m, :]
        return jnp.broadcast_to(ref, g3.shape).reshape(HG_CHUNK, HEAD_DIM)
    target = m if rev else m - 1
    out = gc
    for src, picks_src in selectors:
        out = jnp.where(picks_src, pltpu.roll(gc, (src - target) % HG_CHUNK, 0), out)
    return out


def _hgrn_gates(q_raw, f_raw, la, lc, omlb, rev):
    q = q_raw * (0.5 * HEAD_DIM ** -0.5) * (1.0 + jnp.tanh(0.5 * q_raw))
    k = (0.5 * omlb) * (1.0 - jnp.tanh(0.5 * f_raw))
    log_sig = jnp.minimum(f_raw, 0.0) - jnp.log(1.0 + jnp.exp(-jnp.abs(f_raw)))
    b = lc + log_sig
    g = jnp.maximum(la, b) + jnp.log(1.0 + jnp.exp(-jnp.abs(la - b)))
    return q, k, g, _cumsum_rows(g, rev)


def _hgrn_level_operands(q, k, g, gc, rev, masks):
    bf = jnp.bfloat16
    qb, kb = q.astype(bf), k.astype(bf)
    ops = []
    for m in HG_LEVELS:
        is_query, sign_log2e, selectors = masks[m]
        if m == 1:
            ops.append(jnp.where(is_query, qb * jnp.exp2(g * LOG2_E).astype(bf), kb))
        else:
            gref = _level_reference(gc, m, rev, selectors)
            ops.append(jnp.where(is_query, qb, kb) * jnp.exp2((gc - gref) * sign_log2e).astype(bf))
    return ops


def _hgrn_level_id(rev):
    t = lax.broadcasted_iota(jnp.int32, (HG_CHUNK, HG_CHUNK), 0)
    s = lax.broadcasted_iota(jnp.int32, (HG_CHUNK, HG_CHUNK), 1)
    vis = (s >= t) if rev else (s <= t)
    x = t ^ s
    lvl = jnp.full((HG_CHUNK, HG_CHUNK), len(HG_LEVELS), jnp.int32)
    for i, m in enumerate(HG_LEVELS):
        lvl = jnp.where((x & m) != 0, jnp.minimum(lvl, i), lvl)
    return jnp.where(vis, lvl, -1)


def _hgrn_kernel(qf_ref, if_ref, ff_ref, qb_ref, ib_ref, fb_ref, la_ref, lc_ref, om_ref, of_ref, ob_ref, s_ref):
    f32, bf = jnp.float32, jnp.bfloat16
    nt = (((1,), (1,)), ((), ()))
    tn = (((0,), (0,)), ((), ()))

    @pl.when(pl.program_id(2) == 0)
    def _():
        s_ref[...] = jnp.zeros_like(s_ref)

    row_masks = [_hgrn_row_masks(False), _hgrn_row_masks(True)]
    refs = ((qf_ref, if_ref, ff_ref, of_ref), (qb_ref, ib_ref, fb_ref, ob_ref))
    inst = [(d, h) for d in range(2) for h in range(HG_HB)]
    level_masks = [[lvl == i for i in range(len(HG_LEVELS) + 1)]
                   for lvl in (_hgrn_level_id(False), _hgrn_level_id(True))]

    pre = []
    for d, h in inst:
        q_ref, i_ref, f_ref, _ = refs[d]
        sl = slice(h * HEAD_DIM, (h + 1) * HEAD_DIM)
        q, k, g, gc = _hgrn_gates(q_ref[:, sl], f_ref[:, sl], la_ref[d:d + 1, sl], lc_ref[d:d + 1, sl],
                                  om_ref[d:d + 1, sl], d == 1)
        g_end = gc[0:1, :] if d == 1 else gc[HG_CHUNK - 1:HG_CHUNK, :]
        pre.append(dict(ops=_hgrn_level_operands(q, k, g, gc, d == 1, row_masks[d]), qk=(q.astype(bf), k.astype(bf)),
                        qe=(q * jnp.exp(gc)).astype(bf), kd=(k * jnp.exp(g_end - gc)).astype(bf),
                        decay=jnp.exp(g_end), v=i_ref[:, sl].astype(bf)))

    attn = []
    for (d, h), pr in zip(inst, pre):
        prods = [lax.dot_general(x, x, nt, preferred_element_type=f32) for x in pr["ops"]]
        prods.append(lax.dot_general(pr["qk"][0], pr["qk"][1], nt, preferred_element_type=f32))
        a = jnp.zeros((HG_CHUNK, HG_CHUNK), f32)
        for i, pmat in enumerate(prods):
            a = jnp.where(level_masks[d][i], pmat, a)
        attn.append(a.astype(bf))

    for (d, h), pr, a in zip(inst, pre, attn):
        o_ref = refs[d][3]
        sl = slice(h * HEAD_DIM, (h + 1) * HEAD_DIM)
        state_t = s_ref[d, h]
        o = jnp.dot(a, pr["v"], preferred_element_type=f32)
        o = o + lax.dot_general(pr["qe"], state_t.astype(bf), nt, preferred_element_type=f32)
        o_ref[:, sl] = o.astype(o_ref.dtype)
        s_ref[d, h] = pr["decay"] * state_t + lax.dot_general(pr["v"], pr["kd"], tn, preferred_element_type=f32)


def _hgrn2(p, col0, lb):
    bsz, lp, _ = p.shape
    nchunk = lp // HG_CHUNK
    cw = HG_HB * HEAD_DIM
    ng = HG_WIDTH // cw
    off = col0 // cw
    la = jnp.log(lb)
    lc = jnp.log1p(-lb)
    om = 1.0 - lb
    fwd = lambda seg: pl.BlockSpec((None, HG_CHUNK, cw), lambda b, g, c: (b, c, off + seg * ng + g))
    bwd = lambda seg: pl.BlockSpec((None, HG_CHUNK, cw), lambda b, g, c: (b, nchunk - 1 - c, off + seg * ng + g))
    par = pl.BlockSpec((2, cw), lambda b, g, c: (0, g))
    return pl.pallas_call(
        _hgrn_kernel,
        grid=(bsz, ng, nchunk),
        in_specs=[fwd(0), fwd(1), fwd(2), bwd(0), bwd(1), bwd(3), par, par, par],
        out_specs=[pl.BlockSpec((None, HG_CHUNK, cw), lambda b, g, c: (b, c, g)),
                   pl.BlockSpec((None, HG_CHUNK, cw), lambda b, g, c: (b, nchunk - 1 - c, g))],
        out_shape=[jax.ShapeDtypeStruct((bsz, lp, HG_WIDTH), jnp.bfloat16)] * 2,
        scratch_shapes=[pltpu.VMEM((2, HG_HB, HEAD_DIM, HEAD_DIM), jnp.float32)],
        compiler_params=pltpu.CompilerParams(
            dimension_semantics=("parallel", "parallel", "arbitrary"), vmem_limit_bytes=V7X_VMEM_LIMIT),
        name="hgrn2_bidirectional",
    )(p, p, p, p, p, p, la, lc, om)


EG_ROWS = 320


def _even_gate_kernel(oa_ref, of_ref, ob_ref, ga_ref, gb_ref, gain_ref, y_ref):
    ga = ga_ref[...]
    y_ref[:, :NA_WIDTH] = (oa_ref[...].astype(jnp.float32) * (ga * jax.nn.sigmoid(ga))).astype(y_ref.dtype)
    for h in range(HG_HEADS):
        sl = slice(h * HEAD_DIM, (h + 1) * HEAD_DIM)
        oh = of_ref[:, sl].astype(jnp.float32) + ob_ref[:, sl].astype(jnp.float32)
        oh = oh * lax.rsqrt(jnp.mean(oh * oh, axis=-1, keepdims=True) + EPS) * gain_ref[:, sl]
        gb = gb_ref[:, sl]
        y_ref[:, NA_WIDTH + h * HEAD_DIM:NA_WIDTH + (h + 1) * HEAD_DIM] = (
            oh * (gb * jax.nn.sigmoid(gb))).astype(y_ref.dtype)


def _even_gate(oa, o_fwd, o_bwd, p, gate_col0, norm_gain):
    bsz, lp, _ = oa.shape
    assert gate_col0 % NA_WIDTH == 0
    goff = gate_col0 // NA_WIDTH
    half = lambda off: pl.BlockSpec((None, EG_ROWS, NA_WIDTH), lambda b, r: (b, r, off))
    return pl.pallas_call(
        _even_gate_kernel,
        grid=(bsz, lp // EG_ROWS),
        in_specs=[half(0), half(0), half(0), half(goff), half(goff + 1),
                  pl.BlockSpec((1, HG_WIDTH), lambda b, r: (0, 0))],
        out_specs=pl.BlockSpec((None, EG_ROWS, D_MODEL), lambda b, r: (b, r, 0)),
        out_shape=jax.ShapeDtypeStruct((bsz, lp, D_MODEL), jnp.bfloat16),
        compiler_params=pltpu.CompilerParams(
            dimension_semantics=("parallel", "parallel"), vmem_limit_bytes=V7X_VMEM_LIMIT),
        name="even_gate",
    )(oa, o_fwd, o_bwd, p, p, norm_gain.reshape(1, HG_WIDTH))


NA_MASKED = -1e30
NA_GROUP = 16


def _na_bias_table(rpb):
    cols = np.arange(GRID_W)
    c0 = np.clip(cols - NA_KW // 2, 0, GRID_W - NA_KW)
    col_ok = (cols[None, :] >= c0[:, None]) & (cols[None, :] < c0[:, None] + NA_KW)
    col_idx = np.clip(cols[None, :] - cols[:, None] + NA_KW - 1, 0, 2 * NA_KW - 2)
    onehot = (col_idx[None] == np.arange(2 * NA_KW - 1)[:, None, None]).astype(np.float32)
    picked = jnp.einsum('hrj,jqk->hrqk', rpb.astype(jnp.float32), onehot, precision=lax.Precision.HIGHEST)
    table = jnp.where(col_ok[None, None], picked, NA_MASKED)
    return jnp.concatenate([table[:, :-1], table[:, 1:]], axis=-1)


def _na_kernel(q_ref, k_ref, v_ref, t_ref, mb_ref, o_ref):
    f32, bf = jnp.float32, jnp.bfloat16
    rows = (q_ref.shape[0] - HG_CHUNK) // GRID_W
    scale = HEAD_DIM ** -0.5
    nt = (((1,), (1,)), ((), ()))
    km = k_ref[PAD:PAD + N_META, :].astype(bf)
    vm = v_ref[PAD:PAD + N_META, :].astype(bf)
    mb = mb_ref[...]
    o_ref[0:PAD, :] = jnp.zeros((PAD, HEAD_DIM), o_ref.dtype)

    qm = q_ref[PAD:PAD + N_META, :].astype(bf)
    s = lax.dot_general(qm, km, nt, preferred_element_type=f32) * scale + mb
    e = jnp.exp(s - jnp.max(s, axis=-1, keepdims=True))
    o_ref[PAD:PAD + N_META, :] = (jnp.dot(e.astype(bf), vm, preferred_element_type=f32)
                                  / jnp.sum(e, axis=-1, keepdims=True)).astype(o_ref.dtype)

    def row_group(g, carry):
        rs = [g * NA_GROUP + i for i in range(NA_GROUP)]
        scores = []
        for r in rs:
            r0 = jnp.clip(r - NA_KH // 2, 0, rows - NA_KH)
            q = q_ref[pl.ds(pl.multiple_of(HG_CHUNK + GRID_W * r, GRID_W), GRID_W), :].astype(bf)
            k0 = pl.multiple_of(HG_CHUNK + GRID_W * r0, GRID_W)
            kw = k_ref[pl.ds(k0, NA_KH * GRID_W), :].astype(bf)
            s_win = lax.dot_general(q, kw, nt, preferred_element_type=f32)
            s_meta = lax.dot_general(q, km, nt, preferred_element_type=f32)
            scores.append((r, r0, k0, s_win, s_meta))
        probs = []
        for r, r0, k0, s_win, s_meta in scores:
            bias = jnp.concatenate([t_ref[r0 + i - r + NA_KH - 1] for i in range(0, NA_KH, 2)], axis=1)
            s_win = s_win * scale + bias
            s_meta = s_meta * scale + mb
            m = jnp.maximum(jnp.max(s_win, axis=-1, keepdims=True), jnp.max(s_meta, axis=-1, keepdims=True))
            pw = jnp.exp(s_win - m)
            pm = jnp.exp(s_meta - m)
            den = jnp.sum(pw, axis=-1, keepdims=True) + jnp.sum(pm, axis=-1, keepdims=True)
            probs.append((r, k0, pw.astype(bf), pm.astype(bf), den))
        for r, k0, pw, pm, den in probs:
            vw = v_ref[pl.ds(k0, NA_KH * GRID_W), :].astype(bf)
            o = jnp.dot(pw, vw, preferred_element_type=f32) + jnp.dot(pm, vm, preferred_element_type=f32)
            o_ref[pl.ds(pl.multiple_of(HG_CHUNK + GRID_W * r, GRID_W), GRID_W), :] = (o / den).astype(o_ref.dtype)
        return carry

    lax.fori_loop(0, rows // NA_GROUP, row_group, 0)


def _natten(p, rpb, meta_bias):
    bsz, lp, _ = p.shape
    assert ((lp - HG_CHUNK) // GRID_W) % NA_GROUP == 0
    heads = NA_WIDTH // HEAD_DIM
    blk = lambda seg: pl.BlockSpec((None, lp, HEAD_DIM), lambda b, h: (b, 0, seg * heads + h))
    table = _na_bias_table(rpb)
    return pl.pallas_call(
        _na_kernel,
        grid=(bsz, heads),
        in_specs=[blk(0), blk(1), blk(2),
                  pl.BlockSpec((None, 2 * NA_KH - 2, GRID_W, 2 * GRID_W), lambda b, h: (h, 0, 0, 0)),
                  pl.BlockSpec((None, 1, N_META), lambda b, h: (h, 0, 0))],
        out_specs=pl.BlockSpec((None, lp, HEAD_DIM), lambda b, h: (b, 0, h)),
        out_shape=jax.ShapeDtypeStruct((bsz, lp, NA_WIDTH), jnp.bfloat16),
        compiler_params=pltpu.CompilerParams(
            dimension_semantics=("parallel", "parallel"), vmem_limit_bytes=V7X_VMEM_LIMIT),
        name="neighbourhood_attention",
    )(p, p, p, table, meta_bias.astype(jnp.float32).reshape(heads, 1, N_META))


def _even_mix(p, rpb, meta_bias, lb, norm_gain):
    oa = _natten(p, rpb, meta_bias)
    o_fwd, o_bwd = _hgrn2(p, 3 * NA_WIDTH, lb)
    return _even_gate(oa, o_fwd, o_bwd, p, 3 * NA_WIDTH + 4 * HG_WIDTH, norm_gain)


def kernel(x, meta_tokens, norm_pre, norm_post, ev_w_in, ev_w_out, na_rpb, na_meta_bias, hg_lower, hg_norm,
           od_w_in, od_w_out, hy_short, hy_ffn_w1, hy_ffn_b1, hy_ffn_freq1, hy_ffn_w2, hy_ffn_b2, hy_ffn_freq2,
           hy_ffn_w3, hy_bias):
    b = x.shape[0]
    depth = norm_pre.shape[0]
    f32 = jnp.float32
    lb_all = jnp.cumsum(jax.nn.softmax(hg_lower.astype(f32), axis=0), axis=0)
    lb_all = lb_all - lb_all[:1]
    h, hn = _embed(x, meta_tokens, norm_pre[0])
    h = h.reshape(b * LP, D_MODEL)
    hn = hn.reshape(b * LP, D_MODEL)
    consts = _dft_constants()
    for layer in range(depth):
        j = layer // 2
        if layer % 2 == 0:
            p = _matmul(hn, ev_w_in, j, f32).reshape(b, LP, -1)
            y = _even_mix(p, na_rpb[j], na_meta_bias[j], lb_all[j], hg_norm[j]).reshape(b * LP, D_MODEL)
            w_out = ev_w_out
        else:
            p3 = _matmul(hn, od_w_in, j, jnp.bfloat16).reshape(b, LP, -1)
            y = _odd_mix(p3, hy_short[j], hy_ffn_w1[j], hy_ffn_b1[j], hy_ffn_freq1[j], hy_ffn_w2[j],
                         hy_ffn_b2[j], hy_ffn_freq2[j], hy_ffn_w3[j], hy_bias[j], consts)
            y = y.reshape(b * LP, D_MODEL)
            w_out = od_w_out
        out = _matmul(y, w_out, j, jnp.bfloat16)
        if layer + 1 < depth:
            h, hn = _residual_rmsnorm(h, out, norm_post[layer], norm_pre[layer + 1])
    return _final_residual(h.reshape(b, LP, D_MODEL), out.reshape(b, LP, D_MODEL), norm_post[depth - 1], x.dtype)
```

```python
import math

import jax
import jax.numpy as jnp
import numpy as np
from jax import lax
from jax.experimental import pallas as pl
from jax.experimental.pallas import tpu as pltpu

D_MODEL = 4096
SEQ = 4096
N_META = 16
GRID_W = 64
HEAD_DIM = 128
NA_WIDTH = D_MODEL // 2
NA_KH = 8
NA_KW = 16
HG_WIDTH = D_MODEL // 2
HG_HEADS = HG_WIDTH // HEAD_DIM
HG_CHUNK = 64
HY_WIDTH = D_MODEL
HY_ORDER = 2
HY_EMB = 33
HY_BANDS = (HY_EMB - 1) // 2
HY_FAST_DECAY = 0.3
HY_SLOW_DECAY = 1.5
HY_TARGET = 1e-2
EPS = 1e-6

PAD = HG_CHUNK - N_META
LP = PAD + N_META + SEQ
L_TOK = N_META + SEQ

V7X_VMEM_LIMIT = 56 * 1024 * 1024


EMBED_PIECES = 5


def _embed_kernel(*refs):
    x_refs, meta_ref, g_ref, h_ref, n_ref = refs[:EMBED_PIECES], *refs[EMBED_PIECES:]
    first = jnp.concatenate([jnp.zeros((PAD, D_MODEL), jnp.float32), meta_ref[...].astype(jnp.float32)], axis=0)
    pieces = [jnp.where(pl.program_id(1) == 0, first, x_refs[0][...].astype(jnp.float32))]
    pieces += [x_ref[...].astype(jnp.float32) for x_ref in x_refs[1:]]
    h = jnp.concatenate(pieces, axis=0)
    h_ref[...] = h
    r = lax.rsqrt(jnp.mean(h * h, axis=-1, keepdims=True) + EPS)
    n_ref[...] = (h * r * g_ref[...]).astype(n_ref.dtype)


def _embed(x, meta_tokens, g):
    bsz, seq, d = x.shape
    assert seq == SEQ and LP % (EMBED_PIECES * HG_CHUNK) == 0
    tr = EMBED_PIECES * HG_CHUNK
    piece = lambda k: pl.BlockSpec((None, HG_CHUNK, d), lambda b, r: (b, jnp.maximum(EMBED_PIECES * r + k - 1, 0), 0))
    row = pl.BlockSpec((None, tr, d), lambda b, r: (b, r, 0))
    return pl.pallas_call(
        _embed_kernel,
        grid=(bsz, LP // tr),
        in_specs=[piece(k) for k in range(EMBED_PIECES)] + [pl.BlockSpec((N_META, d), lambda b, r: (0, 0)),
                                                           pl.BlockSpec((1, d), lambda b, r: (0, 0))],
        out_specs=[row, row],
        out_shape=[jax.ShapeDtypeStruct((bsz, LP, d), jnp.float32), jax.ShapeDtypeStruct((bsz, LP, d), jnp.bfloat16)],
        compiler_params=pltpu.CompilerParams(
            dimension_semantics=("parallel", "parallel"), vmem_limit_bytes=V7X_VMEM_LIMIT),
        name="embed_rmsnorm_pre",
    )(*([x] * EMBED_PIECES), meta_tokens, g.reshape(1, d))


def _post_pre_kernel(h_ref, y_ref, g_ref, gn_ref, o_ref, n_ref):
    y = y_ref[...].astype(jnp.float32)
    r = lax.rsqrt(jnp.mean(y * y, axis=-1, keepdims=True) + EPS)
    h = h_ref[...] + y * r * g_ref[...]
    o_ref[...] = h
    rn = lax.rsqrt(jnp.mean(h * h, axis=-1, keepdims=True) + EPS)
    n_ref[...] = (h * rn * gn_ref[...]).astype(n_ref.dtype)


FINAL_PIECES = 4


def _final_kernel(*refs):
    h_refs, y_refs = refs[:FINAL_PIECES], refs[FINAL_PIECES:2 * FINAL_PIECES]
    g_ref, o_ref = refs[2 * FINAL_PIECES:]
    for k, (h_ref, y_ref) in enumerate(zip(h_refs, y_refs)):
        y = y_ref[...].astype(jnp.float32)
        r = lax.rsqrt(jnp.mean(y * y, axis=-1, keepdims=True) + EPS)
        o_ref[k * HG_CHUNK:(k + 1) * HG_CHUNK, :] = (h_ref[...] + y * r * g_ref[...]).astype(o_ref.dtype)


def _final_residual(h3, y3, g, out_dtype):
    bsz, lp, d = h3.shape
    assert PAD + N_META == HG_CHUNK and SEQ % (FINAL_PIECES * HG_CHUNK) == 0
    piece = lambda k: pl.BlockSpec((None, HG_CHUNK, d), lambda b, r: (b, FINAL_PIECES * r + k + 1, 0))
    pieces = [piece(k) for k in range(FINAL_PIECES)]
    return pl.pallas_call(
        _final_kernel,
        grid=(bsz, SEQ // (FINAL_PIECES * HG_CHUNK)),
        in_specs=pieces + pieces + [pl.BlockSpec((1, d), lambda b, r: (0, 0))],
        out_specs=pl.BlockSpec((None, FINAL_PIECES * HG_CHUNK, d), lambda b, r: (b, r, 0)),
        out_shape=jax.ShapeDtypeStruct((bsz, SEQ, d), out_dtype),
        compiler_params=pltpu.CompilerParams(
            dimension_semantics=("parallel", "parallel"), vmem_limit_bytes=V7X_VMEM_LIMIT),
        name="residual_rmsnorm_post_final",
    )(*([h3] * FINAL_PIECES), *([y3] * FINAL_PIECES), g.reshape(1, d))


def _residual_rmsnorm(h2d, y2d, g, g_next, tr=320):
    m, d = h2d.shape
    row = pl.BlockSpec((tr, d), lambda i: (i, 0))
    vec = pl.BlockSpec((1, d), lambda i: (0, 0))
    params = pltpu.CompilerParams(dimension_semantics=("parallel",), vmem_limit_bytes=V7X_VMEM_LIMIT)
    return pl.pallas_call(
        _post_pre_kernel, grid=(m // tr,), in_specs=[row, row, vec, vec], out_specs=[row, row],
        out_shape=[jax.ShapeDtypeStruct((m, d), jnp.float32), jax.ShapeDtypeStruct((m, d), jnp.bfloat16)],
        compiler_params=params, name="residual_rmsnorm_post_pre",
    )(h2d, y2d, g.reshape(1, d), g_next.reshape(1, d))


def _matmul_kernel(x_ref, w_ref, o_ref):
    o_ref[...] = jnp.dot(x_ref[...], w_ref[...].astype(jnp.bfloat16),
                         preferred_element_type=jnp.float32).astype(o_ref.dtype)


def _matmul(x, w_stack, layer, out_dtype, tm=1040, tn=512):
    m, kd = x.shape
    _, _, n = w_stack.shape
    assert m % tm == 0 and n % tn == 0
    return pl.pallas_call(
        _matmul_kernel,
        grid=(m // tm, n // tn),
        in_specs=[pl.BlockSpec((tm, kd), lambda i, j: (i, 0)),
                  pl.BlockSpec((None, kd, tn), lambda i, j: (layer, 0, j))],
        out_specs=pl.BlockSpec((tm, tn), lambda i, j: (i, j)),
        out_shape=jax.ShapeDtypeStruct((m, n), out_dtype),
        compiler_params=pltpu.CompilerParams(
            dimension_semantics=("parallel", "parallel"), vmem_limit_bytes=V7X_VMEM_LIMIT),
        name="projection_matmul",
    )(x, w_stack)


FFT_N1 = 128
FFT_N2 = 65
FFT_N = FFT_N1 * FFT_N2
N2P = 80
SLAB = 264
TC = 128
MID_GROUP = 16
NC_HALF = FFT_N1 // 2 + 1
HALF_P = 72
SLAB_S = 152
C_GROUP = 13
FILL_CHUNKS = 13
B_GROUP = 13


def _dft_constants():
    a = np.arange(FFT_N1)
    c = np.arange(FFT_N1)
    w1 = np.exp(-2j * np.pi * np.outer(c, a) / FFT_N1)
    w1r, w1i = w1.real, w1.imag
    half = FFT_N1 // 2
    f1 = np.block([[w1r[:, :half], -w1i[:, :half]], [w1i[:, :half], w1r[:, :half]]])
    f1_real = np.zeros((2 * HALF_P, FFT_N1), np.float64)
    f1_real[:NC_HALF] = w1r[:NC_HALF]
    f1_real[HALF_P:HALF_P + NC_HALF] = w1i[:NC_HALF]
    g = np.conj(w1).T
    gr, gi = g.real[:half], g.imag[:half]
    i2 = np.block([[gr, -gi], [gi, gr]])
    b = np.arange(FFT_N2)
    d = np.arange(FFT_N2)
    w2 = np.exp(-2j * np.pi * np.outer(d, b) / FFT_N2)
    tw = np.exp(-2j * np.pi * np.outer(c, b) / FFT_N)
    m = w2[None, :, :] * tw[:, None, :]
    ab = np.zeros((FFT_N1, 2 * N2P, 128), np.float64)
    m[NC_HALF:] = m[NC_HALF:, ::-1, :]
    ab[:, :FFT_N2, :FFT_N2] = m.real
    ab[:, N2P:N2P + FFT_N2, :FFT_N2] = m.imag
    bf = jnp.bfloat16
    abt = np.zeros((FFT_N1, N2P, 256), np.float64)
    abt[:, :, :2 * N2P] = np.swapaxes(ab[:, :, :N2P], 1, 2)
    return (jnp.asarray(f1, bf), jnp.asarray(f1_real, bf), jnp.asarray(i2, bf), jnp.asarray(ab, bf),
            jnp.asarray(abt, bf))


def _middle_stage_inputs(y_ref, c, imag_row=FFT_N1, pitch=SLAB):
    yr = y_ref[pl.ds(c, N2P, stride=pitch), :]
    yi = y_ref[pl.ds(imag_row + c, N2P, stride=pitch), :]
    return jnp.concatenate([yr, yi], axis=1).astype(jnp.bfloat16)


def _forward_middle(ab, y2):
    y2 = jnp.concatenate([y2, jnp.zeros((128 - N2P, 2 * TC), jnp.bfloat16)], axis=0)
    q = jnp.dot(ab, y2, preferred_element_type=jnp.float32)
    zr = q[:N2P, :TC] - q[N2P:, TC:]
    zi = q[N2P:, :TC] + q[:N2P, TC:]
    return zr, zi


def _spectrum_kernel(h_ref, t_ref, w3f0_ref, w3f1_ref, w3b0_ref, w3b1_ref, delta_ref, f1_ref, ab_ref, k_ref,
                     kt_ref, y_ref):
    w3 = jnp.concatenate([jnp.concatenate([w3f0_ref[...], w3f1_ref[...]], axis=1),
                          jnp.concatenate([w3b0_ref[...], w3b1_ref[...]], axis=1)], axis=0).astype(jnp.bfloat16)
    delta = delta_ref[...]
    rows = FFT_N // FILL_CHUNKS

    def fill(i, carry):
        r0 = pl.multiple_of(i * rows, 8)
        kt = jnp.dot(h_ref[pl.ds(r0, rows), :], w3, preferred_element_type=jnp.float32)
        window = jnp.exp(t_ref[pl.ds(r0, rows), :] * delta)
        for n in range(HY_ORDER):
            kt_ref[n, pl.ds(r0, rows), :] = kt[:, n * TC:(n + 1) * TC] * window
        return carry

    lax.fori_loop(0, FILL_CHUNKS, fill, 0)
    y_ref[pl.ds(FFT_N2 * SLAB_S, (N2P - FFT_N2) * SLAB_S), :] = jnp.zeros(((N2P - FFT_N2) * SLAB_S, TC), jnp.float32)

    for n in range(HY_ORDER):
        def stage1(g, carry):
            bs = [g * B_GROUP + i for i in range(B_GROUP)]
            x = jnp.concatenate([kt_ref.at[n][pl.ds(b, FFT_N1, stride=FFT_N2), :] for b in bs], axis=1)
            y = jnp.dot(f1_ref[...], x.astype(jnp.bfloat16), preferred_element_type=jnp.float32)
            for i, b in enumerate(bs):
                y_ref[pl.ds(pl.multiple_of(b * SLAB_S, 8), 2 * HALF_P), :] = y[:, i * TC:(i + 1) * TC]
            return carry

        lax.fori_loop(0, FFT_N2 // B_GROUP, stage1, 0)

        def stage2(g, carry):
            cs = [g * C_GROUP + i for i in range(C_GROUP)]
            zs = [_forward_middle(ab_ref[c], _middle_stage_inputs(y_ref, c, HALF_P, SLAB_S)) for c in cs]
            for c, (zr, zi) in zip(cs, zs):
                k_ref[n, c] = (jnp.concatenate([zr, zi], axis=1) * (1.0 / FFT_N)).astype(jnp.bfloat16)
            return carry

        lax.fori_loop(0, NC_HALF // C_GROUP, stage2, 0)


def _filter_spectrum(h2, w3, deltas, consts):
    _, f1_real, _, ab, _ = consts
    nt = HY_WIDTH // TC
    j = np.arange(FFT_N)
    lag = np.where(j < L_TOK, j, FFT_N - j).astype(np.float32)
    neg_t = jnp.asarray(np.broadcast_to((-lag / np.float32(L_TOK - 1))[:, None], (FFT_N, TC)))
    w3_cols = lambda direction, order: pl.BlockSpec((64, TC), lambda i: (0, (direction * HY_ORDER + order) * nt + i))
    return pl.pallas_call(
        _spectrum_kernel,
        grid=(nt,),
        in_specs=[
            pl.BlockSpec((FFT_N, 128), lambda i: (0, 0)),
            pl.BlockSpec((FFT_N, TC), lambda i: (0, 0)),
            w3_cols(0, 0), w3_cols(0, 1), w3_cols(1, 0), w3_cols(1, 1),
            pl.BlockSpec((1, TC), lambda i: (0, i)),
            pl.BlockSpec((2 * HALF_P, FFT_N1), lambda i: (0, 0)),
            pl.BlockSpec((FFT_N1, 2 * N2P, 128), lambda i: (0, 0, 0)),
        ],
        out_specs=pl.BlockSpec((HY_ORDER, None, NC_HALF, N2P, 2 * TC), lambda i: (0, i, 0, 0, 0)),
        out_shape=jax.ShapeDtypeStruct((HY_ORDER, nt, NC_HALF, N2P, 2 * TC), jnp.bfloat16),
        scratch_shapes=[pltpu.VMEM((HY_ORDER, FFT_N, TC), jnp.float32), pltpu.VMEM((N2P * SLAB_S, TC), jnp.float32)],
        compiler_params=pltpu.CompilerParams(dimension_semantics=("parallel",), vmem_limit_bytes=V7X_VMEM_LIMIT),
        name="hyena_filter_spectrum",
    )(h2, neg_t, w3, w3, w3, w3, deltas, f1_real, ab)


def _conv_kernel(u_ref, k_ref, f1_ref, i2_ref, ab_ref, abt_ref, o_ref, y_ref):
    y_ref[pl.ds(FFT_N2 * SLAB, (N2P - FFT_N2) * SLAB), :] = jnp.zeros(((N2P - FFT_N2) * SLAB, TC), jnp.float32)
    half = FFT_N1 // 2

    def stage1(g, carry):
        bs = [g * B_GROUP + i for i in range(B_GROUP)]
        x = jnp.concatenate(
            [jnp.concatenate([u_ref.at[ri][pl.ds(b, half, stride=FFT_N2), :] for b in bs], axis=1) for ri in range(2)],
            axis=0)
        y = jnp.dot(f1_ref[...], x.astype(jnp.bfloat16), preferred_element_type=jnp.float32)
        for i, b in enumerate(bs):
            y_ref[pl.ds(pl.multiple_of(b * SLAB, 8), 2 * FFT_N1), :] = y[:, i * TC:(i + 1) * TC]
        return carry

    lax.fori_loop(0, FFT_N2 // B_GROUP, stage1, 0)

    def inverse_middle(c, z):
        zr, zi = z
        mirrored = c >= NC_HALF
        kk = k_ref[jnp.where(mirrored, FFT_N1 - c, c)].astype(jnp.float32)
        kr, ki = kk[:, :TC], kk[:, TC:] * jnp.where(mirrored, -1.0, 1.0)
        pr = zr * kr - zi * ki
        pi = zr * ki + zi * kr
        rhs = jnp.concatenate([jnp.concatenate([pr, pi], axis=1), jnp.concatenate([pi, -pr], axis=1),
                               jnp.zeros((256 - 2 * N2P, 2 * TC), jnp.float32)], axis=0)
        return jnp.dot(abt_ref[c], rhs.astype(jnp.bfloat16), preferred_element_type=jnp.float32)

    def middle(g, carry):
        cs = [g * MID_GROUP + i for i in range(MID_GROUP)]
        zs = [_forward_middle(ab_ref[c], _middle_stage_inputs(y_ref, c)) for c in cs]
        outs = [inverse_middle(c, z) for c, z in zip(cs, zs)]
        for c, v in zip(cs, outs):
            y_ref[pl.ds(c, N2P, stride=SLAB), :] = v[:N2P, :TC]
            y_ref[pl.ds(FFT_N1 + c, N2P, stride=SLAB), :] = v[:N2P, TC:]
        return carry

    lax.fori_loop(0, FFT_N1 // MID_GROUP, middle, 0)

    def stage4(g, carry):
        bs = [g * B_GROUP + i for i in range(B_GROUP)]
        v = jnp.concatenate([y_ref[pl.ds(pl.multiple_of(b * SLAB, 8), 2 * FFT_N1), :] for b in bs], axis=1)
        x = jnp.dot(i2_ref[...], v.astype(jnp.bfloat16), preferred_element_type=jnp.float32)
        for i, b in enumerate(bs):
            o_ref.at[0][pl.ds(b, half, stride=FFT_N2), :] = x[:half, i * TC:(i + 1) * TC]
            o_ref.at[1][pl.ds(b, half, stride=FFT_N2), :] = x[half:, i * TC:(i + 1) * TC]
        return carry

    lax.fori_loop(0, FFT_N2 // B_GROUP, stage4, 0)


def _long_conv(u, kspec, order, consts):
    f1, _, i2, ab, abt = consts
    bsz, lp, _ = u.shape
    assert bsz == 2 and lp == LP
    nt = HY_WIDTH // TC
    return pl.pallas_call(
        _conv_kernel,
        grid=(nt,),
        in_specs=[
            pl.BlockSpec((bsz, lp, TC), lambda i: (0, 0, i)),
            pl.BlockSpec((None, None, NC_HALF, N2P, 2 * TC), lambda i: (order, i, 0, 0, 0)),
            pl.BlockSpec((2 * FFT_N1, FFT_N1), lambda i: (0, 0)),
            pl.BlockSpec((FFT_N1, 2 * FFT_N1), lambda i: (0, 0)),
            pl.BlockSpec((FFT_N1, 2 * N2P, 128), lambda i: (0, 0, 0), pipeline_mode=pl.Buffered(1)),
            pl.BlockSpec((FFT_N1, N2P, 256), lambda i: (0, 0, 0), pipeline_mode=pl.Buffered(1)),
        ],
        out_specs=pl.BlockSpec((bsz, lp, TC), lambda i: (0, 0, i)),
        out_shape=jax.ShapeDtypeStruct((bsz, lp, HY_WIDTH), jnp.float32),
        scratch_shapes=[pltpu.VMEM((N2P * SLAB, TC), jnp.float32)],
        compiler_params=pltpu.CompilerParams(dimension_semantics=("parallel",), vmem_limit_bytes=V7X_VMEM_LIMIT),
        name="hyena_long_conv",
    )(u, kspec, f1, i2, ab, abt)


def _short_conv_rows(u, w):
    y = pltpu.roll(u, 1, 0) * w[0:1] + u * w[1:2] + pltpu.roll(u, LP - 1, 0) * w[2:3]
    row = lax.broadcasted_iota(jnp.int32, y.shape, 0)
    return jnp.where(row >= PAD, y, 0.0)


def _short_conv_kernel(u_ref, w_ref, o_ref):
    o_ref[...] = _short_conv_rows(u_ref[...].astype(jnp.float32), w_ref[...])


def _gate1_kernel(v_ref, u_ref, w_ref, y_ref, b_ref, o_ref):
    x = _short_conv_rows(u_ref[...].astype(jnp.float32), w_ref[...])
    o_ref[...] = x * (y_ref[...] + v_ref[...] * b_ref[...])


def _gate2_kernel(z_ref, u_ref, w_ref, y_ref, b_ref, g_ref, o_ref):
    g = g_ref[...].astype(jnp.float32)
    z = _short_conv_rows(u_ref[...].astype(jnp.float32), w_ref[...]) * (y_ref[...] + z_ref[...] * b_ref[...])
    o_ref[...] = (z * (g * jax.nn.sigmoid(g))).astype(o_ref.dtype)


def _seq_tile(col_block):
    return pl.BlockSpec((None, LP, TC), lambda b, i: (b, 0, col_block + i))


def _vec_tile(rows, col_block):
    return pl.BlockSpec((rows, TC), lambda b, i: (0, col_block + i))


def _short_conv(p3, w_short_t):
    bsz = p3.shape[0]
    return pl.pallas_call(
        _short_conv_kernel,
        grid=(bsz, HY_WIDTH // TC),
        in_specs=[_seq_tile(0), _vec_tile(3, 0)],
        out_specs=_seq_tile(0),
        out_shape=jax.ShapeDtypeStruct((bsz, LP, HY_WIDTH), jnp.float32),
        compiler_params=pltpu.CompilerParams(
            dimension_semantics=("parallel", "parallel"), vmem_limit_bytes=V7X_VMEM_LIMIT),
        name="hyena_short_conv",
    )(p3, w_short_t)


def _hyena_gate1(v, p3, w_short_t, y, bias):
    bsz = p3.shape[0]
    nt = HY_WIDTH // TC
    return pl.pallas_call(
        _gate1_kernel,
        grid=(bsz, nt),
        in_specs=[_seq_tile(0), _seq_tile(nt), _vec_tile(3, nt), _seq_tile(0), _vec_tile(1, 0)],
        out_specs=_seq_tile(0),
        out_shape=jax.ShapeDtypeStruct((bsz, LP, HY_WIDTH), jnp.float32),
        compiler_params=pltpu.CompilerParams(
            dimension_semantics=("parallel", "parallel"), vmem_limit_bytes=V7X_VMEM_LIMIT),
        name="hyena_gate1",
    )(v, p3, w_short_t, y, bias.reshape(1, HY_WIDTH))


def _hyena_gate2(z1, p3, w_short_t, y, bias):
    bsz = p3.shape[0]
    nt = HY_WIDTH // TC
    return pl.pallas_call(
        _gate2_kernel,
        grid=(bsz, nt),
        in_specs=[_seq_tile(0), _seq_tile(2 * nt), _vec_tile(3, 2 * nt), _seq_tile(0), _vec_tile(1, 0),
                  _seq_tile(3 * nt)],
        out_specs=_seq_tile(0),
        out_shape=jax.ShapeDtypeStruct((bsz, LP, HY_WIDTH), jnp.bfloat16),
        compiler_params=pltpu.CompilerParams(
            dimension_semantics=("parallel", "parallel"), vmem_limit_bytes=V7X_VMEM_LIMIT),
        name="hyena_gate2",
    )(z1, p3, w_short_t, y, bias.reshape(1, HY_WIDTH), p3)


def _filter_positions():
    f32 = np.float32
    l = L_TOK
    t = np.linspace(0.0, 1.0, l, dtype=f32)[:, None]
    w = f32(2.0 * math.pi) * np.arange(l, dtype=f32)[:, None] / f32(l)
    bands = np.linspace(1e-4, HY_BANDS - 1, HY_BANDS, dtype=f32)[None, :]
    z = np.concatenate([t, np.cos(bands * w), -np.sin(bands * w)], axis=-1).astype(f32)
    j = np.arange(FFT_N)
    lag = np.minimum(np.where(j < l, j, FFT_N - j), l - 1)
    zrows = np.zeros((FFT_N, 128), f32)
    zrows[:, :HY_EMB] = z[lag]
    mask = np.zeros((FFT_N, 128), f32)
    mask[:, :64] = (j < l)[:, None]
    mask[:, 64:] = ((j > FFT_N - l) | (j == 0))[:, None]
    return jnp.asarray(zrows), jnp.asarray(mask)


def _hidden_kernel(z_ref, m_ref, w1_ref, b1_ref, fr1_ref, w2_ref, b2_ref, fr2_ref, o_ref):
    hi = lax.Precision.HIGHEST
    a = jnp.dot(z_ref[...], w1_ref[...], precision=hi, preferred_element_type=jnp.float32) + b1_ref[...]
    hid = jnp.sin(fr1_ref[...] * a)
    a = jnp.dot(hid, w2_ref[...], precision=hi, preferred_element_type=jnp.float32) + b2_ref[...]
    hid = jnp.sin(fr2_ref[...] * a)
    o_ref[...] = (jnp.concatenate([hid, hid], axis=1) * m_ref[...]).astype(o_ref.dtype)


def _hyena_hidden(w1, b1, fr1, w2, b2, fr2):
    zrows, mask = _filter_positions()
    rows = FFT_N // FILL_CHUNKS
    hdim = w2.shape[0]
    w1p = jnp.zeros((128, hdim), jnp.float32).at[:HY_EMB].set(w1.astype(jnp.float32))
    tile = pl.BlockSpec((rows, 128), lambda i: (i, 0))
    full = lambda a: pl.BlockSpec(a.shape, lambda i: (0,) * a.ndim)
    vecs = [v.astype(jnp.float32).reshape(1, hdim) for v in (b1, fr1, b2, fr2)]
    w2f = w2.astype(jnp.float32)
    return pl.pallas_call(
        _hidden_kernel,
        grid=(FILL_CHUNKS,),
        in_specs=[tile, tile, full(w1p), full(vecs[0]), full(vecs[1]), full(w2f), full(vecs[2]), full(vecs[3])],
        out_specs=tile,
        out_shape=jax.ShapeDtypeStruct((FFT_N, 128), jnp.bfloat16),
        compiler_params=pltpu.CompilerParams(dimension_semantics=("parallel",), vmem_limit_bytes=V7X_VMEM_LIMIT),
        name="hyena_filter_hidden",
    )(zrows, mask, w1p, vecs[0], vecs[1], w2f, vecs[2], vecs[3])


def _odd_mix(p3, w_short, w1, b1, fr1, w2, b2, fr2, w3, bias, consts):
    min_decay = math.log(HY_TARGET) / HY_SLOW_DECAY
    max_decay = math.log(HY_TARGET) / HY_FAST_DECAY
    deltas = jnp.abs(jnp.linspace(min_decay, max_decay, HY_WIDTH, dtype=jnp.float32)).reshape(1, HY_WIDTH)
    kspec = _filter_spectrum(_hyena_hidden(w1, b1, fr1, w2, b2, fr2), w3, deltas, consts)
    w_short_t = w_short.T
    v = _short_conv(p3, w_short_t)
    y1 = _long_conv(v, kspec, 0, consts)
    z1 = _hyena_gate1(v, p3, w_short_t, y1, bias[0])
    y2 = _long_conv(z1, kspec, 1, consts)
    return _hyena_gate2(z1, p3, w_short_t, y2, bias[1])


HG_HB = 16
HG_LEVELS = (32, 16, 8, 4, 2, 1)
LOG2_E = 1.4426950408889634


def _cumsum_rows(g, rev):
    f32, bf = jnp.float32, jnp.bfloat16
    t = lax.broadcasted_iota(jnp.int32, (HG_CHUNK, 3 * HG_CHUNK), 0)
    s = lax.broadcasted_iota(jnp.int32, (HG_CHUNK, 3 * HG_CHUNK), 1) & (HG_CHUNK - 1)
    tri = jnp.where((s >= t) if rev else (s <= t), 1.0, 0.0).astype(bf)
    hi = g.astype(bf)
    r1 = g - hi.astype(f32)
    mid = r1.astype(bf)
    lo = (r1 - mid.astype(f32)).astype(bf)
    return jnp.dot(tri, jnp.concatenate([hi, mid, lo], axis=0), preferred_element_type=f32)


def _hgrn_row_masks(rev):
    row = lax.broadcasted_iota(jnp.int32, (HG_CHUNK, HEAD_DIM), 0)
    masks = {}
    for m in HG_LEVELS:
        upper = (row & m) != 0
        is_query = jnp.logical_not(upper) if rev else upper
        target = m if rev else m - 1
        selectors = [(src, (row & (2 * m - 1)) == src) for src in range(2 * m) if src != target] if m == 2 else None
        masks[m] = (is_query, jnp.where(is_query, LOG2_E, -LOG2_E), selectors)
    return masks


def _level_reference(gc, m, rev, selectors):
    if m >= 4:
        g3 = gc.reshape(HG_CHUNK // (2 * m), 2 * m, HEAD_DIM)
        ref = g3[:, m:m + 1, :] if rev else g3[:, m - 1:m, :]
        return jnp.broadcast_to(ref, g3.shape).reshape(HG_CHUNK, HEAD_DIM)
    target = m if rev else m - 1
    out = gc
    for src, picks_src in selectors:
        out = jnp.where(picks_src, pltpu.roll(gc, (src - target) % HG_CHUNK, 0), out)
    return out


def _hgrn_gates(q_raw, f_raw, la, lc, omlb, rev):
    q = q_raw * (0.5 * HEAD_DIM ** -0.5) * (1.0 + jnp.tanh(0.5 * q_raw))
    k = (0.5 * omlb) * (1.0 - jnp.tanh(0.5 * f_raw))
    log_sig = jnp.minimum(f_raw, 0.0) - jnp.log(1.0 + jnp.exp(-jnp.abs(f_raw)))
    b = lc + log_sig
    g = jnp.maximum(la, b) + jnp.log(1.0 + jnp.exp(-jnp.abs(la - b)))
    return q, k, g, _cumsum_rows(g, rev)


def _hgrn_level_operands(q, k, g, gc, rev, masks):
    bf = jnp.bfloat16
    qb, kb = q.astype(bf), k.astype(bf)
    ops = []
    for m in HG_LEVELS:
        is_query, sign_log2e, selectors = masks[m]
        if m == 1:
            ops.append(jnp.where(is_query, qb * jnp.exp2(g * LOG2_E).astype(bf), kb))
        else:
            gref = _level_reference(gc, m, rev, selectors)
            ops.append(jnp.where(is_query, qb, kb) * jnp.exp2((gc - gref) * sign_log2e).astype(bf))
    return ops


def _hgrn_level_id(rev):
    t = lax.broadcasted_iota(jnp.int32, (HG_CHUNK, HG_CHUNK), 0)
    s = lax.broadcasted_iota(jnp.int32, (HG_CHUNK, HG_CHUNK), 1)
    vis = (s >= t) if rev else (s <= t)
    x = t ^ s
    lvl = jnp.full((HG_CHUNK, HG_CHUNK), len(HG_LEVELS), jnp.int32)
    for i, m in enumerate(HG_LEVELS):
        lvl = jnp.where((x & m) != 0, jnp.minimum(lvl, i), lvl)
    return jnp.where(vis, lvl, -1)


def _hgrn_kernel(qf_ref, if_ref, ff_ref, qb_ref, ib_ref, fb_ref, la_ref, lc_ref, om_ref, of_ref, ob_ref, s_ref):
    f32, bf = jnp.float32, jnp.bfloat16
    nt = (((1,), (1,)), ((), ()))
    tn = (((0,), (0,)), ((), ()))

    @pl.when(pl.program_id(2) == 0)
    def _():
        s_ref[...] = jnp.zeros_like(s_ref)

    row_masks = [_hgrn_row_masks(False), _hgrn_row_masks(True)]
    refs = ((qf_ref, if_ref, ff_ref, of_ref), (qb_ref, ib_ref, fb_ref, ob_ref))
    inst = [(d, h) for d in range(2) for h in range(HG_HB)]
    level_masks = [[lvl == i for i in range(len(HG_LEVELS) + 1)]
                   for lvl in (_hgrn_level_id(False), _hgrn_level_id(True))]

    pre = []
    for d, h in inst:
        q_ref, i_ref, f_ref, _ = refs[d]
        sl = slice(h * HEAD_DIM, (h + 1) * HEAD_DIM)
        q, k, g, gc = _hgrn_gates(q_ref[:, sl], f_ref[:, sl], la_ref[d:d + 1, sl], lc_ref[d:d + 1, sl],
                                  om_ref[d:d + 1, sl], d == 1)
        g_end = gc[0:1, :] if d == 1 else gc[HG_CHUNK - 1:HG_CHUNK, :]
        pre.append(dict(ops=_hgrn_level_operands(q, k, g, gc, d == 1, row_masks[d]), qk=(q.astype(bf), k.astype(bf)),
                        qe=(q * jnp.exp(gc)).astype(bf), kd=(k * jnp.exp(g_end - gc)).astype(bf),
                        decay=jnp.exp(g_end), v=i_ref[:, sl].astype(bf)))

    attn = []
    for (d, h), pr in zip(inst, pre):
        prods = [lax.dot_general(x, x, nt, preferred_element_type=f32) for x in pr["ops"]]
        prods.append(lax.dot_general(pr["qk"][0], pr["qk"][1], nt, preferred_element_type=f32))
        a = jnp.zeros((HG_CHUNK, HG_CHUNK), f32)
        for i, pmat in enumerate(prods):
            a = jnp.where(level_masks[d][i], pmat, a)
        attn.append(a.astype(bf))

    for (d, h), pr, a in zip(inst, pre, attn):
        o_ref = refs[d][3]
        sl = slice(h * HEAD_DIM, (h + 1) * HEAD_DIM)
        state_t = s_ref[d, h]
        o = jnp.dot(a, pr["v"], preferred_element_type=f32)
        o = o + lax.dot_general(pr["qe"], state_t.astype(bf), nt, preferred_element_type=f32)
        o_ref[:, sl] = o.astype(o_ref.dtype)
        s_ref[d, h] = pr["decay"] * state_t + lax.dot_general(pr["v"], pr["kd"], tn, preferred_element_type=f32)


def _hgrn2(p, col0, lb):
    bsz, lp, _ = p.shape
    nchunk = lp // HG_CHUNK
    cw = HG_HB * HEAD_DIM
    ng = HG_WIDTH // cw
    off = col0 // cw
    la = jnp.log(lb)
    lc = jnp.log1p(-lb)
    om = 1.0 - lb
    fwd = lambda seg: pl.BlockSpec((None, HG_CHUNK, cw), lambda b, g, c: (b, c, off + seg * ng + g))
    bwd = lambda seg: pl.BlockSpec((None, HG_CHUNK, cw), lambda b, g, c: (b, nchunk - 1 - c, off + seg * ng + g))
    par = pl.BlockSpec((2, cw), lambda b, g, c: (0, g))
    return pl.pallas_call(
        _hgrn_kernel,
        grid=(bsz, ng, nchunk),
        in_specs=[fwd(0), fwd(1), fwd(2), bwd(0), bwd(1), bwd(3), par, par, par],
        out_specs=[pl.BlockSpec((None, HG_CHUNK, cw), lambda b, g, c: (b, c, g)),
                   pl.BlockSpec((None, HG_CHUNK, cw), lambda b, g, c: (b, nchunk - 1 - c, g))],
        out_shape=[jax.ShapeDtypeStruct((bsz, lp, HG_WIDTH), jnp.bfloat16)] * 2,
        scratch_shapes=[pltpu.VMEM((2, HG_HB, HEAD_DIM, HEAD_DIM), jnp.float32)],
        compiler_params=pltpu.CompilerParams(
            dimension_semantics=("parallel", "parallel", "arbitrary"), vmem_limit_bytes=V7X_VMEM_LIMIT),
        name="hgrn2_bidirectional",
    )(p, p, p, p, p, p, la, lc, om)


EG_ROWS = 320


def _even_gate_kernel(oa_ref, of_ref, ob_ref, ga_ref, gb_ref, gain_ref, y_ref):
    ga = ga_ref[...]
    y_ref[:, :NA_WIDTH] = (oa_ref[...].astype(jnp.float32) * (ga * jax.nn.sigmoid(ga))).astype(y_ref.dtype)
    for h in range(HG_HEADS):
        sl = slice(h * HEAD_DIM, (h + 1) * HEAD_DIM)
        oh = of_ref[:, sl].astype(jnp.float32) + ob_ref[:, sl].astype(jnp.float32)
        oh = oh * lax.rsqrt(jnp.mean(oh * oh, axis=-1, keepdims=True) + EPS) * gain_ref[:, sl]
        gb = gb_ref[:, sl]
        y_ref[:, NA_WIDTH + h * HEAD_DIM:NA_WIDTH + (h + 1) * HEAD_DIM] = (
            oh * (gb * jax.nn.sigmoid(gb))).astype(y_ref.dtype)


def _even_gate(oa, o_fwd, o_bwd, p, gate_col0, norm_gain):
    bsz, lp, _ = oa.shape
    assert gate_col0 % NA_WIDTH == 0
    goff = gate_col0 // NA_WIDTH
    half = lambda off: pl.BlockSpec((None, EG_ROWS, NA_WIDTH), lambda b, r: (b, r, off))
    return pl.pallas_call(
        _even_gate_kernel,
        grid=(bsz, lp // EG_ROWS),
        in_specs=[half(0), half(0), half(0), half(goff), half(goff + 1),
                  pl.BlockSpec((1, HG_WIDTH), lambda b, r: (0, 0))],
        out_specs=pl.BlockSpec((None, EG_ROWS, D_MODEL), lambda b, r: (b, r, 0)),
        out_shape=jax.ShapeDtypeStruct((bsz, lp, D_MODEL), jnp.bfloat16),
        compiler_params=pltpu.CompilerParams(
            dimension_semantics=("parallel", "parallel"), vmem_limit_bytes=V7X_VMEM_LIMIT),
        name="even_gate",
    )(oa, o_fwd, o_bwd, p, p, norm_gain.reshape(1, HG_WIDTH))


NA_MASKED = -1e30
NA_GROUP = 16


def _na_bias_table(rpb):
    cols = np.arange(GRID_W)
    c0 = np.clip(cols - NA_KW // 2, 0, GRID_W - NA_KW)
    col_ok = (cols[None, :] >= c0[:, None]) & (cols[None, :] < c0[:, None] + NA_KW)
    col_idx = np.clip(cols[None, :] - cols[:, None] + NA_KW - 1, 0, 2 * NA_KW - 2)
    onehot = (col_idx[None] == np.arange(2 * NA_KW - 1)[:, None, None]).astype(np.float32)
    picked = jnp.einsum('hrj,jqk->hrqk', rpb.astype(jnp.float32), onehot, precision=lax.Precision.HIGHEST)
    table = jnp.where(col_ok[None, None], picked, NA_MASKED)
    return jnp.concatenate([table[:, :-1], table[:, 1:]], axis=-1)


def _na_kernel(q_ref, k_ref, v_ref, t_ref, mb_ref, o_ref):
    f32, bf = jnp.float32, jnp.bfloat16
    rows = (q_ref.shape[0] - HG_CHUNK) // GRID_W
    scale = HEAD_DIM ** -0.5
    nt = (((1,), (1,)), ((), ()))
    km = k_ref[PAD:PAD + N_META, :].astype(bf)
    vm = v_ref[PAD:PAD + N_META, :].astype(bf)
    mb = mb_ref[...]
    o_ref[0:PAD, :] = jnp.zeros((PAD, HEAD_DIM), o_ref.dtype)

    qm = q_ref[PAD:PAD + N_META, :].astype(bf)
    s = lax.dot_general(qm, km, nt, preferred_element_type=f32) * scale + mb
    e = jnp.exp(s - jnp.max(s, axis=-1, keepdims=True))
    o_ref[PAD:PAD + N_META, :] = (jnp.dot(e.astype(bf), vm, preferred_element_type=f32)
                                  / jnp.sum(e, axis=-1, keepdims=True)).astype(o_ref.dtype)

    def row_group(g, carry):
        rs = [g * NA_GROUP + i for i in range(NA_GROUP)]
        scores = []
        for r in rs:
            r0 = jnp.clip(r - NA_KH // 2, 0, rows - NA_KH)
            q = q_ref[pl.ds(pl.multiple_of(HG_CHUNK + GRID_W * r, GRID_W), GRID_W), :].astype(bf)
            k0 = pl.multiple_of(HG_CHUNK + GRID_W * r0, GRID_W)
            kw = k_ref[pl.ds(k0, NA_KH * GRID_W), :].astype(bf)
            s_win = lax.dot_general(q, kw, nt, preferred_element_type=f32)
            s_meta = lax.dot_general(q, km, nt, preferred_element_type=f32)
            scores.append((r, r0, k0, s_win, s_meta))
        probs = []
        for r, r0, k0, s_win, s_meta in scores:
            bias = jnp.concatenate([t_ref[r0 + i - r + NA_KH - 1] for i in range(0, NA_KH, 2)], axis=1)
            s_win = s_win * scale + bias
            s_meta = s_meta * scale + mb
            m = jnp.maximum(jnp.max(s_win, axis=-1, keepdims=True), jnp.max(s_meta, axis=-1, keepdims=True))
            pw = jnp.exp(s_win - m)
            pm = jnp.exp(s_meta - m)
            den = jnp.sum(pw, axis=-1, keepdims=True) + jnp.sum(pm, axis=-1, keepdims=True)
            probs.append((r, k0, pw.astype(bf), pm.astype(bf), den))
        for r, k0, pw, pm, den in probs:
            vw = v_ref[pl.ds(k0, NA_KH * GRID_W), :].astype(bf)
            o = jnp.dot(pw, vw, preferred_element_type=f32) + jnp.dot(pm, vm, preferred_element_type=f32)
            o_ref[pl.ds(pl.multiple_of(HG_CHUNK + GRID_W * r, GRID_W), GRID_W), :] = (o / den).astype(o_ref.dtype)
        return carry

    lax.fori_loop(0, rows // NA_GROUP, row_group, 0)


def _natten(p, rpb, meta_bias):
    bsz, lp, _ = p.shape
    assert ((lp - HG_CHUNK) // GRID_W) % NA_GROUP == 0
    heads = NA_WIDTH // HEAD_DIM
    blk = lambda seg: pl.BlockSpec((None, lp, HEAD_DIM), lambda b, h: (b, 0, seg * heads + h))
    table = _na_bias_table(rpb)
    return pl.pallas_call(
        _na_kernel,
        grid=(bsz, heads),
        in_specs=[blk(0), blk(1), blk(2),
                  pl.BlockSpec((None, 2 * NA_KH - 2, GRID_W, 2 * GRID_W), lambda b, h: (h, 0, 0, 0)),
                  pl.BlockSpec((None, 1, N_META), lambda b, h: (h, 0, 0))],
        out_specs=pl.BlockSpec((None, lp, HEAD_DIM), lambda b, h: (b, 0, h)),
        out_shape=jax.ShapeDtypeStruct((bsz, lp, NA_WIDTH), jnp.bfloat16),
        compiler_params=pltpu.CompilerParams(
            dimension_semantics=("parallel", "parallel"), vmem_limit_bytes=V7X_VMEM_LIMIT),
        name="neighbourhood_attention",
    )(p, p, p, table, meta_bias.astype(jnp.float32).reshape(heads, 1, N_META))


def _even_mix(p, rpb, meta_bias, lb, norm_gain):
    oa = _natten(p, rpb, meta_bias)
    o_fwd, o_bwd = _hgrn2(p, 3 * NA_WIDTH, lb)
    return _even_gate(oa, o_fwd, o_bwd, p, 3 * NA_WIDTH + 4 * HG_WIDTH, norm_gain)


def kernel(x, meta_tokens, norm_pre, norm_post, ev_w_in, ev_w_out, na_rpb, na_meta_bias, hg_lower, hg_norm,
           od_w_in, od_w_out, hy_short, hy_ffn_w1, hy_ffn_b1, hy_ffn_freq1, hy_ffn_w2, hy_ffn_b2, hy_ffn_freq2,
           hy_ffn_w3, hy_bias):
    b = x.shape[0]
    depth = norm_pre.shape[0]
    f32 = jnp.float32
    lb_all = jnp.cumsum(jax.nn.softmax(hg_lower.astype(f32), axis=0), axis=0)
    lb_all = lb_all - lb_all[:1]
    h, hn = _embed(x, meta_tokens, norm_pre[0])
    h = h.reshape(b * LP, D_MODEL)
    hn = hn.reshape(b * LP, D_MODEL)
    consts = _dft_constants()
    for layer in range(depth):
        j = layer // 2
        if layer % 2 == 0:
            p = _matmul(hn, ev_w_in, j, f32).reshape(b, LP, -1)
            y = _even_mix(p, na_rpb[j], na_meta_bias[j], lb_all[j], hg_norm[j]).reshape(b * LP, D_MODEL)
            w_out = ev_w_out
        else:
            p3 = _matmul(hn, od_w_in, j, jnp.bfloat16).reshape(b, LP, -1)
            y = _odd_mix(p3, hy_short[j], hy_ffn_w1[j], hy_ffn_b1[j], hy_ffn_freq1[j], hy_ffn_w2[j],
                         hy_ffn_b2[j], hy_ffn_freq2[j], hy_ffn_w3[j], hy_bias[j], consts)
            y = y.reshape(b * LP, D_MODEL)
            w_out = od_w_out
        out = _matmul(y, w_out, j, jnp.bfloat16)
        if layer + 1 < depth:
            h, hn = _residual_rmsnorm(h, out, norm_post[layer], norm_pre[layer + 1])
    return _final_residual(h.reshape(b, LP, D_MODEL), out.reshape(b, LP, D_MODEL), norm_post[depth - 1], x.dtype)
```

```python
import math

import jax
import jax.numpy as jnp
import numpy as np
from jax import lax
from jax.experimental import pallas as pl
from jax.experimental.pallas import tpu as pltpu

D_MODEL = 4096
SEQ = 4096
N_META = 16
GRID_W = 64
HEAD_DIM = 128
NA_WIDTH = D_MODEL // 2
NA_KH = 8
NA_KW = 16
HG_WIDTH = D_MODEL // 2
HG_HEADS = HG_WIDTH // HEAD_DIM
HG_CHUNK = 64
HY_WIDTH = D_MODEL
HY_ORDER = 2
HY_EMB = 33
HY_BANDS = (HY_EMB - 1) // 2
HY_FAST_DECAY = 0.3
HY_SLOW_DECAY = 1.5
HY_TARGET = 1e-2
EPS = 1e-6

PAD = HG_CHUNK - N_META
LP = PAD + N_META + SEQ
L_TOK = N_META + SEQ

V7X_VMEM_LIMIT = 56 * 1024 * 1024


EMBED_PIECES = 5


def _embed_kernel(*refs):
    x_refs, meta_ref, g_ref, h_ref, n_ref = refs[:EMBED_PIECES], *refs[EMBED_PIECES:]
    first = jnp.concatenate([jnp.zeros((PAD, D_MODEL), jnp.float32), meta_ref[...].astype(jnp.float32)], axis=0)
    pieces = [jnp.where(pl.program_id(1) == 0, first, x_refs[0][...].astype(jnp.float32))]
    pieces += [x_ref[...].astype(jnp.float32) for x_ref in x_refs[1:]]
    h = jnp.concatenate(pieces, axis=0)
    h_ref[...] = h
    r = lax.rsqrt(jnp.mean(h * h, axis=-1, keepdims=True) + EPS)
    n_ref[...] = (h * r * g_ref[...]).astype(n_ref.dtype)


def _embed(x, meta_tokens, g):
    bsz, seq, d = x.shape
    assert seq == SEQ and LP % (EMBED_PIECES * HG_CHUNK) == 0
    tr = EMBED_PIECES * HG_CHUNK
    piece = lambda k: pl.BlockSpec((None, HG_CHUNK, d), lambda b, r: (b, jnp.maximum(EMBED_PIECES * r + k - 1, 0), 0))
    row = pl.BlockSpec((None, tr, d), lambda b, r: (b, r, 0))
    return pl.pallas_call(
        _embed_kernel,
        grid=(bsz, LP // tr),
        in_specs=[piece(k) for k in range(EMBED_PIECES)] + [pl.BlockSpec((N_META, d), lambda b, r: (0, 0)),
                                                           pl.BlockSpec((1, d), lambda b, r: (0, 0))],
        out_specs=[row, row],
        out_shape=[jax.ShapeDtypeStruct((bsz, LP, d), jnp.float32), jax.ShapeDtypeStruct((bsz, LP, d), jnp.bfloat16)],
        compiler_params=pltpu.CompilerParams(
            dimension_semantics=("parallel", "parallel"), vmem_limit_bytes=V7X_VMEM_LIMIT),
        name="embed_rmsnorm_pre",
    )(*([x] * EMBED_PIECES), meta_tokens, g.reshape(1, d))


def _post_pre_kernel(h_ref, y_ref, g_ref, gn_ref, o_ref, n_ref):
    y = y_ref[...].astype(jnp.float32)
    r = lax.rsqrt(jnp.mean(y * y, axis=-1, keepdims=True) + EPS)
    h = h_ref[...] + y * r * g_ref[...]
    o_ref[...] = h
    rn = lax.rsqrt(jnp.mean(h * h, axis=-1, keepdims=True) + EPS)
    n_ref[...] = (h * rn * gn_ref[...]).astype(n_ref.dtype)


FINAL_PIECES = 4


def _final_kernel(*refs):
    h_refs, y_refs = refs[:FINAL_PIECES], refs[FINAL_PIECES:2 * FINAL_PIECES]
    g_ref, o_ref = refs[2 * FINAL_PIECES:]
    for k, (h_ref, y_ref) in enumerate(zip(h_refs, y_refs)):
        y = y_ref[...].astype(jnp.float32)
        r = lax.rsqrt(jnp.mean(y * y, axis=-1, keepdims=True) + EPS)
        o_ref[k * HG_CHUNK:(k + 1) * HG_CHUNK, :] = (h_ref[...] + y * r * g_ref[...]).astype(o_ref.dtype)


def _final_residual(h3, y3, g, out_dtype):
    bsz, lp, d = h3.shape
    assert PAD + N_META == HG_CHUNK and SEQ % (FINAL_PIECES * HG_CHUNK) == 0
    piece = lambda k: pl.BlockSpec((None, HG_CHUNK, d), lambda b, r: (b, FINAL_PIECES * r + k + 1, 0))
    pieces = [piece(k) for k in range(FINAL_PIECES)]
    return pl.pallas_call(
        _final_kernel,
        grid=(bsz, SEQ // (FINAL_PIECES * HG_CHUNK)),
        in_specs=pieces + pieces + [pl.BlockSpec((1, d), lambda b, r: (0, 0))],
        out_specs=pl.BlockSpec((None, FINAL_PIECES * HG_CHUNK, d), lambda b, r: (b, r, 0)),
        out_shape=jax.ShapeDtypeStruct((bsz, SEQ, d), out_dtype),
        compiler_params=pltpu.CompilerParams(
            dimension_semantics=("parallel", "parallel"), vmem_limit_bytes=V7X_VMEM_LIMIT),
        name="residual_rmsnorm_post_final",
    )(*([h3] * FINAL_PIECES), *([y3] * FINAL_PIECES), g.reshape(1, d))


def _residual_rmsnorm(h2d, y2d, g, g_next, tr=320):
    m, d = h2d.shape
    row = pl.BlockSpec((tr, d), lambda i: (i, 0))
    vec = pl.BlockSpec((1, d), lambda i: (0, 0))
    params = pltpu.CompilerParams(dimension_semantics=("parallel",), vmem_limit_bytes=V7X_VMEM_LIMIT)
    return pl.pallas_call(
        _post_pre_kernel, grid=(m // tr,), in_specs=[row, row, vec, vec], out_specs=[row, row],
        out_shape=[jax.ShapeDtypeStruct((m, d), jnp.float32), jax.ShapeDtypeStruct((m, d), jnp.bfloat16)],
        compiler_params=params, name="residual_rmsnorm_post_pre",
    )(h2d, y2d, g.reshape(1, d), g_next.reshape(1, d))


def _matmul_kernel(x_ref, w_ref, o_ref):
    o_ref[...] = jnp.dot(x_ref[...], w_ref[...].astype(jnp.bfloat16),
                         preferred_element_type=jnp.float32).astype(o_ref.dtype)


def _matmul(x, w_stack, layer, out_dtype, tm=1040, tn=512):
    m, kd = x.shape
    _, _, n = w_stack.shape
    assert m % tm == 0 and n % tn == 0
    return pl.pallas_call(
        _matmul_kernel,
        grid=(m // tm, n // tn),
        in_specs=[pl.BlockSpec((tm, kd), lambda i, j: (i, 0)),
                  pl.BlockSpec((None, kd, tn), lambda i, j: (layer, 0, j))],
        out_specs=pl.BlockSpec((tm, tn), lambda i, j: (i, j)),
        out_shape=jax.ShapeDtypeStruct((m, n), out_dtype),
        compiler_params=pltpu.CompilerParams(
            dimension_semantics=("parallel", "parallel"), vmem_limit_bytes=V7X_VMEM_LIMIT),
        name="projection_matmul",
    )(x, w_stack)


FFT_N1 = 128
FFT_N2 = 65
FFT_N = FFT_N1 * FFT_N2
N2P = 80
MIDR = 72
SLAB = 264
TC = 128
MID_GROUP = 16
NC_HALF = FFT_N1 // 2 + 1
HALF_P = 72
SLAB_S = 152
C_GROUP = 13
FILL_CHUNKS = 13
B_GROUP = 13


def _dft_constants():
    a = np.arange(FFT_N1)
    c = np.arange(FFT_N1)
    w1 = np.exp(-2j * np.pi * np.outer(c, a) / FFT_N1)
    w1r, w1i = w1.real, w1.imag
    half = FFT_N1 // 2
    f1 = np.block([[w1r[:, :half], -w1i[:, :half]], [w1i[:, :half], w1r[:, :half]]])
    f1_real = np.zeros((2 * HALF_P, FFT_N1), np.float64)
    f1_real[:NC_HALF] = w1r[:NC_HALF]
    f1_real[HALF_P:HALF_P + NC_HALF] = w1i[:NC_HALF]
    g = np.conj(w1).T
    gr, gi = g.real[:half], g.imag[:half]
    i2 = np.block([[gr, -gi], [gi, gr]])
    b = np.arange(FFT_N2)
    d = np.arange(FFT_N2)
    w2 = np.exp(-2j * np.pi * np.outer(d, b) / FFT_N2)
    tw = np.exp(-2j * np.pi * np.outer(c, b) / FFT_N)
    m = w2[None, :, :] * tw[:, None, :]
    ab = np.zeros((FFT_N1, 2 * MIDR, 128), np.float64)
    m[NC_HALF:] = m[NC_HALF:, ::-1, :]
    ab[:, :FFT_N2, :FFT_N2] = m.real
    ab[:, MIDR:MIDR + FFT_N2, :FFT_N2] = m.imag
    bf = jnp.bfloat16
    abt = np.zeros((FFT_N1, N2P, 256), np.float64)
    abt[:, :, :2 * MIDR] = np.swapaxes(ab[:, :, :N2P], 1, 2)
    return (jnp.asarray(f1, bf), jnp.asarray(f1_real, bf), jnp.asarray(i2, bf), jnp.asarray(ab, bf),
            jnp.asarray(abt, bf))


def _middle_stage_inputs(y_ref, c, imag_row=FFT_N1, pitch=SLAB):
    yr = y_ref[pl.ds(c, MIDR, stride=pitch), :]
    yi = y_ref[pl.ds(imag_row + c, MIDR, stride=pitch), :]
    return jnp.concatenate([yr, yi], axis=1)


def _forward_middle(ab, y2):
    y2 = jnp.concatenate([y2, jnp.zeros((128 - MIDR, 2 * TC), jnp.float32)], axis=0).astype(jnp.bfloat16)
    q = jnp.dot(ab, y2, preferred_element_type=jnp.float32)
    zr = q[:MIDR, :TC] - q[MIDR:, TC:]
    zi = q[MIDR:, :TC] + q[:MIDR, TC:]
    return zr, zi


def _spectrum_kernel(h_ref, t_ref, w3f0_ref, w3f1_ref, w3b0_ref, w3b1_ref, delta_ref, f1_ref, ab_ref, k_ref,
                     kt_ref, y_ref):
    w3 = jnp.concatenate([jnp.concatenate([w3f0_ref[...], w3f1_ref[...]], axis=1),
                          jnp.concatenate([w3b0_ref[...], w3b1_ref[...]], axis=1)], axis=0).astype(jnp.bfloat16)
    delta = delta_ref[...]
    rows = FFT_N // FILL_CHUNKS

    def fill(i, carry):
        r0 = pl.multiple_of(i * rows, 8)
        kt = jnp.dot(h_ref[pl.ds(r0, rows), :], w3, preferred_element_type=jnp.float32)
        window = jnp.exp(t_ref[pl.ds(r0, rows), :] * delta)
        for n in range(HY_ORDER):
            kt_ref[n, pl.ds(r0, rows), :] = kt[:, n * TC:(n + 1) * TC] * window
        return carry

    lax.fori_loop(0, FILL_CHUNKS, fill, 0)
    y_ref[pl.ds(FFT_N2 * SLAB_S, (MIDR - FFT_N2) * SLAB_S), :] = jnp.zeros(((MIDR - FFT_N2) * SLAB_S, TC), jnp.float32)

    for n in range(HY_ORDER):
        def stage1(g, carry):
            bs = [g * B_GROUP + i for i in range(B_GROUP)]
            x = jnp.concatenate([kt_ref.at[n][pl.ds(b, FFT_N1, stride=FFT_N2), :] for b in bs], axis=1)
            y = jnp.dot(f1_ref[...], x.astype(jnp.bfloat16), preferred_element_type=jnp.float32)
            for i, b in enumerate(bs):
                y_ref[pl.ds(pl.multiple_of(b * SLAB_S, 8), 2 * HALF_P), :] = y[:, i * TC:(i + 1) * TC]
            return carry

        lax.fori_loop(0, FFT_N2 // B_GROUP, stage1, 0)

        def stage2(g, carry):
            cs = [g * C_GROUP + i for i in range(C_GROUP)]
            zs = [_forward_middle(ab_ref[c], _middle_stage_inputs(y_ref, c, HALF_P, SLAB_S)) for c in cs]
            for c, (zr, zi) in zip(cs, zs):
                z = jnp.concatenate([zr, zi], axis=1) * (1.0 / FFT_N)
                k_ref[n, c] = jnp.concatenate(
                    [z, jnp.zeros((N2P - MIDR, 2 * TC), jnp.float32)], axis=0).astype(jnp.bfloat16)
            return carry

        lax.fori_loop(0, NC_HALF // C_GROUP, stage2, 0)


def _filter_spectrum(h2, w3, deltas, consts):
    _, f1_real, _, ab, _ = consts
    nt = HY_WIDTH // TC
    j = np.arange(FFT_N)
    lag = np.where(j < L_TOK, j, FFT_N - j).astype(np.float32)
    neg_t = jnp.asarray(np.broadcast_to((-lag / np.float32(L_TOK - 1))[:, None], (FFT_N, TC)))
    w3_cols = lambda direction, order: pl.BlockSpec((64, TC), lambda i: (0, (direction * HY_ORDER + order) * nt + i))
    return pl.pallas_call(
        _spectrum_kernel,
        grid=(nt,),
        in_specs=[
            pl.BlockSpec((FFT_N, 128), lambda i: (0, 0)),
            pl.BlockSpec((FFT_N, TC), lambda i: (0, 0)),
            w3_cols(0, 0), w3_cols(0, 1), w3_cols(1, 0), w3_cols(1, 1),
            pl.BlockSpec((1, TC), lambda i: (0, i)),
            pl.BlockSpec((2 * HALF_P, FFT_N1), lambda i: (0, 0)),
            pl.BlockSpec((FFT_N1, 2 * MIDR, 128), lambda i: (0, 0, 0)),
        ],
        out_specs=pl.BlockSpec((HY_ORDER, None, NC_HALF, N2P, 2 * TC), lambda i: (0, i, 0, 0, 0)),
        out_shape=jax.ShapeDtypeStruct((HY_ORDER, nt, NC_HALF, N2P, 2 * TC), jnp.bfloat16),
        scratch_shapes=[pltpu.VMEM((HY_ORDER, FFT_N, TC), jnp.float32), pltpu.VMEM((MIDR * SLAB_S, TC), jnp.float32)],
        compiler_params=pltpu.CompilerParams(dimension_semantics=("parallel",), vmem_limit_bytes=V7X_VMEM_LIMIT),
        name="hyena_filter_spectrum",
    )(h2, neg_t, w3, w3, w3, w3, deltas, f1_real, ab)


def _conv_kernel(u_ref, k_ref, f1_ref, i2_ref, ab_ref, abt_ref, o_ref, y_ref):
    y_ref[pl.ds(FFT_N2 * SLAB, (MIDR - FFT_N2) * SLAB), :] = jnp.zeros(((MIDR - FFT_N2) * SLAB, TC), jnp.float32)
    half = FFT_N1 // 2

    def stage1(g, carry):
        bs = [g * B_GROUP + i for i in range(B_GROUP)]
        x = jnp.concatenate(
            [jnp.concatenate([u_ref.at[ri][pl.ds(b, half, stride=FFT_N2), :] for b in bs], axis=1) for ri in range(2)],
            axis=0)
        y = jnp.dot(f1_ref[...], x.astype(jnp.bfloat16), preferred_element_type=jnp.float32)
        for i, b in enumerate(bs):
            y_ref[pl.ds(pl.multiple_of(b * SLAB, 8), 2 * FFT_N1), :] = y[:, i * TC:(i + 1) * TC]
        return carry

    lax.fori_loop(0, FFT_N2 // B_GROUP, stage1, 0)

    def inverse_middle(c, z):
        zr, zi = z
        mirrored = c >= NC_HALF
        kk = k_ref[jnp.where(mirrored, FFT_N1 - c, c)].astype(jnp.float32)[:MIDR]
        kr, ki = kk[:, :TC], kk[:, TC:] * jnp.where(mirrored, -1.0, 1.0)
        pr = zr * kr - zi * ki
        pi = zr * ki + zi * kr
        rhs = jnp.concatenate([jnp.concatenate([pr, pi], axis=1), jnp.concatenate([pi, -pr], axis=1),
                               jnp.zeros((256 - 2 * MIDR, 2 * TC), jnp.float32)], axis=0)
        return jnp.dot(abt_ref[c], rhs.astype(jnp.bfloat16), preferred_element_type=jnp.float32)

    def middle(g, carry):
        cs = [g * MID_GROUP + i for i in range(MID_GROUP)]
        zs = [_forward_middle(ab_ref[c], _middle_stage_inputs(y_ref, c)) for c in cs]
        outs = [inverse_middle(c, z) for c, z in zip(cs, zs)]
        for c, v in zip(cs, outs):
            y_ref[pl.ds(c, MIDR, stride=SLAB), :] = v[:MIDR, :TC]
            y_ref[pl.ds(FFT_N1 + c, MIDR, stride=SLAB), :] = v[:MIDR, TC:]
        return carry

    lax.fori_loop(0, FFT_N1 // MID_GROUP, middle, 0)

    def stage4(g, carry):
        bs = [g * B_GROUP + i for i in range(B_GROUP)]
        v = jnp.concatenate([y_ref[pl.ds(pl.multiple_of(b * SLAB, 8), 2 * FFT_N1), :] for b in bs], axis=1)
        x = jnp.dot(i2_ref[...], v.astype(jnp.bfloat16), preferred_element_type=jnp.float32)
        for i, b in enumerate(bs):
            o_ref.at[0][pl.ds(b, half, stride=FFT_N2), :] = x[:half, i * TC:(i + 1) * TC]
            o_ref.at[1][pl.ds(b, half, stride=FFT_N2), :] = x[half:, i * TC:(i + 1) * TC]
        return carry

    lax.fori_loop(0, FFT_N2 // B_GROUP, stage4, 0)


def _long_conv(u, kspec, order, consts):
    f1, _, i2, ab, abt = consts
    bsz, lp, _ = u.shape
    assert bsz == 2 and lp == LP
    nt = HY_WIDTH // TC
    return pl.pallas_call(
        _conv_kernel,
        grid=(nt,),
        in_specs=[
            pl.BlockSpec((bsz, lp, TC), lambda i: (0, 0, i)),
            pl.BlockSpec((None, None, NC_HALF, N2P, 2 * TC), lambda i: (order, i, 0, 0, 0)),
            pl.BlockSpec((2 * FFT_N1, FFT_N1), lambda i: (0, 0)),
            pl.BlockSpec((FFT_N1, 2 * FFT_N1), lambda i: (0, 0)),
            pl.BlockSpec((FFT_N1, 2 * MIDR, 128), lambda i: (0, 0, 0), pipeline_mode=pl.Buffered(1)),
            pl.BlockSpec((FFT_N1, N2P, 256), lambda i: (0, 0, 0), pipeline_mode=pl.Buffered(1)),
        ],
        out_specs=pl.BlockSpec((bsz, lp, TC), lambda i: (0, 0, i)),
        out_shape=jax.ShapeDtypeStruct((bsz, lp, HY_WIDTH), jnp.float32),
        scratch_shapes=[pltpu.VMEM((MIDR * SLAB, TC), jnp.float32)],
        compiler_params=pltpu.CompilerParams(dimension_semantics=("parallel",), vmem_limit_bytes=V7X_VMEM_LIMIT),
        name="hyena_long_conv",
    )(u, kspec, f1, i2, ab, abt)


def _short_conv_rows(u, w):
    y = pltpu.roll(u, 1, 0) * w[0:1] + u * w[1:2] + pltpu.roll(u, LP - 1, 0) * w[2:3]
    row = lax.broadcasted_iota(jnp.int32, y.shape, 0)
    return jnp.where(row >= PAD, y, 0.0)


def _short_conv_kernel(u_ref, w_ref, o_ref):
    o_ref[...] = _short_conv_rows(u_ref[...].astype(jnp.float32), w_ref[...])


def _gate1_kernel(v_ref, u_ref, w_ref, y_ref, b_ref, o_ref):
    x = _short_conv_rows(u_ref[...].astype(jnp.float32), w_ref[...])
    o_ref[...] = x * (y_ref[...] + v_ref[...] * b_ref[...])


def _gate2_kernel(z_ref, u_ref, w_ref, y_ref, b_ref, g_ref, o_ref):
    g = g_ref[...].astype(jnp.float32)
    z = _short_conv_rows(u_ref[...].astype(jnp.float32), w_ref[...]) * (y_ref[...] + z_ref[...] * b_ref[...])
    o_ref[...] = (z * (g * jax.nn.sigmoid(g))).astype(o_ref.dtype)


def _seq_tile(col_block):
    return pl.BlockSpec((None, LP, TC), lambda b, i: (b, 0, col_block + i))


def _vec_tile(rows, col_block):
    return pl.BlockSpec((rows, TC), lambda b, i: (0, col_block + i))


def _short_conv(p3, w_short_t):
    bsz = p3.shape[0]
    return pl.pallas_call(
        _short_conv_kernel,
        grid=(bsz, HY_WIDTH // TC),
        in_specs=[_seq_tile(0), _vec_tile(3, 0)],
        out_specs=_seq_tile(0),
        out_shape=jax.ShapeDtypeStruct((bsz, LP, HY_WIDTH), jnp.float32),
        compiler_params=pltpu.CompilerParams(
            dimension_semantics=("parallel", "parallel"), vmem_limit_bytes=V7X_VMEM_LIMIT),
        name="hyena_short_conv",
    )(p3, w_short_t)


def _hyena_gate1(v, p3, w_short_t, y, bias):
    bsz = p3.shape[0]
    nt = HY_WIDTH // TC
    return pl.pallas_call(
        _gate1_kernel,
        grid=(bsz, nt),
        in_specs=[_seq_tile(0), _seq_tile(nt), _vec_tile(3, nt), _seq_tile(0), _vec_tile(1, 0)],
        out_specs=_seq_tile(0),
        out_shape=jax.ShapeDtypeStruct((bsz, LP, HY_WIDTH), jnp.float32),
        compiler_params=pltpu.CompilerParams(
            dimension_semantics=("parallel", "parallel"), vmem_limit_bytes=V7X_VMEM_LIMIT),
        name="hyena_gate1",
    )(v, p3, w_short_t, y, bias.reshape(1, HY_WIDTH))


def _hyena_gate2(z1, p3, w_short_t, y, bias):
    bsz = p3.shape[0]
    nt = HY_WIDTH // TC
    return pl.pallas_call(
        _gate2_kernel,
        grid=(bsz, nt),
        in_specs=[_seq_tile(0), _seq_tile(2 * nt), _vec_tile(3, 2 * nt), _seq_tile(0), _vec_tile(1, 0),
                  _seq_tile(3 * nt)],
        out_specs=_seq_tile(0),
        out_shape=jax.ShapeDtypeStruct((bsz, LP, HY_WIDTH), jnp.bfloat16),
        compiler_params=pltpu.CompilerParams(
            dimension_semantics=("parallel", "parallel"), vmem_limit_bytes=V7X_VMEM_LIMIT),
        name="hyena_gate2",
    )(z1, p3, w_short_t, y, bias.reshape(1, HY_WIDTH), p3)


def _filter_positions():
    f32 = np.float32
    l = L_TOK
    t = np.linspace(0.0, 1.0, l, dtype=f32)[:, None]
    w = f32(2.0 * math.pi) * np.arange(l, dtype=f32)[:, None] / f32(l)
    bands = np.linspace(1e-4, HY_BANDS - 1, HY_BANDS, dtype=f32)[None, :]
    z = np.concatenate([t, np.cos(bands * w), -np.sin(bands * w)], axis=-1).astype(f32)
    j = np.arange(FFT_N)
    lag = np.minimum(np.where(j < l, j, FFT_N - j), l - 1)
    zrows = np.zeros((FFT_N, 128), f32)
    zrows[:, :HY_EMB] = z[lag]
    mask = np.zeros((FFT_N, 128), f32)
    mask[:, :64] = (j < l)[:, None]
    mask[:, 64:] = ((j > FFT_N - l) | (j == 0))[:, None]
    return jnp.asarray(zrows), jnp.asarray(mask)


def _hidden_kernel(z_ref, m_ref, w1_ref, b1_ref, fr1_ref, w2_ref, b2_ref, fr2_ref, o_ref):
    hi = lax.Precision.HIGHEST
    a = jnp.dot(z_ref[...], w1_ref[...], precision=hi, preferred_element_type=jnp.float32) + b1_ref[...]
    hid = jnp.sin(fr1_ref[...] * a)
    a = jnp.dot(hid, w2_ref[...], precision=hi, preferred_element_type=jnp.float32) + b2_ref[...]
    hid = jnp.sin(fr2_ref[...] * a)
    o_ref[...] = (jnp.concatenate([hid, hid], axis=1) * m_ref[...]).astype(o_ref.dtype)


def _hyena_hidden(w1, b1, fr1, w2, b2, fr2):
    zrows, mask = _filter_positions()
    rows = FFT_N // FILL_CHUNKS
    hdim = w2.shape[0]
    w1p = jnp.zeros((128, hdim), jnp.float32).at[:HY_EMB].set(w1.astype(jnp.float32))
    tile = pl.BlockSpec((rows, 128), lambda i: (i, 0))
    full = lambda a: pl.BlockSpec(a.shape, lambda i: (0,) * a.ndim)
    vecs = [v.astype(jnp.float32).reshape(1, hdim) for v in (b1, fr1, b2, fr2)]
    w2f = w2.astype(jnp.float32)
    return pl.pallas_call(
        _hidden_kernel,
        grid=(FILL_CHUNKS,),
        in_specs=[tile, tile, full(w1p), full(vecs[0]), full(vecs[1]), full(w2f), full(vecs[2]), full(vecs[3])],
        out_specs=tile,
        out_shape=jax.ShapeDtypeStruct((FFT_N, 128), jnp.bfloat16),
        compiler_params=pltpu.CompilerParams(dimension_semantics=("parallel",), vmem_limit_bytes=V7X_VMEM_LIMIT),
        name="hyena_filter_hidden",
    )(zrows, mask, w1p, vecs[0], vecs[1], w2f, vecs[2], vecs[3])


def _odd_mix(p3, w_short, w1, b1, fr1, w2, b2, fr2, w3, bias, consts):
    min_decay = math.log(HY_TARGET) / HY_SLOW_DECAY
    max_decay = math.log(HY_TARGET) / HY_FAST_DECAY
    deltas = jnp.abs(jnp.linspace(min_decay, max_decay, HY_WIDTH, dtype=jnp.float32)).reshape(1, HY_WIDTH)
    kspec = _filter_spectrum(_hyena_hidden(w1, b1, fr1, w2, b2, fr2), w3, deltas, consts)
    w_short_t = w_short.T
    v = _short_conv(p3, w_short_t)
    y1 = _long_conv(v, kspec, 0, consts)
    z1 = _hyena_gate1(v, p3, w_short_t, y1, bias[0])
    y2 = _long_conv(z1, kspec, 1, consts)
    return _hyena_gate2(z1, p3, w_short_t, y2, bias[1])


HG_HB = 16
HG_LEVELS = (32, 16, 8, 4, 2, 1)
LOG2_E = 1.4426950408889634


def _cumsum_rows(g, rev):
    f32, bf = jnp.float32, jnp.bfloat16
    t = lax.broadcasted_iota(jnp.int32, (HG_CHUNK, 3 * HG_CHUNK), 0)
    s = lax.broadcasted_iota(jnp.int32, (HG_CHUNK, 3 * HG_CHUNK), 1) & (HG_CHUNK - 1)
    tri = jnp.where((s >= t) if rev else (s <= t), 1.0, 0.0).astype(bf)
    hi = g.astype(bf)
    r1 = g - hi.astype(f32)
    mid = r1.astype(bf)
    lo = (r1 - mid.astype(f32)).astype(bf)
    return jnp.dot(tri, jnp.concatenate([hi, mid, lo], axis=0), preferred_element_type=f32)


def _hgrn_row_masks(rev):
    row = lax.broadcasted_iota(jnp.int32, (HG_CHUNK, HEAD_DIM), 0)
    masks = {}
    for m in HG_LEVELS:
        upper = (row & m) != 0
        is_query = jnp.logical_not(upper) if rev else upper
        target = m if rev else m - 1
        selectors = [(src, (row & (2 * m - 1)) == src) for src in range(2 * m) if src != target] if m == 2 else None
        masks[m] = (is_query, jnp.where(is_query, LOG2_E, -LOG2_E), selectors)
    return masks


def _level_reference(gc, m, rev, selectors):
    if m >= 4:
        g3 = gc.reshape(HG_CHUNK // (2 * m), 2 * m, HEAD_DIM)
        ref = g3[:, m:m + 1, :] if rev else g3[:, m - 1:m, :]
        return jnp.broadcast_to(ref, g3.shape).reshape(HG_CHUNK, HEAD_DIM)
    target = m if rev else m - 1
    out = gc
    for src, picks_src in selectors:
        out = jnp.where(picks_src, pltpu.roll(gc, (src - target) % HG_CHUNK, 0), out)
    return out


def _hgrn_gates(q_raw, f_raw, la, lc, omlb, rev):
    q = q_raw * (0.5 * HEAD_DIM ** -0.5) * (1.0 + jnp.tanh(0.5 * q_raw))
    k = (0.5 * omlb) * (1.0 - jnp.tanh(0.5 * f_raw))
    log_sig = jnp.minimum(f_raw, 0.0) - jnp.log(1.0 + jnp.exp(-jnp.abs(f_raw)))
    b = lc + log_sig
    g = jnp.maximum(la, b) + jnp.log(1.0 + jnp.exp(-jnp.abs(la - b)))
    return q, k, g, _cumsum_rows(g, rev)


def _hgrn_level_operands(q, k, g, gc, rev, masks):
    bf = jnp.bfloat16
    qb, kb = q.astype(bf), k.astype(bf)
    ops = []
    for m in HG_LEVELS:
        is_query, sign_log2e, selectors = masks[m]
        if m == 1:
            ops.append(jnp.where(is_query, qb * jnp.exp2(g * LOG2_E).astype(bf), kb))
        else:
            gref = _level_reference(gc, m, rev, selectors)
            ops.append(jnp.where(is_query, qb, kb) * jnp.exp2((gc - gref) * sign_log2e).astype(bf))
    return ops


def _hgrn_level_id(rev):
    t = lax.broadcasted_iota(jnp.int32, (HG_CHUNK, HG_CHUNK), 0)
    s = lax.broadcasted_iota(jnp.int32, (HG_CHUNK, HG_CHUNK), 1)
    vis = (s >= t) if rev else (s <= t)
    x = t ^ s
    lvl = jnp.full((HG_CHUNK, HG_CHUNK), len(HG_LEVELS), jnp.int32)
    for i, m in enumerate(HG_LEVELS):
        lvl = jnp.where((x & m) != 0, jnp.minimum(lvl, i), lvl)
    return jnp.where(vis, lvl, -1)


def _hgrn_kernel(qf_ref, if_ref, ff_ref, qb_ref, ib_ref, fb_ref, la_ref, lc_ref, om_ref, of_ref, ob_ref, s_ref):
    f32, bf = jnp.float32, jnp.bfloat16
    nt = (((1,), (1,)), ((), ()))
    tn = (((0,), (0,)), ((), ()))

    @pl.when(pl.program_id(2) == 0)
    def _():
        s_ref[...] = jnp.zeros_like(s_ref)

    row_masks = [_hgrn_row_masks(False), _hgrn_row_masks(True)]
    refs = ((qf_ref, if_ref, ff_ref, of_ref), (qb_ref, ib_ref, fb_ref, ob_ref))
    inst = [(d, h) for d in range(2) for h in range(HG_HB)]
    level_masks = [[lvl == i for i in range(len(HG_LEVELS) + 1)]
                   for lvl in (_hgrn_level_id(False), _hgrn_level_id(True))]

    pre = []
    for d, h in inst:
        q_ref, i_ref, f_ref, _ = refs[d]
        sl = slice(h * HEAD_DIM, (h + 1) * HEAD_DIM)
        q, k, g, gc = _hgrn_gates(q_ref[:, sl], f_ref[:, sl], la_ref[d:d + 1, sl], lc_ref[d:d + 1, sl],
                                  om_ref[d:d + 1, sl], d == 1)
        g_end = gc[0:1, :] if d == 1 else gc[HG_CHUNK - 1:HG_CHUNK, :]
        pre.append(dict(ops=_hgrn_level_operands(q, k, g, gc, d == 1, row_masks[d]), qk=(q.astype(bf), k.astype(bf)),
                        qe=(q * jnp.exp(gc)).astype(bf), kd=(k * jnp.exp(g_end - gc)).astype(bf),
                        decay=jnp.exp(g_end), v=i_ref[:, sl].astype(bf)))

    attn = []
    for (d, h), pr in zip(inst, pre):
        prods = [lax.dot_general(x, x, nt, preferred_element_type=f32) for x in pr["ops"]]
        prods.append(lax.dot_general(pr["qk"][0], pr["qk"][1], nt, preferred_element_type=f32))
        a = jnp.zeros((HG_CHUNK, HG_CHUNK), f32)
        for i, pmat in enumerate(prods):
            a = jnp.where(level_masks[d][i], pmat, a)
        attn.append(a.astype(bf))

    for (d, h), pr, a in zip(inst, pre, attn):
        o_ref = refs[d][3]
        sl = slice(h * HEAD_DIM, (h + 1) * HEAD_DIM)
        state_t = s_ref[d, h]
        o = jnp.dot(a, pr["v"], preferred_element_type=f32)
        o = o + lax.dot_general(pr["qe"], state_t.astype(bf), nt, preferred_element_type=f32)
        o_ref[:, sl] = o.astype(o_ref.dtype)
        s_ref[d, h] = pr["decay"] * state_t + lax.dot_general(pr["v"], pr["kd"], tn, preferred_element_type=f32)


def _hgrn2(p, col0, lb):
    bsz, lp, _ = p.shape
    nchunk = lp // HG_CHUNK
    cw = HG_HB * HEAD_DIM
    ng = HG_WIDTH // cw
    off = col0 // cw
    la = jnp.log(lb)
    lc = jnp.log1p(-lb)
    om = 1.0 - lb
    fwd = lambda seg: pl.BlockSpec((None, HG_CHUNK, cw), lambda b, g, c: (b, c, off + seg * ng + g))
    bwd = lambda seg: pl.BlockSpec((None, HG_CHUNK, cw), lambda b, g, c: (b, nchunk - 1 - c, off + seg * ng + g))
    par = pl.BlockSpec((2, cw), lambda b, g, c: (0, g))
    return pl.pallas_call(
        _hgrn_kernel,
        grid=(bsz, ng, nchunk),
        in_specs=[fwd(0), fwd(1), fwd(2), bwd(0), bwd(1), bwd(3), par, par, par],
        out_specs=[pl.BlockSpec((None, HG_CHUNK, cw), lambda b, g, c: (b, c, g)),
                   pl.BlockSpec((None, HG_CHUNK, cw), lambda b, g, c: (b, nchunk - 1 - c, g))],
        out_shape=[jax.ShapeDtypeStruct((bsz, lp, HG_WIDTH), jnp.bfloat16)] * 2,
        scratch_shapes=[pltpu.VMEM((2, HG_HB, HEAD_DIM, HEAD_DIM), jnp.float32)],
        compiler_params=pltpu.CompilerParams(
            dimension_semantics=("parallel", "parallel", "arbitrary"), vmem_limit_bytes=V7X_VMEM_LIMIT),
        name="hgrn2_bidirectional",
    )(p, p, p, p, p, p, la, lc, om)


EG_ROWS = 320


def _even_gate_kernel(oa_ref, of_ref, ob_ref, ga_ref, gb_ref, gain_ref, y_ref):
    ga = ga_ref[...]
    y_ref[:, :NA_WIDTH] = (oa_ref[...].astype(jnp.float32) * (ga * jax.nn.sigmoid(ga))).astype(y_ref.dtype)
    for h in range(HG_HEADS):
        sl = slice(h * HEAD_DIM, (h + 1) * HEAD_DIM)
        oh = of_ref[:, sl].astype(jnp.float32) + ob_ref[:, sl].astype(jnp.float32)
        oh = oh * lax.rsqrt(jnp.mean(oh * oh, axis=-1, keepdims=True) + EPS) * gain_ref[:, sl]
        gb = gb_ref[:, sl]
        y_ref[:, NA_WIDTH + h * HEAD_DIM:NA_WIDTH + (h + 1) * HEAD_DIM] = (
            oh * (gb * jax.nn.sigmoid(gb))).astype(y_ref.dtype)


def _even_gate(oa, o_fwd, o_bwd, p, gate_col0, norm_gain):
    bsz, lp, _ = oa.shape
    assert gate_col0 % NA_WIDTH == 0
    goff = gate_col0 // NA_WIDTH
    half = lambda off: pl.BlockSpec((None, EG_ROWS, NA_WIDTH), lambda b, r: (b, r, off))
    return pl.pallas_call(
        _even_gate_kernel,
        grid=(bsz, lp // EG_ROWS),
        in_specs=[half(0), half(0), half(0), half(goff), half(goff + 1),
                  pl.BlockSpec((1, HG_WIDTH), lambda b, r: (0, 0))],
        out_specs=pl.BlockSpec((None, EG_ROWS, D_MODEL), lambda b, r: (b, r, 0)),
        out_shape=jax.ShapeDtypeStruct((bsz, lp, D_MODEL), jnp.bfloat16),
        compiler_params=pltpu.CompilerParams(
            dimension_semantics=("parallel", "parallel"), vmem_limit_bytes=V7X_VMEM_LIMIT),
        name="even_gate",
    )(oa, o_fwd, o_bwd, p, p, norm_gain.reshape(1, HG_WIDTH))


NA_MASKED = -1e30
NA_GROUP = 16


def _na_bias_table(rpb):
    cols = np.arange(GRID_W)
    c0 = np.clip(cols - NA_KW // 2, 0, GRID_W - NA_KW)
    col_ok = (cols[None, :] >= c0[:, None]) & (cols[None, :] < c0[:, None] + NA_KW)
    col_idx = np.clip(cols[None, :] - cols[:, None] + NA_KW - 1, 0, 2 * NA_KW - 2)
    onehot = (col_idx[None] == np.arange(2 * NA_KW - 1)[:, None, None]).astype(np.float32)
    picked = jnp.einsum('hrj,jqk->hrqk', rpb.astype(jnp.float32), onehot, precision=lax.Precision.HIGHEST)
    table = jnp.where(col_ok[None, None], picked, NA_MASKED)
    return jnp.concatenate([table[:, :-1], table[:, 1:]], axis=-1)


def _na_kernel(q_ref, k_ref, v_ref, t_ref, mb_ref, o_ref):
    f32, bf = jnp.float32, jnp.bfloat16
    rows = (q_ref.shape[0] - HG_CHUNK) // GRID_W
    scale = HEAD_DIM ** -0.5
    nt = (((1,), (1,)), ((), ()))
    km = k_ref[PAD:PAD + N_META, :].astype(bf)
    vm = v_ref[PAD:PAD + N_META, :].astype(bf)
    mb = mb_ref[...]
    o_ref[0:PAD, :] = jnp.zeros((PAD, HEAD_DIM), o_ref.dtype)

    qm = q_ref[PAD:PAD + N_META, :].astype(bf)
    s = lax.dot_general(qm, km, nt, preferred_element_type=f32) * scale + mb
    e = jnp.exp(s - jnp.max(s, axis=-1, keepdims=True))
    o_ref[PAD:PAD + N_META, :] = (jnp.dot(e.astype(bf), vm, preferred_element_type=f32)
                                  / jnp.sum(e, axis=-1, keepdims=True)).astype(o_ref.dtype)

    def row_group(g, carry):
        rs = [g * NA_GROUP + i for i in range(NA_GROUP)]
        scores = []
        for r in rs:
            r0 = jnp.clip(r - NA_KH // 2, 0, rows - NA_KH)
            q = q_ref[pl.ds(pl.multiple_of(HG_CHUNK + GRID_W * r, GRID_W), GRID_W), :].astype(bf)
            k0 = pl.multiple_of(HG_CHUNK + GRID_W * r0, GRID_W)
            kw = k_ref[pl.ds(k0, NA_KH * GRID_W), :].astype(bf)
            s_win = lax.dot_general(q, kw, nt, preferred_element_type=f32)
            s_meta = lax.dot_general(q, km, nt, preferred_element_type=f32)
            scores.append((r, r0, k0, s_win, s_meta))
        probs = []
        for r, r0, k0, s_win, s_meta in scores:
            bias = jnp.concatenate([t_ref[r0 + i - r + NA_KH - 1] for i in range(0, NA_KH, 2)], axis=1)
            s_win = s_win * scale + bias
            s_meta = s_meta * scale + mb
            m = jnp.maximum(jnp.max(s_win, axis=-1, keepdims=True), jnp.max(s_meta, axis=-1, keepdims=True))
            pw = jnp.exp(s_win - m)
            pm = jnp.exp(s_meta - m)
            den = jnp.sum(pw, axis=-1, keepdims=True) + jnp.sum(pm, axis=-1, keepdims=True)
            probs.append((r, k0, pw.astype(bf), pm.astype(bf), den))
        for r, k0, pw, pm, den in probs:
            vw = v_ref[pl.ds(k0, NA_KH * GRID_W), :].astype(bf)
            o = jnp.dot(pw, vw, preferred_element_type=f32) + jnp.dot(pm, vm, preferred_element_type=f32)
            o_ref[pl.ds(pl.multiple_of(HG_CHUNK + GRID_W * r, GRID_W), GRID_W), :] = (o / den).astype(o_ref.dtype)
        return carry

    lax.fori_loop(0, rows // NA_GROUP, row_group, 0)


def _natten(p, rpb, meta_bias):
    bsz, lp, _ = p.shape
    assert ((lp - HG_CHUNK) // GRID_W) % NA_GROUP == 0
    heads = NA_WIDTH // HEAD_DIM
    blk = lambda seg: pl.BlockSpec((None, lp, HEAD_DIM), lambda b, h: (b, 0, seg * heads + h))
    table = _na_bias_table(rpb)
    return pl.pallas_call(
        _na_kernel,
        grid=(bsz, heads),
        in_specs=[blk(0), blk(1), blk(2),
                  pl.BlockSpec((None, 2 * NA_KH - 2, GRID_W, 2 * GRID_W), lambda b, h: (h, 0, 0, 0)),
                  pl.BlockSpec((None, 1, N_META), lambda b, h: (h, 0, 0))],
        out_specs=pl.BlockSpec((None, lp, HEAD_DIM), lambda b, h: (b, 0, h)),
        out_shape=jax.ShapeDtypeStruct((bsz, lp, NA_WIDTH), jnp.bfloat16),
        compiler_params=pltpu.CompilerParams(
            dimension_semantics=("parallel", "parallel"), vmem_limit_bytes=V7X_VMEM_LIMIT),
        name="neighbourhood_attention",
    )(p, p, p, table, meta_bias.astype(jnp.float32).reshape(heads, 1, N_META))


def _even_mix(p, rpb, meta_bias, lb, norm_gain):
    oa = _natten(p, rpb, meta_bias)
    o_fwd, o_bwd = _hgrn2(p, 3 * NA_WIDTH, lb)
    return _even_gate(oa, o_fwd, o_bwd, p, 3 * NA_WIDTH + 4 * HG_WIDTH, norm_gain)


def kernel(x, meta_tokens, norm_pre, norm_post, ev_w_in, ev_w_out, na_rpb, na_meta_bias, hg_lower, hg_norm,
           od_w_in, od_w_out, hy_short, hy_ffn_w1, hy_ffn_b1, hy_ffn_freq1, hy_ffn_w2, hy_ffn_b2, hy_ffn_freq2,
           hy_ffn_w3, hy_bias):
    b = x.shape[0]
    depth = norm_pre.shape[0]
    f32 = jnp.float32
    lb_all = jnp.cumsum(jax.nn.softmax(hg_lower.astype(f32), axis=0), axis=0)
    lb_all = lb_all - lb_all[:1]
    h, hn = _embed(x, meta_tokens, norm_pre[0])
    h = h.reshape(b * LP, D_MODEL)
    hn = hn.reshape(b * LP, D_MODEL)
    consts = _dft_constants()
    for layer in range(depth):
        j = layer // 2
        if layer % 2 == 0:
            p = _matmul(hn, ev_w_in, j, f32).reshape(b, LP, -1)
            y = _even_mix(p, na_rpb[j], na_meta_bias[j], lb_all[j], hg_norm[j]).reshape(b * LP, D_MODEL)
            w_out = ev_w_out
        else:
            p3 = _matmul(hn, od_w_in, j, jnp.bfloat16).reshape(b, LP, -1)
            y = _odd_mix(p3, hy_short[j], hy_ffn_w1[j], hy_ffn_b1[j], hy_ffn_freq1[j], hy_ffn_w2[j],
                         hy_ffn_b2[j], hy_ffn_freq2[j], hy_ffn_w3[j], hy_bias[j], consts)
            y = y.reshape(b * LP, D_MODEL)
            w_out = od_w_out
        out = _matmul(y, w_out, j, jnp.bfloat16)
        if layer + 1 < depth:
            h, hn = _residual_rmsnorm(h, out, norm_post[layer], norm_pre[layer + 1])
    return _final_residual(h.reshape(b, LP, D_MODEL), out.reshape(b, LP, D_MODEL), norm_post[depth - 1], x.dtype)
```

```python
import math

import jax
import jax.numpy as jnp
import numpy as np
from jax import lax
from jax.experimental import pallas as pl
from jax.experimental.pallas import tpu as pltpu

D_MODEL = 4096
SEQ = 4096
N_META = 16
GRID_W = 64
HEAD_DIM = 128
NA_WIDTH = D_MODEL // 2
NA_KH = 8
NA_KW = 16
HG_WIDTH = D_MODEL // 2
HG_HEADS = HG_WIDTH // HEAD_DIM
HG_CHUNK = 64
HY_WIDTH = D_MODEL
HY_ORDER = 2
HY_EMB = 33
HY_BANDS = (HY_EMB - 1) // 2
HY_FAST_DECAY = 0.3
HY_SLOW_DECAY = 1.5
HY_TARGET = 1e-2
EPS = 1e-6

PAD = HG_CHUNK - N_META
LP = PAD + N_META + SEQ
L_TOK = N_META + SEQ

V7X_VMEM_LIMIT = 56 * 1024 * 1024


EMBED_PIECES = 5


def _embed_kernel(*refs):
    x_refs, meta_ref, g_ref, h_ref, n_ref = refs[:EMBED_PIECES], *refs[EMBED_PIECES:]
    first = jnp.concatenate([jnp.zeros((PAD, D_MODEL), jnp.float32), meta_ref[...].astype(jnp.float32)], axis=0)
    pieces = [jnp.where(pl.program_id(1) == 0, first, x_refs[0][...].astype(jnp.float32))]
    pieces += [x_ref[...].astype(jnp.float32) for x_ref in x_refs[1:]]
    h = jnp.concatenate(pieces, axis=0)
    h_ref[...] = h
    r = lax.rsqrt(jnp.mean(h * h, axis=-1, keepdims=True) + EPS)
    n_ref[...] = (h * r * g_ref[...]).astype(n_ref.dtype)


def _embed(x, meta_tokens, g):
    bsz, seq, d = x.shape
    assert seq == SEQ and LP % (EMBED_PIECES * HG_CHUNK) == 0
    tr = EMBED_PIECES * HG_CHUNK
    piece = lambda k: pl.BlockSpec((None, HG_CHUNK, d), lambda b, r: (b, jnp.maximum(EMBED_PIECES * r + k - 1, 0), 0))
    row = pl.BlockSpec((None, tr, d), lambda b, r: (b, r, 0))
    return pl.pallas_call(
        _embed_kernel,
        grid=(bsz, LP // tr),
        in_specs=[piece(k) for k in range(EMBED_PIECES)] + [pl.BlockSpec((N_META, d), lambda b, r: (0, 0)),
                                                           pl.BlockSpec((1, d), lambda b, r: (0, 0))],
        out_specs=[row, row],
        out_shape=[jax.ShapeDtypeStruct((bsz, LP, d), jnp.float32), jax.ShapeDtypeStruct((bsz, LP, d), jnp.bfloat16)],
        compiler_params=pltpu.CompilerParams(
            dimension_semantics=("parallel", "parallel"), vmem_limit_bytes=V7X_VMEM_LIMIT),
        name="embed_rmsnorm_pre",
    )(*([x] * EMBED_PIECES), meta_tokens, g.reshape(1, d))


def _post_pre_kernel(h_ref, y_ref, g_ref, gn_ref, o_ref, n_ref):
    y = y_ref[...].astype(jnp.float32)
    r = lax.rsqrt(jnp.mean(y * y, axis=-1, keepdims=True) + EPS)
    h = h_ref[...] + y * r * g_ref[...]
    o_ref[...] = h
    rn = lax.rsqrt(jnp.mean(h * h, axis=-1, keepdims=True) + EPS)
    n_ref[...] = (h * rn * gn_ref[...]).astype(n_ref.dtype)


FINAL_PIECES = 4


def _final_kernel(*refs):
    h_refs, y_refs = refs[:FINAL_PIECES], refs[FINAL_PIECES:2 * FINAL_PIECES]
    g_ref, o_ref = refs[2 * FINAL_PIECES:]
    for k, (h_ref, y_ref) in enumerate(zip(h_refs, y_refs)):
        y = y_ref[...].astype(jnp.float32)
        r = lax.rsqrt(jnp.mean(y * y, axis=-1, keepdims=True) + EPS)
        o_ref[k * HG_CHUNK:(k + 1) * HG_CHUNK, :] = (h_ref[...] + y * r * g_ref[...]).astype(o_ref.dtype)


def _final_residual(h3, y3, g, out_dtype):
    bsz, lp, d = h3.shape
    assert PAD + N_META == HG_CHUNK and SEQ % (FINAL_PIECES * HG_CHUNK) == 0
    piece = lambda k: pl.BlockSpec((None, HG_CHUNK, d), lambda b, r: (b, FINAL_PIECES * r + k + 1, 0))
    pieces = [piece(k) for k in range(FINAL_PIECES)]
    return pl.pallas_call(
        _final_kernel,
        grid=(bsz, SEQ // (FINAL_PIECES * HG_CHUNK)),
        in_specs=pieces + pieces + [pl.BlockSpec((1, d), lambda b, r: (0, 0))],
        out_specs=pl.BlockSpec((None, FINAL_PIECES * HG_CHUNK, d), lambda b, r: (b, r, 0)),
        out_shape=jax.ShapeDtypeStruct((bsz, SEQ, d), out_dtype),
        compiler_params=pltpu.CompilerParams(
            dimension_semantics=("parallel", "parallel"), vmem_limit_bytes=V7X_VMEM_LIMIT),
        name="residual_rmsnorm_post_final",
    )(*([h3] * FINAL_PIECES), *([y3] * FINAL_PIECES), g.reshape(1, d))


def _residual_rmsnorm(h2d, y2d, g, g_next, tr=320):
    m, d = h2d.shape
    row = pl.BlockSpec((tr, d), lambda i: (i, 0))
    vec = pl.BlockSpec((1, d), lambda i: (0, 0))
    params = pltpu.CompilerParams(dimension_semantics=("parallel",), vmem_limit_bytes=V7X_VMEM_LIMIT)
    return pl.pallas_call(
        _post_pre_kernel, grid=(m // tr,), in_specs=[row, row, vec, vec], out_specs=[row, row],
        out_shape=[jax.ShapeDtypeStruct((m, d), jnp.float32), jax.ShapeDtypeStruct((m, d), jnp.bfloat16)],
        compiler_params=params, name="residual_rmsnorm_post_pre",
    )(h2d, y2d, g.reshape(1, d), g_next.reshape(1, d))


def _matmul_kernel(x_ref, w_ref, o_ref):
    o_ref[...] = jnp.dot(x_ref[...], w_ref[...].astype(jnp.bfloat16),
                         preferred_element_type=jnp.float32).astype(o_ref.dtype)


def _matmul(x, w_stack, layer, out_dtype, tm=1040, tn=512):
    m, kd = x.shape
    _, _, n = w_stack.shape
    assert m % tm == 0 and n % tn == 0
    return pl.pallas_call(
        _matmul_kernel,
        grid=(m // tm, n // tn),
        in_specs=[pl.BlockSpec((tm, kd), lambda i, j: (i, 0)),
                  pl.BlockSpec((None, kd, tn), lambda i, j: (layer, 0, j))],
        out_specs=pl.BlockSpec((tm, tn), lambda i, j: (i, j)),
        out_shape=jax.ShapeDtypeStruct((m, n), out_dtype),
        compiler_params=pltpu.CompilerParams(
            dimension_semantics=("parallel", "parallel"), vmem_limit_bytes=V7X_VMEM_LIMIT),
        name="projection_matmul",
    )(x, w_stack)


FFT_N1 = 128
FFT_N2 = 65
FFT_N = FFT_N1 * FFT_N2
N2P = 80
MIDR = 72
SLAB = 264
TC = 128
MID_GROUP = 16
NC_HALF = FFT_N1 // 2 + 1
HALF_P = 72
SLAB_S = 152
C_GROUP = 13
FILL_CHUNKS = 13
B_GROUP = 13


def _dft_constants():
    a = np.arange(FFT_N1)
    c = np.arange(FFT_N1)
    w1 = np.exp(-2j * np.pi * np.outer(c, a) / FFT_N1)
    w1r, w1i = w1.real, w1.imag
    half = FFT_N1 // 2
    f1 = np.block([[w1r[:, :half], -w1i[:, :half]], [w1i[:, :half], w1r[:, :half]]])
    f1_real = np.zeros((2 * HALF_P, FFT_N1), np.float64)
    f1_real[:NC_HALF] = w1r[:NC_HALF]
    f1_real[HALF_P:HALF_P + NC_HALF] = w1i[:NC_HALF]
    g = np.conj(w1).T
    gr, gi = g.real[:half], g.imag[:half]
    i2 = np.block([[gr, -gi], [gi, gr]])
    b = np.arange(FFT_N2)
    d = np.arange(FFT_N2)
    w2 = np.exp(-2j * np.pi * np.outer(d, b) / FFT_N2)
    tw = np.exp(-2j * np.pi * np.outer(c, b) / FFT_N)
    m = w2[None, :, :] * tw[:, None, :]
    ab = np.zeros((FFT_N1, 2 * MIDR, 128), np.float64)
    m[NC_HALF:] = m[NC_HALF:, ::-1, :]
    ab[:, :FFT_N2, :FFT_N2] = m.real
    ab[:, MIDR:MIDR + FFT_N2, :FFT_N2] = m.imag
    bf = jnp.bfloat16
    abt = np.zeros((FFT_N1, N2P, 256), np.float64)
    abt[:, :, :2 * MIDR] = np.swapaxes(ab[:, :, :N2P], 1, 2)
    return (jnp.asarray(f1, bf), jnp.asarray(f1_real, bf), jnp.asarray(i2, bf), jnp.asarray(ab, bf),
            jnp.asarray(abt, bf))


def _middle_stage_inputs(y_ref, c, imag_row=FFT_N1, pitch=SLAB):
    yr = y_ref[pl.ds(c, MIDR, stride=pitch), :]
    yi = y_ref[pl.ds(imag_row + c, MIDR, stride=pitch), :]
    return jnp.concatenate([yr, yi], axis=1)


def _forward_middle(ab, y2):
    y2 = jnp.concatenate([y2, jnp.zeros((128 - MIDR, 2 * TC), jnp.float32)], axis=0).astype(jnp.bfloat16)
    q = jnp.dot(ab, y2, preferred_element_type=jnp.float32)
    zr = q[:MIDR, :TC] - q[MIDR:, TC:]
    zi = q[MIDR:, :TC] + q[:MIDR, TC:]
    return zr, zi


def _spectrum_kernel(h_ref, t_ref, w3f0_ref, w3f1_ref, w3b0_ref, w3b1_ref, delta_ref, f1_ref, ab_ref, k_ref,
                     kt_ref, y_ref):
    w3 = jnp.concatenate([jnp.concatenate([w3f0_ref[...], w3f1_ref[...]], axis=1),
                          jnp.concatenate([w3b0_ref[...], w3b1_ref[...]], axis=1)], axis=0).astype(jnp.bfloat16)
    delta = delta_ref[...]
    rows = FFT_N // FILL_CHUNKS

    def fill(i, carry):
        r0 = pl.multiple_of(i * rows, 8)
        kt = jnp.dot(h_ref[pl.ds(r0, rows), :], w3, preferred_element_type=jnp.float32)
        window = jnp.exp(t_ref[pl.ds(r0, rows), :] * delta)
        for n in range(HY_ORDER):
            kt_ref[n, pl.ds(r0, rows), :] = kt[:, n * TC:(n + 1) * TC] * window
        return carry

    lax.fori_loop(0, FILL_CHUNKS, fill, 0)
    y_ref[pl.ds(FFT_N2 * SLAB_S, (MIDR - FFT_N2) * SLAB_S), :] = jnp.zeros(((MIDR - FFT_N2) * SLAB_S, TC), jnp.float32)

    for n in range(HY_ORDER):
        def stage1(g, carry):
            bs = [g * B_GROUP + i for i in range(B_GROUP)]
            x = jnp.concatenate([kt_ref.at[n][pl.ds(b, FFT_N1, stride=FFT_N2), :] for b in bs], axis=1)
            y = jnp.dot(f1_ref[...], x.astype(jnp.bfloat16), preferred_element_type=jnp.float32)
            for i, b in enumerate(bs):
                y_ref[pl.ds(pl.multiple_of(b * SLAB_S, 8), 2 * HALF_P), :] = y[:, i * TC:(i + 1) * TC]
            return carry

        lax.fori_loop(0, FFT_N2 // B_GROUP, stage1, 0)

        def stage2(g, carry):
            cs = [g * C_GROUP + i for i in range(C_GROUP)]
            zs = [_forward_middle(ab_ref[c], _middle_stage_inputs(y_ref, c, HALF_P, SLAB_S)) for c in cs]
            for c, (zr, zi) in zip(cs, zs):
                z = jnp.concatenate([zr, zi], axis=1) * (1.0 / FFT_N)
                k_ref[n, c] = jnp.concatenate(
                    [z, jnp.zeros((N2P - MIDR, 2 * TC), jnp.float32)], axis=0).astype(jnp.bfloat16)
            return carry

        lax.fori_loop(0, NC_HALF // C_GROUP, stage2, 0)


def _filter_spectrum(h2, w3, deltas, consts):
    _, f1_real, _, ab, _ = consts
    nt = HY_WIDTH // TC
    j = np.arange(FFT_N)
    lag = np.where(j < L_TOK, j, FFT_N - j).astype(np.float32)
    neg_t = jnp.asarray(np.broadcast_to((-lag / np.float32(L_TOK - 1))[:, None], (FFT_N, TC)))
    w3_cols = lambda direction, order: pl.BlockSpec((64, TC), lambda i: (0, (direction * HY_ORDER + order) * nt + i))
    return pl.pallas_call(
        _spectrum_kernel,
        grid=(nt,),
        in_specs=[
            pl.BlockSpec((FFT_N, 128), lambda i: (0, 0)),
            pl.BlockSpec((FFT_N, TC), lambda i: (0, 0)),
            w3_cols(0, 0), w3_cols(0, 1), w3_cols(1, 0), w3_cols(1, 1),
            pl.BlockSpec((1, TC), lambda i: (0, i)),
            pl.BlockSpec((2 * HALF_P, FFT_N1), lambda i: (0, 0)),
            pl.BlockSpec((FFT_N1, 2 * MIDR, 128), lambda i: (0, 0, 0)),
        ],
        out_specs=pl.BlockSpec((HY_ORDER, None, NC_HALF, N2P, 2 * TC), lambda i: (0, i, 0, 0, 0)),
        out_shape=jax.ShapeDtypeStruct((HY_ORDER, nt, NC_HALF, N2P, 2 * TC), jnp.bfloat16),
        scratch_shapes=[pltpu.VMEM((HY_ORDER, FFT_N, TC), jnp.float32), pltpu.VMEM((MIDR * SLAB_S, TC), jnp.float32)],
        compiler_params=pltpu.CompilerParams(dimension_semantics=("parallel",), vmem_limit_bytes=V7X_VMEM_LIMIT),
        name="hyena_filter_spectrum",
    )(h2, neg_t, w3, w3, w3, w3, deltas, f1_real, ab)


def _conv_kernel(u_ref, k_ref, f1_ref, i2_ref, ab_ref, abt_ref, o_ref, y_ref):
    _conv_body(u_ref, k_ref, f1_ref, i2_ref, ab_ref, abt_ref, o_ref, y_ref)


def _conv_raw_kernel(p_ref, w_ref, k_ref, f1_ref, i2_ref, ab_ref, abt_ref, o_ref, v_ref, y_ref):
    for ri in range(2):
        v_ref[ri] = _short_conv_rows(p_ref[ri].astype(jnp.float32), w_ref[...])
    _conv_body(v_ref, k_ref, f1_ref, i2_ref, ab_ref, abt_ref, o_ref, y_ref)


def _conv_body(u_ref, k_ref, f1_ref, i2_ref, ab_ref, abt_ref, o_ref, y_ref):
    y_ref[pl.ds(FFT_N2 * SLAB, (MIDR - FFT_N2) * SLAB), :] = jnp.zeros(((MIDR - FFT_N2) * SLAB, TC), jnp.float32)
    half = FFT_N1 // 2

    def stage1(g, carry):
        bs = [g * B_GROUP + i for i in range(B_GROUP)]
        x = jnp.concatenate(
            [jnp.concatenate([u_ref.at[ri][pl.ds(b, half, stride=FFT_N2), :] for b in bs], axis=1) for ri in range(2)],
            axis=0)
        y = jnp.dot(f1_ref[...], x.astype(jnp.bfloat16), preferred_element_type=jnp.float32)
        for i, b in enumerate(bs):
            y_ref[pl.ds(pl.multiple_of(b * SLAB, 8), 2 * FFT_N1), :] = y[:, i * TC:(i + 1) * TC]
        return carry

    lax.fori_loop(0, FFT_N2 // B_GROUP, stage1, 0)

    def inverse_middle(c, z):
        zr, zi = z
        mirrored = c >= NC_HALF
        kk = k_ref[jnp.where(mirrored, FFT_N1 - c, c)].astype(jnp.float32)[:MIDR]
        kr, ki = kk[:, :TC], kk[:, TC:] * jnp.where(mirrored, -1.0, 1.0)
        pr = zr * kr - zi * ki
        pi = zr * ki + zi * kr
        rhs = jnp.concatenate([jnp.concatenate([pr, pi], axis=1), jnp.concatenate([pi, -pr], axis=1),
                               jnp.zeros((256 - 2 * MIDR, 2 * TC), jnp.float32)], axis=0)
        return jnp.dot(abt_ref[c], rhs.astype(jnp.bfloat16), preferred_element_type=jnp.float32)

    def middle(g, carry):
        cs = [g * MID_GROUP + i for i in range(MID_GROUP)]
        zs = [_forward_middle(ab_ref[c], _middle_stage_inputs(y_ref, c)) for c in cs]
        outs = [inverse_middle(c, z) for c, z in zip(cs, zs)]
        for c, v in zip(cs, outs):
            y_ref[pl.ds(c, MIDR, stride=SLAB), :] = v[:MIDR, :TC]
            y_ref[pl.ds(FFT_N1 + c, MIDR, stride=SLAB), :] = v[:MIDR, TC:]
        return carry

    lax.fori_loop(0, FFT_N1 // MID_GROUP, middle, 0)

    def stage4(g, carry):
        bs = [g * B_GROUP + i for i in range(B_GROUP)]
        v = jnp.concatenate([y_ref[pl.ds(pl.multiple_of(b * SLAB, 8), 2 * FFT_N1), :] for b in bs], axis=1)
        x = jnp.dot(i2_ref[...], v.astype(jnp.bfloat16), preferred_element_type=jnp.float32)
        for i, b in enumerate(bs):
            o_ref.at[0][pl.ds(b, half, stride=FFT_N2), :] = x[:half, i * TC:(i + 1) * TC]
            o_ref.at[1][pl.ds(b, half, stride=FFT_N2), :] = x[half:, i * TC:(i + 1) * TC]
        return carry

    lax.fori_loop(0, FFT_N2 // B_GROUP, stage4, 0)


def _long_conv(u, kspec, order, consts, w_short_t=None):
    f1, _, i2, ab, abt = consts
    bsz, lp, _ = u.shape
    assert bsz == 2 and lp == LP
    nt = HY_WIDTH // TC
    seq = pl.BlockSpec((bsz, lp, TC), lambda i: (0, 0, i))
    out = jax.ShapeDtypeStruct((bsz, lp, HY_WIDTH), jnp.float32)
    raw = w_short_t is not None
    return pl.pallas_call(
        _conv_raw_kernel if raw else _conv_kernel,
        grid=(nt,),
        in_specs=[seq] + ([pl.BlockSpec((3, TC), lambda i: (0, i))] if raw else []) + [
            pl.BlockSpec((None, None, NC_HALF, N2P, 2 * TC), lambda i: (order, i, 0, 0, 0)),
            pl.BlockSpec((2 * FFT_N1, FFT_N1), lambda i: (0, 0)),
            pl.BlockSpec((FFT_N1, 2 * FFT_N1), lambda i: (0, 0)),
            pl.BlockSpec((FFT_N1, 2 * MIDR, 128), lambda i: (0, 0, 0), pipeline_mode=pl.Buffered(1)),
            pl.BlockSpec((FFT_N1, N2P, 256), lambda i: (0, 0, 0), pipeline_mode=pl.Buffered(1)),
        ],
        out_specs=[seq, seq] if raw else seq,
        out_shape=[out, out] if raw else out,
        scratch_shapes=[pltpu.VMEM((MIDR * SLAB, TC), jnp.float32)],
        compiler_params=pltpu.CompilerParams(dimension_semantics=("parallel",), vmem_limit_bytes=V7X_VMEM_LIMIT),
        name="hyena_long_conv",
    )(*((u, w_short_t) if raw else (u,)), kspec, f1, i2, ab, abt)


def _short_conv_rows(u, w):
    y = pltpu.roll(u, 1, 0) * w[0:1] + u * w[1:2] + pltpu.roll(u, LP - 1, 0) * w[2:3]
    row = lax.broadcasted_iota(jnp.int32, y.shape, 0)
    return jnp.where(row >= PAD, y, 0.0)


def _gate1_kernel(v_ref, u_ref, w_ref, y_ref, b_ref, o_ref):
    x = _short_conv_rows(u_ref[...].astype(jnp.float32), w_ref[...])
    o_ref[...] = x * (y_ref[...] + v_ref[...] * b_ref[...])


def _gate2_kernel(z_ref, u_ref, w_ref, y_ref, b_ref, g_ref, o_ref):
    g = g_ref[...].astype(jnp.float32)
    z = _short_conv_rows(u_ref[...].astype(jnp.float32), w_ref[...]) * (y_ref[...] + z_ref[...] * b_ref[...])
    o_ref[...] = (z * (g * jax.nn.sigmoid(g))).astype(o_ref.dtype)


def _seq_tile(col_block):
    return pl.BlockSpec((None, LP, TC), lambda b, i: (b, 0, col_block + i))


def _vec_tile(rows, col_block):
    return pl.BlockSpec((rows, TC), lambda b, i: (0, col_block + i))


def _hyena_gate1(v, p3, w_short_t, y, bias):
    bsz = p3.shape[0]
    nt = HY_WIDTH // TC
    return pl.pallas_call(
        _gate1_kernel,
        grid=(bsz, nt),
        in_specs=[_seq_tile(0), _seq_tile(nt), _vec_tile(3, nt), _seq_tile(0), _vec_tile(1, 0)],
        out_specs=_seq_tile(0),
        out_shape=jax.ShapeDtypeStruct((bsz, LP, HY_WIDTH), jnp.float32),
        compiler_params=pltpu.CompilerParams(
            dimension_semantics=("parallel", "parallel"), vmem_limit_bytes=V7X_VMEM_LIMIT),
        name="hyena_gate1",
    )(v, p3, w_short_t, y, bias.reshape(1, HY_WIDTH))


def _hyena_gate2(z1, p3, w_short_t, y, bias):
    bsz = p3.shape[0]
    nt = HY_WIDTH // TC
    return pl.pallas_call(
        _gate2_kernel,
        grid=(bsz, nt),
        in_specs=[_seq_tile(0), _seq_tile(2 * nt), _vec_tile(3, 2 * nt), _seq_tile(0), _vec_tile(1, 0),
                  _seq_tile(3 * nt)],
        out_specs=_seq_tile(0),
        out_shape=jax.ShapeDtypeStruct((bsz, LP, HY_WIDTH), jnp.bfloat16),
        compiler_params=pltpu.CompilerParams(
            dimension_semantics=("parallel", "parallel"), vmem_limit_bytes=V7X_VMEM_LIMIT),
        name="hyena_gate2",
    )(z1, p3, w_short_t, y, bias.reshape(1, HY_WIDTH), p3)


def _filter_positions():
    f32 = np.float32
    l = L_TOK
    t = np.linspace(0.0, 1.0, l, dtype=f32)[:, None]
    w = f32(2.0 * math.pi) * np.arange(l, dtype=f32)[:, None] / f32(l)
    bands = np.linspace(1e-4, HY_BANDS - 1, HY_BANDS, dtype=f32)[None, :]
    z = np.concatenate([t, np.cos(bands * w), -np.sin(bands * w)], axis=-1).astype(f32)
    j = np.arange(FFT_N)
    lag = np.minimum(np.where(j < l, j, FFT_N - j), l - 1)
    zrows = np.zeros((FFT_N, 128), f32)
    zrows[:, :HY_EMB] = z[lag]
    mask = np.zeros((FFT_N, 128), f32)
    mask[:, :64] = (j < l)[:, None]
    mask[:, 64:] = ((j > FFT_N - l) | (j == 0))[:, None]
    return jnp.asarray(zrows), jnp.asarray(mask)


def _hidden_kernel(z_ref, m_ref, w1_ref, b1_ref, fr1_ref, w2_ref, b2_ref, fr2_ref, o_ref):
    hi = lax.Precision.HIGHEST
    a = jnp.dot(z_ref[...], w1_ref[...], precision=hi, preferred_element_type=jnp.float32) + b1_ref[...]
    hid = jnp.sin(fr1_ref[...] * a)
    a = jnp.dot(hid, w2_ref[...], precision=hi, preferred_element_type=jnp.float32) + b2_ref[...]
    hid = jnp.sin(fr2_ref[...] * a)
    o_ref[...] = (jnp.concatenate([hid, hid], axis=1) * m_ref[...]).astype(o_ref.dtype)


def _hyena_hidden(w1, b1, fr1, w2, b2, fr2):
    zrows, mask = _filter_positions()
    rows = FFT_N // FILL_CHUNKS
    hdim = w2.shape[0]
    w1p = jnp.zeros((128, hdim), jnp.float32).at[:HY_EMB].set(w1.astype(jnp.float32))
    tile = pl.BlockSpec((rows, 128), lambda i: (i, 0))
    full = lambda a: pl.BlockSpec(a.shape, lambda i: (0,) * a.ndim)
    vecs = [v.astype(jnp.float32).reshape(1, hdim) for v in (b1, fr1, b2, fr2)]
    w2f = w2.astype(jnp.float32)
    return pl.pallas_call(
        _hidden_kernel,
        grid=(FILL_CHUNKS,),
        in_specs=[tile, tile, full(w1p), full(vecs[0]), full(vecs[1]), full(w2f), full(vecs[2]), full(vecs[3])],
        out_specs=tile,
        out_shape=jax.ShapeDtypeStruct((FFT_N, 128), jnp.bfloat16),
        compiler_params=pltpu.CompilerParams(dimension_semantics=("parallel",), vmem_limit_bytes=V7X_VMEM_LIMIT),
        name="hyena_filter_hidden",
    )(zrows, mask, w1p, vecs[0], vecs[1], w2f, vecs[2], vecs[3])


def _odd_mix(p3, w_short, w1, b1, fr1, w2, b2, fr2, w3, bias, consts):
    min_decay = math.log(HY_TARGET) / HY_SLOW_DECAY
    max_decay = math.log(HY_TARGET) / HY_FAST_DECAY
    deltas = jnp.abs(jnp.linspace(min_decay, max_decay, HY_WIDTH, dtype=jnp.float32)).reshape(1, HY_WIDTH)
    kspec = _filter_spectrum(_hyena_hidden(w1, b1, fr1, w2, b2, fr2), w3, deltas, consts)
    w_short_t = w_short.T
    y1, v = _long_conv(p3, kspec, 0, consts, w_short_t)
    z1 = _hyena_gate1(v, p3, w_short_t, y1, bias[0])
    y2 = _long_conv(z1, kspec, 1, consts)
    return _hyena_gate2(z1, p3, w_short_t, y2, bias[1])


HG_HB = 16
HG_LEVELS = (32, 16, 8, 4, 2, 1)
LOG2_E = 1.4426950408889634


def _cumsum_rows(g, rev):
    f32, bf = jnp.float32, jnp.bfloat16
    t = lax.broadcasted_iota(jnp.int32, (HG_CHUNK, 3 * HG_CHUNK), 0)
    s = lax.broadcasted_iota(jnp.int32, (HG_CHUNK, 3 * HG_CHUNK), 1) & (HG_CHUNK - 1)
    tri = jnp.where((s >= t) if rev else (s <= t), 1.0, 0.0).astype(bf)
    hi = g.astype(bf)
    r1 = g - hi.astype(f32)
    mid = r1.astype(bf)
    lo = (r1 - mid.astype(f32)).astype(bf)
    return jnp.dot(tri, jnp.concatenate([hi, mid, lo], axis=0), preferred_element_type=f32)


def _hgrn_row_masks(rev):
    row = lax.broadcasted_iota(jnp.int32, (HG_CHUNK, HEAD_DIM), 0)
    masks = {}
    for m in HG_LEVELS:
        upper = (row & m) != 0
        is_query = jnp.logical_not(upper) if rev else upper
        target = m if rev else m - 1
        selectors = [(src, (row & (2 * m - 1)) == src) for src in range(2 * m) if src != target] if m == 2 else None
        masks[m] = (is_query, jnp.where(is_query, LOG2_E, -LOG2_E), selectors)
    return masks


def _level_reference(gc, m, rev, selectors):
    if m >= 4:
        g3 = gc.reshape(HG_CHUNK // (2 * m), 2 * m, HEAD_DIM)
        ref = g3[:, m:m + 1, :] if rev else g3[:, m - 1:m, :]
        return jnp.broadcast_to(ref, g3.shape).reshape(HG_CHUNK, HEAD_DIM)
    target = m if rev else m - 1
    out = gc
    for src, picks_src in selectors:
        out = jnp.where(picks_src, pltpu.roll(gc, (src - target) % HG_CHUNK, 0), out)
    return out


def _hgrn_gates(q_raw, f_raw, la, lc, omlb, rev):
    q = q_raw * (0.5 * HEAD_DIM ** -0.5) * (1.0 + jnp.tanh(0.5 * q_raw))
    k = (0.5 * omlb) * (1.0 - jnp.tanh(0.5 * f_raw))
    log_sig = jnp.minimum(f_raw, 0.0) - jnp.log(1.0 + jnp.exp(-jnp.abs(f_raw)))
    b = lc + log_sig
    g = jnp.maximum(la, b) + jnp.log(1.0 + jnp.exp(-jnp.abs(la - b)))
    return q, k, g, _cumsum_rows(g, rev)


def _hgrn_level_operands(q, k, g, gc, rev, masks):
    bf = jnp.bfloat16
    qb, kb = q.astype(bf), k.astype(bf)
    ops = []
    for m in HG_LEVELS:
        is_query, sign_log2e, selectors = masks[m]
        if m == 1:
            ops.append(jnp.where(is_query, qb * jnp.exp2(g * LOG2_E).astype(bf), kb))
        else:
            gref = _level_reference(gc, m, rev, selectors)
            ops.append(jnp.where(is_query, qb, kb) * jnp.exp2((gc - gref) * sign_log2e).astype(bf))
    return ops


def _hgrn_level_id(rev):
    t = lax.broadcasted_iota(jnp.int32, (HG_CHUNK, HG_CHUNK), 0)
    s = lax.broadcasted_iota(jnp.int32, (HG_CHUNK, HG_CHUNK), 1)
    vis = (s >= t) if rev else (s <= t)
    x = t ^ s
    lvl = jnp.full((HG_CHUNK, HG_CHUNK), len(HG_LEVELS), jnp.int32)
    for i, m in enumerate(HG_LEVELS):
        lvl = jnp.where((x & m) != 0, jnp.minimum(lvl, i), lvl)
    return jnp.where(vis, lvl, -1)


def _hgrn_kernel(qf_ref, if_ref, ff_ref, qb_ref, ib_ref, fb_ref, la_ref, lc_ref, om_ref, of_ref, ob_ref, s_ref):
    f32, bf = jnp.float32, jnp.bfloat16
    nt = (((1,), (1,)), ((), ()))
    tn = (((0,), (0,)), ((), ()))

    @pl.when(pl.program_id(2) == 0)
    def _():
        s_ref[...] = jnp.zeros_like(s_ref)

    row_masks = [_hgrn_row_masks(False), _hgrn_row_masks(True)]
    refs = ((qf_ref, if_ref, ff_ref, of_ref), (qb_ref, ib_ref, fb_ref, ob_ref))
    inst = [(d, h) for d in range(2) for h in range(HG_HB)]
    level_masks = [[lvl == i for i in range(len(HG_LEVELS) + 1)]
                   for lvl in (_hgrn_level_id(False), _hgrn_level_id(True))]

    pre = []
    for d, h in inst:
        q_ref, i_ref, f_ref, _ = refs[d]
        sl = slice(h * HEAD_DIM, (h + 1) * HEAD_DIM)
        q, k, g, gc = _hgrn_gates(q_ref[:, sl], f_ref[:, sl], la_ref[d:d + 1, sl], lc_ref[d:d + 1, sl],
                                  om_ref[d:d + 1, sl], d == 1)
        g_end = gc[0:1, :] if d == 1 else gc[HG_CHUNK - 1:HG_CHUNK, :]
        pre.append(dict(ops=_hgrn_level_operands(q, k, g, gc, d == 1, row_masks[d]), qk=(q.astype(bf), k.astype(bf)),
                        qe=(q * jnp.exp(gc)).astype(bf), kd=(k * jnp.exp(g_end - gc)).astype(bf),
                        decay=jnp.exp(g_end), v=i_ref[:, sl].astype(bf)))

    attn = []
    for (d, h), pr in zip(inst, pre):
        prods = [lax.dot_general(x, x, nt, preferred_element_type=f32) for x in pr["ops"]]
        prods.append(lax.dot_general(pr["qk"][0], pr["qk"][1], nt, preferred_element_type=f32))
        a = jnp.zeros((HG_CHUNK, HG_CHUNK), f32)
        for i, pmat in enumerate(prods):
            a = jnp.where(level_masks[d][i], pmat, a)
        attn.append(a.astype(bf))

    for (d, h), pr, a in zip(inst, pre, attn):
        o_ref = refs[d][3]
        sl = slice(h * HEAD_DIM, (h + 1) * HEAD_DIM)
        state_t = s_ref[d, h]
        o = jnp.dot(a, pr["v"], preferred_element_type=f32)
        o = o + lax.dot_general(pr["qe"], state_t.astype(bf), nt, preferred_element_type=f32)
        o_ref[:, sl] = o.astype(o_ref.dtype)
        s_ref[d, h] = pr["decay"] * state_t + lax.dot_general(pr["v"], pr["kd"], tn, preferred_element_type=f32)


def _hgrn2(p, col0, lb):
    bsz, lp, _ = p.shape
    nchunk = lp // HG_CHUNK
    cw = HG_HB * HEAD_DIM
    ng = HG_WIDTH // cw
    off = col0 // cw
    la = jnp.log(lb)
    lc = jnp.log1p(-lb)
    om = 1.0 - lb
    fwd = lambda seg: pl.BlockSpec((None, HG_CHUNK, cw), lambda b, g, c: (b, c, off + seg * ng + g))
    bwd = lambda seg: pl.BlockSpec((None, HG_CHUNK, cw), lambda b, g, c: (b, nchunk - 1 - c, off + seg * ng + g))
    par = pl.BlockSpec((2, cw), lambda b, g, c: (0, g))
    return pl.pallas_call(
        _hgrn_kernel,
        grid=(bsz, ng, nchunk),
        in_specs=[fwd(0), fwd(1), fwd(2), bwd(0), bwd(1), bwd(3), par, par, par],
        out_specs=[pl.BlockSpec((None, HG_CHUNK, cw), lambda b, g, c: (b, c, g)),
                   pl.BlockSpec((None, HG_CHUNK, cw), lambda b, g, c: (b, nchunk - 1 - c, g))],
        out_shape=[jax.ShapeDtypeStruct((bsz, lp, HG_WIDTH), jnp.bfloat16)] * 2,
        scratch_shapes=[pltpu.VMEM((2, HG_HB, HEAD_DIM, HEAD_DIM), jnp.float32)],
        compiler_params=pltpu.CompilerParams(
            dimension_semantics=("parallel", "parallel", "arbitrary"), vmem_limit_bytes=V7X_VMEM_LIMIT),
        name="hgrn2_bidirectional",
    )(p, p, p, p, p, p, la, lc, om)


EG_ROWS = 320


def _even_gate_kernel(oa_ref, of_ref, ob_ref, ga_ref, gb_ref, gain_ref, y_ref):
    ga = ga_ref[...]
    y_ref[:, :NA_WIDTH] = (oa_ref[...].astype(jnp.float32) * (ga * jax.nn.sigmoid(ga))).astype(y_ref.dtype)
    for h in range(HG_HEADS):
        sl = slice(h * HEAD_DIM, (h + 1) * HEAD_DIM)
        oh = of_ref[:, sl].astype(jnp.float32) + ob_ref[:, sl].astype(jnp.float32)
        oh = oh * lax.rsqrt(jnp.mean(oh * oh, axis=-1, keepdims=True) + EPS) * gain_ref[:, sl]
        gb = gb_ref[:, sl]
        y_ref[:, NA_WIDTH + h * HEAD_DIM:NA_WIDTH + (h + 1) * HEAD_DIM] = (
            oh * (gb * jax.nn.sigmoid(gb))).astype(y_ref.dtype)


def _even_gate(oa, o_fwd, o_bwd, p, gate_col0, norm_gain):
    bsz, lp, _ = oa.shape
    assert gate_col0 % NA_WIDTH == 0
    goff = gate_col0 // NA_WIDTH
    half = lambda off: pl.BlockSpec((None, EG_ROWS, NA_WIDTH), lambda b, r: (b, r, off))
    return pl.pallas_call(
        _even_gate_kernel,
        grid=(bsz, lp // EG_ROWS),
        in_specs=[half(0), half(0), half(0), half(goff), half(goff + 1),
                  pl.BlockSpec((1, HG_WIDTH), lambda b, r: (0, 0))],
        out_specs=pl.BlockSpec((None, EG_ROWS, D_MODEL), lambda b, r: (b, r, 0)),
        out_shape=jax.ShapeDtypeStruct((bsz, lp, D_MODEL), jnp.bfloat16),
        compiler_params=pltpu.CompilerParams(
            dimension_semantics=("parallel", "parallel"), vmem_limit_bytes=V7X_VMEM_LIMIT),
        name="even_gate",
    )(oa, o_fwd, o_bwd, p, p, norm_gain.reshape(1, HG_WIDTH))


NA_MASKED = -1e30
NA_GROUP = 16


def _na_bias_table(rpb):
    cols = np.arange(GRID_W)
    c0 = np.clip(cols - NA_KW // 2, 0, GRID_W - NA_KW)
    col_ok = (cols[None, :] >= c0[:, None]) & (cols[None, :] < c0[:, None] + NA_KW)
    col_idx = np.clip(cols[None, :] - cols[:, None] + NA_KW - 1, 0, 2 * NA_KW - 2)
    onehot = (col_idx[None] == np.arange(2 * NA_KW - 1)[:, None, None]).astype(np.float32)
    picked = jnp.einsum('hrj,jqk->hrqk', rpb.astype(jnp.float32), onehot, precision=lax.Precision.HIGHEST)
    table = jnp.where(col_ok[None, None], picked, NA_MASKED)
    return jnp.concatenate([table[:, :-1], table[:, 1:]], axis=-1)


def _na_kernel(q_ref, k_ref, v_ref, t_ref, mb_ref, o_ref):
    f32, bf = jnp.float32, jnp.bfloat16
    rows = (q_ref.shape[0] - HG_CHUNK) // GRID_W
    scale = HEAD_DIM ** -0.5
    nt = (((1,), (1,)), ((), ()))
    km = k_ref[PAD:PAD + N_META, :].astype(bf)
    vm = v_ref[PAD:PAD + N_META, :].astype(bf)
    mb = mb_ref[...]
    o_ref[0:PAD, :] = jnp.zeros((PAD, HEAD_DIM), o_ref.dtype)

    qm = q_ref[PAD:PAD + N_META, :].astype(bf)
    s = lax.dot_general(qm, km, nt, preferred_element_type=f32) * scale + mb
    e = jnp.exp(s - jnp.max(s, axis=-1, keepdims=True))
    o_ref[PAD:PAD + N_META, :] = (jnp.dot(e.astype(bf), vm, preferred_element_type=f32)
                                  / jnp.sum(e, axis=-1, keepdims=True)).astype(o_ref.dtype)

    def row_group(g, carry):
        rs = [g * NA_GROUP + i for i in range(NA_GROUP)]
        scores = []
        for r in rs:
            r0 = jnp.clip(r - NA_KH // 2, 0, rows - NA_KH)
            q = q_ref[pl.ds(pl.multiple_of(HG_CHUNK + GRID_W * r, GRID_W), GRID_W), :].astype(bf)
            k0 = pl.multiple_of(HG_CHUNK + GRID_W * r0, GRID_W)
            kw = k_ref[pl.ds(k0, NA_KH * GRID_W), :].astype(bf)
            s_win = lax.dot_general(q, kw, nt, preferred_element_type=f32)
            s_meta = lax.dot_general(q, km, nt, preferred_element_type=f32)
            scores.append((r, r0, k0, s_win, s_meta))
        probs = []
        for r, r0, k0, s_win, s_meta in scores:
            bias = jnp.concatenate([t_ref[r0 + i - r + NA_KH - 1] for i in range(0, NA_KH, 2)], axis=1)
            s_win = s_win * scale + bias
            s_meta = s_meta * scale + mb
            m = jnp.maximum(jnp.max(s_win, axis=-1, keepdims=True), jnp.max(s_meta, axis=-1, keepdims=True))
            pw = jnp.exp(s_win - m)
            pm = jnp.exp(s_meta - m)
            den = jnp.sum(pw, axis=-1, keepdims=True) + jnp.sum(pm, axis=-1, keepdims=True)
            probs.append((r, k0, pw.astype(bf), pm.astype(bf), den))
        for r, k0, pw, pm, den in probs:
            vw = v_ref[pl.ds(k0, NA_KH * GRID_W), :].astype(bf)
            o = jnp.dot(pw, vw, preferred_element_type=f32) + jnp.dot(pm, vm, preferred_element_type=f32)
            o_ref[pl.ds(pl.multiple_of(HG_CHUNK + GRID_W * r, GRID_W), GRID_W), :] = (o / den).astype(o_ref.dtype)
        return carry

    lax.fori_loop(0, rows // NA_GROUP, row_group, 0)


def _natten(p, rpb, meta_bias):
    bsz, lp, _ = p.shape
    assert ((lp - HG_CHUNK) // GRID_W) % NA_GROUP == 0
    heads = NA_WIDTH // HEAD_DIM
    blk = lambda seg: pl.BlockSpec((None, lp, HEAD_DIM), lambda b, h: (b, 0, seg * heads + h))
    table = _na_bias_table(rpb)
    return pl.pallas_call(
        _na_kernel,
        grid=(bsz, heads),
        in_specs=[blk(0), blk(1), blk(2),
                  pl.BlockSpec((None, 2 * NA_KH - 2, GRID_W, 2 * GRID_W), lambda b, h: (h, 0, 0, 0)),
                  pl.BlockSpec((None, 1, N_META), lambda b, h: (h, 0, 0))],
        out_specs=pl.BlockSpec((None, lp, HEAD_DIM), lambda b, h: (b, 0, h)),
        out_shape=jax.ShapeDtypeStruct((bsz, lp, NA_WIDTH), jnp.bfloat16),
        compiler_params=pltpu.CompilerParams(
            dimension_semantics=("parallel", "parallel"), vmem_limit_bytes=V7X_VMEM_LIMIT),
        name="neighbourhood_attention",
    )(p, p, p, table, meta_bias.astype(jnp.float32).reshape(heads, 1, N_META))


def _even_mix(p, rpb, meta_bias, lb, norm_gain):
    oa = _natten(p, rpb, meta_bias)
    o_fwd, o_bwd = _hgrn2(p, 3 * NA_WIDTH, lb)
    return _even_gate(oa, o_fwd, o_bwd, p, 3 * NA_WIDTH + 4 * HG_WIDTH, norm_gain)


def kernel(x, meta_tokens, norm_pre, norm_post, ev_w_in, ev_w_out, na_rpb, na_meta_bias, hg_lower, hg_norm,
           od_w_in, od_w_out, hy_short, hy_ffn_w1, hy_ffn_b1, hy_ffn_freq1, hy_ffn_w2, hy_ffn_b2, hy_ffn_freq2,
           hy_ffn_w3, hy_bias):
    b = x.shape[0]
    depth = norm_pre.shape[0]
    f32 = jnp.float32
    lb_all = jnp.cumsum(jax.nn.softmax(hg_lower.astype(f32), axis=0), axis=0)
    lb_all = lb_all - lb_all[:1]
    h, hn = _embed(x, meta_tokens, norm_pre[0])
    h = h.reshape(b * LP, D_MODEL)
    hn = hn.reshape(b * LP, D_MODEL)
    consts = _dft_constants()
    for layer in range(depth):
        j = layer // 2
        if layer % 2 == 0:
            p = _matmul(hn, ev_w_in, j, f32).reshape(b, LP, -1)
            y = _even_mix(p, na_rpb[j], na_meta_bias[j], lb_all[j], hg_norm[j]).reshape(b * LP, D_MODEL)
            w_out = ev_w_out
        else:
            p3 = _matmul(hn, od_w_in, j, jnp.bfloat16).reshape(b, LP, -1)
            y = _odd_mix(p3, hy_short[j], hy_ffn_w1[j], hy_ffn_b1[j], hy_ffn_freq1[j], hy_ffn_w2[j],
                         hy_ffn_b2[j], hy_ffn_freq2[j], hy_ffn_w3[j], hy_bias[j], consts)
            y = y.reshape(b * LP, D_MODEL)
            w_out = od_w_out
        out = _matmul(y, w_out, j, jnp.bfloat16)
        if layer + 1 < depth:
            h, hn = _residual_rmsnorm(h, out, norm_post[layer], norm_pre[layer + 1])
    return _final_residual(h.reshape(b, LP, D_MODEL), out.reshape(b, LP, D_MODEL), norm_post[depth - 1], x.dtype)
```

```python
import math

import jax
import jax.numpy as jnp
import numpy as np
from jax import lax
from jax.experimental import pallas as pl
from jax.experimental.pallas import tpu as pltpu

D_MODEL = 4096
SEQ = 4096
N_META = 16
GRID_W = 64
HEAD_DIM = 128
NA_WIDTH = D_MODEL // 2
NA_KH = 8
NA_KW = 16
HG_WIDTH = D_MODEL // 2
HG_HEADS = HG_WIDTH // HEAD_DIM
HG_CHUNK = 64
HY_WIDTH = D_MODEL
HY_ORDER = 2
HY_EMB = 33
HY_BANDS = (HY_EMB - 1) // 2
HY_FAST_DECAY = 0.3
HY_SLOW_DECAY = 1.5
HY_TARGET = 1e-2
EPS = 1e-6

PAD = HG_CHUNK - N_META
LP = PAD + N_META + SEQ
L_TOK = N_META + SEQ

V7X_VMEM_LIMIT = 56 * 1024 * 1024


EMBED_PIECES = 5


def _embed_kernel(*refs):
    x_refs, meta_ref, g_ref, h_ref, n_ref = refs[:EMBED_PIECES], *refs[EMBED_PIECES:]
    first = jnp.concatenate([jnp.zeros((PAD, D_MODEL), jnp.float32), meta_ref[...].astype(jnp.float32)], axis=0)
    pieces = [jnp.where(pl.program_id(1) == 0, first, x_refs[0][...].astype(jnp.float32))]
    pieces += [x_ref[...].astype(jnp.float32) for x_ref in x_refs[1:]]
    h = jnp.concatenate(pieces, axis=0)
    h_ref[...] = h
    r = lax.rsqrt(jnp.mean(h * h, axis=-1, keepdims=True) + EPS)
    n_ref[...] = (h * r * g_ref[...]).astype(n_ref.dtype)


def _embed(x, meta_tokens, g):
    bsz, seq, d = x.shape
    assert seq == SEQ and LP % (EMBED_PIECES * HG_CHUNK) == 0
    tr = EMBED_PIECES * HG_CHUNK
    piece = lambda k: pl.BlockSpec((None, HG_CHUNK, d), lambda b, r: (b, jnp.maximum(EMBED_PIECES * r + k - 1, 0), 0))
    row = pl.BlockSpec((None, tr, d), lambda b, r: (b, r, 0))
    return pl.pallas_call(
        _embed_kernel,
        grid=(bsz, LP // tr),
        in_specs=[piece(k) for k in range(EMBED_PIECES)] + [pl.BlockSpec((N_META, d), lambda b, r: (0, 0)),
                                                           pl.BlockSpec((1, d), lambda b, r: (0, 0))],
        out_specs=[row, row],
        out_shape=[jax.ShapeDtypeStruct((bsz, LP, d), jnp.float32), jax.ShapeDtypeStruct((bsz, LP, d), jnp.bfloat16)],
        compiler_params=pltpu.CompilerParams(
            dimension_semantics=("parallel", "parallel"), vmem_limit_bytes=V7X_VMEM_LIMIT),
        name="embed_rmsnorm_pre",
    )(*([x] * EMBED_PIECES), meta_tokens, g.reshape(1, d))


def _post_pre_kernel(h_ref, y_ref, g_ref, gn_ref, o_ref, n_ref):
    y = y_ref[...].astype(jnp.float32)
    r = lax.rsqrt(jnp.mean(y * y, axis=-1, keepdims=True) + EPS)
    h = h_ref[...] + y * r * g_ref[...]
    o_ref[...] = h
    rn = lax.rsqrt(jnp.mean(h * h, axis=-1, keepdims=True) + EPS)
    n_ref[...] = (h * rn * gn_ref[...]).astype(n_ref.dtype)


FINAL_PIECES = 4


def _final_kernel(*refs):
    h_refs, y_refs = refs[:FINAL_PIECES], refs[FINAL_PIECES:2 * FINAL_PIECES]
    g_ref, o_ref = refs[2 * FINAL_PIECES:]
    for k, (h_ref, y_ref) in enumerate(zip(h_refs, y_refs)):
        y = y_ref[...].astype(jnp.float32)
        r = lax.rsqrt(jnp.mean(y * y, axis=-1, keepdims=True) + EPS)
        o_ref[k * HG_CHUNK:(k + 1) * HG_CHUNK, :] = (h_ref[...] + y * r * g_ref[...]).astype(o_ref.dtype)


def _final_residual(h3, y3, g, out_dtype):
    bsz, lp, d = h3.shape
    assert PAD + N_META == HG_CHUNK and SEQ % (FINAL_PIECES * HG_CHUNK) == 0
    piece = lambda k: pl.BlockSpec((None, HG_CHUNK, d), lambda b, r: (b, FINAL_PIECES * r + k + 1, 0))
    pieces = [piece(k) for k in range(FINAL_PIECES)]
    return pl.pallas_call(
        _final_kernel,
        grid=(bsz, SEQ // (FINAL_PIECES * HG_CHUNK)),
        in_specs=pieces + pieces + [pl.BlockSpec((1, d), lambda b, r: (0, 0))],
        out_specs=pl.BlockSpec((None, FINAL_PIECES * HG_CHUNK, d), lambda b, r: (b, r, 0)),
        out_shape=jax.ShapeDtypeStruct((bsz, SEQ, d), out_dtype),
        compiler_params=pltpu.CompilerParams(
            dimension_semantics=("parallel", "parallel"), vmem_limit_bytes=V7X_VMEM_LIMIT),
        name="residual_rmsnorm_post_final",
    )(*([h3] * FINAL_PIECES), *([y3] * FINAL_PIECES), g.reshape(1, d))


def _residual_rmsnorm(h2d, y2d, g, g_next, tr=320):
    m, d = h2d.shape
    row = pl.BlockSpec((tr, d), lambda i: (i, 0))
    vec = pl.BlockSpec((1, d), lambda i: (0, 0))
    params = pltpu.CompilerParams(dimension_semantics=("parallel",), vmem_limit_bytes=V7X_VMEM_LIMIT)
    return pl.pallas_call(
        _post_pre_kernel, grid=(m // tr,), in_specs=[row, row, vec, vec], out_specs=[row, row],
        out_shape=[jax.ShapeDtypeStruct((m, d), jnp.float32), jax.ShapeDtypeStruct((m, d), jnp.bfloat16)],
        compiler_params=params, name="residual_rmsnorm_post_pre",
    )(h2d, y2d, g.reshape(1, d), g_next.reshape(1, d))


def _matmul_kernel(x_ref, w_ref, o_ref):
    o_ref[...] = jnp.dot(x_ref[...], w_ref[...].astype(jnp.bfloat16),
                         preferred_element_type=jnp.float32).astype(o_ref.dtype)


def _matmul(x, w_stack, layer, out_dtype, tm=1040, tn=512):
    m, kd = x.shape
    _, _, n = w_stack.shape
    assert m % tm == 0 and n % tn == 0
    return pl.pallas_call(
        _matmul_kernel,
        grid=(m // tm, n // tn),
        in_specs=[pl.BlockSpec((tm, kd), lambda i, j: (i, 0)),
                  pl.BlockSpec((None, kd, tn), lambda i, j: (layer, 0, j))],
        out_specs=pl.BlockSpec((tm, tn), lambda i, j: (i, j)),
        out_shape=jax.ShapeDtypeStruct((m, n), out_dtype),
        compiler_params=pltpu.CompilerParams(
            dimension_semantics=("parallel", "parallel"), vmem_limit_bytes=V7X_VMEM_LIMIT),
        name="projection_matmul",
    )(x, w_stack)


FFT_N1 = 128
FFT_N2 = 65
FFT_N = FFT_N1 * FFT_N2
N2P = 80
MIDR = 72
SLAB = 264
TC = 128
MID_GROUP = 16
NC_HALF = FFT_N1 // 2 + 1
HALF_P = 72
SLAB_S = 152
C_GROUP = 13
FILL_CHUNKS = 13
B_GROUP = 13


def _dft_constants():
    a = np.arange(FFT_N1)
    c = np.arange(FFT_N1)
    w1 = np.exp(-2j * np.pi * np.outer(c, a) / FFT_N1)
    w1r, w1i = w1.real, w1.imag
    half = FFT_N1 // 2
    f1 = np.block([[w1r[:, :half], -w1i[:, :half]], [w1i[:, :half], w1r[:, :half]]])
    f1_real = np.zeros((2 * HALF_P, FFT_N1), np.float64)
    f1_real[:NC_HALF] = w1r[:NC_HALF]
    f1_real[HALF_P:HALF_P + NC_HALF] = w1i[:NC_HALF]
    g = np.conj(w1).T
    gr, gi = g.real[:half], g.imag[:half]
    i2 = np.block([[gr, -gi], [gi, gr]])
    b = np.arange(FFT_N2)
    d = np.arange(FFT_N2)
    w2 = np.exp(-2j * np.pi * np.outer(d, b) / FFT_N2)
    tw = np.exp(-2j * np.pi * np.outer(c, b) / FFT_N)
    m = w2[None, :, :] * tw[:, None, :]
    ab = np.zeros((FFT_N1, 2 * MIDR, 128), np.float64)
    m[NC_HALF:] = m[NC_HALF:, ::-1, :]
    ab[:, :FFT_N2, :FFT_N2] = m.real
    ab[:, MIDR:MIDR + FFT_N2, :FFT_N2] = m.imag
    bf = jnp.bfloat16
    abt = np.zeros((FFT_N1, N2P, 256), np.float64)
    abt[:, :, :2 * MIDR] = np.swapaxes(ab[:, :, :N2P], 1, 2)
    return (jnp.asarray(f1, bf), jnp.asarray(f1_real, bf), jnp.asarray(i2, bf), jnp.asarray(ab, bf),
            jnp.asarray(abt, bf))


def _middle_stage_inputs(y_ref, c, imag_row=FFT_N1, pitch=SLAB):
    yr = y_ref[pl.ds(c, MIDR, stride=pitch), :]
    yi = y_ref[pl.ds(imag_row + c, MIDR, stride=pitch), :]
    return jnp.concatenate([yr, yi], axis=1)


def _forward_middle(ab, y2):
    y2 = jnp.concatenate([y2, jnp.zeros((128 - MIDR, 2 * TC), jnp.float32)], axis=0).astype(jnp.bfloat16)
    q = jnp.dot(ab, y2, preferred_element_type=jnp.float32)
    zr = q[:MIDR, :TC] - q[MIDR:, TC:]
    zi = q[MIDR:, :TC] + q[:MIDR, TC:]
    return zr, zi


def _spectrum_kernel(h_ref, t_ref, w3f0_ref, w3f1_ref, w3b0_ref, w3b1_ref, delta_ref, f1_ref, ab_ref, k_ref,
                     kt_ref, y_ref):
    w3 = jnp.concatenate([jnp.concatenate([w3f0_ref[...], w3f1_ref[...]], axis=1),
                          jnp.concatenate([w3b0_ref[...], w3b1_ref[...]], axis=1)], axis=0).astype(jnp.bfloat16)
    delta = delta_ref[...]
    rows = FFT_N // FILL_CHUNKS

    def fill(i, carry):
        r0 = pl.multiple_of(i * rows, 8)
        kt = jnp.dot(h_ref[pl.ds(r0, rows), :], w3, preferred_element_type=jnp.float32)
        window = jnp.exp(t_ref[pl.ds(r0, rows), :] * delta)
        for n in range(HY_ORDER):
            kt_ref[n, pl.ds(r0, rows), :] = kt[:, n * TC:(n + 1) * TC] * window
        return carry

    lax.fori_loop(0, FILL_CHUNKS, fill, 0)
    y_ref[pl.ds(FFT_N2 * SLAB_S, (MIDR - FFT_N2) * SLAB_S), :] = jnp.zeros(((MIDR - FFT_N2) * SLAB_S, TC), jnp.float32)

    for n in range(HY_ORDER):
        def stage1(g, carry):
            bs = [g * B_GROUP + i for i in range(B_GROUP)]
            x = jnp.concatenate([kt_ref.at[n][pl.ds(b, FFT_N1, stride=FFT_N2), :] for b in bs], axis=1)
            y = jnp.dot(f1_ref[...], x.astype(jnp.bfloat16), preferred_element_type=jnp.float32)
            for i, b in enumerate(bs):
                y_ref[pl.ds(pl.multiple_of(b * SLAB_S, 8), 2 * HALF_P), :] = y[:, i * TC:(i + 1) * TC]
            return carry

        lax.fori_loop(0, FFT_N2 // B_GROUP, stage1, 0)

        def stage2(g, carry):
            cs = [g * C_GROUP + i for i in range(C_GROUP)]
            zs = [_forward_middle(ab_ref[c], _middle_stage_inputs(y_ref, c, HALF_P, SLAB_S)) for c in cs]
            for c, (zr, zi) in zip(cs, zs):
                z = jnp.concatenate([zr, zi], axis=1) * (1.0 / FFT_N)
                k_ref[n, c] = jnp.concatenate(
                    [z, jnp.zeros((N2P - MIDR, 2 * TC), jnp.float32)], axis=0).astype(jnp.bfloat16)
            return carry

        lax.fori_loop(0, NC_HALF // C_GROUP, stage2, 0)


def _filter_spectrum(h2, w3, deltas, consts):
    _, f1_real, _, ab, _ = consts
    nt = HY_WIDTH // TC
    j = np.arange(FFT_N)
    lag = np.where(j < L_TOK, j, FFT_N - j).astype(np.float32)
    neg_t = jnp.asarray(np.broadcast_to((-lag / np.float32(L_TOK - 1))[:, None], (FFT_N, TC)))
    w3_cols = lambda direction, order: pl.BlockSpec((64, TC), lambda i: (0, (direction * HY_ORDER + order) * nt + i))
    return pl.pallas_call(
        _spectrum_kernel,
        grid=(nt,),
        in_specs=[
            pl.BlockSpec((FFT_N, 128), lambda i: (0, 0)),
            pl.BlockSpec((FFT_N, TC), lambda i: (0, 0)),
            w3_cols(0, 0), w3_cols(0, 1), w3_cols(1, 0), w3_cols(1, 1),
            pl.BlockSpec((1, TC), lambda i: (0, i)),
            pl.BlockSpec((2 * HALF_P, FFT_N1), lambda i: (0, 0)),
            pl.BlockSpec((FFT_N1, 2 * MIDR, 128), lambda i: (0, 0, 0)),
        ],
        out_specs=pl.BlockSpec((HY_ORDER, None, NC_HALF, N2P, 2 * TC), lambda i: (0, i, 0, 0, 0)),
        out_shape=jax.ShapeDtypeStruct((HY_ORDER, nt, NC_HALF, N2P, 2 * TC), jnp.bfloat16),
        scratch_shapes=[pltpu.VMEM((HY_ORDER, FFT_N, TC), jnp.float32), pltpu.VMEM((MIDR * SLAB_S, TC), jnp.float32)],
        compiler_params=pltpu.CompilerParams(dimension_semantics=("parallel",), vmem_limit_bytes=V7X_VMEM_LIMIT),
        name="hyena_filter_spectrum",
    )(h2, neg_t, w3, w3, w3, w3, deltas, f1_real, ab)


def _conv_kernel(u_ref, k_ref, f1_ref, i2_ref, ab_ref, abt_ref, o_ref, y_ref):
    _conv_body(u_ref, k_ref, f1_ref, i2_ref, ab_ref, abt_ref, o_ref, y_ref)


def _conv_raw_kernel(p_ref, w_ref, k_ref, f1_ref, i2_ref, ab_ref, abt_ref, o_ref, v_ref, y_ref):
    for ri in range(2):
        v_ref[ri] = _short_conv_rows(p_ref[ri].astype(jnp.float32), w_ref[...])
    _conv_body(v_ref, k_ref, f1_ref, i2_ref, ab_ref, abt_ref, o_ref, y_ref)


def _conv_body(u_ref, k_ref, f1_ref, i2_ref, ab_ref, abt_ref, o_ref, y_ref):
    y_ref[pl.ds(FFT_N2 * SLAB, (MIDR - FFT_N2) * SLAB), :] = jnp.zeros(((MIDR - FFT_N2) * SLAB, TC), jnp.float32)
    half = FFT_N1 // 2

    def stage1(g, carry):
        bs = [g * B_GROUP + i for i in range(B_GROUP)]
        x = jnp.concatenate(
            [jnp.concatenate([u_ref.at[ri][pl.ds(b, half, stride=FFT_N2), :] for b in bs], axis=1) for ri in range(2)],
            axis=0)
        y = jnp.dot(f1_ref[...], x.astype(jnp.bfloat16), preferred_element_type=jnp.float32)
        for i, b in enumerate(bs):
            y_ref[pl.ds(pl.multiple_of(b * SLAB, 8), 2 * FFT_N1), :] = y[:, i * TC:(i + 1) * TC]
        return carry

    lax.fori_loop(0, FFT_N2 // B_GROUP, stage1, 0)

    def inverse_middle(c, z):
        zr, zi = z
        mirrored = c >= NC_HALF
        kk = k_ref[jnp.where(mirrored, FFT_N1 - c, c)].astype(jnp.float32)[:MIDR]
        kr, ki = kk[:, :TC], kk[:, TC:] * jnp.where(mirrored, -1.0, 1.0)
        pr = zr * kr - zi * ki
        pi = zr * ki + zi * kr
        rhs = jnp.concatenate([jnp.concatenate([pr, pi], axis=1), jnp.concatenate([pi, -pr], axis=1),
                               jnp.zeros((256 - 2 * MIDR, 2 * TC), jnp.float32)], axis=0)
        return jnp.dot(abt_ref[c], rhs.astype(jnp.bfloat16), preferred_element_type=jnp.float32)

    def middle(g, carry):
        cs = [g * MID_GROUP + i for i in range(MID_GROUP)]
        zs = [_forward_middle(ab_ref[c], _middle_stage_inputs(y_ref, c)) for c in cs]
        outs = [inverse_middle(c, z) for c, z in zip(cs, zs)]
        for c, v in zip(cs, outs):
            y_ref[pl.ds(c, MIDR, stride=SLAB), :] = v[:MIDR, :TC]
            y_ref[pl.ds(FFT_N1 + c, MIDR, stride=SLAB), :] = v[:MIDR, TC:]
        return carry

    lax.fori_loop(0, FFT_N1 // MID_GROUP, middle, 0)

    def stage4(g, carry):
        bs = [g * B_GROUP + i for i in range(B_GROUP)]
        v = jnp.concatenate([y_ref[pl.ds(pl.multiple_of(b * SLAB, 8), 2 * FFT_N1), :] for b in bs], axis=1)
        x = jnp.dot(i2_ref[...], v.astype(jnp.bfloat16), preferred_element_type=jnp.float32)
        for i, b in enumerate(bs):
            o_ref.at[0][pl.ds(b, half, stride=FFT_N2), :] = x[:half, i * TC:(i + 1) * TC]
            o_ref.at[1][pl.ds(b, half, stride=FFT_N2), :] = x[half:, i * TC:(i + 1) * TC]
        return carry

    lax.fori_loop(0, FFT_N2 // B_GROUP, stage4, 0)


def _long_conv(u, kspec, order, consts, w_short_t=None):
    f1, _, i2, ab, abt = consts
    bsz, lp, _ = u.shape
    assert bsz == 2 and lp == LP
    nt = HY_WIDTH // TC
    seq = pl.BlockSpec((bsz, lp, TC), lambda i: (0, 0, i))
    out = jax.ShapeDtypeStruct((bsz, lp, HY_WIDTH), jnp.float32)
    raw = w_short_t is not None
    return pl.pallas_call(
        _conv_raw_kernel if raw else _conv_kernel,
        grid=(nt,),
        in_specs=[seq] + ([pl.BlockSpec((3, TC), lambda i: (0, i))] if raw else []) + [
            pl.BlockSpec((None, None, NC_HALF, N2P, 2 * TC), lambda i: (order, i, 0, 0, 0)),
            pl.BlockSpec((2 * FFT_N1, FFT_N1), lambda i: (0, 0)),
            pl.BlockSpec((FFT_N1, 2 * FFT_N1), lambda i: (0, 0)),
            pl.BlockSpec((FFT_N1, 2 * MIDR, 128), lambda i: (0, 0, 0), pipeline_mode=pl.Buffered(1)),
            pl.BlockSpec((FFT_N1, N2P, 256), lambda i: (0, 0, 0), pipeline_mode=pl.Buffered(1)),
        ],
        out_specs=[seq, seq] if raw else seq,
        out_shape=[out, out] if raw else out,
        scratch_shapes=[pltpu.VMEM((MIDR * SLAB, TC), jnp.float32)],
        compiler_params=pltpu.CompilerParams(dimension_semantics=("parallel",), vmem_limit_bytes=V7X_VMEM_LIMIT),
        name="hyena_long_conv",
    )(*((u, w_short_t) if raw else (u,)), kspec, f1, i2, ab, abt)


def _short_conv_rows(u, w):
    y = pltpu.roll(u, 1, 0) * w[0:1] + u * w[1:2] + pltpu.roll(u, LP - 1, 0) * w[2:3]
    row = lax.broadcasted_iota(jnp.int32, y.shape, 0)
    return jnp.where(row >= PAD, y, 0.0)


def _gate1_kernel(v_ref, u_ref, w_ref, y_ref, b_ref, o_ref):
    x = _short_conv_rows(u_ref[...].astype(jnp.float32), w_ref[...])
    o_ref[...] = x * (y_ref[...] + v_ref[...] * b_ref[...])


def _gate2_kernel(z_ref, u_ref, w_ref, y_ref, b_ref, g_ref, o_ref):
    g = g_ref[...].astype(jnp.float32)
    z = _short_conv_rows(u_ref[...].astype(jnp.float32), w_ref[...]) * (y_ref[...] + z_ref[...] * b_ref[...])
    o_ref[...] = (z * (g * jax.nn.sigmoid(g))).astype(o_ref.dtype)


def _seq_tile(col_block):
    return pl.BlockSpec((None, LP, TC), lambda b, i: (b, 0, col_block + i))


def _vec_tile(rows, col_block):
    return pl.BlockSpec((rows, TC), lambda b, i: (0, col_block + i))


def _hyena_gate1(v, p3, w_short_t, y, bias):
    bsz = p3.shape[0]
    nt = HY_WIDTH // TC
    return pl.pallas_call(
        _gate1_kernel,
        grid=(bsz, nt),
        in_specs=[_seq_tile(0), _seq_tile(nt), _vec_tile(3, nt), _seq_tile(0), _vec_tile(1, 0)],
        out_specs=_seq_tile(0),
        out_shape=jax.ShapeDtypeStruct((bsz, LP, HY_WIDTH), jnp.float32),
        compiler_params=pltpu.CompilerParams(
            dimension_semantics=("parallel", "parallel"), vmem_limit_bytes=V7X_VMEM_LIMIT),
        name="hyena_gate1",
    )(v, p3, w_short_t, y, bias.reshape(1, HY_WIDTH))


def _hyena_gate2(z1, p3, w_short_t, y, bias):
    bsz = p3.shape[0]
    nt = HY_WIDTH // TC
    return pl.pallas_call(
        _gate2_kernel,
        grid=(bsz, nt),
        in_specs=[_seq_tile(0), _seq_tile(2 * nt), _vec_tile(3, 2 * nt), _seq_tile(0), _vec_tile(1, 0),
                  _seq_tile(3 * nt)],
        out_specs=_seq_tile(0),
        out_shape=jax.ShapeDtypeStruct((bsz, LP, HY_WIDTH), jnp.bfloat16),
        compiler_params=pltpu.CompilerParams(
            dimension_semantics=("parallel", "parallel"), vmem_limit_bytes=V7X_VMEM_LIMIT),
        name="hyena_gate2",
    )(z1, p3, w_short_t, y, bias.reshape(1, HY_WIDTH), p3)


def _filter_positions():
    f32 = np.float32
    l = L_TOK
    t = np.linspace(0.0, 1.0, l, dtype=f32)[:, None]
    w = f32(2.0 * math.pi) * np.arange(l, dtype=f32)[:, None] / f32(l)
    bands = np.linspace(1e-4, HY_BANDS - 1, HY_BANDS, dtype=f32)[None, :]
    z = np.concatenate([t, np.cos(bands * w), -np.sin(bands * w)], axis=-1).astype(f32)
    j = np.arange(FFT_N)
    lag = np.minimum(np.where(j < l, j, FFT_N - j), l - 1)
    zrows = np.zeros((FFT_N, 128), f32)
    zrows[:, :HY_EMB] = z[lag]
    mask = np.zeros((FFT_N, 128), f32)
    mask[:, :64] = (j < l)[:, None]
    mask[:, 64:] = ((j > FFT_N - l) | (j == 0))[:, None]
    return jnp.asarray(zrows), jnp.asarray(mask)


def _hidden_kernel(z_ref, m_ref, w1_ref, b1_ref, fr1_ref, w2_ref, b2_ref, fr2_ref, o_ref):
    hi = lax.Precision.HIGHEST
    a = jnp.dot(z_ref[...], w1_ref[...], precision=hi, preferred_element_type=jnp.float32) + b1_ref[...]
    hid = jnp.sin(fr1_ref[...] * a)
    a = jnp.dot(hid, w2_ref[...], precision=hi, preferred_element_type=jnp.float32) + b2_ref[...]
    hid = jnp.sin(fr2_ref[...] * a)
    o_ref[...] = (jnp.concatenate([hid, hid], axis=1) * m_ref[...]).astype(o_ref.dtype)


def _hyena_hidden(w1, b1, fr1, w2, b2, fr2):
    zrows, mask = _filter_positions()
    rows = FFT_N // FILL_CHUNKS
    hdim = w2.shape[0]
    w1p = jnp.zeros((128, hdim), jnp.float32).at[:HY_EMB].set(w1.astype(jnp.float32))
    tile = pl.BlockSpec((rows, 128), lambda i: (i, 0))
    full = lambda a: pl.BlockSpec(a.shape, lambda i: (0,) * a.ndim)
    vecs = [v.astype(jnp.float32).reshape(1, hdim) for v in (b1, fr1, b2, fr2)]
    w2f = w2.astype(jnp.float32)
    return pl.pallas_call(
        _hidden_kernel,
        grid=(FILL_CHUNKS,),
        in_specs=[tile, tile, full(w1p), full(vecs[0]), full(vecs[1]), full(w2f), full(vecs[2]), full(vecs[3])],
        out_specs=tile,
        out_shape=jax.ShapeDtypeStruct((FFT_N, 128), jnp.bfloat16),
        compiler_params=pltpu.CompilerParams(dimension_semantics=("parallel",), vmem_limit_bytes=V7X_VMEM_LIMIT),
        name="hyena_filter_hidden",
    )(zrows, mask, w1p, vecs[0], vecs[1], w2f, vecs[2], vecs[3])


def _odd_mix(p3, w_short, w1, b1, fr1, w2, b2, fr2, w3, bias, consts):
    min_decay = math.log(HY_TARGET) / HY_SLOW_DECAY
    max_decay = math.log(HY_TARGET) / HY_FAST_DECAY
    deltas = jnp.abs(jnp.linspace(min_decay, max_decay, HY_WIDTH, dtype=jnp.float32)).reshape(1, HY_WIDTH)
    kspec = _filter_spectrum(_hyena_hidden(w1, b1, fr1, w2, b2, fr2), w3, deltas, consts)
    w_short_t = w_short.T
    y1, v = _long_conv(p3, kspec, 0, consts, w_short_t)
    z1 = _hyena_gate1(v, p3, w_short_t, y1, bias[0])
    y2 = _long_conv(z1, kspec, 1, consts)
    return _hyena_gate2(z1, p3, w_short_t, y2, bias[1])


HG_HB = 16
HG_LEVELS = (32, 16, 8, 4, 2, 1)
LOG2_E = 1.4426950408889634


def _cumsum_rows(g, rev):
    f32, bf = jnp.float32, jnp.bfloat16
    t = lax.broadcasted_iota(jnp.int32, (HG_CHUNK, 3 * HG_CHUNK), 0)
    s = lax.broadcasted_iota(jnp.int32, (HG_CHUNK, 3 * HG_CHUNK), 1) & (HG_CHUNK - 1)
    tri = jnp.where((s >= t) if rev else (s <= t), 1.0, 0.0).astype(bf)
    hi = g.astype(bf)
    r1 = g - hi.astype(f32)
    mid = r1.astype(bf)
    lo = (r1 - mid.astype(f32)).astype(bf)
    return jnp.dot(tri, jnp.concatenate([hi, mid, lo], axis=0), preferred_element_type=f32)


def _hgrn_row_masks(rev):
    row = lax.broadcasted_iota(jnp.int32, (HG_CHUNK, HEAD_DIM), 0)
    masks = {}
    for m in HG_LEVELS:
        upper = (row & m) != 0
        is_query = jnp.logical_not(upper) if rev else upper
        target = m if rev else m - 1
        selectors = [(src, (row & (2 * m - 1)) == src) for src in range(2 * m) if src != target] if m == 2 else None
        masks[m] = (is_query, jnp.where(is_query, LOG2_E, -LOG2_E), selectors)
    return masks


def _level_reference(gc, m, rev, selectors):
    if m >= 4:
        g3 = gc.reshape(HG_CHUNK // (2 * m), 2 * m, HEAD_DIM)
        ref = g3[:, m:m + 1, :] if rev else g3[:, m - 1:m, :]
        return jnp.broadcast_to(ref, g3.shape).reshape(HG_CHUNK, HEAD_DIM)
    target = m if rev else m - 1
    out = gc
    for src, picks_src in selectors:
        out = jnp.where(picks_src, pltpu.roll(gc, (src - target) % HG_CHUNK, 0), out)
    return out


def _hgrn_gates(q_raw, f_raw, la, lc, omlb, rev):
    q = q_raw * (0.5 * HEAD_DIM ** -0.5) * (1.0 + jnp.tanh(0.5 * q_raw))
    k = (0.5 * omlb) * (1.0 - jnp.tanh(0.5 * f_raw))
    log_sig = jnp.minimum(f_raw, 0.0) - jnp.log(1.0 + jnp.exp(-jnp.abs(f_raw)))
    b = lc + log_sig
    g = jnp.maximum(la, b) + jnp.log(1.0 + jnp.exp(-jnp.abs(la - b)))
    return q, k, g, _cumsum_rows(g, rev)


def _hgrn_level_operands(q, k, g, gc, rev, masks):
    bf = jnp.bfloat16
    qb, kb = q.astype(bf), k.astype(bf)
    ops = []
    for m in HG_LEVELS:
        is_query, sign_log2e, selectors = masks[m]
        if m == 1:
            ops.append(jnp.where(is_query, qb * jnp.exp2(g * LOG2_E).astype(bf), kb))
        else:
            gref = _level_reference(gc, m, rev, selectors)
            ops.append(jnp.where(is_query, qb, kb) * jnp.exp2((gc - gref) * sign_log2e).astype(bf))
    return ops


def _hgrn_level_id(rev):
    t = lax.broadcasted_iota(jnp.int32, (HG_CHUNK, HG_CHUNK), 0)
    s = lax.broadcasted_iota(jnp.int32, (HG_CHUNK, HG_CHUNK), 1)
    vis = (s >= t) if rev else (s <= t)
    x = t ^ s
    lvl = jnp.full((HG_CHUNK, HG_CHUNK), len(HG_LEVELS), jnp.int32)
    for i, m in enumerate(HG_LEVELS):
        lvl = jnp.where((x & m) != 0, jnp.minimum(lvl, i), lvl)
    return jnp.where(vis, lvl, -1)


def _hgrn_kernel(qf_ref, if_ref, ff_ref, qb_ref, ib_ref, fb_ref, la_ref, lc_ref, om_ref, of_ref, ob_ref, s_ref):
    f32, bf = jnp.float32, jnp.bfloat16
    nt = (((1,), (1,)), ((), ()))
    tn = (((0,), (0,)), ((), ()))

    @pl.when(pl.program_id(2) == 0)
    def _():
        s_ref[...] = jnp.zeros_like(s_ref)

    row_masks = [_hgrn_row_masks(False), _hgrn_row_masks(True)]
    refs = ((qf_ref, if_ref, ff_ref, of_ref), (qb_ref, ib_ref, fb_ref, ob_ref))
    inst = [(d, h) for d in range(2) for h in range(HG_HB)]
    level_masks = [[lvl == i for i in range(len(HG_LEVELS) + 1)]
                   for lvl in (_hgrn_level_id(False), _hgrn_level_id(True))]

    pre = []
    for d, h in inst:
        q_ref, i_ref, f_ref, _ = refs[d]
        sl = slice(h * HEAD_DIM, (h + 1) * HEAD_DIM)
        q, k, g, gc = _hgrn_gates(q_ref[:, sl], f_ref[:, sl], la_ref[d:d + 1, sl], lc_ref[d:d + 1, sl],
                                  om_ref[d:d + 1, sl], d == 1)
        g_end = gc[0:1, :] if d == 1 else gc[HG_CHUNK - 1:HG_CHUNK, :]
        pre.append(dict(ops=_hgrn_level_operands(q, k, g, gc, d == 1, row_masks[d]), qk=(q.astype(bf), k.astype(bf)),
                        qe=(q * jnp.exp(gc)).astype(bf), kd=(k * jnp.exp(g_end - gc)).astype(bf),
                        decay=jnp.exp(g_end), v=i_ref[:, sl].astype(bf)))

    attn = []
    for (d, h), pr in zip(inst, pre):
        prods = [lax.dot_general(x, x, nt, preferred_element_type=f32) for x in pr["ops"]]
        prods.append(lax.dot_general(pr["qk"][0], pr["qk"][1], nt, preferred_element_type=f32))
        a = jnp.zeros((HG_CHUNK, HG_CHUNK), f32)
        for i, pmat in enumerate(prods):
            a = jnp.where(level_masks[d][i], pmat, a)
        attn.append(a.astype(bf))

    for (d, h), pr, a in zip(inst, pre, attn):
        o_ref = refs[d][3]
        sl = slice(h * HEAD_DIM, (h + 1) * HEAD_DIM)
        state_t = s_ref[d, h]
        o = jnp.dot(a, pr["v"], preferred_element_type=f32)
        o = o + lax.dot_general(pr["qe"], state_t.astype(bf), nt, preferred_element_type=f32)
        o_ref[:, sl] = o.astype(o_ref.dtype)
        s_ref[d, h] = pr["decay"] * state_t + lax.dot_general(pr["v"], pr["kd"], tn, preferred_element_type=f32)


def _hgrn2(p, col0, lb):
    bsz, lp, _ = p.shape
    nchunk = lp // HG_CHUNK
    cw = HG_HB * HEAD_DIM
    ng = HG_WIDTH // cw
    off = col0 // cw
    la = jnp.log(lb)
    lc = jnp.log1p(-lb)
    om = 1.0 - lb
    fwd = lambda seg: pl.BlockSpec((None, HG_CHUNK, cw), lambda b, g, c: (b, c, off + seg * ng + g))
    bwd = lambda seg: pl.BlockSpec((None, HG_CHUNK, cw), lambda b, g, c: (b, nchunk - 1 - c, off + seg * ng + g))
    par = pl.BlockSpec((2, cw), lambda b, g, c: (0, g))
    return pl.pallas_call(
        _hgrn_kernel,
        grid=(bsz, ng, nchunk),
        in_specs=[fwd(0), fwd(1), fwd(2), bwd(0), bwd(1), bwd(3), par, par, par],
        out_specs=[pl.BlockSpec((None, HG_CHUNK, cw), lambda b, g, c: (b, c, g)),
                   pl.BlockSpec((None, HG_CHUNK, cw), lambda b, g, c: (b, nchunk - 1 - c, g))],
        out_shape=[jax.ShapeDtypeStruct((bsz, lp, HG_WIDTH), jnp.bfloat16)] * 2,
        scratch_shapes=[pltpu.VMEM((2, HG_HB, HEAD_DIM, HEAD_DIM), jnp.float32)],
        compiler_params=pltpu.CompilerParams(
            dimension_semantics=("parallel", "parallel", "arbitrary"), vmem_limit_bytes=V7X_VMEM_LIMIT),
        name="hgrn2_bidirectional",
    )(p, p, p, p, p, p, la, lc, om)


EG_ROWS = 320


def _even_gate_kernel(oa_ref, of_ref, ob_ref, ga_ref, gb_ref, gain_ref, y_ref):
    ga = ga_ref[...]
    y_ref[:, :NA_WIDTH] = (oa_ref[...].astype(jnp.float32) * (ga * jax.nn.sigmoid(ga))).astype(y_ref.dtype)
    for h in range(HG_HEADS):
        sl = slice(h * HEAD_DIM, (h + 1) * HEAD_DIM)
        oh = of_ref[:, sl].astype(jnp.float32) + ob_ref[:, sl].astype(jnp.float32)
        oh = oh * lax.rsqrt(jnp.mean(oh * oh, axis=-1, keepdims=True) + EPS) * gain_ref[:, sl]
        gb = gb_ref[:, sl]
        y_ref[:, NA_WIDTH + h * HEAD_DIM:NA_WIDTH + (h + 1) * HEAD_DIM] = (
            oh * (gb * jax.nn.sigmoid(gb))).astype(y_ref.dtype)


def _even_gate(oa, o_fwd, o_bwd, p, gate_col0, norm_gain):
    bsz, lp, _ = oa.shape
    assert gate_col0 % NA_WIDTH == 0
    goff = gate_col0 // NA_WIDTH
    half = lambda off: pl.BlockSpec((None, EG_ROWS, NA_WIDTH), lambda b, r: (b, r, off))
    return pl.pallas_call(
        _even_gate_kernel,
        grid=(bsz, lp // EG_ROWS),
        in_specs=[half(0), half(0), half(0), half(goff), half(goff + 1),
                  pl.BlockSpec((1, HG_WIDTH), lambda b, r: (0, 0))],
        out_specs=pl.BlockSpec((None, EG_ROWS, D_MODEL), lambda b, r: (b, r, 0)),
        out_shape=jax.ShapeDtypeStruct((bsz, lp, D_MODEL), jnp.bfloat16),
        compiler_params=pltpu.CompilerParams(
            dimension_semantics=("parallel", "parallel"), vmem_limit_bytes=V7X_VMEM_LIMIT),
        name="even_gate",
    )(oa, o_fwd, o_bwd, p, p, norm_gain.reshape(1, HG_WIDTH))


NA_MASKED = -1e30
NA_GROUP = 64


def _na_bias_table(rpb):
    cols = np.arange(GRID_W)
    c0 = np.clip(cols - NA_KW // 2, 0, GRID_W - NA_KW)
    col_ok = (cols[None, :] >= c0[:, None]) & (cols[None, :] < c0[:, None] + NA_KW)
    col_idx = np.clip(cols[None, :] - cols[:, None] + NA_KW - 1, 0, 2 * NA_KW - 2)
    onehot = (col_idx[None] == np.arange(2 * NA_KW - 1)[:, None, None]).astype(np.float32)
    picked = jnp.einsum('hrj,jqk->hrqk', rpb.astype(jnp.float32), onehot, precision=lax.Precision.HIGHEST)
    table = jnp.where(col_ok[None, None], picked, NA_MASKED)
    return jnp.concatenate([table[:, :-1], table[:, 1:]], axis=-1)


def _na_kernel(q_ref, k_ref, v_ref, t_ref, mb_ref, o_ref):
    f32, bf = jnp.float32, jnp.bfloat16
    rows = (q_ref.shape[0] - HG_CHUNK) // GRID_W
    scale = HEAD_DIM ** -0.5
    nt = (((1,), (1,)), ((), ()))
    km = k_ref[PAD:PAD + N_META, :].astype(bf)
    vm = v_ref[PAD:PAD + N_META, :].astype(bf)
    mb = mb_ref[...]
    o_ref[0:PAD, :] = jnp.zeros((PAD, HEAD_DIM), o_ref.dtype)

    qm = q_ref[PAD:PAD + N_META, :].astype(bf)
    s = lax.dot_general(qm, km, nt, preferred_element_type=f32) * scale + mb
    e = jnp.exp(s - jnp.max(s, axis=-1, keepdims=True))
    o_ref[PAD:PAD + N_META, :] = (jnp.dot(e.astype(bf), vm, preferred_element_type=f32)
                                  / jnp.sum(e, axis=-1, keepdims=True)).astype(o_ref.dtype)

    def row_group(g, carry):
        rs = [g * NA_GROUP + i for i in range(NA_GROUP)]
        scores = []
        for r in rs:
            r0 = jnp.clip(r - NA_KH // 2, 0, rows - NA_KH)
            q = q_ref[pl.ds(pl.multiple_of(HG_CHUNK + GRID_W * r, GRID_W), GRID_W), :].astype(bf)
            k0 = pl.multiple_of(HG_CHUNK + GRID_W * r0, GRID_W)
            kw = k_ref[pl.ds(k0, NA_KH * GRID_W), :].astype(bf)
            s_win = lax.dot_general(q, kw, nt, preferred_element_type=f32)
            s_meta = lax.dot_general(q, km, nt, preferred_element_type=f32)
            scores.append((r, r0, k0, s_win, s_meta))
        probs = []
        for r, r0, k0, s_win, s_meta in scores:
            bias = jnp.concatenate([t_ref[r0 + i - r + NA_KH - 1] for i in range(0, NA_KH, 2)], axis=1)
            s_win = s_win * scale + bias
            s_meta = s_meta * scale + mb
            m = jnp.maximum(jnp.max(s_win, axis=-1, keepdims=True), jnp.max(s_meta, axis=-1, keepdims=True))
            pw = jnp.exp(s_win - m)
            pm = jnp.exp(s_meta - m)
            den = jnp.sum(pw, axis=-1, keepdims=True) + jnp.sum(pm, axis=-1, keepdims=True)
            probs.append((r, k0, pw.astype(bf), pm.astype(bf), den))
        for r, k0, pw, pm, den in probs:
            vw = v_ref[pl.ds(k0, NA_KH * GRID_W), :].astype(bf)
            o = jnp.dot(pw, vw, preferred_element_type=f32) + jnp.dot(pm, vm, preferred_element_type=f32)
            o_ref[pl.ds(pl.multiple_of(HG_CHUNK + GRID_W * r, GRID_W), GRID_W), :] = (o / den).astype(o_ref.dtype)
        return carry

    lax.fori_loop(0, rows // NA_GROUP, row_group, 0)


def _natten(p, rpb, meta_bias):
    bsz, lp, _ = p.shape
    assert ((lp - HG_CHUNK) // GRID_W) % NA_GROUP == 0
    heads = NA_WIDTH // HEAD_DIM
    blk = lambda seg: pl.BlockSpec((None, lp, HEAD_DIM), lambda b, h: (b, 0, seg * heads + h))
    table = _na_bias_table(rpb)
    return pl.pallas_call(
        _na_kernel,
        grid=(bsz, heads),
        in_specs=[blk(0), blk(1), blk(2),
                  pl.BlockSpec((None, 2 * NA_KH - 2, GRID_W, 2 * GRID_W), lambda b, h: (h, 0, 0, 0)),
                  pl.BlockSpec((None, 1, N_META), lambda b, h: (h, 0, 0))],
        out_specs=pl.BlockSpec((None, lp, HEAD_DIM), lambda b, h: (b, 0, h)),
        out_shape=jax.ShapeDtypeStruct((bsz, lp, NA_WIDTH), jnp.bfloat16),
        compiler_params=pltpu.CompilerParams(
            dimension_semantics=("parallel", "parallel"), vmem_limit_bytes=V7X_VMEM_LIMIT),
        name="neighbourhood_attention",
    )(p, p, p, table, meta_bias.astype(jnp.float32).reshape(heads, 1, N_META))


def _even_mix(p, rpb, meta_bias, lb, norm_gain):
    oa = _natten(p, rpb, meta_bias)
    o_fwd, o_bwd = _hgrn2(p, 3 * NA_WIDTH, lb)
    return _even_gate(oa, o_fwd, o_bwd, p, 3 * NA_WIDTH + 4 * HG_WIDTH, norm_gain)


def kernel(x, meta_tokens, norm_pre, norm_post, ev_w_in, ev_w_out, na_rpb, na_meta_bias, hg_lower, hg_norm,
           od_w_in, od_w_out, hy_short, hy_ffn_w1, hy_ffn_b1, hy_ffn_freq1, hy_ffn_w2, hy_ffn_b2, hy_ffn_freq2,
           hy_ffn_w3, hy_bias):
    b = x.shape[0]
    depth = norm_pre.shape[0]
    f32 = jnp.float32
    lb_all = jnp.cumsum(jax.nn.softmax(hg_lower.astype(f32), axis=0), axis=0)
    lb_all = lb_all - lb_all[:1]
    h, hn = _embed(x, meta_tokens, norm_pre[0])
    h = h.reshape(b * LP, D_MODEL)
    hn = hn.reshape(b * LP, D_MODEL)
    consts = _dft_constants()
    for layer in range(depth):
        j = layer // 2
        if layer % 2 == 0:
            p = _matmul(hn, ev_w_in, j, f32).reshape(b, LP, -1)
            y = _even_mix(p, na_rpb[j], na_meta_bias[j], lb_all[j], hg_norm[j]).reshape(b * LP, D_MODEL)
            w_out = ev_w_out
        else:
            p3 = _matmul(hn, od_w_in, j, jnp.bfloat16).reshape(b, LP, -1)
            y = _odd_mix(p3, hy_short[j], hy_ffn_w1[j], hy_ffn_b1[j], hy_ffn_freq1[j], hy_ffn_w2[j],
                         hy_ffn_b2[j], hy_ffn_freq2[j], hy_ffn_w3[j], hy_bias[j], consts)
            y = y.reshape(b * LP, D_MODEL)
            w_out = od_w_out
        out = _matmul(y, w_out, j, jnp.bfloat16)
        if layer + 1 < depth:
            h, hn = _residual_rmsnorm(h, out, norm_post[layer], norm_pre[layer + 1])
    return _final_residual(h.reshape(b, LP, D_MODEL), out.reshape(b, LP, D_MODEL), norm_post[depth - 1], x.dtype)
```

```python
import math

import jax
import jax.numpy as jnp
import numpy as np
from jax import lax
from jax.experimental import pallas as pl
from jax.experimental.pallas import tpu as pltpu

D_MODEL = 4096
SEQ = 4096
N_META = 16
GRID_W = 64
HEAD_DIM = 128
NA_WIDTH = D_MODEL // 2
NA_KH = 8
NA_KW = 16
HG_WIDTH = D_MODEL // 2
HG_HEADS = HG_WIDTH // HEAD_DIM
HG_CHUNK = 64
HY_WIDTH = D_MODEL
HY_ORDER = 2
HY_EMB = 33
HY_BANDS = (HY_EMB - 1) // 2
HY_FAST_DECAY = 0.3
HY_SLOW_DECAY = 1.5
HY_TARGET = 1e-2
EPS = 1e-6

PAD = HG_CHUNK - N_META
LP = PAD + N_META + SEQ
L_TOK = N_META + SEQ

V7X_VMEM_LIMIT = 56 * 1024 * 1024


EMBED_PIECES = 5


def _embed_kernel(*refs):
    x_refs, meta_ref, g_ref, h_ref, n_ref = refs[:EMBED_PIECES], *refs[EMBED_PIECES:]
    first = jnp.concatenate([jnp.zeros((PAD, D_MODEL), jnp.float32), meta_ref[...].astype(jnp.float32)], axis=0)
    pieces = [jnp.where(pl.program_id(1) == 0, first, x_refs[0][...].astype(jnp.float32))]
    pieces += [x_ref[...].astype(jnp.float32) for x_ref in x_refs[1:]]
    h = jnp.concatenate(pieces, axis=0)
    h_ref[...] = h
    r = lax.rsqrt(jnp.mean(h * h, axis=-1, keepdims=True) + EPS)
    n_ref[...] = (h * r * g_ref[...]).astype(n_ref.dtype)


def _embed(x, meta_tokens, g):
    bsz, seq, d = x.shape
    assert seq == SEQ and LP % (EMBED_PIECES * HG_CHUNK) == 0
    tr = EMBED_PIECES * HG_CHUNK
    piece = lambda k: pl.BlockSpec((None, HG_CHUNK, d), lambda b, r: (b, jnp.maximum(EMBED_PIECES * r + k - 1, 0), 0))
    row = pl.BlockSpec((None, tr, d), lambda b, r: (b, r, 0))
    return pl.pallas_call(
        _embed_kernel,
        grid=(bsz, LP // tr),
        in_specs=[piece(k) for k in range(EMBED_PIECES)] + [pl.BlockSpec((N_META, d), lambda b, r: (0, 0)),
                                                           pl.BlockSpec((1, d), lambda b, r: (0, 0))],
        out_specs=[row, row],
        out_shape=[jax.ShapeDtypeStruct((bsz, LP, d), jnp.float32), jax.ShapeDtypeStruct((bsz, LP, d), jnp.bfloat16)],
        compiler_params=pltpu.CompilerParams(
            dimension_semantics=("parallel", "parallel"), vmem_limit_bytes=V7X_VMEM_LIMIT),
        name="embed_rmsnorm_pre",
    )(*([x] * EMBED_PIECES), meta_tokens, g.reshape(1, d))


def _post_pre_kernel(h_ref, y_ref, g_ref, gn_ref, o_ref, n_ref):
    y = y_ref[...].astype(jnp.float32)
    r = lax.rsqrt(jnp.mean(y * y, axis=-1, keepdims=True) + EPS)
    h = h_ref[...] + y * r * g_ref[...]
    o_ref[...] = h
    rn = lax.rsqrt(jnp.mean(h * h, axis=-1, keepdims=True) + EPS)
    n_ref[...] = (h * rn * gn_ref[...]).astype(n_ref.dtype)


FINAL_PIECES = 4


def _final_kernel(*refs):
    h_refs, y_refs = refs[:FINAL_PIECES], refs[FINAL_PIECES:2 * FINAL_PIECES]
    g_ref, o_ref = refs[2 * FINAL_PIECES:]
    for k, (h_ref, y_ref) in enumerate(zip(h_refs, y_refs)):
        y = y_ref[...].astype(jnp.float32)
        r = lax.rsqrt(jnp.mean(y * y, axis=-1, keepdims=True) + EPS)
        o_ref[k * HG_CHUNK:(k + 1) * HG_CHUNK, :] = (h_ref[...] + y * r * g_ref[...]).astype(o_ref.dtype)


def _final_residual(h3, y3, g, out_dtype):
    bsz, lp, d = h3.shape
    assert PAD + N_META == HG_CHUNK and SEQ % (FINAL_PIECES * HG_CHUNK) == 0
    piece = lambda k: pl.BlockSpec((None, HG_CHUNK, d), lambda b, r: (b, FINAL_PIECES * r + k + 1, 0))
    pieces = [piece(k) for k in range(FINAL_PIECES)]
    return pl.pallas_call(
        _final_kernel,
        grid=(bsz, SEQ // (FINAL_PIECES * HG_CHUNK)),
        in_specs=pieces + pieces + [pl.BlockSpec((1, d), lambda b, r: (0, 0))],
        out_specs=pl.BlockSpec((None, FINAL_PIECES * HG_CHUNK, d), lambda b, r: (b, r, 0)),
        out_shape=jax.ShapeDtypeStruct((bsz, SEQ, d), out_dtype),
        compiler_params=pltpu.CompilerParams(
            dimension_semantics=("parallel", "parallel"), vmem_limit_bytes=V7X_VMEM_LIMIT),
        name="residual_rmsnorm_post_final",
    )(*([h3] * FINAL_PIECES), *([y3] * FINAL_PIECES), g.reshape(1, d))


def _residual_rmsnorm(h2d, y2d, g, g_next, tr=320):
    m, d = h2d.shape
    row = pl.BlockSpec((tr, d), lambda i: (i, 0))
    vec = pl.BlockSpec((1, d), lambda i: (0, 0))
    params = pltpu.CompilerParams(dimension_semantics=("parallel",), vmem_limit_bytes=V7X_VMEM_LIMIT)
    return pl.pallas_call(
        _post_pre_kernel, grid=(m // tr,), in_specs=[row, row, vec, vec], out_specs=[row, row],
        out_shape=[jax.ShapeDtypeStruct((m, d), jnp.float32), jax.ShapeDtypeStruct((m, d), jnp.bfloat16)],
        compiler_params=params, name="residual_rmsnorm_post_pre",
    )(h2d, y2d, g.reshape(1, d), g_next.reshape(1, d))


def _matmul_kernel(x_ref, w_ref, o_ref):
    o_ref[...] = jnp.dot(x_ref[...], w_ref[...].astype(jnp.bfloat16),
                         preferred_element_type=jnp.float32).astype(o_ref.dtype)


def _matmul(x, w_stack, layer, out_dtype, tm=1040, tn=512):
    m, kd = x.shape
    _, _, n = w_stack.shape
    assert m % tm == 0 and n % tn == 0
    return pl.pallas_call(
        _matmul_kernel,
        grid=(m // tm, n // tn),
        in_specs=[pl.BlockSpec((tm, kd), lambda i, j: (i, 0)),
                  pl.BlockSpec((None, kd, tn), lambda i, j: (layer, 0, j))],
        out_specs=pl.BlockSpec((tm, tn), lambda i, j: (i, j)),
        out_shape=jax.ShapeDtypeStruct((m, n), out_dtype),
        compiler_params=pltpu.CompilerParams(
            dimension_semantics=("parallel", "parallel"), vmem_limit_bytes=V7X_VMEM_LIMIT),
        name="projection_matmul",
    )(x, w_stack)


FFT_N1 = 128
FFT_N2 = 65
FFT_N = FFT_N1 * FFT_N2
N2P = 80
MIDR = 72
SLAB = 264
TC = 128
MID_GROUP = 16
NC_HALF = FFT_N1 // 2 + 1
HALF_P = 72
SLAB_S = 152
C_GROUP = 65
FILL_CHUNKS = 13
B_GROUP = 13


def _dft_constants():
    a = np.arange(FFT_N1)
    c = np.arange(FFT_N1)
    w1 = np.exp(-2j * np.pi * np.outer(c, a) / FFT_N1)
    w1r, w1i = w1.real, w1.imag
    half = FFT_N1 // 2
    f1 = np.block([[w1r[:, :half], -w1i[:, :half]], [w1i[:, :half], w1r[:, :half]]])
    f1_real = np.zeros((2 * HALF_P, FFT_N1), np.float64)
    f1_real[:NC_HALF] = w1r[:NC_HALF]
    f1_real[HALF_P:HALF_P + NC_HALF] = w1i[:NC_HALF]
    g = np.conj(w1).T
    gr, gi = g.real[:half], g.imag[:half]
    i2 = np.block([[gr, -gi], [gi, gr]])
    b = np.arange(FFT_N2)
    d = np.arange(FFT_N2)
    w2 = np.exp(-2j * np.pi * np.outer(d, b) / FFT_N2)
    tw = np.exp(-2j * np.pi * np.outer(c, b) / FFT_N)
    m = w2[None, :, :] * tw[:, None, :]
    ab = np.zeros((FFT_N1, 2 * MIDR, 128), np.float64)
    m[NC_HALF:] = m[NC_HALF:, ::-1, :]
    ab[:, :FFT_N2, :FFT_N2] = m.real
    ab[:, MIDR:MIDR + FFT_N2, :FFT_N2] = m.imag
    bf = jnp.bfloat16
    abt = np.zeros((FFT_N1, N2P, 256), np.float64)
    abt[:, :, :2 * MIDR] = np.swapaxes(ab[:, :, :N2P], 1, 2)
    return (jnp.asarray(f1, bf), jnp.asarray(f1_real, bf), jnp.asarray(i2, bf), jnp.asarray(ab, bf),
            jnp.asarray(abt, bf))


def _middle_stage_inputs(y_ref, c, imag_row=FFT_N1, pitch=SLAB):
    yr = y_ref[pl.ds(c, MIDR, stride=pitch), :]
    yi = y_ref[pl.ds(imag_row + c, MIDR, stride=pitch), :]
    return jnp.concatenate([yr, yi], axis=1)


def _forward_middle(ab, y2):
    y2 = jnp.concatenate([y2, jnp.zeros((128 - MIDR, 2 * TC), jnp.float32)], axis=0).astype(jnp.bfloat16)
    q = jnp.dot(ab, y2, preferred_element_type=jnp.float32)
    zr = q[:MIDR, :TC] - q[MIDR:, TC:]
    zi = q[MIDR:, :TC] + q[:MIDR, TC:]
    return zr, zi


def _spectrum_kernel(h_ref, t_ref, w3f0_ref, w3f1_ref, w3b0_ref, w3b1_ref, delta_ref, f1_ref, ab_ref, k_ref,
                     kt_ref, y_ref):
    w3 = jnp.concatenate([jnp.concatenate([w3f0_ref[...], w3f1_ref[...]], axis=1),
                          jnp.concatenate([w3b0_ref[...], w3b1_ref[...]], axis=1)], axis=0).astype(jnp.bfloat16)
    delta = delta_ref[...]
    rows = FFT_N // FILL_CHUNKS

    def fill(i, carry):
        r0 = pl.multiple_of(i * rows, 8)
        kt = jnp.dot(h_ref[pl.ds(r0, rows), :], w3, preferred_element_type=jnp.float32)
        window = jnp.exp(t_ref[pl.ds(r0, rows), :] * delta)
        for n in range(HY_ORDER):
            kt_ref[n, pl.ds(r0, rows), :] = kt[:, n * TC:(n + 1) * TC] * window
        return carry

    lax.fori_loop(0, FILL_CHUNKS, fill, 0)
    y_ref[pl.ds(FFT_N2 * SLAB_S, (MIDR - FFT_N2) * SLAB_S), :] = jnp.zeros(((MIDR - FFT_N2) * SLAB_S, TC), jnp.float32)

    for n in range(HY_ORDER):
        def stage1(g, carry):
            bs = [g * B_GROUP + i for i in range(B_GROUP)]
            x = jnp.concatenate([kt_ref.at[n][pl.ds(b, FFT_N1, stride=FFT_N2), :] for b in bs], axis=1)
            y = jnp.dot(f1_ref[...], x.astype(jnp.bfloat16), preferred_element_type=jnp.float32)
            for i, b in enumerate(bs):
                y_ref[pl.ds(pl.multiple_of(b * SLAB_S, 8), 2 * HALF_P), :] = y[:, i * TC:(i + 1) * TC]
            return carry

        lax.fori_loop(0, FFT_N2 // B_GROUP, stage1, 0)

        def stage2(g, carry):
            cs = [g * C_GROUP + i for i in range(C_GROUP)]
            zs = [_forward_middle(ab_ref[c], _middle_stage_inputs(y_ref, c, HALF_P, SLAB_S)) for c in cs]
            for c, (zr, zi) in zip(cs, zs):
                z = jnp.concatenate([zr, zi], axis=1) * (1.0 / FFT_N)
                k_ref[n, c] = jnp.concatenate(
                    [z, jnp.zeros((N2P - MIDR, 2 * TC), jnp.float32)], axis=0).astype(jnp.bfloat16)
            return carry

        lax.fori_loop(0, NC_HALF // C_GROUP, stage2, 0)


def _filter_spectrum(h2, w3, deltas, consts):
    _, f1_real, _, ab, _ = consts
    nt = HY_WIDTH // TC
    j = np.arange(FFT_N)
    lag = np.where(j < L_TOK, j, FFT_N - j).astype(np.float32)
    neg_t = jnp.asarray(np.broadcast_to((-lag / np.float32(L_TOK - 1))[:, None], (FFT_N, TC)))
    w3_cols = lambda direction, order: pl.BlockSpec((64, TC), lambda i: (0, (direction * HY_ORDER + order) * nt + i))
    return pl.pallas_call(
        _spectrum_kernel,
        grid=(nt,),
        in_specs=[
            pl.BlockSpec((FFT_N, 128), lambda i: (0, 0)),
            pl.BlockSpec((FFT_N, TC), lambda i: (0, 0)),
            w3_cols(0, 0), w3_cols(0, 1), w3_cols(1, 0), w3_cols(1, 1),
            pl.BlockSpec((1, TC), lambda i: (0, i)),
            pl.BlockSpec((2 * HALF_P, FFT_N1), lambda i: (0, 0)),
            pl.BlockSpec((FFT_N1, 2 * MIDR, 128), lambda i: (0, 0, 0)),
        ],
        out_specs=pl.BlockSpec((HY_ORDER, None, NC_HALF, N2P, 2 * TC), lambda i: (0, i, 0, 0, 0)),
        out_shape=jax.ShapeDtypeStruct((HY_ORDER, nt, NC_HALF, N2P, 2 * TC), jnp.bfloat16),
        scratch_shapes=[pltpu.VMEM((HY_ORDER, FFT_N, TC), jnp.float32), pltpu.VMEM((MIDR * SLAB_S, TC), jnp.float32)],
        compiler_params=pltpu.CompilerParams(dimension_semantics=("parallel",), vmem_limit_bytes=V7X_VMEM_LIMIT),
        name="hyena_filter_spectrum",
    )(h2, neg_t, w3, w3, w3, w3, deltas, f1_real, ab)


def _conv_kernel(u_ref, k_ref, f1_ref, i2_ref, ab_ref, abt_ref, o_ref, y_ref):
    _conv_body(u_ref, k_ref, f1_ref, i2_ref, ab_ref, abt_ref, o_ref, y_ref)


def _conv_raw_kernel(p_ref, w_ref, k_ref, f1_ref, i2_ref, ab_ref, abt_ref, o_ref, v_ref, y_ref):
    for ri in range(2):
        v_ref[ri] = _short_conv_rows(p_ref[ri].astype(jnp.float32), w_ref[...])
    _conv_body(v_ref, k_ref, f1_ref, i2_ref, ab_ref, abt_ref, o_ref, y_ref)


def _conv_body(u_ref, k_ref, f1_ref, i2_ref, ab_ref, abt_ref, o_ref, y_ref):
    y_ref[pl.ds(FFT_N2 * SLAB, (MIDR - FFT_N2) * SLAB), :] = jnp.zeros(((MIDR - FFT_N2) * SLAB, TC), jnp.float32)
    half = FFT_N1 // 2

    def stage1(g, carry):
        bs = [g * B_GROUP + i for i in range(B_GROUP)]
        x = jnp.concatenate(
            [jnp.concatenate([u_ref.at[ri][pl.ds(b, half, stride=FFT_N2), :] for b in bs], axis=1) for ri in range(2)],
            axis=0)
        y = jnp.dot(f1_ref[...], x.astype(jnp.bfloat16), preferred_element_type=jnp.float32)
        for i, b in enumerate(bs):
            y_ref[pl.ds(pl.multiple_of(b * SLAB, 8), 2 * FFT_N1), :] = y[:, i * TC:(i + 1) * TC]
        return carry

    lax.fori_loop(0, FFT_N2 // B_GROUP, stage1, 0)

    def inverse_middle(c, z):
        zr, zi = z
        mirrored = c >= NC_HALF
        kk = k_ref[jnp.where(mirrored, FFT_N1 - c, c)].astype(jnp.float32)[:MIDR]
        kr, ki = kk[:, :TC], kk[:, TC:] * jnp.where(mirrored, -1.0, 1.0)
        pr = zr * kr - zi * ki
        pi = zr * ki + zi * kr
        rhs = jnp.concatenate([jnp.concatenate([pr, pi], axis=1), jnp.concatenate([pi, -pr], axis=1),
                               jnp.zeros((256 - 2 * MIDR, 2 * TC), jnp.float32)], axis=0)
        return jnp.dot(abt_ref[c], rhs.astype(jnp.bfloat16), preferred_element_type=jnp.float32)

    def middle(g, carry):
        cs = [g * MID_GROUP + i for i in range(MID_GROUP)]
        zs = [_forward_middle(ab_ref[c], _middle_stage_inputs(y_ref, c)) for c in cs]
        outs = [inverse_middle(c, z) for c, z in zip(cs, zs)]
        for c, v in zip(cs, outs):
            y_ref[pl.ds(c, MIDR, stride=SLAB), :] = v[:MIDR, :TC]
            y_ref[pl.ds(FFT_N1 + c, MIDR, stride=SLAB), :] = v[:MIDR, TC:]
        return carry

    lax.fori_loop(0, FFT_N1 // MID_GROUP, middle, 0)

    def stage4(g, carry):
        bs = [g * B_GROUP + i for i in range(B_GROUP)]
        v = jnp.concatenate([y_ref[pl.ds(pl.multiple_of(b * SLAB, 8), 2 * FFT_N1), :] for b in bs], axis=1)
        x = jnp.dot(i2_ref[...], v.astype(jnp.bfloat16), preferred_element_type=jnp.float32)
        for i, b in enumerate(bs):
            o_ref.at[0][pl.ds(b, half, stride=FFT_N2), :] = x[:half, i * TC:(i + 1) * TC]
            o_ref.at[1][pl.ds(b, half, stride=FFT_N2), :] = x[half:, i * TC:(i + 1) * TC]
        return carry

    lax.fori_loop(0, FFT_N2 // B_GROUP, stage4, 0)


def _long_conv(u, kspec, order, consts, w_short_t=None):
    f1, _, i2, ab, abt = consts
    bsz, lp, _ = u.shape
    assert bsz == 2 and lp == LP
    nt = HY_WIDTH // TC
    seq = pl.BlockSpec((bsz, lp, TC), lambda i: (0, 0, i))
    out = jax.ShapeDtypeStruct((bsz, lp, HY_WIDTH), jnp.float32)
    raw = w_short_t is not None
    return pl.pallas_call(
        _conv_raw_kernel if raw else _conv_kernel,
        grid=(nt,),
        in_specs=[seq] + ([pl.BlockSpec((3, TC), lambda i: (0, i))] if raw else []) + [
            pl.BlockSpec((None, None, NC_HALF, N2P, 2 * TC), lambda i: (order, i, 0, 0, 0)),
            pl.BlockSpec((2 * FFT_N1, FFT_N1), lambda i: (0, 0)),
            pl.BlockSpec((FFT_N1, 2 * FFT_N1), lambda i: (0, 0)),
            pl.BlockSpec((FFT_N1, 2 * MIDR, 128), lambda i: (0, 0, 0), pipeline_mode=pl.Buffered(1)),
            pl.BlockSpec((FFT_N1, N2P, 256), lambda i: (0, 0, 0), pipeline_mode=pl.Buffered(1)),
        ],
        out_specs=[seq, seq] if raw else seq,
        out_shape=[out, out] if raw else out,
        scratch_shapes=[pltpu.VMEM((MIDR * SLAB, TC), jnp.float32)],
        compiler_params=pltpu.CompilerParams(dimension_semantics=("parallel",), vmem_limit_bytes=V7X_VMEM_LIMIT),
        name="hyena_long_conv",
    )(*((u, w_short_t) if raw else (u,)), kspec, f1, i2, ab, abt)


def _short_conv_rows(u, w):
    y = pltpu.roll(u, 1, 0) * w[0:1] + u * w[1:2] + pltpu.roll(u, LP - 1, 0) * w[2:3]
    row = lax.broadcasted_iota(jnp.int32, y.shape, 0)
    return jnp.where(row >= PAD, y, 0.0)


def _gate1_kernel(v_ref, u_ref, w_ref, y_ref, b_ref, o_ref):
    x = _short_conv_rows(u_ref[...].astype(jnp.float32), w_ref[...])
    o_ref[...] = x * (y_ref[...] + v_ref[...] * b_ref[...])


def _gate2_kernel(z_ref, u_ref, w_ref, y_ref, b_ref, g_ref, o_ref):
    g = g_ref[...].astype(jnp.float32)
    z = _short_conv_rows(u_ref[...].astype(jnp.float32), w_ref[...]) * (y_ref[...] + z_ref[...] * b_ref[...])
    o_ref[...] = (z * (g * jax.nn.sigmoid(g))).astype(o_ref.dtype)


def _seq_tile(col_block):
    return pl.BlockSpec((None, LP, TC), lambda b, i: (b, 0, col_block + i))


def _vec_tile(rows, col_block):
    return pl.BlockSpec((rows, TC), lambda b, i: (0, col_block + i))


def _hyena_gate1(v, p3, w_short_t, y, bias):
    bsz = p3.shape[0]
    nt = HY_WIDTH // TC
    return pl.pallas_call(
        _gate1_kernel,
        grid=(bsz, nt),
        in_specs=[_seq_tile(0), _seq_tile(nt), _vec_tile(3, nt), _seq_tile(0), _vec_tile(1, 0)],
        out_specs=_seq_tile(0),
        out_shape=jax.ShapeDtypeStruct((bsz, LP, HY_WIDTH), jnp.float32),
        compiler_params=pltpu.CompilerParams(
            dimension_semantics=("parallel", "parallel"), vmem_limit_bytes=V7X_VMEM_LIMIT),
        name="hyena_gate1",
    )(v, p3, w_short_t, y, bias.reshape(1, HY_WIDTH))


def _hyena_gate2(z1, p3, w_short_t, y, bias):
    bsz = p3.shape[0]
    nt = HY_WIDTH // TC
    return pl.pallas_call(
        _gate2_kernel,
        grid=(bsz, nt),
        in_specs=[_seq_tile(0), _seq_tile(2 * nt), _vec_tile(3, 2 * nt), _seq_tile(0), _vec_tile(1, 0),
                  _seq_tile(3 * nt)],
        out_specs=_seq_tile(0),
        out_shape=jax.ShapeDtypeStruct((bsz, LP, HY_WIDTH), jnp.bfloat16),
        compiler_params=pltpu.CompilerParams(
            dimension_semantics=("parallel", "parallel"), vmem_limit_bytes=V7X_VMEM_LIMIT),
        name="hyena_gate2",
    )(z1, p3, w_short_t, y, bias.reshape(1, HY_WIDTH), p3)


def _filter_positions():
    f32 = np.float32
    l = L_TOK
    t = np.linspace(0.0, 1.0, l, dtype=f32)[:, None]
    w = f32(2.0 * math.pi) * np.arange(l, dtype=f32)[:, None] / f32(l)
    bands = np.linspace(1e-4, HY_BANDS - 1, HY_BANDS, dtype=f32)[None, :]
    z = np.concatenate([t, np.cos(bands * w), -np.sin(bands * w)], axis=-1).astype(f32)
    j = np.arange(FFT_N)
    lag = np.minimum(np.where(j < l, j, FFT_N - j), l - 1)
    zrows = np.zeros((FFT_N, 128), f32)
    zrows[:, :HY_EMB] = z[lag]
    mask = np.zeros((FFT_N, 128), f32)
    mask[:, :64] = (j < l)[:, None]
    mask[:, 64:] = ((j > FFT_N - l) | (j == 0))[:, None]
    return jnp.asarray(zrows), jnp.asarray(mask)


def _hidden_kernel(z_ref, m_ref, w1_ref, b1_ref, fr1_ref, w2_ref, b2_ref, fr2_ref, o_ref):
    hi = lax.Precision.HIGHEST
    a = jnp.dot(z_ref[...], w1_ref[...], precision=hi, preferred_element_type=jnp.float32) + b1_ref[...]
    hid = jnp.sin(fr1_ref[...] * a)
    a = jnp.dot(hid, w2_ref[...], precision=hi, preferred_element_type=jnp.float32) + b2_ref[...]
    hid = jnp.sin(fr2_ref[...] * a)
    o_ref[...] = (jnp.concatenate([hid, hid], axis=1) * m_ref[...]).astype(o_ref.dtype)


def _hyena_hidden(w1, b1, fr1, w2, b2, fr2):
    zrows, mask = _filter_positions()
    rows = FFT_N // FILL_CHUNKS
    hdim = w2.shape[0]
    w1p = jnp.zeros((128, hdim), jnp.float32).at[:HY_EMB].set(w1.astype(jnp.float32))
    tile = pl.BlockSpec((rows, 128), lambda i: (i, 0))
    full = lambda a: pl.BlockSpec(a.shape, lambda i: (0,) * a.ndim)
    vecs = [v.astype(jnp.float32).reshape(1, hdim) for v in (b1, fr1, b2, fr2)]
    w2f = w2.astype(jnp.float32)
    return pl.pallas_call(
        _hidden_kernel,
        grid=(FILL_CHUNKS,),
        in_specs=[tile, tile, full(w1p), full(vecs[0]), full(vecs[1]), full(w2f), full(vecs[2]), full(vecs[3])],
        out_specs=tile,
        out_shape=jax.ShapeDtypeStruct((FFT_N, 128), jnp.bfloat16),
        compiler_params=pltpu.CompilerParams(dimension_semantics=("parallel",), vmem_limit_bytes=V7X_VMEM_LIMIT),
        name="hyena_filter_hidden",
    )(zrows, mask, w1p, vecs[0], vecs[1], w2f, vecs[2], vecs[3])


def _odd_mix(p3, w_short, w1, b1, fr1, w2, b2, fr2, w3, bias, consts):
    min_decay = math.log(HY_TARGET) / HY_SLOW_DECAY
    max_decay = math.log(HY_TARGET) / HY_FAST_DECAY
    deltas = jnp.abs(jnp.linspace(min_decay, max_decay, HY_WIDTH, dtype=jnp.float32)).reshape(1, HY_WIDTH)
    kspec = _filter_spectrum(_hyena_hidden(w1, b1, fr1, w2, b2, fr2), w3, deltas, consts)
    w_short_t = w_short.T
    y1, v = _long_conv(p3, kspec, 0, consts, w_short_t)
    z1 = _hyena_gate1(v, p3, w_short_t, y1, bias[0])
    y2 = _long_conv(z1, kspec, 1, consts)
    return _hyena_gate2(z1, p3, w_short_t, y2, bias[1])


HG_HB = 16
HG_LEVELS = (32, 16, 8, 4, 2, 1)
LOG2_E = 1.4426950408889634


def _cumsum_rows(g, rev):
    f32, bf = jnp.float32, jnp.bfloat16
    t = lax.broadcasted_iota(jnp.int32, (HG_CHUNK, 3 * HG_CHUNK), 0)
    s = lax.broadcasted_iota(jnp.int32, (HG_CHUNK, 3 * HG_CHUNK), 1) & (HG_CHUNK - 1)
    tri = jnp.where((s >= t) if rev else (s <= t), 1.0, 0.0).astype(bf)
    hi = g.astype(bf)
    r1 = g - hi.astype(f32)
    mid = r1.astype(bf)
    lo = (r1 - mid.astype(f32)).astype(bf)
    return jnp.dot(tri, jnp.concatenate([hi, mid, lo], axis=0), preferred_element_type=f32)


def _hgrn_row_masks(rev):
    row = lax.broadcasted_iota(jnp.int32, (HG_CHUNK, HEAD_DIM), 0)
    masks = {}
    for m in HG_LEVELS:
        upper = (row & m) != 0
        is_query = jnp.logical_not(upper) if rev else upper
        target = m if rev else m - 1
        selectors = [(src, (row & (2 * m - 1)) == src) for src in range(2 * m) if src != target] if m == 2 else None
        masks[m] = (is_query, jnp.where(is_query, LOG2_E, -LOG2_E), selectors)
    return masks


def _level_reference(gc, m, rev, selectors):
    if m >= 4:
        g3 = gc.reshape(HG_CHUNK // (2 * m), 2 * m, HEAD_DIM)
        ref = g3[:, m:m + 1, :] if rev else g3[:, m - 1:m, :]
        return jnp.broadcast_to(ref, g3.shape).reshape(HG_CHUNK, HEAD_DIM)
    target = m if rev else m - 1
    out = gc
    for src, picks_src in selectors:
        out = jnp.where(picks_src, pltpu.roll(gc, (src - target) % HG_CHUNK, 0), out)
    return out


def _hgrn_gates(q_raw, f_raw, la, lc, omlb, rev):
    q = q_raw * (0.5 * HEAD_DIM ** -0.5) * (1.0 + jnp.tanh(0.5 * q_raw))
    k = (0.5 * omlb) * (1.0 - jnp.tanh(0.5 * f_raw))
    log_sig = jnp.minimum(f_raw, 0.0) - jnp.log(1.0 + jnp.exp(-jnp.abs(f_raw)))
    b = lc + log_sig
    g = jnp.maximum(la, b) + jnp.log(1.0 + jnp.exp(-jnp.abs(la - b)))
    return q, k, g, _cumsum_rows(g, rev)


def _hgrn_level_operands(q, k, g, gc, rev, masks):
    bf = jnp.bfloat16
    qb, kb = q.astype(bf), k.astype(bf)
    ops = []
    for m in HG_LEVELS:
        is_query, sign_log2e, selectors = masks[m]
        if m == 1:
            ops.append(jnp.where(is_query, qb * jnp.exp2(g * LOG2_E).astype(bf), kb))
        else:
            gref = _level_reference(gc, m, rev, selectors)
            ops.append(jnp.where(is_query, qb, kb) * jnp.exp2((gc - gref) * sign_log2e).astype(bf))
    return ops


def _hgrn_level_id(rev):
    t = lax.broadcasted_iota(jnp.int32, (HG_CHUNK, HG_CHUNK), 0)
    s = lax.broadcasted_iota(jnp.int32, (HG_CHUNK, HG_CHUNK), 1)
    vis = (s >= t) if rev else (s <= t)
    x = t ^ s
    lvl = jnp.full((HG_CHUNK, HG_CHUNK), len(HG_LEVELS), jnp.int32)
    for i, m in enumerate(HG_LEVELS):
        lvl = jnp.where((x & m) != 0, jnp.minimum(lvl, i), lvl)
    return jnp.where(vis, lvl, -1)


def _hgrn_kernel(qf_ref, if_ref, ff_ref, qb_ref, ib_ref, fb_ref, la_ref, lc_ref, om_ref, of_ref, ob_ref, s_ref):
    f32, bf = jnp.float32, jnp.bfloat16
    nt = (((1,), (1,)), ((), ()))
    tn = (((0,), (0,)), ((), ()))

    @pl.when(pl.program_id(2) == 0)
    def _():
        s_ref[...] = jnp.zeros_like(s_ref)

    row_masks = [_hgrn_row_masks(False), _hgrn_row_masks(True)]
    refs = ((qf_ref, if_ref, ff_ref, of_ref), (qb_ref, ib_ref, fb_ref, ob_ref))
    inst = [(d, h) for d in range(2) for h in range(HG_HB)]
    level_masks = [[lvl == i for i in range(len(HG_LEVELS) + 1)]
                   for lvl in (_hgrn_level_id(False), _hgrn_level_id(True))]

    pre = []
    for d, h in inst:
        q_ref, i_ref, f_ref, _ = refs[d]
        sl = slice(h * HEAD_DIM, (h + 1) * HEAD_DIM)
        q, k, g, gc = _hgrn_gates(q_ref[:, sl], f_ref[:, sl], la_ref[d:d + 1, sl], lc_ref[d:d + 1, sl],
                                  om_ref[d:d + 1, sl], d == 1)
        g_end = gc[0:1, :] if d == 1 else gc[HG_CHUNK - 1:HG_CHUNK, :]
        pre.append(dict(ops=_hgrn_level_operands(q, k, g, gc, d == 1, row_masks[d]), qk=(q.astype(bf), k.astype(bf)),
                        qe=(q * jnp.exp(gc)).astype(bf), kd=(k * jnp.exp(g_end - gc)).astype(bf),
                        decay=jnp.exp(g_end), v=i_ref[:, sl].astype(bf)))

    attn = []
    for (d, h), pr in zip(inst, pre):
        prods = [lax.dot_general(x, x, nt, preferred_element_type=f32) for x in pr["ops"]]
        prods.append(lax.dot_general(pr["qk"][0], pr["qk"][1], nt, preferred_element_type=f32))
        a = jnp.zeros((HG_CHUNK, HG_CHUNK), f32)
        for i, pmat in enumerate(prods):
            a = jnp.where(level_masks[d][i], pmat, a)
        attn.append(a.astype(bf))

    for (d, h), pr, a in zip(inst, pre, attn):
        o_ref = refs[d][3]
        sl = slice(h * HEAD_DIM, (h + 1) * HEAD_DIM)
        state_t = s_ref[d, h]
        o = jnp.dot(a, pr["v"], preferred_element_type=f32)
        o = o + lax.dot_general(pr["qe"], state_t.astype(bf), nt, preferred_element_type=f32)
        o_ref[:, sl] = o.astype(o_ref.dtype)
        s_ref[d, h] = pr["decay"] * state_t + lax.dot_general(pr["v"], pr["kd"], tn, preferred_element_type=f32)


def _hgrn2(p, col0, lb):
    bsz, lp, _ = p.shape
    nchunk = lp // HG_CHUNK
    cw = HG_HB * HEAD_DIM
    ng = HG_WIDTH // cw
    off = col0 // cw
    la = jnp.log(lb)
    lc = jnp.log1p(-lb)
    om = 1.0 - lb
    fwd = lambda seg: pl.BlockSpec((None, HG_CHUNK, cw), lambda b, g, c: (b, c, off + seg * ng + g))
    bwd = lambda seg: pl.BlockSpec((None, HG_CHUNK, cw), lambda b, g, c: (b, nchunk - 1 - c, off + seg * ng + g))
    par = pl.BlockSpec((2, cw), lambda b, g, c: (0, g))
    return pl.pallas_call(
        _hgrn_kernel,
        grid=(bsz, ng, nchunk),
        in_specs=[fwd(0), fwd(1), fwd(2), bwd(0), bwd(1), bwd(3), par, par, par],
        out_specs=[pl.BlockSpec((None, HG_CHUNK, cw), lambda b, g, c: (b, c, g)),
                   pl.BlockSpec((None, HG_CHUNK, cw), lambda b, g, c: (b, nchunk - 1 - c, g))],
        out_shape=[jax.ShapeDtypeStruct((bsz, lp, HG_WIDTH), jnp.bfloat16)] * 2,
        scratch_shapes=[pltpu.VMEM((2, HG_HB, HEAD_DIM, HEAD_DIM), jnp.float32)],
        compiler_params=pltpu.CompilerParams(
            dimension_semantics=("parallel", "parallel", "arbitrary"), vmem_limit_bytes=V7X_VMEM_LIMIT),
        name="hgrn2_bidirectional",
    )(p, p, p, p, p, p, la, lc, om)


EG_ROWS = 320


def _even_gate_kernel(oa_ref, of_ref, ob_ref, ga_ref, gb_ref, gain_ref, y_ref):
    ga = ga_ref[...]
    y_ref[:, :NA_WIDTH] = (oa_ref[...].astype(jnp.float32) * (ga * jax.nn.sigmoid(ga))).astype(y_ref.dtype)
    for h in range(HG_HEADS):
        sl = slice(h * HEAD_DIM, (h + 1) * HEAD_DIM)
        oh = of_ref[:, sl].astype(jnp.float32) + ob_ref[:, sl].astype(jnp.float32)
        oh = oh * lax.rsqrt(jnp.mean(oh * oh, axis=-1, keepdims=True) + EPS) * gain_ref[:, sl]
        gb = gb_ref[:, sl]
        y_ref[:, NA_WIDTH + h * HEAD_DIM:NA_WIDTH + (h + 1) * HEAD_DIM] = (
            oh * (gb * jax.nn.sigmoid(gb))).astype(y_ref.dtype)


def _even_gate(oa, o_fwd, o_bwd, p, gate_col0, norm_gain):
    bsz, lp, _ = oa.shape
    assert gate_col0 % NA_WIDTH == 0
    goff = gate_col0 // NA_WIDTH
    half = lambda off: pl.BlockSpec((None, EG_ROWS, NA_WIDTH), lambda b, r: (b, r, off))
    return pl.pallas_call(
        _even_gate_kernel,
        grid=(bsz, lp // EG_ROWS),
        in_specs=[half(0), half(0), half(0), half(goff), half(goff + 1),
                  pl.BlockSpec((1, HG_WIDTH), lambda b, r: (0, 0))],
        out_specs=pl.BlockSpec((None, EG_ROWS, D_MODEL), lambda b, r: (b, r, 0)),
        out_shape=jax.ShapeDtypeStruct((bsz, lp, D_MODEL), jnp.bfloat16),
        compiler_params=pltpu.CompilerParams(
            dimension_semantics=("parallel", "parallel"), vmem_limit_bytes=V7X_VMEM_LIMIT),
        name="even_gate",
    )(oa, o_fwd, o_bwd, p, p, norm_gain.reshape(1, HG_WIDTH))


NA_MASKED = -1e30
NA_GROUP = 64


def _na_bias_table(rpb):
    cols = np.arange(GRID_W)
    c0 = np.clip(cols - NA_KW // 2, 0, GRID_W - NA_KW)
    col_ok = (cols[None, :] >= c0[:, None]) & (cols[None, :] < c0[:, None] + NA_KW)
    col_idx = np.clip(cols[None, :] - cols[:, None] + NA_KW - 1, 0, 2 * NA_KW - 2)
    onehot = (col_idx[None] == np.arange(2 * NA_KW - 1)[:, None, None]).astype(np.float32)
    picked = jnp.einsum('hrj,jqk->hrqk', rpb.astype(jnp.float32), onehot, precision=lax.Precision.HIGHEST)
    table = jnp.where(col_ok[None, None], picked, NA_MASKED)
    return jnp.concatenate([table[:, :-1], table[:, 1:]], axis=-1)


def _na_kernel(q_ref, k_ref, v_ref, t_ref, mb_ref, o_ref):
    f32, bf = jnp.float32, jnp.bfloat16
    rows = (q_ref.shape[0] - HG_CHUNK) // GRID_W
    scale = HEAD_DIM ** -0.5
    nt = (((1,), (1,)), ((), ()))
    km = k_ref[PAD:PAD + N_META, :].astype(bf)
    vm = v_ref[PAD:PAD + N_META, :].astype(bf)
    mb = mb_ref[...]
    o_ref[0:PAD, :] = jnp.zeros((PAD, HEAD_DIM), o_ref.dtype)

    qm = q_ref[PAD:PAD + N_META, :].astype(bf)
    s = lax.dot_general(qm, km, nt, preferred_element_type=f32) * scale + mb
    e = jnp.exp(s - jnp.max(s, axis=-1, keepdims=True))
    o_ref[PAD:PAD + N_META, :] = (jnp.dot(e.astype(bf), vm, preferred_element_type=f32)
                                  / jnp.sum(e, axis=-1, keepdims=True)).astype(o_ref.dtype)

    def row_group(g, carry):
        rs = [g * NA_GROUP + i for i in range(NA_GROUP)]
        scores = []
        for r in rs:
            r0 = jnp.clip(r - NA_KH // 2, 0, rows - NA_KH)
            q = q_ref[pl.ds(pl.multiple_of(HG_CHUNK + GRID_W * r, GRID_W), GRID_W), :].astype(bf)
            k0 = pl.multiple_of(HG_CHUNK + GRID_W * r0, GRID_W)
            kw = k_ref[pl.ds(k0, NA_KH * GRID_W), :].astype(bf)
            s_win = lax.dot_general(q, kw, nt, preferred_element_type=f32)
            s_meta = lax.dot_general(q, km, nt, preferred_element_type=f32)
            scores.append((r, r0, k0, s_win, s_meta))
        probs = []
        for r, r0, k0, s_win, s_meta in scores:
            bias = jnp.concatenate([t_ref[r0 + i - r + NA_KH - 1] for i in range(0, NA_KH, 2)], axis=1)
            s_win = s_win * scale + bias
            s_meta = s_meta * scale + mb
            m = jnp.maximum(jnp.max(s_win, axis=-1, keepdims=True), jnp.max(s_meta, axis=-1, keepdims=True))
            pw = jnp.exp(s_win - m)
            pm = jnp.exp(s_meta - m)
            den = jnp.sum(pw, axis=-1, keepdims=True) + jnp.sum(pm, axis=-1, keepdims=True)
            probs.append((r, k0, pw.astype(bf), pm.astype(bf), den))
        for r, k0, pw, pm, den in probs:
            vw = v_ref[pl.ds(k0, NA_KH * GRID_W), :].astype(bf)
            o = jnp.dot(pw, vw, preferred_element_type=f32) + jnp.dot(pm, vm, preferred_element_type=f32)
            o_ref[pl.ds(pl.multiple_of(HG_CHUNK + GRID_W * r, GRID_W), GRID_W), :] = (o / den).astype(o_ref.dtype)
        return carry

    lax.fori_loop(0, rows // NA_GROUP, row_group, 0)


def _natten(p, rpb, meta_bias):
    bsz, lp, _ = p.shape
    assert ((lp - HG_CHUNK) // GRID_W) % NA_GROUP == 0
    heads = NA_WIDTH // HEAD_DIM
    blk = lambda seg: pl.BlockSpec((None, lp, HEAD_DIM), lambda b, h: (b, 0, seg * heads + h))
    table = _na_bias_table(rpb)
    return pl.pallas_call(
        _na_kernel,
        grid=(bsz, heads),
        in_specs=[blk(0), blk(1), blk(2),
                  pl.BlockSpec((None, 2 * NA_KH - 2, GRID_W, 2 * GRID_W), lambda b, h: (h, 0, 0, 0)),
                  pl.BlockSpec((None, 1, N_META), lambda b, h: (h, 0, 0))],
        out_specs=pl.BlockSpec((None, lp, HEAD_DIM), lambda b, h: (b, 0, h)),
        out_shape=jax.ShapeDtypeStruct((bsz, lp, NA_WIDTH), jnp.bfloat16),
        compiler_params=pltpu.CompilerParams(
            dimension_semantics=("parallel", "parallel"), vmem_limit_bytes=V7X_VMEM_LIMIT),
        name="neighbourhood_attention",
    )(p, p, p, table, meta_bias.astype(jnp.float32).reshape(heads, 1, N_META))


def _even_mix(p, rpb, meta_bias, lb, norm_gain):
    oa = _natten(p, rpb, meta_bias)
    o_fwd, o_bwd = _hgrn2(p, 3 * NA_WIDTH, lb)
    return _even_gate(oa, o_fwd, o_bwd, p, 3 * NA_WIDTH + 4 * HG_WIDTH, norm_gain)


def kernel(x, meta_tokens, norm_pre, norm_post, ev_w_in, ev_w_out, na_rpb, na_meta_bias, hg_lower, hg_norm,
           od_w_in, od_w_out, hy_short, hy_ffn_w1, hy_ffn_b1, hy_ffn_freq1, hy_ffn_w2, hy_ffn_b2, hy_ffn_freq2,
           hy_ffn_w3, hy_bias):
    b = x.shape[0]
    depth = norm_pre.shape[0]
    f32 = jnp.float32
    lb_all = jnp.cumsum(jax.nn.softmax(hg_lower.astype(f32), axis=0), axis=0)
    lb_all = lb_all - lb_all[:1]
    h, hn = _embed(x, meta_tokens, norm_pre[0])
    h = h.reshape(b * LP, D_MODEL)
    hn = hn.reshape(b * LP, D_MODEL)
    consts = _dft_constants()
    for layer in range(depth):
        j = layer // 2
        if layer % 2 == 0:
            p = _matmul(hn, ev_w_in, j, f32).reshape(b, LP, -1)
            y = _even_mix(p, na_rpb[j], na_meta_bias[j], lb_all[j], hg_norm[j]).reshape(b * LP, D_MODEL)
            w_out = ev_w_out
        else:
            p3 = _matmul(hn, od_w_in, j, jnp.bfloat16).reshape(b, LP, -1)
            y = _odd_mix(p3, hy_short[j], hy_ffn_w1[j], hy_ffn_b1[j], hy_ffn_freq1[j], hy_ffn_w2[j],
                         hy_ffn_b2[j], hy_ffn_freq2[j], hy_ffn_w3[j], hy_bias[j], consts)
            y = y.reshape(b * LP, D_MODEL)
            w_out = od_w_out
        out = _matmul(y, w_out, j, jnp.bfloat16)
        if layer + 1 < depth:
            h, hn = _residual_rmsnorm(h, out, norm_post[layer], norm_pre[layer + 1])
    return _final_residual(h.reshape(b, LP, D_MODEL), out.reshape(b, LP, D_MODEL), norm_post[depth - 1], x.dtype)
```

```python
import math

import jax
import jax.numpy as jnp
import numpy as np
from jax import lax
from jax.experimental import pallas as pl
from jax.experimental.pallas import tpu as pltpu

D_MODEL = 4096
SEQ = 4096
N_META = 16
GRID_W = 64
HEAD_DIM = 128
NA_WIDTH = D_MODEL // 2
NA_KH = 8
NA_KW = 16
HG_WIDTH = D_MODEL // 2
HG_HEADS = HG_WIDTH // HEAD_DIM
HG_CHUNK = 64
HY_WIDTH = D_MODEL
HY_ORDER = 2
HY_EMB = 33
HY_BANDS = (HY_EMB - 1) // 2
HY_FAST_DECAY = 0.3
HY_SLOW_DECAY = 1.5
HY_TARGET = 1e-2
EPS = 1e-6

PAD = HG_CHUNK - N_META
LP = PAD + N_META + SEQ
L_TOK = N_META + SEQ

V7X_VMEM_LIMIT = 56 * 1024 * 1024


EMBED_PIECES = 5


def _embed_kernel(*refs):
    x_refs, meta_ref, g_ref, h_ref, n_ref = refs[:EMBED_PIECES], *refs[EMBED_PIECES:]
    first = jnp.concatenate([jnp.zeros((PAD, D_MODEL), jnp.float32), meta_ref[...].astype(jnp.float32)], axis=0)
    pieces = [jnp.where(pl.program_id(1) == 0, first, x_refs[0][...].astype(jnp.float32))]
    pieces += [x_ref[...].astype(jnp.float32) for x_ref in x_refs[1:]]
    h = jnp.concatenate(pieces, axis=0)
    h_ref[...] = h
    r = lax.rsqrt(jnp.mean(h * h, axis=-1, keepdims=True) + EPS)
    n_ref[...] = (h * r * g_ref[...]).astype(n_ref.dtype)


def _embed(x, meta_tokens, g):
    bsz, seq, d = x.shape
    assert seq == SEQ and LP % (EMBED_PIECES * HG_CHUNK) == 0
    tr = EMBED_PIECES * HG_CHUNK
    piece = lambda k: pl.BlockSpec((None, HG_CHUNK, d), lambda b, r: (b, jnp.maximum(EMBED_PIECES * r + k - 1, 0), 0))
    row = pl.BlockSpec((None, tr, d), lambda b, r: (b, r, 0))
    return pl.pallas_call(
        _embed_kernel,
        grid=(bsz, LP // tr),
        in_specs=[piece(k) for k in range(EMBED_PIECES)] + [pl.BlockSpec((N_META, d), lambda b, r: (0, 0)),
                                                           pl.BlockSpec((1, d), lambda b, r: (0, 0))],
        out_specs=[row, row],
        out_shape=[jax.ShapeDtypeStruct((bsz, LP, d), jnp.float32), jax.ShapeDtypeStruct((bsz, LP, d), jnp.bfloat16)],
        compiler_params=pltpu.CompilerParams(
            dimension_semantics=("parallel", "parallel"), vmem_limit_bytes=V7X_VMEM_LIMIT),
        name="embed_rmsnorm_pre",
    )(*([x] * EMBED_PIECES), meta_tokens, g.reshape(1, d))


def _post_pre_kernel(h_ref, y_ref, g_ref, gn_ref, o_ref, n_ref):
    y = y_ref[...].astype(jnp.float32)
    r = lax.rsqrt(jnp.mean(y * y, axis=-1, keepdims=True) + EPS)
    h = h_ref[...] + y * r * g_ref[...]
    o_ref[...] = h
    rn = lax.rsqrt(jnp.mean(h * h, axis=-1, keepdims=True) + EPS)
    n_ref[...] = (h * rn * gn_ref[...]).astype(n_ref.dtype)


FINAL_PIECES = 4


def _final_kernel(*refs):
    h_refs, y_refs = refs[:FINAL_PIECES], refs[FINAL_PIECES:2 * FINAL_PIECES]
    g_ref, o_ref = refs[2 * FINAL_PIECES:]
    for k, (h_ref, y_ref) in enumerate(zip(h_refs, y_refs)):
        y = y_ref[...].astype(jnp.float32)
        r = lax.rsqrt(jnp.mean(y * y, axis=-1, keepdims=True) + EPS)
        o_ref[k * HG_CHUNK:(k + 1) * HG_CHUNK, :] = (h_ref[...] + y * r * g_ref[...]).astype(o_ref.dtype)


def _final_residual(h3, y3, g, out_dtype):
    bsz, lp, d = h3.shape
    assert PAD + N_META == HG_CHUNK and SEQ % (FINAL_PIECES * HG_CHUNK) == 0
    piece = lambda k: pl.BlockSpec((None, HG_CHUNK, d), lambda b, r: (b, FINAL_PIECES * r + k + 1, 0))
    pieces = [piece(k) for k in range(FINAL_PIECES)]
    return pl.pallas_call(
        _final_kernel,
        grid=(bsz, SEQ // (FINAL_PIECES * HG_CHUNK)),
        in_specs=pieces + pieces + [pl.BlockSpec((1, d), lambda b, r: (0, 0))],
        out_specs=pl.BlockSpec((None, FINAL_PIECES * HG_CHUNK, d), lambda b, r: (b, r, 0)),
        out_shape=jax.ShapeDtypeStruct((bsz, SEQ, d), out_dtype),
        compiler_params=pltpu.CompilerParams(
            dimension_semantics=("parallel", "parallel"), vmem_limit_bytes=V7X_VMEM_LIMIT),
        name="residual_rmsnorm_post_final",
    )(*([h3] * FINAL_PIECES), *([y3] * FINAL_PIECES), g.reshape(1, d))


def _residual_rmsnorm(h2d, y2d, g, g_next, tr=320):
    m, d = h2d.shape
    row = pl.BlockSpec((tr, d), lambda i: (i, 0))
    vec = pl.BlockSpec((1, d), lambda i: (0, 0))
    params = pltpu.CompilerParams(dimension_semantics=("parallel",), vmem_limit_bytes=V7X_VMEM_LIMIT)
    return pl.pallas_call(
        _post_pre_kernel, grid=(m // tr,), in_specs=[row, row, vec, vec], out_specs=[row, row],
        out_shape=[jax.ShapeDtypeStruct((m, d), jnp.float32), jax.ShapeDtypeStruct((m, d), jnp.bfloat16)],
        compiler_params=params, name="residual_rmsnorm_post_pre",
    )(h2d, y2d, g.reshape(1, d), g_next.reshape(1, d))


def _matmul_kernel(x_ref, w_ref, o_ref):
    o_ref[...] = jnp.dot(x_ref[...], w_ref[...].astype(jnp.bfloat16),
                         preferred_element_type=jnp.float32).astype(o_ref.dtype)


def _matmul(x, w_stack, layer, out_dtype, tm=1040, tn=512):
    m, kd = x.shape
    _, _, n = w_stack.shape
    assert m % tm == 0 and n % tn == 0
    return pl.pallas_call(
        _matmul_kernel,
        grid=(m // tm, n // tn),
        in_specs=[pl.BlockSpec((tm, kd), lambda i, j: (i, 0)),
                  pl.BlockSpec((None, kd, tn), lambda i, j: (layer, 0, j))],
        out_specs=pl.BlockSpec((tm, tn), lambda i, j: (i, j)),
        out_shape=jax.ShapeDtypeStruct((m, n), out_dtype),
        compiler_params=pltpu.CompilerParams(
            dimension_semantics=("parallel", "parallel"), vmem_limit_bytes=V7X_VMEM_LIMIT),
        name="projection_matmul",
    )(x, w_stack)


FFT_N1 = 128
FFT_N2 = 65
FFT_N = FFT_N1 * FFT_N2
N2P = 80
MIDR = 72
SLAB = 264
TC = 128
MID_GROUP = 16
NC_HALF = FFT_N1 // 2 + 1
HALF_P = 72
SLAB_S = 152
C_GROUP = 65
FILL_CHUNKS = 13
B_GROUP = 65


def _dft_constants():
    a = np.arange(FFT_N1)
    c = np.arange(FFT_N1)
    w1 = np.exp(-2j * np.pi * np.outer(c, a) / FFT_N1)
    w1r, w1i = w1.real, w1.imag
    half = FFT_N1 // 2
    f1 = np.block([[w1r[:, :half], -w1i[:, :half]], [w1i[:, :half], w1r[:, :half]]])
    f1_real = np.zeros((2 * HALF_P, FFT_N1), np.float64)
    f1_real[:NC_HALF] = w1r[:NC_HALF]
    f1_real[HALF_P:HALF_P + NC_HALF] = w1i[:NC_HALF]
    g = np.conj(w1).T
    gr, gi = g.real[:half], g.imag[:half]
    i2 = np.block([[gr, -gi], [gi, gr]])
    b = np.arange(FFT_N2)
    d = np.arange(FFT_N2)
    w2 = np.exp(-2j * np.pi * np.outer(d, b) / FFT_N2)
    tw = np.exp(-2j * np.pi * np.outer(c, b) / FFT_N)
    m = w2[None, :, :] * tw[:, None, :]
    ab = np.zeros((FFT_N1, 2 * MIDR, 128), np.float64)
    m[NC_HALF:] = m[NC_HALF:, ::-1, :]
    ab[:, :FFT_N2, :FFT_N2] = m.real
    ab[:, MIDR:MIDR + FFT_N2, :FFT_N2] = m.imag
    bf = jnp.bfloat16
    abt = np.zeros((FFT_N1, N2P, 256), np.float64)
    abt[:, :, :2 * MIDR] = np.swapaxes(ab[:, :, :N2P], 1, 2)
    return (jnp.asarray(f1, bf), jnp.asarray(f1_real, bf), jnp.asarray(i2, bf), jnp.asarray(ab, bf),
            jnp.asarray(abt, bf))


def _middle_stage_inputs(y_ref, c, imag_row=FFT_N1, pitch=SLAB):
    yr = y_ref[pl.ds(c, MIDR, stride=pitch), :]
    yi = y_ref[pl.ds(imag_row + c, MIDR, stride=pitch), :]
    return jnp.concatenate([yr, yi], axis=1)


def _forward_middle(ab, y2):
    y2 = jnp.concatenate([y2, jnp.zeros((128 - MIDR, 2 * TC), jnp.float32)], axis=0).astype(jnp.bfloat16)
    q = jnp.dot(ab, y2, preferred_element_type=jnp.float32)
    zr = q[:MIDR, :TC] - q[MIDR:, TC:]
    zi = q[MIDR:, :TC] + q[:MIDR, TC:]
    return zr, zi


def _spectrum_kernel(h_ref, t_ref, w3f0_ref, w3f1_ref, w3b0_ref, w3b1_ref, delta_ref, f1_ref, ab_ref, k_ref,
                     kt_ref, y_ref):
    w3 = jnp.concatenate([jnp.concatenate([w3f0_ref[...], w3f1_ref[...]], axis=1),
                          jnp.concatenate([w3b0_ref[...], w3b1_ref[...]], axis=1)], axis=0).astype(jnp.bfloat16)
    delta = delta_ref[...]
    rows = FFT_N // FILL_CHUNKS

    def fill(i, carry):
        r0 = pl.multiple_of(i * rows, 8)
        kt = jnp.dot(h_ref[pl.ds(r0, rows), :], w3, preferred_element_type=jnp.float32)
        window = jnp.exp(t_ref[pl.ds(r0, rows), :] * delta)
        for n in range(HY_ORDER):
            kt_ref[n, pl.ds(r0, rows), :] = kt[:, n * TC:(n + 1) * TC] * window
        return carry

    lax.fori_loop(0, FILL_CHUNKS, fill, 0)
    y_ref[pl.ds(FFT_N2 * SLAB_S, (MIDR - FFT_N2) * SLAB_S), :] = jnp.zeros(((MIDR - FFT_N2) * SLAB_S, TC), jnp.float32)

    for n in range(HY_ORDER):
        def stage1(g, carry):
            bs = [g * B_GROUP + i for i in range(B_GROUP)]
            x = jnp.concatenate([kt_ref.at[n][pl.ds(b, FFT_N1, stride=FFT_N2), :] for b in bs], axis=1)
            y = jnp.dot(f1_ref[...], x.astype(jnp.bfloat16), preferred_element_type=jnp.float32)
            for i, b in enumerate(bs):
                y_ref[pl.ds(pl.multiple_of(b * SLAB_S, 8), 2 * HALF_P), :] = y[:, i * TC:(i + 1) * TC]
            return carry

        lax.fori_loop(0, FFT_N2 // B_GROUP, stage1, 0)

        def stage2(g, carry):
            cs = [g * C_GROUP + i for i in range(C_GROUP)]
            zs = [_forward_middle(ab_ref[c], _middle_stage_inputs(y_ref, c, HALF_P, SLAB_S)) for c in cs]
            for c, (zr, zi) in zip(cs, zs):
                z = jnp.concatenate([zr, zi], axis=1) * (1.0 / FFT_N)
                k_ref[n, c] = jnp.concatenate(
                    [z, jnp.zeros((N2P - MIDR, 2 * TC), jnp.float32)], axis=0).astype(jnp.bfloat16)
            return carry

        lax.fori_loop(0, NC_HALF // C_GROUP, stage2, 0)


def _filter_spectrum(h2, w3, deltas, consts):
    _, f1_real, _, ab, _ = consts
    nt = HY_WIDTH // TC
    j = np.arange(FFT_N)
    lag = np.where(j < L_TOK, j, FFT_N - j).astype(np.float32)
    neg_t = jnp.asarray(np.broadcast_to((-lag / np.float32(L_TOK - 1))[:, None], (FFT_N, TC)))
    w3_cols = lambda direction, order: pl.BlockSpec((64, TC), lambda i: (0, (direction * HY_ORDER + order) * nt + i))
    return pl.pallas_call(
        _spectrum_kernel,
        grid=(nt,),
        in_specs=[
            pl.BlockSpec((FFT_N, 128), lambda i: (0, 0)),
            pl.BlockSpec((FFT_N, TC), lambda i: (0, 0)),
            w3_cols(0, 0), w3_cols(0, 1), w3_cols(1, 0), w3_cols(1, 1),
            pl.BlockSpec((1, TC), lambda i: (0, i)),
            pl.BlockSpec((2 * HALF_P, FFT_N1), lambda i: (0, 0)),
            pl.BlockSpec((FFT_N1, 2 * MIDR, 128), lambda i: (0, 0, 0)),
        ],
        out_specs=pl.BlockSpec((HY_ORDER, None, NC_HALF, N2P, 2 * TC), lambda i: (0, i, 0, 0, 0)),
        out_shape=jax.ShapeDtypeStruct((HY_ORDER, nt, NC_HALF, N2P, 2 * TC), jnp.bfloat16),
        scratch_shapes=[pltpu.VMEM((HY_ORDER, FFT_N, TC), jnp.float32), pltpu.VMEM((MIDR * SLAB_S, TC), jnp.float32)],
        compiler_params=pltpu.CompilerParams(dimension_semantics=("parallel",), vmem_limit_bytes=V7X_VMEM_LIMIT),
        name="hyena_filter_spectrum",
    )(h2, neg_t, w3, w3, w3, w3, deltas, f1_real, ab)


def _conv_kernel(u_ref, k_ref, f1_ref, i2_ref, ab_ref, abt_ref, o_ref, y_ref):
    _conv_body(u_ref, k_ref, f1_ref, i2_ref, ab_ref, abt_ref, o_ref, y_ref)


def _conv_raw_kernel(p_ref, w_ref, k_ref, f1_ref, i2_ref, ab_ref, abt_ref, o_ref, v_ref, y_ref):
    for ri in range(2):
        v_ref[ri] = _short_conv_rows(p_ref[ri].astype(jnp.float32), w_ref[...])
    _conv_body(v_ref, k_ref, f1_ref, i2_ref, ab_ref, abt_ref, o_ref, y_ref)


def _conv_body(u_ref, k_ref, f1_ref, i2_ref, ab_ref, abt_ref, o_ref, y_ref):
    y_ref[pl.ds(FFT_N2 * SLAB, (MIDR - FFT_N2) * SLAB), :] = jnp.zeros(((MIDR - FFT_N2) * SLAB, TC), jnp.float32)
    half = FFT_N1 // 2

    def stage1(g, carry):
        bs = [g * B_GROUP + i for i in range(B_GROUP)]
        x = jnp.concatenate(
            [jnp.concatenate([u_ref.at[ri][pl.ds(b, half, stride=FFT_N2), :] for b in bs], axis=1) for ri in range(2)],
            axis=0)
        y = jnp.dot(f1_ref[...], x.astype(jnp.bfloat16), preferred_element_type=jnp.float32)
        for i, b in enumerate(bs):
            y_ref[pl.ds(pl.multiple_of(b * SLAB, 8), 2 * FFT_N1), :] = y[:, i * TC:(i + 1) * TC]
        return carry

    lax.fori_loop(0, FFT_N2 // B_GROUP, stage1, 0)

    def inverse_middle(c, z):
        zr, zi = z
        mirrored = c >= NC_HALF
        kk = k_ref[jnp.where(mirrored, FFT_N1 - c, c)].astype(jnp.float32)[:MIDR]
        kr, ki = kk[:, :TC], kk[:, TC:] * jnp.where(mirrored, -1.0, 1.0)
        pr = zr * kr - zi * ki
        pi = zr * ki + zi * kr
        rhs = jnp.concatenate([jnp.concatenate([pr, pi], axis=1), jnp.concatenate([pi, -pr], axis=1),
                               jnp.zeros((256 - 2 * MIDR, 2 * TC), jnp.float32)], axis=0)
        return jnp.dot(abt_ref[c], rhs.astype(jnp.bfloat16), preferred_element_type=jnp.float32)

    def middle(g, carry):
        cs = [g * MID_GROUP + i for i in range(MID_GROUP)]
        zs = [_forward_middle(ab_ref[c], _middle_stage_inputs(y_ref, c)) for c in cs]
        outs = [inverse_middle(c, z) for c, z in zip(cs, zs)]
        for c, v in zip(cs, outs):
            y_ref[pl.ds(c, MIDR, stride=SLAB), :] = v[:MIDR, :TC]
            y_ref[pl.ds(FFT_N1 + c, MIDR, stride=SLAB), :] = v[:MIDR, TC:]
        return carry

    lax.fori_loop(0, FFT_N1 // MID_GROUP, middle, 0)

    def stage4(g, carry):
        bs = [g * B_GROUP + i for i in range(B_GROUP)]
        v = jnp.concatenate([y_ref[pl.ds(pl.multiple_of(b * SLAB, 8), 2 * FFT_N1), :] for b in bs], axis=1)
        x = jnp.dot(i2_ref[...], v.astype(jnp.bfloat16), preferred_element_type=jnp.float32)
        for i, b in enumerate(bs):
            o_ref.at[0][pl.ds(b, half, stride=FFT_N2), :] = x[:half, i * TC:(i + 1) * TC]
            o_ref.at[1][pl.ds(b, half, stride=FFT_N2), :] = x[half:, i * TC:(i + 1) * TC]
        return carry

    lax.fori_loop(0, FFT_N2 // B_GROUP, stage4, 0)


def _long_conv(u, kspec, order, consts, w_short_t=None):
    f1, _, i2, ab, abt = consts
    bsz, lp, _ = u.shape
    assert bsz == 2 and lp == LP
    nt = HY_WIDTH // TC
    seq = pl.BlockSpec((bsz, lp, TC), lambda i: (0, 0, i))
    out = jax.ShapeDtypeStruct((bsz, lp, HY_WIDTH), jnp.float32)
    raw = w_short_t is not None
    return pl.pallas_call(
        _conv_raw_kernel if raw else _conv_kernel,
        grid=(nt,),
        in_specs=[seq] + ([pl.BlockSpec((3, TC), lambda i: (0, i))] if raw else []) + [
            pl.BlockSpec((None, None, NC_HALF, N2P, 2 * TC), lambda i: (order, i, 0, 0, 0)),
            pl.BlockSpec((2 * FFT_N1, FFT_N1), lambda i: (0, 0)),
            pl.BlockSpec((FFT_N1, 2 * FFT_N1), lambda i: (0, 0)),
            pl.BlockSpec((FFT_N1, 2 * MIDR, 128), lambda i: (0, 0, 0), pipeline_mode=pl.Buffered(1)),
            pl.BlockSpec((FFT_N1, N2P, 256), lambda i: (0, 0, 0), pipeline_mode=pl.Buffered(1)),
        ],
        out_specs=[seq, seq] if raw else seq,
        out_shape=[out, out] if raw else out,
        scratch_shapes=[pltpu.VMEM((MIDR * SLAB, TC), jnp.float32)],
        compiler_params=pltpu.CompilerParams(dimension_semantics=("parallel",), vmem_limit_bytes=V7X_VMEM_LIMIT),
        name="hyena_long_conv",
    )(*((u, w_short_t) if raw else (u,)), kspec, f1, i2, ab, abt)


def _short_conv_rows(u, w):
    y = pltpu.roll(u, 1, 0) * w[0:1] + u * w[1:2] + pltpu.roll(u, LP - 1, 0) * w[2:3]
    row = lax.broadcasted_iota(jnp.int32, y.shape, 0)
    return jnp.where(row >= PAD, y, 0.0)


def _gate1_kernel(v_ref, u_ref, w_ref, y_ref, b_ref, o_ref):
    x = _short_conv_rows(u_ref[...].astype(jnp.float32), w_ref[...])
    o_ref[...] = x * (y_ref[...] + v_ref[...] * b_ref[...])


def _gate2_kernel(z_ref, u_ref, w_ref, y_ref, b_ref, g_ref, o_ref):
    g = g_ref[...].astype(jnp.float32)
    z = _short_conv_rows(u_ref[...].astype(jnp.float32), w_ref[...]) * (y_ref[...] + z_ref[...] * b_ref[...])
    o_ref[...] = (z * (g * jax.nn.sigmoid(g))).astype(o_ref.dtype)


def _seq_tile(col_block):
    return pl.BlockSpec((None, LP, TC), lambda b, i: (b, 0, col_block + i))


def _vec_tile(rows, col_block):
    return pl.BlockSpec((rows, TC), lambda b, i: (0, col_block + i))


def _hyena_gate1(v, p3, w_short_t, y, bias):
    bsz = p3.shape[0]
    nt = HY_WIDTH // TC
    return pl.pallas_call(
        _gate1_kernel,
        grid=(bsz, nt),
        in_specs=[_seq_tile(0), _seq_tile(nt), _vec_tile(3, nt), _seq_tile(0), _vec_tile(1, 0)],
        out_specs=_seq_tile(0),
        out_shape=jax.ShapeDtypeStruct((bsz, LP, HY_WIDTH), jnp.float32),
        compiler_params=pltpu.CompilerParams(
            dimension_semantics=("parallel", "parallel"), vmem_limit_bytes=V7X_VMEM_LIMIT),
        name="hyena_gate1",
    )(v, p3, w_short_t, y, bias.reshape(1, HY_WIDTH))


def _hyena_gate2(z1, p3, w_short_t, y, bias):
    bsz = p3.shape[0]
    nt = HY_WIDTH // TC
    return pl.pallas_call(
        _gate2_kernel,
        grid=(bsz, nt),
        in_specs=[_seq_tile(0), _seq_tile(2 * nt), _vec_tile(3, 2 * nt), _seq_tile(0), _vec_tile(1, 0),
                  _seq_tile(3 * nt)],
        out_specs=_seq_tile(0),
        out_shape=jax.ShapeDtypeStruct((bsz, LP, HY_WIDTH), jnp.bfloat16),
        compiler_params=pltpu.CompilerParams(
            dimension_semantics=("parallel", "parallel"), vmem_limit_bytes=V7X_VMEM_LIMIT),
        name="hyena_gate2",
    )(z1, p3, w_short_t, y, bias.reshape(1, HY_WIDTH), p3)


def _filter_positions():
    f32 = np.float32
    l = L_TOK
    t = np.linspace(0.0, 1.0, l, dtype=f32)[:, None]
    w = f32(2.0 * math.pi) * np.arange(l, dtype=f32)[:, None] / f32(l)
    bands = np.linspace(1e-4, HY_BANDS - 1, HY_BANDS, dtype=f32)[None, :]
    z = np.concatenate([t, np.cos(bands * w), -np.sin(bands * w)], axis=-1).astype(f32)
    j = np.arange(FFT_N)
    lag = np.minimum(np.where(j < l, j, FFT_N - j), l - 1)
    zrows = np.zeros((FFT_N, 128), f32)
    zrows[:, :HY_EMB] = z[lag]
    mask = np.zeros((FFT_N, 128), f32)
    mask[:, :64] = (j < l)[:, None]
    mask[:, 64:] = ((j > FFT_N - l) | (j == 0))[:, None]
    return jnp.asarray(zrows), jnp.asarray(mask)


def _hidden_kernel(z_ref, m_ref, w1_ref, b1_ref, fr1_ref, w2_ref, b2_ref, fr2_ref, o_ref):
    hi = lax.Precision.HIGHEST
    a = jnp.dot(z_ref[...], w1_ref[...], precision=hi, preferred_element_type=jnp.float32) + b1_ref[...]
    hid = jnp.sin(fr1_ref[...] * a)
    a = jnp.dot(hid, w2_ref[...], precision=hi, preferred_element_type=jnp.float32) + b2_ref[...]
    hid = jnp.sin(fr2_ref[...] * a)
    o_ref[...] = (jnp.concatenate([hid, hid], axis=1) * m_ref[...]).astype(o_ref.dtype)


def _hyena_hidden(w1, b1, fr1, w2, b2, fr2):
    zrows, mask = _filter_positions()
    rows = FFT_N // FILL_CHUNKS
    hdim = w2.shape[0]
    w1p = jnp.zeros((128, hdim), jnp.float32).at[:HY_EMB].set(w1.astype(jnp.float32))
    tile = pl.BlockSpec((rows, 128), lambda i: (i, 0))
    full = lambda a: pl.BlockSpec(a.shape, lambda i: (0,) * a.ndim)
    vecs = [v.astype(jnp.float32).reshape(1, hdim) for v in (b1, fr1, b2, fr2)]
    w2f = w2.astype(jnp.float32)
    return pl.pallas_call(
        _hidden_kernel,
        grid=(FILL_CHUNKS,),
        in_specs=[tile, tile, full(w1p), full(vecs[0]), full(vecs[1]), full(w2f), full(vecs[2]), full(vecs[3])],
        out_specs=tile,
        out_shape=jax.ShapeDtypeStruct((FFT_N, 128), jnp.bfloat16),
        compiler_params=pltpu.CompilerParams(dimension_semantics=("parallel",), vmem_limit_bytes=V7X_VMEM_LIMIT),
        name="hyena_filter_hidden",
    )(zrows, mask, w1p, vecs[0], vecs[1], w2f, vecs[2], vecs[3])


def _odd_mix(p3, w_short, w1, b1, fr1, w2, b2, fr2, w3, bias, consts):
    min_decay = math.log(HY_TARGET) / HY_SLOW_DECAY
    max_decay = math.log(HY_TARGET) / HY_FAST_DECAY
    deltas = jnp.abs(jnp.linspace(min_decay, max_decay, HY_WIDTH, dtype=jnp.float32)).reshape(1, HY_WIDTH)
    kspec = _filter_spectrum(_hyena_hidden(w1, b1, fr1, w2, b2, fr2), w3, deltas, consts)
    w_short_t = w_short.T
    y1, v = _long_conv(p3, kspec, 0, consts, w_short_t)
    z1 = _hyena_gate1(v, p3, w_short_t, y1, bias[0])
    y2 = _long_conv(z1, kspec, 1, consts)
    return _hyena_gate2(z1, p3, w_short_t, y2, bias[1])


HG_HB = 16
HG_LEVELS = (32, 16, 8, 4, 2, 1)
LOG2_E = 1.4426950408889634


def _cumsum_rows(g, rev):
    f32, bf = jnp.float32, jnp.bfloat16
    t = lax.broadcasted_iota(jnp.int32, (HG_CHUNK, 3 * HG_CHUNK), 0)
    s = lax.broadcasted_iota(jnp.int32, (HG_CHUNK, 3 * HG_CHUNK), 1) & (HG_CHUNK - 1)
    tri = jnp.where((s >= t) if rev else (s <= t), 1.0, 0.0).astype(bf)
    hi = g.astype(bf)
    r1 = g - hi.astype(f32)
    mid = r1.astype(bf)
    lo = (r1 - mid.astype(f32)).astype(bf)
    return jnp.dot(tri, jnp.concatenate([hi, mid, lo], axis=0), preferred_element_type=f32)


def _hgrn_row_masks(rev):
    row = lax.broadcasted_iota(jnp.int32, (HG_CHUNK, HEAD_DIM), 0)
    masks = {}
    for m in HG_LEVELS:
        upper = (row & m) != 0
        is_query = jnp.logical_not(upper) if rev else upper
        target = m if rev else m - 1
        selectors = [(src, (row & (2 * m - 1)) == src) for src in range(2 * m) if src != target] if m == 2 else None
        masks[m] = (is_query, jnp.where(is_query, LOG2_E, -LOG2_E), selectors)
    return masks


def _level_reference(gc, m, rev, selectors):
    if m >= 4:
        g3 = gc.reshape(HG_CHUNK // (2 * m), 2 * m, HEAD_DIM)
        ref = g3[:, m:m + 1, :] if rev else g3[:, m - 1:m, :]
        return jnp.broadcast_to(ref, g3.shape).reshape(HG_CHUNK, HEAD_DIM)
    target = m if rev else m - 1
    out = gc
    for src, picks_src in selectors:
        out = jnp.where(picks_src, pltpu.roll(gc, (src - target) % HG_CHUNK, 0), out)
    return out


def _hgrn_gates(q_raw, f_raw, la, lc, omlb, rev):
    q = q_raw * (0.5 * HEAD_DIM ** -0.5) * (1.0 + jnp.tanh(0.5 * q_raw))
    k = (0.5 * omlb) * (1.0 - jnp.tanh(0.5 * f_raw))
    log_sig = jnp.minimum(f_raw, 0.0) - jnp.log(1.0 + jnp.exp(-jnp.abs(f_raw)))
    b = lc + log_sig
    g = jnp.maximum(la, b) + jnp.log(1.0 + jnp.exp(-jnp.abs(la - b)))
    return q, k, g, _cumsum_rows(g, rev)


def _hgrn_level_operands(q, k, g, gc, rev, masks):
    bf = jnp.bfloat16
    qb, kb = q.astype(bf), k.astype(bf)
    ops = []
    for m in HG_LEVELS:
        is_query, sign_log2e, selectors = masks[m]
        if m == 1:
            ops.append(jnp.where(is_query, qb * jnp.exp2(g * LOG2_E).astype(bf), kb))
        else:
            gref = _level_reference(gc, m, rev, selectors)
            ops.append(jnp.where(is_query, qb, kb) * jnp.exp2((gc - gref) * sign_log2e).astype(bf))
    return ops


def _hgrn_level_id(rev):
    t = lax.broadcasted_iota(jnp.int32, (HG_CHUNK, HG_CHUNK), 0)
    s = lax.broadcasted_iota(jnp.int32, (HG_CHUNK, HG_CHUNK), 1)
    vis = (s >= t) if rev else (s <= t)
    x = t ^ s
    lvl = jnp.full((HG_CHUNK, HG_CHUNK), len(HG_LEVELS), jnp.int32)
    for i, m in enumerate(HG_LEVELS):
        lvl = jnp.where((x & m) != 0, jnp.minimum(lvl, i), lvl)
    return jnp.where(vis, lvl, -1)


def _hgrn_kernel(qf_ref, if_ref, ff_ref, qb_ref, ib_ref, fb_ref, la_ref, lc_ref, om_ref, of_ref, ob_ref, s_ref):
    f32, bf = jnp.float32, jnp.bfloat16
    nt = (((1,), (1,)), ((), ()))
    tn = (((0,), (0,)), ((), ()))

    @pl.when(pl.program_id(2) == 0)
    def _():
        s_ref[...] = jnp.zeros_like(s_ref)

    row_masks = [_hgrn_row_masks(False), _hgrn_row_masks(True)]
    refs = ((qf_ref, if_ref, ff_ref, of_ref), (qb_ref, ib_ref, fb_ref, ob_ref))
    inst = [(d, h) for d in range(2) for h in range(HG_HB)]
    level_masks = [[lvl == i for i in range(len(HG_LEVELS) + 1)]
                   for lvl in (_hgrn_level_id(False), _hgrn_level_id(True))]

    pre = []
    for d, h in inst:
        q_ref, i_ref, f_ref, _ = refs[d]
        sl = slice(h * HEAD_DIM, (h + 1) * HEAD_DIM)
        q, k, g, gc = _hgrn_gates(q_ref[:, sl], f_ref[:, sl], la_ref[d:d + 1, sl], lc_ref[d:d + 1, sl],
                                  om_ref[d:d + 1, sl], d == 1)
        g_end = gc[0:1, :] if d == 1 else gc[HG_CHUNK - 1:HG_CHUNK, :]
        pre.append(dict(ops=_hgrn_level_operands(q, k, g, gc, d == 1, row_masks[d]), qk=(q.astype(bf), k.astype(bf)),
                        qe=(q * jnp.exp(gc)).astype(bf), kd=(k * jnp.exp(g_end - gc)).astype(bf),
                        decay=jnp.exp(g_end), v=i_ref[:, sl].astype(bf)))

    attn = []
    for (d, h), pr in zip(inst, pre):
        prods = [lax.dot_general(x, x, nt, preferred_element_type=f32) for x in pr["ops"]]
        prods.append(lax.dot_general(pr["qk"][0], pr["qk"][1], nt, preferred_element_type=f32))
        a = jnp.zeros((HG_CHUNK, HG_CHUNK), f32)
        for i, pmat in enumerate(prods):
            a = jnp.where(level_masks[d][i], pmat, a)
        attn.append(a.astype(bf))

    for (d, h), pr, a in zip(inst, pre, attn):
        o_ref = refs[d][3]
        sl = slice(h * HEAD_DIM, (h + 1) * HEAD_DIM)
        state_t = s_ref[d, h]
        o = jnp.dot(a, pr["v"], preferred_element_type=f32)
        o = o + lax.dot_general(pr["qe"], state_t.astype(bf), nt, preferred_element_type=f32)
        o_ref[:, sl] = o.astype(o_ref.dtype)
        s_ref[d, h] = pr["decay"] * state_t + lax.dot_general(pr["v"], pr["kd"], tn, preferred_element_type=f32)


def _hgrn2(p, col0, lb):
    bsz, lp, _ = p.shape
    nchunk = lp // HG_CHUNK
    cw = HG_HB * HEAD_DIM
    ng = HG_WIDTH // cw
    off = col0 // cw
    la = jnp.log(lb)
    lc = jnp.log1p(-lb)
    om = 1.0 - lb
    fwd = lambda seg: pl.BlockSpec((None, HG_CHUNK, cw), lambda b, g, c: (b, c, off + seg * ng + g))
    bwd = lambda seg: pl.BlockSpec((None, HG_CHUNK, cw), lambda b, g, c: (b, nchunk - 1 - c, off + seg * ng + g))
    par = pl.BlockSpec((2, cw), lambda b, g, c: (0, g))
    return pl.pallas_call(
        _hgrn_kernel,
        grid=(bsz, ng, nchunk),
        in_specs=[fwd(0), fwd(1), fwd(2), bwd(0), bwd(1), bwd(3), par, par, par],
        out_specs=[pl.BlockSpec((None, HG_CHUNK, cw), lambda b, g, c: (b, c, g)),
                   pl.BlockSpec((None, HG_CHUNK, cw), lambda b, g, c: (b, nchunk - 1 - c, g))],
        out_shape=[jax.ShapeDtypeStruct((bsz, lp, HG_WIDTH), jnp.bfloat16)] * 2,
        scratch_shapes=[pltpu.VMEM((2, HG_HB, HEAD_DIM, HEAD_DIM), jnp.float32)],
        compiler_params=pltpu.CompilerParams(
            dimension_semantics=("parallel", "parallel", "arbitrary"), vmem_limit_bytes=V7X_VMEM_LIMIT),
        name="hgrn2_bidirectional",
    )(p, p, p, p, p, p, la, lc, om)


EG_ROWS = 320


def _even_gate_kernel(oa_ref, of_ref, ob_ref, ga_ref, gb_ref, gain_ref, y_ref):
    ga = ga_ref[...]
    y_ref[:, :NA_WIDTH] = (oa_ref[...].astype(jnp.float32) * (ga * jax.nn.sigmoid(ga))).astype(y_ref.dtype)
    for h in range(HG_HEADS):
        sl = slice(h * HEAD_DIM, (h + 1) * HEAD_DIM)
        oh = of_ref[:, sl].astype(jnp.float32) + ob_ref[:, sl].astype(jnp.float32)
        oh = oh * lax.rsqrt(jnp.mean(oh * oh, axis=-1, keepdims=True) + EPS) * gain_ref[:, sl]
        gb = gb_ref[:, sl]
        y_ref[:, NA_WIDTH + h * HEAD_DIM:NA_WIDTH + (h + 1) * HEAD_DIM] = (
            oh * (gb * jax.nn.sigmoid(gb))).astype(y_ref.dtype)


def _even_gate(oa, o_fwd, o_bwd, p, gate_col0, norm_gain):
    bsz, lp, _ = oa.shape
    assert gate_col0 % NA_WIDTH == 0
    goff = gate_col0 // NA_WIDTH
    half = lambda off: pl.BlockSpec((None, EG_ROWS, NA_WIDTH), lambda b, r: (b, r, off))
    return pl.pallas_call(
        _even_gate_kernel,
        grid=(bsz, lp // EG_ROWS),
        in_specs=[half(0), half(0), half(0), half(goff), half(goff + 1),
                  pl.BlockSpec((1, HG_WIDTH), lambda b, r: (0, 0))],
        out_specs=pl.BlockSpec((None, EG_ROWS, D_MODEL), lambda b, r: (b, r, 0)),
        out_shape=jax.ShapeDtypeStruct((bsz, lp, D_MODEL), jnp.bfloat16),
        compiler_params=pltpu.CompilerParams(
            dimension_semantics=("parallel", "parallel"), vmem_limit_bytes=V7X_VMEM_LIMIT),
        name="even_gate",
    )(oa, o_fwd, o_bwd, p, p, norm_gain.reshape(1, HG_WIDTH))


NA_MASKED = -1e30
NA_GROUP = 64


def _na_bias_table(rpb):
    cols = np.arange(GRID_W)
    c0 = np.clip(cols - NA_KW // 2, 0, GRID_W - NA_KW)
    col_ok = (cols[None, :] >= c0[:, None]) & (cols[None, :] < c0[:, None] + NA_KW)
    col_idx = np.clip(cols[None, :] - cols[:, None] + NA_KW - 1, 0, 2 * NA_KW - 2)
    onehot = (col_idx[None] == np.arange(2 * NA_KW - 1)[:, None, None]).astype(np.float32)
    picked = jnp.einsum('hrj,jqk->hrqk', rpb.astype(jnp.float32), onehot, precision=lax.Precision.HIGHEST)
    table = jnp.where(col_ok[None, None], picked, NA_MASKED)
    return jnp.concatenate([table[:, :-1], table[:, 1:]], axis=-1)


def _na_kernel(q_ref, k_ref, v_ref, t_ref, mb_ref, o_ref):
    f32, bf = jnp.float32, jnp.bfloat16
    rows = (q_ref.shape[0] - HG_CHUNK) // GRID_W
    scale = HEAD_DIM ** -0.5
    nt = (((1,), (1,)), ((), ()))
    km = k_ref[PAD:PAD + N_META, :].astype(bf)
    vm = v_ref[PAD:PAD + N_META, :].astype(bf)
    mb = mb_ref[...]
    o_ref[0:PAD, :] = jnp.zeros((PAD, HEAD_DIM), o_ref.dtype)

    qm = q_ref[PAD:PAD + N_META, :].astype(bf)
    s = lax.dot_general(qm, km, nt, preferred_element_type=f32) * scale + mb
    e = jnp.exp(s - jnp.max(s, axis=-1, keepdims=True))
    o_ref[PAD:PAD + N_META, :] = (jnp.dot(e.astype(bf), vm, preferred_element_type=f32)
                                  / jnp.sum(e, axis=-1, keepdims=True)).astype(o_ref.dtype)

    def row_group(g, carry):
        rs = [g * NA_GROUP + i for i in range(NA_GROUP)]
        scores = []
        for r in rs:
            r0 = jnp.clip(r - NA_KH // 2, 0, rows - NA_KH)
            q = q_ref[pl.ds(pl.multiple_of(HG_CHUNK + GRID_W * r, GRID_W), GRID_W), :].astype(bf)
            k0 = pl.multiple_of(HG_CHUNK + GRID_W * r0, GRID_W)
            kw = k_ref[pl.ds(k0, NA_KH * GRID_W), :].astype(bf)
            s_win = lax.dot_general(q, kw, nt, preferred_element_type=f32)
            s_meta = lax.dot_general(q, km, nt, preferred_element_type=f32)
            scores.append((r, r0, k0, s_win, s_meta))
        probs = []
        for r, r0, k0, s_win, s_meta in scores:
            bias = jnp.concatenate([t_ref[r0 + i - r + NA_KH - 1] for i in range(0, NA_KH, 2)], axis=1)
            s_win = s_win * scale + bias
            s_meta = s_meta * scale + mb
            m = jnp.maximum(jnp.max(s_win, axis=-1, keepdims=True), jnp.max(s_meta, axis=-1, keepdims=True))
            pw = jnp.exp(s_win - m)
            pm = jnp.exp(s_meta - m)
            den = jnp.sum(pw, axis=-1, keepdims=True) + jnp.sum(pm, axis=-1, keepdims=True)
            probs.append((r, k0, pw.astype(bf), pm.astype(bf), den))
        for r, k0, pw, pm, den in probs:
            vw = v_ref[pl.ds(k0, NA_KH * GRID_W), :].astype(bf)
            o = jnp.dot(pw, vw, preferred_element_type=f32) + jnp.dot(pm, vm, preferred_element_type=f32)
            o_ref[pl.ds(pl.multiple_of(HG_CHUNK + GRID_W * r, GRID_W), GRID_W), :] = (o / den).astype(o_ref.dtype)
        return carry

    lax.fori_loop(0, rows // NA_GROUP, row_group, 0)


def _natten(p, rpb, meta_bias):
    bsz, lp, _ = p.shape
    assert ((lp - HG_CHUNK) // GRID_W) % NA_GROUP == 0
    heads = NA_WIDTH // HEAD_DIM
    blk = lambda seg: pl.BlockSpec((None, lp, HEAD_DIM), lambda b, h: (b, 0, seg * heads + h))
    table = _na_bias_table(rpb)
    return pl.pallas_call(
        _na_kernel,
        grid=(bsz, heads),
        in_specs=[blk(0), blk(1), blk(2),
                  pl.BlockSpec((None, 2 * NA_KH - 2, GRID_W, 2 * GRID_W), lambda b, h: (h, 0, 0, 0)),
                  pl.BlockSpec((None, 1, N_META), lambda b, h: (h, 0, 0))],
        out_specs=pl.BlockSpec((None, lp, HEAD_DIM), lambda b, h: (b, 0, h)),
        out_shape=jax.ShapeDtypeStruct((bsz, lp, NA_WIDTH), jnp.bfloat16),
        compiler_params=pltpu.CompilerParams(
            dimension_semantics=("parallel", "parallel"), vmem_limit_bytes=V7X_VMEM_LIMIT),
        name="neighbourhood_attention",
    )(p, p, p, table, meta_bias.astype(jnp.float32).reshape(heads, 1, N_META))


def _even_mix(p, rpb, meta_bias, lb, norm_gain):
    oa = _natten(p, rpb, meta_bias)
    o_fwd, o_bwd = _hgrn2(p, 3 * NA_WIDTH, lb)
    return _even_gate(oa, o_fwd, o_bwd, p, 3 * NA_WIDTH + 4 * HG_WIDTH, norm_gain)


def kernel(x, meta_tokens, norm_pre, norm_post, ev_w_in, ev_w_out, na_rpb, na_meta_bias, hg_lower, hg_norm,
           od_w_in, od_w_out, hy_short, hy_ffn_w1, hy_ffn_b1, hy_ffn_freq1, hy_ffn_w2, hy_ffn_b2, hy_ffn_freq2,
           hy_ffn_w3, hy_bias):
    b = x.shape[0]
    depth = norm_pre.shape[0]
    f32 = jnp.float32
    lb_all = jnp.cumsum(jax.nn.softmax(hg_lower.astype(f32), axis=0), axis=0)
    lb_all = lb_all - lb_all[:1]
    h, hn = _embed(x, meta_tokens, norm_pre[0])
    h = h.reshape(b * LP, D_MODEL)
    hn = hn.reshape(b * LP, D_MODEL)
    consts = _dft_constants()
    for layer in range(depth):
        j = layer // 2
        if layer % 2 == 0:
            p = _matmul(hn, ev_w_in, j, f32).reshape(b, LP, -1)
            y = _even_mix(p, na_rpb[j], na_meta_bias[j], lb_all[j], hg_norm[j]).reshape(b * LP, D_MODEL)
            w_out = ev_w_out
        else:
            p3 = _matmul(hn, od_w_in, j, jnp.bfloat16).reshape(b, LP, -1)
            y = _odd_mix(p3, hy_short[j], hy_ffn_w1[j], hy_ffn_b1[j], hy_ffn_freq1[j], hy_ffn_w2[j],
                         hy_ffn_b2[j], hy_ffn_freq2[j], hy_ffn_w3[j], hy_bias[j], consts)
            y = y.reshape(b * LP, D_MODEL)
            w_out = od_w_out
        out = _matmul(y, w_out, j, jnp.bfloat16)
        if layer + 1 < depth:
            h, hn = _residual_rmsnorm(h, out, norm_post[layer], norm_pre[layer + 1])
    return _final_residual(h.reshape(b, LP, D_MODEL), out.reshape(b, LP, D_MODEL), norm_post[depth - 1], x.dtype)
```
